```python
import jax, jax.numpy as jnp
from jax import lax
import numpy as np

D_MODEL = 2048
BATCH = 16
SEQ = 2048
DEPTH = 1

D_CONV = D_MODEL // 2
CONV_HEADS = 16
CONV_HEAD_DIM = D_CONV // CONV_HEADS
CONV_WIDTH = 3
D_LRU = D_MODEL // 2
LRU_HEADS = 8
LRU_HEAD_DIM = D_LRU // LRU_HEADS
LRU_CONV_WIDTH = 4
LRU_C = 8.0
D_MIX = D_CONV + D_LRU
D_IN = 3 * D_CONV + 2 * D_LRU
N_GROUPS = 8
EXPERTS_PER_GROUP = 8
N_EXPERTS = N_GROUPS * EXPERTS_PER_GROUP
TOP_K = 2
D_EXPERT = D_MODEL // 4
EXPERT_BLOCK = 128
EPS = 1e-6

kernel_name = "hybrid_conv_rglru_hiermoe_adaln"


def rms_normalize(x):
    xf = x.astype(jnp.float32)
    return xf * lax.rsqrt(jnp.mean(xf * xf, axis=-1, keepdims=True) + EPS)


def rmsnorm(x, g):
    return (rms_normalize(x) * g.astype(jnp.float32)).astype(x.dtype)


def modulated_rmsnorm(x, g, shift, scale):
    y = rms_normalize(x) * g.astype(jnp.float32)
    y = y * (1.0 + scale.astype(jnp.float32)[:, None, :]) + shift.astype(jnp.float32)[:, None, :]
    return y.astype(x.dtype)


def head_rmsnorm(y, g, n_heads, head_dim):
    b, s, _ = y.shape
    yn = rms_normalize(y.reshape(b, s, n_heads, head_dim)).reshape(b, s, n_heads * head_dim)
    return (yn * g.astype(jnp.float32)).astype(y.dtype)


def causal_depthwise_conv(u, w):
    k, ch = w.shape
    return lax.conv_general_dilated(
        u, w[:, None, :].astype(u.dtype), window_strides=(1,), padding=[(k - 1, 0)],
        dimension_numbers=("NWC", "WIO", "NWC"), feature_group_count=ch)


def rg_lru(xb, w_a, b_a, w_x, b_x, lam):
    bsz, s, dm = xb.shape
    xh = xb.reshape(bsz, s, LRU_HEADS, LRU_HEAD_DIM)
    r = jax.nn.sigmoid((jnp.einsum("bshi,hij->bshj", xh, w_a).reshape(bsz, s, dm) + b_a).astype(jnp.float32))
    i = jax.nn.sigmoid((jnp.einsum("bshi,hij->bshj", xh, w_x).reshape(bsz, s, dm) + b_x).astype(jnp.float32))
    log_a = -LRU_C * r * jax.nn.softplus(-lam.astype(jnp.float32))
    a = jnp.exp(log_a)
    mult = jnp.sqrt(-jnp.expm1(2.0 * log_a))
    u = mult * i * xb.astype(jnp.float32)

    def step(h, inp):
        a_t, u_t = inp
        h = a_t * h + u_t
        return h, h

    h0 = jnp.zeros((bsz, dm), jnp.float32)
    _, hs = lax.scan(step, h0, (jnp.swapaxes(a, 0, 1), jnp.swapaxes(u, 0, 1)))
    return jnp.swapaxes(hs, 0, 1).astype(xb.dtype)


def token_mix(h, w_in, conv3_w, conv4_w, conv4_b, lru_w_a, lru_b_a, lru_w_x, lru_b_x, lru_lambda,
              head_norm_conv_g, head_norm_lru_g, w_out):
    proj = h @ w_in
    b_a, c_a, x_a, g_b, x_b = jnp.split(
        proj, [D_CONV, 2 * D_CONV, 3 * D_CONV, 3 * D_CONV + D_LRU], axis=-1)
    y_a = b_a * causal_depthwise_conv(c_a * x_a, conv3_w)
    x_b = causal_depthwise_conv(x_b, conv4_w) + conv4_b.astype(x_b.dtype)
    y_b = rg_lru(x_b, lru_w_a, lru_b_a, lru_w_x, lru_b_x, lru_lambda) * jax.nn.gelu(g_b)
    y = jnp.concatenate([
        head_rmsnorm(y_a, head_norm_conv_g, CONV_HEADS, CONV_HEAD_DIM),
        head_rmsnorm(y_b, head_norm_lru_g, LRU_HEADS, LRU_HEAD_DIM)], axis=-1)
    return y @ w_out


def hier_moe(h, route_w_group, route_b_group, route_w_expert, route_b_expert, w_e_gate, w_e_up, w_e_down):
    n, d = h.shape
    group_logits = (h @ route_w_group).astype(jnp.float32) + route_b_group.astype(jnp.float32)
    group_probs = jax.nn.softmax(group_logits, axis=-1)
    p_top, g_top = lax.top_k(group_probs, 1)
    p_g, g_idx = p_top[:, 0], g_top[:, 0]
    all_logits = jnp.einsum("nd,gde->nge", h, route_w_expert).astype(jnp.float32) \
        + route_b_expert.astype(jnp.float32)
    in_group = jnp.take_along_axis(all_logits, g_idx[:, None, None], axis=1)[:, 0, :]
    top_vals, top_idx = lax.top_k(in_group, TOP_K)
    weights = (p_g[:, None] * jax.nn.softmax(top_vals, axis=-1)).astype(h.dtype)
    expert_id = g_idx[:, None] * EXPERTS_PER_GROUP + top_idx

    n_assign = n * TOP_K
    eid = expert_id.reshape(-1).astype(jnp.int32)
    tok = jnp.repeat(jnp.arange(n, dtype=jnp.int32), TOP_K)
    wts = weights.reshape(-1)
    order = jnp.argsort(eid)
    s_eid, s_tok, s_w = eid[order], tok[order], wts[order]
    counts = jnp.bincount(eid, length=N_EXPERTS)
    starts = jnp.cumsum(counts) - counts
    padded = ((counts + EXPERT_BLOCK - 1) // EXPERT_BLOCK) * EXPERT_BLOCK
    padded_ends = jnp.cumsum(padded)
    padded_starts = padded_ends - padded
    dest = padded_starts[s_eid] + (jnp.arange(n_assign, dtype=jnp.int32) - starts[s_eid])
    n_rows = -(-n_assign // EXPERT_BLOCK) * EXPERT_BLOCK + N_EXPERTS * EXPERT_BLOCK
    n_blocks = n_rows // EXPERT_BLOCK
    row_tok = jnp.full((n_rows,), n, jnp.int32).at[dest].set(s_tok)
    row_w = jnp.zeros((n_rows,), h.dtype).at[dest].set(s_w)
    block_expert = jnp.minimum(
        jnp.searchsorted(padded_ends, jnp.arange(n_blocks, dtype=jnp.int32) * EXPERT_BLOCK, side="right"),
        N_EXPERTS - 1)
    h_ext = jnp.concatenate([h, jnp.zeros((1, d), h.dtype)], axis=0)
    xb = h_ext[row_tok].reshape(n_blocks, EXPERT_BLOCK, d)

    def expert_block(args):
        x_blk, e = args
        return (jax.nn.silu(x_blk @ w_e_gate[e]) * (x_blk @ w_e_up[e])) @ w_e_down[e]

    yb = lax.map(expert_block, (xb, block_expert)).reshape(n_rows, d)
    y = jnp.zeros((n + 1, d), h.dtype).at[row_tok].add(yb * row_w[:, None])
    return y[:n]


def setup_inputs(seed: int = 0) -> dict:
    key = jax.random.key(seed)
    ks = jax.random.split(key, 32)
    f32 = jnp.float32
    nrm = lambda k, shape, scale: jax.random.normal(k, shape, f32) * scale
    L = DEPTH
    u = jax.random.uniform(ks[14], (L, D_LRU), f32, minval=0.9, maxval=0.999)
    s = u ** (1.0 / LRU_C)
    lru_lambda = jnp.log(s) - jnp.log1p(-s)
    return {
        "x": nrm(ks[0], (BATCH, SEQ, D_MODEL), 1.0),
        "c": nrm(ks[1], (BATCH, D_MODEL), 1.0),
        "ada_w": nrm(ks[2], (L, D_MODEL, 6 * D_MODEL), 0.5 * D_MODEL ** -0.5),
        "ada_b": nrm(ks[3], (L, 6 * D_MODEL), 0.02),
        "norm1_g": 1.0 + nrm(ks[4], (L, D_MODEL), 0.02),
        "w_in": nrm(ks[5], (L, D_MODEL, D_IN), D_MODEL ** -0.5),
        "conv3_w": nrm(ks[6], (L, CONV_WIDTH, D_CONV), CONV_WIDTH ** -0.5),
        "conv4_w": nrm(ks[7], (L, LRU_CONV_WIDTH, D_LRU), LRU_CONV_WIDTH ** -0.5),
        "conv4_b": nrm(ks[8], (L, D_LRU), 0.02),
        "lru_w_a": nrm(ks[9], (L, LRU_HEADS, LRU_HEAD_DIM, LRU_HEAD_DIM), LRU_HEAD_DIM ** -0.5),
        "lru_b_a": nrm(ks[10], (L, D_LRU), 0.02),
        "lru_w_x": nrm(ks[11], (L, LRU_HEADS, LRU_HEAD_DIM, LRU_HEAD_DIM), LRU_HEAD_DIM ** -0.5),
        "lru_b_x": nrm(ks[12], (L, D_LRU), 0.02),
        "lru_lambda": lru_lambda,
        "head_norm_conv_g": 1.0 + nrm(ks[15], (L, D_CONV), 0.02),
        "head_norm_lru_g": 1.0 + nrm(ks[16], (L, D_LRU), 0.02),
        "w_out": nrm(ks[17], (L, D_MIX, D_MODEL), D_MIX ** -0.5),
        "norm2_g": 1.0 + nrm(ks[18], (L, D_MODEL), 0.02),
        "route_w_group": nrm(ks[19], (L, D_MODEL, N_GROUPS), D_MODEL ** -0.5),
        "route_b_group": nrm(ks[20], (L, N_GROUPS), 0.01),
        "route_w_expert": nrm(ks[21], (L, N_GROUPS, D_MODEL, EXPERTS_PER_GROUP), D_MODEL ** -0.5),
        "route_b_expert": nrm(ks[22], (L, N_GROUPS, EXPERTS_PER_GROUP), 0.01),
        "w_e_gate": nrm(ks[23], (L, N_EXPERTS, D_MODEL, D_EXPERT), D_MODEL ** -0.5),
        "w_e_up": nrm(ks[24], (L, N_EXPERTS, D_MODEL, D_EXPERT), D_MODEL ** -0.5),
        "w_e_down": nrm(ks[25], (L, N_EXPERTS, D_EXPERT, D_MODEL), D_EXPERT ** -0.5),
        "final_norm_g": 1.0 + nrm(ks[26], (D_MODEL,), 0.02),
    }


def reference(x, c, ada_w, ada_b, norm1_g, w_in, conv3_w, conv4_w, conv4_b, lru_w_a, lru_b_a,
              lru_w_x, lru_b_x, lru_lambda, head_norm_conv_g, head_norm_lru_g, w_out, norm2_g,
              route_w_group, route_b_group, route_w_expert, route_b_expert,
              w_e_gate, w_e_up, w_e_down, final_norm_g):
    bsz, s, d = x.shape
    for l in range(DEPTH):
        mod = jax.nn.silu(c) @ ada_w[l] + ada_b[l]
        sh1, sc1, g1, sh2, sc2, g2 = jnp.split(mod, 6, axis=-1)
        h = modulated_rmsnorm(x, norm1_g[l], sh1, sc1)
        mix = token_mix(h, w_in[l], conv3_w[l], conv4_w[l], conv4_b[l], lru_w_a[l], lru_b_a[l],
                        lru_w_x[l], lru_b_x[l], lru_lambda[l], head_norm_conv_g[l],
                        head_norm_lru_g[l], w_out[l])
        x = x + g1[:, None, :] * mix
        h = modulated_rmsnorm(x, norm2_g[l], sh2, sc2)
        y = hier_moe(h.reshape(bsz * s, d), route_w_group[l], route_b_group[l], route_w_expert[l],
                     route_b_expert[l], w_e_gate[l], w_e_up[l], w_e_down[l]).reshape(bsz, s, d)
        x = x + g2[:, None, :] * y
    return rmsnorm(x, final_norm_g)
```

```python
import functools

import jax
import jax.numpy as jnp
from jax import lax
from jax.experimental import pallas as pl
from jax.experimental.pallas import tpu as pltpu

D_MODEL = 2048
D_CONV = 1024
CONV_HEAD_DIM = 64
D_LRU = 1024
LRU_HEADS = 8
LRU_HEAD_DIM = 128
LRU_C = 8.0
D_MIX = D_CONV + D_LRU
D_IN = 3 * D_CONV + 2 * D_LRU
N_GROUPS = 8
EXPERTS_PER_GROUP = 8
N_EXPERTS = 64
D_EXPERT = 512
EPS = 1e-6

LANES = 128
SUBLANES = 8
ROUTE_W = LANES
MOE_BLOCK = 256
VMEM_LIMIT = 56 * 1024 * 1024

F32 = jnp.float32
BF16 = jnp.bfloat16


def _dot(a, b):
    return jnp.dot(a, b, preferred_element_type=F32)


def _ada_kernel(c_ref, w_ref, b_ref, o_ref):
    c = c_ref[...]
    s = (c * jax.nn.sigmoid(c)).astype(BF16)
    o_ref[...] = _dot(s, w_ref[...].astype(BF16)) + b_ref[...]


def _ada(c, w, b):
    bsz, d = c.shape
    n = w.shape[1]
    tn = 1024
    return pl.pallas_call(
        _ada_kernel,
        grid=(n // tn,),
        in_specs=[
            pl.BlockSpec((bsz, d), lambda j: (0, 0)),
            pl.BlockSpec((d, tn), lambda j: (0, j)),
            pl.BlockSpec((1, tn), lambda j: (0, j)),
        ],
        out_specs=pl.BlockSpec((bsz, tn), lambda j: (0, j)),
        out_shape=jax.ShapeDtypeStruct((bsz, n), F32),
        compiler_params=pltpu.CompilerParams(
            dimension_semantics=("arbitrary",), vmem_limit_bytes=VMEM_LIMIT),
        name="ada",
    )(c, w, b.reshape(1, n))


def _modulated_norm(x, g, sc, sh):
    ms = jnp.mean(x * x, axis=-1, keepdims=True)
    return x * lax.rsqrt(ms + EPS) * (g * (1.0 + sc)) + sh


def _in_proj_kernel(x_ref, sc_ref, sh_ref, g_ref, w_ref, o_ref, *, n_chunk):
    hb = _modulated_norm(x_ref[...], g_ref[...], sc_ref[...], sh_ref[...]).astype(BF16)
    for j in range(D_IN // n_chunk):
        cols = slice(j * n_chunk, (j + 1) * n_chunk)
        o_ref[:, cols] = _dot(hb, w_ref[:, cols])


def _in_proj(x, sc, sh, g, w_bf):
    bsz, s, d = x.shape
    tm = 256
    per_batch = pl.BlockSpec((None, 1, d), lambda b, t: (b, 0, 0))
    return pl.pallas_call(
        functools.partial(_in_proj_kernel, n_chunk=1024),
        grid=(bsz, s // tm),
        in_specs=[
            pl.BlockSpec((None, tm, d), lambda b, t: (b, t, 0)),
            per_batch, per_batch,
            pl.BlockSpec((1, d), lambda b, t: (0, 0)),
            pl.BlockSpec((d, D_IN), lambda b, t: (0, 0), pipeline_mode=pl.Buffered(1)),
        ],
        out_specs=pl.BlockSpec((None, tm, D_IN), lambda b, t: (b, t, 0)),
        out_shape=jax.ShapeDtypeStruct((bsz, s, D_IN), F32),
        compiler_params=pltpu.CompilerParams(
            dimension_semantics=("arbitrary", "arbitrary"), vmem_limit_bytes=VMEM_LIMIT),
        name="in_proj",
    )(x, sc, sh, g, w_bf)


def _shift_rows(x, prev8, d, row):
    rolled = pltpu.roll(x, d, 0)
    head = jnp.where(row[:SUBLANES] >= d, rolled[:SUBLANES], pltpu.roll(prev8, d, 0))
    return jnp.concatenate([head, rolled[SUBLANES:]], axis=0)


def _head_rms(y, p_ref, g):
    sq = y * y
    hi = sq.astype(BF16)
    lo = (sq - hi.astype(F32)).astype(BF16)
    ms = _dot(hi, p_ref[...]) + _dot(lo, p_ref[...])
    return y * lax.rsqrt(ms + EPS) * g


def _gelu_tanh(x):
    return 0.5 * x * (1.0 + jnp.tanh(0.7978845608028654 * (x + 0.044715 * (x * x * x))))


def _mix_kernel(p_ref, w3_ref, w4_ref, b4_ref, wax_ref, ba_ref, bx_ref, lam_ref, ga_ref, gb_ref,
                pa_ref, pb_ref, y_ref, ua_carry, xb_carry, h_carry, *, ts):
    @pl.when(pl.program_id(1) == 0)
    def _():
        ua_carry[...] = jnp.zeros_like(ua_carry)
        xb_carry[...] = jnp.zeros_like(xb_carry)
        h_carry[...] = jnp.zeros_like(h_carry)

    row = lax.broadcasted_iota(jnp.int32, (ts, LANES), 0)

    for k in range(D_CONV // LANES):
        cols = slice(k * LANES, (k + 1) * LANES)
        b_a = p_ref[:, k * LANES:(k + 1) * LANES]
        c_a = p_ref[:, D_CONV + k * LANES:D_CONV + (k + 1) * LANES]
        x_a = p_ref[:, 2 * D_CONV + k * LANES:2 * D_CONV + (k + 1) * LANES]
        u = c_a * x_a
        prev = ua_carry[:, cols]
        conv = (w3_ref[2:3, cols] * u
                + w3_ref[1:2, cols] * _shift_rows(u, prev, 1, row)
                + w3_ref[0:1, cols] * _shift_rows(u, prev, 2, row))
        ua_carry[:, cols] = u[ts - SUBLANES:, :]
        y_a = b_a * conv
        y_ref[:, cols] = _head_rms(y_a, pa_ref, ga_ref[:, cols]).astype(BF16)

        g_b = p_ref[:, 3 * D_CONV + k * LANES:3 * D_CONV + (k + 1) * LANES]
        x_b = p_ref[:, 3 * D_CONV + D_LRU + k * LANES:3 * D_CONV + D_LRU + (k + 1) * LANES]
        prevb = xb_carry[:, cols]
        xc = (w4_ref[3:4, cols] * x_b
              + w4_ref[2:3, cols] * _shift_rows(x_b, prevb, 1, row)
              + w4_ref[1:2, cols] * _shift_rows(x_b, prevb, 2, row)
              + w4_ref[0:1, cols] * _shift_rows(x_b, prevb, 3, row)
              + b4_ref[:, cols])
        xb_carry[:, cols] = x_b[ts - SUBLANES:, :]
        gates = _dot(xc.astype(BF16), wax_ref[k])
        r = jax.nn.sigmoid(gates[:, :LANES] + ba_ref[:, cols])
        i = jax.nn.sigmoid(gates[:, LANES:] + bx_ref[:, cols])
        nlam = -lam_ref[:, cols]
        softplus = jnp.maximum(nlam, 0.0) + jnp.log1p(jnp.exp(-jnp.abs(nlam)))
        log_a = (-LRU_C) * r * softplus
        a = jnp.exp(log_a)
        mult = jnp.sqrt(-jnp.tanh(log_a) * (a * a + 1.0))
        v = mult * i * xc
        d = 1
        while d < ts:
            keep = row >= d
            a_sh = jnp.where(keep, pltpu.roll(a, d, 0), 1.0)
            v_sh = jnp.where(keep, pltpu.roll(v, d, 0), 0.0)
            v = v + a * v_sh
            a = a * a_sh
            d *= 2
        hs = v + a * h_carry[:, cols]
        h_carry[:, cols] = hs[ts - 1:ts, :]
        y_b = hs * _gelu_tanh(g_b)
        y_ref[:, D_CONV + k * LANES:D_CONV + (k + 1) * LANES] = _head_rms(
            y_b, pb_ref, gb_ref[:, cols]).astype(BF16)


def _mix(proj, w3, w4, b4, wax_bf, b_a, b_x, lam, g_a, g_b):
    bsz, s, _ = proj.shape
    ts = 256
    lane = jnp.arange(LANES)
    p_a = ((lane[:, None] // CONV_HEAD_DIM) == (lane[None, :] // CONV_HEAD_DIM)).astype(BF16) / CONV_HEAD_DIM
    p_b = jnp.full((LANES, LANES), 1.0 / LRU_HEAD_DIM, BF16)
    full = lambda shape: pl.BlockSpec(shape, lambda b, t: (0,) * len(shape))
    row = lambda a: a.reshape(1, -1)
    return pl.pallas_call(
        functools.partial(_mix_kernel, ts=ts),
        grid=(bsz, s // ts),
        in_specs=[
            pl.BlockSpec((None, ts, D_IN), lambda b, t: (b, t, 0)),
            full((3, D_CONV)), full((4, D_LRU)), full((1, D_LRU)),
            full((LRU_HEADS, LRU_HEAD_DIM, 2 * LRU_HEAD_DIM)),
            full((1, D_LRU)), full((1, D_LRU)), full((1, D_LRU)),
            full((1, D_CONV)), full((1, D_LRU)),
            full((LANES, LANES)), full((LANES, LANES)),
        ],
        out_specs=pl.BlockSpec((None, ts, D_MIX), lambda b, t: (b, t, 0)),
        out_shape=jax.ShapeDtypeStruct((bsz, s, D_MIX), BF16),
        scratch_shapes=[
            pltpu.VMEM((SUBLANES, D_CONV), F32),
            pltpu.VMEM((SUBLANES, D_LRU), F32),
            pltpu.VMEM((1, D_LRU), F32),
        ],
        compiler_params=pltpu.CompilerParams(
            dimension_semantics=("arbitrary", "arbitrary"), vmem_limit_bytes=VMEM_LIMIT),
        name="mix",
    )(proj, w3, w4, row(b4), wax_bf, row(b_a), row(b_x), row(lam), row(g_a), row(g_b), p_a, p_b)


def _route(logits):
    lane = lax.broadcasted_iota(jnp.int32, logits.shape, 1)
    neg = -jnp.inf
    is_g = lane < N_GROUPS
    gl = jnp.where(is_g, logits, neg)
    ge = jnp.exp(gl - jnp.max(gl, axis=-1, keepdims=True))
    p_g = 1.0 / jnp.sum(ge, axis=-1, keepdims=True)
    g_idx = jnp.min(jnp.where(is_g & (ge >= 1.0), lane, LANES), axis=-1, keepdims=True)
    sel = (lane >= N_GROUPS) & (lane < N_GROUPS + N_EXPERTS) & (((lane - N_GROUPS) >> 3) == g_idx)
    el = jnp.where(sel, logits, neg)
    m1 = jnp.max(el, axis=-1, keepdims=True)
    i1 = jnp.min(jnp.where(sel & (el == m1), lane, LANES), axis=-1, keepdims=True)
    sel2 = sel & (lane != i1)
    el2 = jnp.where(sel2, logits, neg)
    m2 = jnp.max(el2, axis=-1, keepdims=True)
    i2 = jnp.min(jnp.where(sel2 & (el2 == m2), lane, LANES), axis=-1, keepdims=True)
    e2 = jnp.exp(m2 - m1)
    w1 = p_g * (1.0 / (1.0 + e2))
    w2 = p_g * (e2 / (1.0 + e2))
    out = jnp.where(lane == 0, (i1 - N_GROUPS).astype(F32), 0.0)
    out = jnp.where(lane == 1, (i2 - N_GROUPS).astype(F32), out)
    out = jnp.where(lane == 2, w1, out)
    out = jnp.where(lane == 3, w2, out)
    return out


def _out_kernel(y_ref, x_ref, g1_ref, w_ref, sc_ref, sh_ref, g_ref, wr_ref, br_ref,
                x1_ref, h2_ref, rt_ref):
    mix = _dot(y_ref[...], w_ref[...])
    x1 = x_ref[...] + g1_ref[...] * mix
    x1_ref[...] = x1
    h2 = _modulated_norm(x1, g_ref[...], sc_ref[...], sh_ref[...])
    h2_ref[...] = h2
    logits = _dot(h2.astype(BF16), wr_ref[...]) + br_ref[...]
    rt_ref[...] = _route(logits)


def _out(y, x, g1, w_bf, sc2, sh2, g, wr_bf, br):
    bsz, s, d = x.shape
    tm = 256
    per_batch = pl.BlockSpec((None, 1, d), lambda b, t: (b, 0, 0))
    tile = lambda w: pl.BlockSpec((None, tm, w), lambda b, t: (b, t, 0))
    const = lambda shape, **kw: pl.BlockSpec(shape, lambda b, t: (0, 0), **kw)
    return pl.pallas_call(
        _out_kernel,
        grid=(bsz, s // tm),
        in_specs=[
            tile(D_MIX), tile(d), per_batch,
            const((D_MIX, d), pipeline_mode=pl.Buffered(1)),
            per_batch, per_batch, const((1, d)),
            const((d, ROUTE_W)), const((1, ROUTE_W)),
        ],
        out_specs=[tile(d), tile(d), tile(ROUTE_W)],
        out_shape=[
            jax.ShapeDtypeStruct((bsz, s, d), F32),
            jax.ShapeDtypeStruct((bsz, s, d), F32),
            jax.ShapeDtypeStruct((bsz, s, ROUTE_W), F32),
        ],
        compiler_params=pltpu.CompilerParams(
            dimension_semantics=("arbitrary", "arbitrary"), vmem_limit_bytes=VMEM_LIMIT),
        name="out_route",
    )(y, x, g1, w_bf, sc2, sh2, g, wr_bf, br)


def _moe_kernel(be_ref, nvalid_ref, tok_ref, dst_ref, h_hbm, wg_ref, wu_ref, wd_ref, out_hbm,
                wg_b, wu_b, wd_b, xbuf, ybuf, gsem, ssem):
    i = pl.program_id(0)
    n_valid = nvalid_ref[i]

    @pl.when(n_valid > 0)
    def _():
        prev_e = be_ref[jnp.maximum(i - 1, 0)]

        @pl.when((i == 0) | (be_ref[i] != prev_e))
        def _():
            wg_b[...] = wg_ref[...].astype(BF16)
            wu_b[...] = wu_ref[...].astype(BF16)
            wd_b[...] = wd_ref[...].astype(BF16)

        def gather(r, carry):
            pltpu.make_async_copy(h_hbm.at[pl.ds(tok_ref[0, r], 1)], xbuf.at[pl.ds(r, 1)], gsem).start()
            return carry

        lax.fori_loop(0, MOE_BLOCK, gather, 0)
        pltpu.make_async_copy(h_hbm.at[pl.ds(0, MOE_BLOCK)], xbuf, gsem).wait()

        xb = xbuf[...].astype(BF16)
        g = _dot(xb, wg_b[...])
        u = _dot(xb, wu_b[...])
        act = (g * jax.nn.sigmoid(g) * u).astype(BF16)
        ybuf[...] = _dot(act, wd_b[...])

        def scatter(r, carry):
            pltpu.make_async_copy(ybuf.at[pl.ds(r, 1)], out_hbm.at[pl.ds(dst_ref[0, r], 1)], ssem).start()
            return carry

        lax.fori_loop(0, n_valid, scatter, 0)
        n_full = pl.multiple_of((n_valid // SUBLANES) * SUBLANES, SUBLANES)

        @pl.when(n_full > 0)
        def _():
            pltpu.make_async_copy(ybuf.at[pl.ds(0, n_full)], out_hbm.at[pl.ds(0, n_full)], ssem).wait()

        def wait_row(r, carry):
            pltpu.make_async_copy(ybuf.at[pl.ds(0, 1)], out_hbm.at[pl.ds(0, 1)], ssem).wait()
            return carry

        lax.fori_loop(0, n_valid - n_full, wait_row, 0)


def _moe(block_expert, n_valid, row_tok, dst_row, h2, w_gate, w_up, w_down, n_out_rows):
    n_blocks = block_expert.shape[0]
    d = h2.shape[1]
    idx_spec = pl.BlockSpec((None, 1, MOE_BLOCK), lambda i, be, nu: (i, 0, 0), memory_space=pltpu.SMEM)
    w_spec = lambda shape: pl.BlockSpec((None,) + shape, lambda i, be, nu: (be[i], 0, 0))
    grid_spec = pltpu.PrefetchScalarGridSpec(
        num_scalar_prefetch=2,
        grid=(n_blocks,),
        in_specs=[
            idx_spec, idx_spec,
            pl.BlockSpec(memory_space=pl.ANY),
            w_spec((d, D_EXPERT)), w_spec((d, D_EXPERT)), w_spec((D_EXPERT, d)),
        ],
        out_specs=pl.BlockSpec(memory_space=pl.ANY),
        scratch_shapes=[
            pltpu.VMEM((d, D_EXPERT), BF16),
            pltpu.VMEM((d, D_EXPERT), BF16),
            pltpu.VMEM((D_EXPERT, d), BF16),
            pltpu.VMEM((MOE_BLOCK, d), F32),
            pltpu.VMEM((MOE_BLOCK, d), F32),
            pltpu.SemaphoreType.DMA(()),
            pltpu.SemaphoreType.DMA(()),
        ],
    )
    return pl.pallas_call(
        _moe_kernel,
        grid_spec=grid_spec,
        out_shape=jax.ShapeDtypeStruct((n_out_rows, d), F32),
        compiler_params=pltpu.CompilerParams(
            dimension_semantics=("arbitrary",), vmem_limit_bytes=VMEM_LIMIT,
            disable_bounds_checks=True),
        name="moe",
    )(block_expert, n_valid,
      row_tok.reshape(n_blocks, 1, MOE_BLOCK), dst_row.reshape(n_blocks, 1, MOE_BLOCK),
      h2, w_gate, w_up, w_down)


def _dispatch(rt, n_tok):
    n_assign = 2 * n_tok
    eid = rt[:, 0:2].astype(jnp.int32).reshape(-1)
    order = jnp.argsort(eid).astype(jnp.int32)
    s_eid = eid[order]
    counts = jnp.bincount(eid, length=N_EXPERTS).astype(jnp.int32)
    starts = jnp.cumsum(counts) - counts
    padded = ((counts + MOE_BLOCK - 1) // MOE_BLOCK) * MOE_BLOCK
    padded_ends = jnp.cumsum(padded)
    padded_starts = padded_ends - padded
    dest = padded_starts[s_eid] + (jnp.arange(n_assign, dtype=jnp.int32) - starts[s_eid])
    n_rows = n_assign + N_EXPERTS * MOE_BLOCK
    n_blocks = n_rows // MOE_BLOCK
    tok = order >> 1
    slot = order & 1
    row_tok = jnp.zeros((n_rows,), jnp.int32).at[dest].set(tok)
    dst_row = jnp.zeros((n_rows,), jnp.int32).at[dest].set(slot * n_tok + tok)
    block_start = jnp.arange(n_blocks, dtype=jnp.int32) * MOE_BLOCK
    block_expert = jnp.minimum(
        jnp.searchsorted(padded_ends, block_start, side="right"), N_EXPERTS - 1).astype(jnp.int32)
    n_valid = jnp.clip(counts[block_expert] - (block_start - padded_starts[block_expert]), 0, MOE_BLOCK)
    n_valid = jnp.where(block_start < padded_ends[-1], n_valid, 0).astype(jnp.int32)
    return block_expert, n_valid, row_tok, dst_row


def _final_kernel(x1_ref, y0_ref, y1_ref, rt_ref, g2_ref, gf_ref, o_ref):
    rt = rt_ref[...]
    y = rt[:, 2:3] * y0_ref[...] + rt[:, 3:4] * y1_ref[...]
    x2 = x1_ref[...] + g2_ref[...] * y
    ms = jnp.mean(x2 * x2, axis=-1, keepdims=True)
    o_ref[...] = x2 * lax.rsqrt(ms + EPS) * gf_ref[...]


def _final(x1, y2, rt, g2, gf, seq):
    n, d = x1.shape
    tm = 512
    per_seq = seq // tm
    n_t = n // tm
    return pl.pallas_call(
        _final_kernel,
        grid=(n_t,),
        in_specs=[
            pl.BlockSpec((tm, d), lambda t: (t, 0)),
            pl.BlockSpec((tm, d), lambda t: (t, 0)),
            pl.BlockSpec((tm, d), lambda t: (t + n_t, 0)),
            pl.BlockSpec((tm, ROUTE_W), lambda t: (t, 0)),
            pl.BlockSpec((None, 1, d), lambda t: (t // per_seq, 0, 0)),
            pl.BlockSpec((1, d), lambda t: (0, 0)),
        ],
        out_specs=pl.BlockSpec((tm, d), lambda t: (t, 0)),
        out_shape=jax.ShapeDtypeStruct((n, d), F32),
        compiler_params=pltpu.CompilerParams(
            dimension_semantics=("arbitrary",), vmem_limit_bytes=VMEM_LIMIT),
        name="final",
    )(x1, y2, y2, rt, g2, gf)


def kernel(x, c, ada_w, ada_b, norm1_g, w_in, conv3_w, conv4_w, conv4_b, lru_w_a, lru_b_a, lru_w_x, lru_b_x, lru_lambda, head_norm_conv_g, head_norm_lru_g, w_out, norm2_g, route_w_group, route_b_group, route_w_expert, route_b_expert, w_e_gate, w_e_up, w_e_down, final_norm_g):
    bsz, s, d = x.shape
    n_tok = bsz * s
    l = 0

    mod = _ada(c, ada_w[l], ada_b[l]).reshape(bsz, 6, 1, d)
    sh1, sc1, g1, sh2, sc2, g2 = (mod[:, j] for j in range(6))

    proj = _in_proj(x, sc1, sh1, norm1_g[l].reshape(1, d), w_in[l].astype(BF16))

    wax = jnp.concatenate([lru_w_a[l], lru_w_x[l]], axis=-1).astype(BF16)
    y = _mix(proj, conv3_w[l], conv4_w[l], conv4_b[l], wax, lru_b_a[l], lru_b_x[l], lru_lambda[l],
             head_norm_conv_g[l], head_norm_lru_g[l])

    w_route = jnp.concatenate(
        [route_w_group[l], jnp.transpose(route_w_expert[l], (1, 0, 2)).reshape(d, N_EXPERTS),
         jnp.zeros((d, ROUTE_W - N_GROUPS - N_EXPERTS), F32)], axis=1).astype(BF16)
    b_route = jnp.concatenate(
        [route_b_group[l], route_b_expert[l].reshape(-1),
         jnp.zeros((ROUTE_W - N_GROUPS - N_EXPERTS,), F32)]).reshape(1, ROUTE_W)
    x1, h2, rt = _out(y, x, g1, w_out[l].astype(BF16), sc2, sh2, norm2_g[l].reshape(1, d), w_route, b_route)

    rt = rt.reshape(n_tok, ROUTE_W)
    block_expert, n_valid, row_tok, dst_row = _dispatch(rt, n_tok)
    y2 = _moe(block_expert, n_valid, row_tok, dst_row, h2.reshape(n_tok, d),
              w_e_gate[l], w_e_up[l], w_e_down[l], 2 * n_tok)

    out = _final(x1.reshape(n_tok, d), y2, rt, g2, final_norm_g.reshape(1, d), s)
    return out.reshape(bsz, s, d)
```

```python
import functools

import jax
import jax.numpy as jnp
from jax import lax
from jax.experimental import pallas as pl
from jax.experimental.pallas import tpu as pltpu

D_MODEL = 2048
D_CONV = 1024
CONV_HEAD_DIM = 64
D_LRU = 1024
LRU_HEADS = 8
LRU_HEAD_DIM = 128
LRU_C = 8.0
D_MIX = D_CONV + D_LRU
D_IN = 3 * D_CONV + 2 * D_LRU
N_GROUPS = 8
EXPERTS_PER_GROUP = 8
N_EXPERTS = 64
D_EXPERT = 512
EPS = 1e-6

LANES = 128
SUBLANES = 8
ROUTE_W = LANES
MOE_BLOCK = 256
VMEM_LIMIT = 56 * 1024 * 1024

F32 = jnp.float32
BF16 = jnp.bfloat16


def _dot(a, b):
    return jnp.dot(a, b, preferred_element_type=F32)


def _ada_kernel(c_ref, w_ref, b_ref, o_ref):
    c = c_ref[...]
    s = (c * jax.nn.sigmoid(c)).astype(BF16)
    o_ref[...] = _dot(s, w_ref[...].astype(BF16)) + b_ref[...]


def _ada(c, w, b):
    bsz, d = c.shape
    n = w.shape[1]
    tn = 1024
    return pl.pallas_call(
        _ada_kernel,
        grid=(n // tn,),
        in_specs=[
            pl.BlockSpec((bsz, d), lambda j: (0, 0)),
            pl.BlockSpec((d, tn), lambda j: (0, j)),
            pl.BlockSpec((1, tn), lambda j: (0, j)),
        ],
        out_specs=pl.BlockSpec((bsz, tn), lambda j: (0, j)),
        out_shape=jax.ShapeDtypeStruct((bsz, n), F32),
        compiler_params=pltpu.CompilerParams(
            dimension_semantics=("arbitrary",), vmem_limit_bytes=VMEM_LIMIT),
        name="ada",
    )(c, w, b.reshape(1, n))


def _modulated_norm(x, g, sc, sh):
    ms = jnp.mean(x * x, axis=-1, keepdims=True)
    return x * lax.rsqrt(ms + EPS) * (g * (1.0 + sc)) + sh


def _in_proj_kernel(x_ref, sc_ref, sh_ref, g_ref, w_ref, o_ref, *, n_chunk):
    hb = _modulated_norm(x_ref[...], g_ref[...], sc_ref[...], sh_ref[...]).astype(BF16)
    for j in range(D_IN // n_chunk):
        cols = slice(j * n_chunk, (j + 1) * n_chunk)
        o_ref[:, cols] = _dot(hb, w_ref[:, cols])


def _in_proj(x, sc, sh, g, w_bf):
    bsz, s, d = x.shape
    tm = 256
    per_batch = pl.BlockSpec((None, 1, d), lambda b, t: (b, 0, 0))
    return pl.pallas_call(
        functools.partial(_in_proj_kernel, n_chunk=1024),
        grid=(bsz, s // tm),
        in_specs=[
            pl.BlockSpec((None, tm, d), lambda b, t: (b, t, 0)),
            per_batch, per_batch,
            pl.BlockSpec((1, d), lambda b, t: (0, 0)),
            pl.BlockSpec((d, D_IN), lambda b, t: (0, 0), pipeline_mode=pl.Buffered(1)),
        ],
        out_specs=pl.BlockSpec((None, tm, D_IN), lambda b, t: (b, t, 0)),
        out_shape=jax.ShapeDtypeStruct((bsz, s, D_IN), F32),
        compiler_params=pltpu.CompilerParams(
            dimension_semantics=("arbitrary", "arbitrary"), vmem_limit_bytes=VMEM_LIMIT),
        name="in_proj",
    )(x, sc, sh, g, w_bf)


def _shift_rows(x, prev8, d, row):
    rolled = pltpu.roll(x, d, 0)
    head = jnp.where(row[:SUBLANES] >= d, rolled[:SUBLANES], pltpu.roll(prev8, d, 0))
    return jnp.concatenate([head, rolled[SUBLANES:]], axis=0)


def _head_rms(y, p_ref, g):
    sq = y * y
    hi = sq.astype(BF16)
    lo = (sq - hi.astype(F32)).astype(BF16)
    ms = _dot(hi, p_ref[...]) + _dot(lo, p_ref[...])
    return y * lax.rsqrt(ms + EPS) * g


def _gelu_tanh(x):
    return 0.5 * x * (1.0 + jnp.tanh(0.7978845608028654 * (x + 0.044715 * (x * x * x))))


def _mix_kernel(p_ref, w3_ref, w4_ref, b4_ref, wax_ref, ba_ref, bx_ref, lam_ref, ga_ref, gb_ref,
                pa_ref, pb_ref, y_ref, ua_carry, xb_carry, h_carry, *, ts):
    @pl.when(pl.program_id(1) == 0)
    def _():
        ua_carry[...] = jnp.zeros_like(ua_carry)
        xb_carry[...] = jnp.zeros_like(xb_carry)
        h_carry[...] = jnp.zeros_like(h_carry)

    row = lax.broadcasted_iota(jnp.int32, (ts, LANES), 0)

    for k in range(D_CONV // LANES):
        cols = slice(k * LANES, (k + 1) * LANES)
        b_a = p_ref[:, k * LANES:(k + 1) * LANES]
        c_a = p_ref[:, D_CONV + k * LANES:D_CONV + (k + 1) * LANES]
        x_a = p_ref[:, 2 * D_CONV + k * LANES:2 * D_CONV + (k + 1) * LANES]
        u = c_a * x_a
        prev = ua_carry[:, cols]
        conv = (w3_ref[2:3, cols] * u
                + w3_ref[1:2, cols] * _shift_rows(u, prev, 1, row)
                + w3_ref[0:1, cols] * _shift_rows(u, prev, 2, row))
        ua_carry[:, cols] = u[ts - SUBLANES:, :]
        y_a = b_a * conv
        y_ref[:, cols] = _head_rms(y_a, pa_ref, ga_ref[:, cols]).astype(BF16)

        g_b = p_ref[:, 3 * D_CONV + k * LANES:3 * D_CONV + (k + 1) * LANES]
        x_b = p_ref[:, 3 * D_CONV + D_LRU + k * LANES:3 * D_CONV + D_LRU + (k + 1) * LANES]
        prevb = xb_carry[:, cols]
        xc = (w4_ref[3:4, cols] * x_b
              + w4_ref[2:3, cols] * _shift_rows(x_b, prevb, 1, row)
              + w4_ref[1:2, cols] * _shift_rows(x_b, prevb, 2, row)
              + w4_ref[0:1, cols] * _shift_rows(x_b, prevb, 3, row)
              + b4_ref[:, cols])
        xb_carry[:, cols] = x_b[ts - SUBLANES:, :]
        gates = _dot(xc.astype(BF16), wax_ref[k])
        r = jax.nn.sigmoid(gates[:, :LANES] + ba_ref[:, cols])
        i = jax.nn.sigmoid(gates[:, LANES:] + bx_ref[:, cols])
        nlam = -lam_ref[:, cols]
        softplus = jnp.maximum(nlam, 0.0) + jnp.log1p(jnp.exp(-jnp.abs(nlam)))
        log_a = (-LRU_C) * r * softplus
        a = jnp.exp(log_a)
        mult = jnp.sqrt(-jnp.tanh(log_a) * (a * a + 1.0))
        v = mult * i * xc
        d = 1
        while d < ts:
            keep = row >= d
            a_sh = jnp.where(keep, pltpu.roll(a, d, 0), 1.0)
            v_sh = jnp.where(keep, pltpu.roll(v, d, 0), 0.0)
            v = v + a * v_sh
            a = a * a_sh
            d *= 2
        hs = v + a * h_carry[:, cols]
        h_carry[:, cols] = hs[ts - 1:ts, :]
        y_b = hs * _gelu_tanh(g_b)
        y_ref[:, D_CONV + k * LANES:D_CONV + (k + 1) * LANES] = _head_rms(
            y_b, pb_ref, gb_ref[:, cols]).astype(BF16)


def _mix(proj, w3, w4, b4, wax_bf, b_a, b_x, lam, g_a, g_b):
    bsz, s, _ = proj.shape
    ts = 256
    lane = jnp.arange(LANES)
    p_a = ((lane[:, None] // CONV_HEAD_DIM) == (lane[None, :] // CONV_HEAD_DIM)).astype(BF16) / CONV_HEAD_DIM
    p_b = jnp.full((LANES, LANES), 1.0 / LRU_HEAD_DIM, BF16)
    full = lambda shape: pl.BlockSpec(shape, lambda b, t: (0,) * len(shape))
    row = lambda a: a.reshape(1, -1)
    return pl.pallas_call(
        functools.partial(_mix_kernel, ts=ts),
        grid=(bsz, s // ts),
        in_specs=[
            pl.BlockSpec((None, ts, D_IN), lambda b, t: (b, t, 0)),
            full((3, D_CONV)), full((4, D_LRU)), full((1, D_LRU)),
            full((LRU_HEADS, LRU_HEAD_DIM, 2 * LRU_HEAD_DIM)),
            full((1, D_LRU)), full((1, D_LRU)), full((1, D_LRU)),
            full((1, D_CONV)), full((1, D_LRU)),
            full((LANES, LANES)), full((LANES, LANES)),
        ],
        out_specs=pl.BlockSpec((None, ts, D_MIX), lambda b, t: (b, t, 0)),
        out_shape=jax.ShapeDtypeStruct((bsz, s, D_MIX), BF16),
        scratch_shapes=[
            pltpu.VMEM((SUBLANES, D_CONV), F32),
            pltpu.VMEM((SUBLANES, D_LRU), F32),
            pltpu.VMEM((1, D_LRU), F32),
        ],
        compiler_params=pltpu.CompilerParams(
            dimension_semantics=("arbitrary", "arbitrary"), vmem_limit_bytes=VMEM_LIMIT),
        name="mix",
    )(proj, w3, w4, row(b4), wax_bf, row(b_a), row(b_x), row(lam), row(g_a), row(g_b), p_a, p_b)


def _route(logits, tri, carry):
    lane = lax.broadcasted_iota(jnp.int32, logits.shape, 1)
    neg = -jnp.inf
    is_g = lane < N_GROUPS
    gl = jnp.where(is_g, logits, neg)
    ge = jnp.exp(gl - jnp.max(gl, axis=-1, keepdims=True))
    p_g = 1.0 / jnp.sum(ge, axis=-1, keepdims=True)
    g_idx = jnp.min(jnp.where(is_g & (ge >= 1.0), lane, LANES), axis=-1, keepdims=True)
    sel = (lane >= N_GROUPS) & (lane < N_GROUPS + N_EXPERTS) & (((lane - N_GROUPS) >> 3) == g_idx)
    el = jnp.where(sel, logits, neg)
    m1 = jnp.max(el, axis=-1, keepdims=True)
    i1 = jnp.min(jnp.where(sel & (el == m1), lane, LANES), axis=-1, keepdims=True)
    sel2 = sel & (lane != i1)
    el2 = jnp.where(sel2, logits, neg)
    m2 = jnp.max(el2, axis=-1, keepdims=True)
    i2 = jnp.min(jnp.where(sel2 & (el2 == m2), lane, LANES), axis=-1, keepdims=True)
    e2 = jnp.exp(m2 - m1)
    w1 = p_g * (1.0 / (1.0 + e2))
    w2 = p_g * (e2 / (1.0 + e2))
    out = jnp.where(lane == 0, (i1 - N_GROUPS).astype(F32), 0.0)
    out = jnp.where(lane == 1, (i2 - N_GROUPS).astype(F32), out)
    out = jnp.where(lane == 2, w1, out)
    out = jnp.where(lane == 3, w2, out)
    oh1 = lane == i1
    oh2 = lane == i2
    both = jnp.where(oh1 | oh2, 1.0, 0.0)
    before = _dot(tri, both.astype(BF16)) + carry
    out = jnp.where(lane == 4, jnp.sum(jnp.where(oh1, before, 0.0), axis=-1, keepdims=True), out)
    out = jnp.where(lane == 5, jnp.sum(jnp.where(oh2, before, 0.0), axis=-1, keepdims=True), out)
    return out, carry + jnp.sum(both, axis=0, keepdims=True)


def _pack_bf16_pair(lo, hi):
    lo_bits = lax.bitcast_convert_type(lo.astype(BF16).astype(F32), jnp.uint32)
    hi_bits = lax.bitcast_convert_type(hi.astype(BF16).astype(F32), jnp.uint32)
    return (lo_bits >> 16) | hi_bits


def _unpack_bf16_pair(w):
    lo = lax.bitcast_convert_type(w << 16, F32).astype(BF16)
    hi = lax.bitcast_convert_type(w & jnp.uint32(0xFFFF0000), F32).astype(BF16)
    return lo, hi


def _out_kernel(y_ref, x_ref, g1_ref, w_ref, sc_ref, sh_ref, g_ref, wr_ref, br_ref, tri_ref,
                x1_ref, h2p_ref, rt_ref, cnt_ref, carry):
    @pl.when((pl.program_id(0) == 0) & (pl.program_id(1) == 0))
    def _():
        carry[...] = jnp.zeros_like(carry)

    mix = _dot(y_ref[...], w_ref[...])
    x1 = x_ref[...] + g1_ref[...] * mix
    x1_ref[...] = x1
    h2 = _modulated_norm(x1, g_ref[...], sc_ref[...], sh_ref[...])
    half = h2.shape[1] // 2
    h2p_ref[...] = _pack_bf16_pair(h2[:, :half], h2[:, half:])
    logits = _dot(h2.astype(BF16), wr_ref[...]) + br_ref[...]
    rt, new_carry = _route(logits, tri_ref[...], carry[...])
    rt_ref[...] = rt
    carry[...] = new_carry
    cnt_ref[...] = new_carry


def _out(y, x, g1, w_bf, sc2, sh2, g, wr_bf, br):
    bsz, s, d = x.shape
    tm = 256
    per_batch = pl.BlockSpec((None, 1, d), lambda b, t: (b, 0, 0))
    tile = lambda w: pl.BlockSpec((None, tm, w), lambda b, t: (b, t, 0))
    const = lambda shape, **kw: pl.BlockSpec(shape, lambda b, t: (0, 0), **kw)
    return pl.pallas_call(
        _out_kernel,
        grid=(bsz, s // tm),
        in_specs=[
            tile(D_MIX), tile(d), per_batch,
            const((D_MIX, d), pipeline_mode=pl.Buffered(1)),
            per_batch, per_batch, const((1, d)),
            const((d, ROUTE_W)), const((1, ROUTE_W)), const((tm, tm)),
        ],
        out_specs=[tile(d), tile(d // 2), tile(ROUTE_W), const((1, ROUTE_W))],
        out_shape=[
            jax.ShapeDtypeStruct((bsz, s, d), F32),
            jax.ShapeDtypeStruct((bsz, s, d // 2), jnp.uint32),
            jax.ShapeDtypeStruct((bsz, s, ROUTE_W), F32),
            jax.ShapeDtypeStruct((1, ROUTE_W), F32),
        ],
        scratch_shapes=[pltpu.VMEM((1, ROUTE_W), F32)],
        compiler_params=pltpu.CompilerParams(
            dimension_semantics=("arbitrary", "arbitrary"), vmem_limit_bytes=VMEM_LIMIT),
        name="out_route",
    )(y, x, g1, w_bf, sc2, sh2, g, wr_bf, br, jnp.tril(jnp.ones((tm, tm), BF16), -1))


def _plan(rt, counts_f, n_tok):
    counts = counts_f[0, N_GROUPS:N_GROUPS + N_EXPERTS].astype(jnp.int32)
    padded = ((counts + MOE_BLOCK - 1) // MOE_BLOCK) * MOE_BLOCK
    padded_ends = jnp.cumsum(padded)
    padded_starts = padded_ends - padded
    n_rows = 2 * n_tok + N_EXPERTS * MOE_BLOCK
    n_blocks = n_rows // MOE_BLOCK
    idx = rt[:, 0:6].astype(jnp.int32)
    pos = padded_starts[idx[:, 0:2]] + idx[:, 4:6]
    block_start = jnp.arange(n_blocks, dtype=jnp.int32) * MOE_BLOCK
    block_expert = jnp.minimum(
        jnp.sum(padded_ends[None, :] <= block_start[:, None], axis=1), N_EXPERTS - 1).astype(jnp.int32)
    n_valid = jnp.clip(counts[block_expert] - (block_start - padded_starts[block_expert]), 0, MOE_BLOCK)
    n_valid = jnp.where(block_start < padded_ends[-1], n_valid, 0).astype(jnp.int32)
    n_used = padded_ends[-1] // MOE_BLOCK
    block_index = jnp.minimum(jnp.arange(n_blocks, dtype=jnp.int32), n_used - 1)
    return pos[:, 0], pos[:, 1], block_expert, n_valid, block_index, n_rows


def _dispatch_kernel(p0_ref, p1_ref, h_ref, zeros_hbm, xs_hbm, sem, *, tm):
    del zeros_hbm

    def issue(r, carry):
        src = h_ref.at[pl.ds(r, 1)]
        pltpu.make_async_copy(src, xs_hbm.at[pl.ds(p0_ref[0, r], 1)], sem).start()
        pltpu.make_async_copy(src, xs_hbm.at[pl.ds(p1_ref[0, r], 1)], sem).start()
        return carry

    lax.fori_loop(0, tm, issue, 0)
    for _ in range(2):
        pltpu.make_async_copy(h_ref, xs_hbm.at[pl.ds(0, tm)], sem).wait()


def _dispatch(pos0, pos1, h2p, n_rows):
    n, w = h2p.shape
    tm = 256
    idx_spec = pl.BlockSpec((None, 1, tm), lambda t: (t, 0, 0), memory_space=pltpu.SMEM)
    return pl.pallas_call(
        functools.partial(_dispatch_kernel, tm=tm),
        grid=(n // tm,),
        in_specs=[idx_spec, idx_spec, pl.BlockSpec((tm, w), lambda t: (t, 0)),
                  pl.BlockSpec(memory_space=pl.ANY)],
        out_specs=pl.BlockSpec(memory_space=pl.ANY),
        out_shape=jax.ShapeDtypeStruct((n_rows, w), h2p.dtype),
        input_output_aliases={3: 0},
        scratch_shapes=[pltpu.SemaphoreType.DMA(())],
        compiler_params=pltpu.CompilerParams(
            dimension_semantics=("arbitrary",), vmem_limit_bytes=VMEM_LIMIT,
            disable_bounds_checks=True),
        name="dispatch",
    )(pos0.reshape(n // tm, 1, tm), pos1.reshape(n // tm, 1, tm), h2p, jnp.zeros((n_rows, w), h2p.dtype))


def _moe_kernel(be_ref, nvalid_ref, bi_ref, x_ref, wg_ref, wu_ref, wd_ref, y_ref, wg_b, wu_b, wd_b):
    i = pl.program_id(0)
    n_valid = nvalid_ref[i]

    @pl.when(n_valid > 0)
    def _():
        prev_e = be_ref[jnp.maximum(i - 1, 0)]

        @pl.when((i == 0) | (be_ref[i] != prev_e))
        def _():
            wg_b[...] = wg_ref[...].astype(BF16)
            wu_b[...] = wu_ref[...].astype(BF16)
            wd_b[...] = wd_ref[...].astype(BF16)

        lo, hi = _unpack_bf16_pair(x_ref[...])
        half = x_ref.shape[1]
        g = _dot(lo, wg_b[:half, :]) + _dot(hi, wg_b[half:, :])
        u = _dot(lo, wu_b[:half, :]) + _dot(hi, wu_b[half:, :])
        act = (g * jax.nn.sigmoid(g) * u).astype(BF16)
        y_ref[...] = _dot(act, wd_b[...])

    @pl.when(n_valid == 0)
    def _():
        y_ref[...] = jnp.zeros_like(y_ref)


def _moe(block_expert, n_valid, block_index, xs, w_gate, w_up, w_down):
    n_blocks = block_expert.shape[0]
    n_rows, half = xs.shape
    d = 2 * half
    w_spec = lambda shape: pl.BlockSpec((None,) + shape, lambda i, be, nv, bi: (be[i], 0, 0))
    grid_spec = pltpu.PrefetchScalarGridSpec(
        num_scalar_prefetch=3,
        grid=(n_blocks,),
        in_specs=[
            pl.BlockSpec((MOE_BLOCK, half), lambda i, be, nv, bi: (bi[i], 0)),
            w_spec((d, D_EXPERT)), w_spec((d, D_EXPERT)), w_spec((D_EXPERT, d)),
        ],
        out_specs=pl.BlockSpec((MOE_BLOCK, d), lambda i, be, nv, bi: (i, 0)),
        scratch_shapes=[
            pltpu.VMEM((d, D_EXPERT), BF16),
            pltpu.VMEM((d, D_EXPERT), BF16),
            pltpu.VMEM((D_EXPERT, d), BF16),
        ],
    )
    return pl.pallas_call(
        _moe_kernel,
        grid_spec=grid_spec,
        out_shape=jax.ShapeDtypeStruct((n_rows, d), F32),
        compiler_params=pltpu.CompilerParams(
            dimension_semantics=("arbitrary",), vmem_limit_bytes=VMEM_LIMIT),
        name="moe",
    )(block_expert, n_valid, block_index, xs, w_gate, w_up, w_down)


def _final_kernel(p0_ref, p1_ref, x1_ref, rt_ref, g2_ref, gf_ref, yb_hbm, o_ref, y0buf, y1buf, sem, *, tm):
    def issue(r, carry):
        pltpu.make_async_copy(yb_hbm.at[pl.ds(p0_ref[0, r], 1)], y0buf.at[pl.ds(r, 1)], sem).start()
        pltpu.make_async_copy(yb_hbm.at[pl.ds(p1_ref[0, r], 1)], y1buf.at[pl.ds(r, 1)], sem).start()
        return carry

    lax.fori_loop(0, tm, issue, 0)
    pltpu.make_async_copy(yb_hbm.at[pl.ds(0, tm)], y0buf, sem).wait()
    pltpu.make_async_copy(yb_hbm.at[pl.ds(0, tm)], y1buf, sem).wait()

    rt = rt_ref[...]
    y = rt[:, 2:3] * y0buf[...] + rt[:, 3:4] * y1buf[...]
    x2 = x1_ref[...] + g2_ref[...] * y
    ms = jnp.mean(x2 * x2, axis=-1, keepdims=True)
    o_ref[...] = x2 * lax.rsqrt(ms + EPS) * gf_ref[...]


def _final(pos0, pos1, x1, rt, g2, gf, yb, seq):
    n, d = x1.shape
    tm = 256
    per_seq = seq // tm
    idx_spec = pl.BlockSpec((None, 1, tm), lambda t: (t, 0, 0), memory_space=pltpu.SMEM)
    return pl.pallas_call(
        functools.partial(_final_kernel, tm=tm),
        grid=(n // tm,),
        in_specs=[
            idx_spec, idx_spec,
            pl.BlockSpec((tm, d), lambda t: (t, 0)),
            pl.BlockSpec((tm, ROUTE_W), lambda t: (t, 0)),
            pl.BlockSpec((None, 1, d), lambda t: (t // per_seq, 0, 0)),
            pl.BlockSpec((1, d), lambda t: (0, 0)),
            pl.BlockSpec(memory_space=pl.ANY),
        ],
        out_specs=pl.BlockSpec((tm, d), lambda t: (t, 0)),
        out_shape=jax.ShapeDtypeStruct((n, d), F32),
        scratch_shapes=[
            pltpu.VMEM((tm, d), F32),
            pltpu.VMEM((tm, d), F32),
            pltpu.SemaphoreType.DMA(()),
        ],
        compiler_params=pltpu.CompilerParams(
            dimension_semantics=("arbitrary",), vmem_limit_bytes=VMEM_LIMIT,
            disable_bounds_checks=True),
        name="final",
    )(pos0.reshape(n // tm, 1, tm), pos1.reshape(n // tm, 1, tm), x1, rt, g2, gf, yb)


def kernel(x, c, ada_w, ada_b, norm1_g, w_in, conv3_w, conv4_w, conv4_b, lru_w_a, lru_b_a, lru_w_x, lru_b_x, lru_lambda, head_norm_conv_g, head_norm_lru_g, w_out, norm2_g, route_w_group, route_b_group, route_w_expert, route_b_expert, w_e_gate, w_e_up, w_e_down, final_norm_g):
    bsz, s, d = x.shape
    n_tok = bsz * s
    l = 0

    mod = _ada(c, ada_w[l], ada_b[l]).reshape(bsz, 6, 1, d)
    sh1, sc1, g1, sh2, sc2, g2 = (mod[:, j] for j in range(6))

    proj = _in_proj(x, sc1, sh1, norm1_g[l].reshape(1, d), w_in[l].astype(BF16))

    wax = jnp.concatenate([lru_w_a[l], lru_w_x[l]], axis=-1).astype(BF16)
    y = _mix(proj, conv3_w[l], conv4_w[l], conv4_b[l], wax, lru_b_a[l], lru_b_x[l], lru_lambda[l],
             head_norm_conv_g[l], head_norm_lru_g[l])

    w_route = jnp.concatenate(
        [route_w_group[l], jnp.transpose(route_w_expert[l], (1, 0, 2)).reshape(d, N_EXPERTS),
         jnp.zeros((d, ROUTE_W - N_GROUPS - N_EXPERTS), F32)], axis=1).astype(BF16)
    b_route = jnp.concatenate(
        [route_b_group[l], route_b_expert[l].reshape(-1),
         jnp.zeros((ROUTE_W - N_GROUPS - N_EXPERTS,), F32)]).reshape(1, ROUTE_W)
    x1, h2p, rt, counts = _out(y, x, g1, w_out[l].astype(BF16), sc2, sh2, norm2_g[l].reshape(1, d),
                               w_route, b_route)

    rt = rt.reshape(n_tok, ROUTE_W)
    pos0, pos1, block_expert, n_valid, block_index, n_rows = _plan(rt, counts, n_tok)
    xs = _dispatch(pos0, pos1, h2p.reshape(n_tok, d // 2), n_rows)
    yb = _moe(block_expert, n_valid, block_index, xs, w_e_gate[l], w_e_up[l], w_e_down[l])
    out = _final(pos0, pos1, x1.reshape(n_tok, d), rt, g2, final_norm_g.reshape(1, d), yb, s)
    return out.reshape(bsz, s, d)
```

```python
import functools

import jax
import jax.numpy as jnp
from jax import lax
from jax.experimental import pallas as pl
from jax.experimental.pallas import tpu as pltpu

D_MODEL = 2048
D_CONV = 1024
CONV_HEAD_DIM = 64
D_LRU = 1024
LRU_HEADS = 8
LRU_HEAD_DIM = 128
LRU_C = 8.0
D_MIX = D_CONV + D_LRU
D_IN = 3 * D_CONV + 2 * D_LRU
N_GROUPS = 8
EXPERTS_PER_GROUP = 8
N_EXPERTS = 64
D_EXPERT = 512
EPS = 1e-6

LANES = 128
SUBLANES = 8
ROUTE_W = LANES
MOE_BLOCK = 256
DMA_ISSUE_UNROLL = 8
VMEM_LIMIT = 56 * 1024 * 1024

F32 = jnp.float32
BF16 = jnp.bfloat16


def _dot(a, b):
    return jnp.dot(a, b, preferred_element_type=F32)


def _ada_kernel(c_ref, w_ref, b_ref, o_ref):
    c = c_ref[...]
    s = (c * jax.nn.sigmoid(c)).astype(BF16)
    o_ref[...] = _dot(s, w_ref[...].astype(BF16)) + b_ref[...]


def _ada(c, w, b):
    bsz, d = c.shape
    n = w.shape[1]
    tn = 1024
    return pl.pallas_call(
        _ada_kernel,
        grid=(n // tn,),
        in_specs=[
            pl.BlockSpec((bsz, d), lambda j: (0, 0)),
            pl.BlockSpec((d, tn), lambda j: (0, j)),
            pl.BlockSpec((1, tn), lambda j: (0, j)),
        ],
        out_specs=pl.BlockSpec((bsz, tn), lambda j: (0, j)),
        out_shape=jax.ShapeDtypeStruct((bsz, n), F32),
        compiler_params=pltpu.CompilerParams(
            dimension_semantics=("arbitrary",), vmem_limit_bytes=VMEM_LIMIT),
        name="ada",
    )(c, w, b.reshape(1, n))


def _modulated_norm(x, g, sc, sh):
    ms = jnp.mean(x * x, axis=-1, keepdims=True)
    return x * lax.rsqrt(ms + EPS) * (g * (1.0 + sc)) + sh


def _in_proj_kernel(x_ref, sc_ref, sh_ref, g_ref, w_ref, o_ref, *, n_chunk):
    hb = _modulated_norm(x_ref[...], g_ref[...], sc_ref[...], sh_ref[...]).astype(BF16)
    for j in range(D_IN // n_chunk):
        cols = slice(j * n_chunk, (j + 1) * n_chunk)
        o_ref[:, cols] = _dot(hb, w_ref[:, cols])


def _in_proj(x, sc, sh, g, w_bf):
    bsz, s, d = x.shape
    tm = 256
    per_batch = pl.BlockSpec((None, 1, d), lambda b, t: (b, 0, 0))
    return pl.pallas_call(
        functools.partial(_in_proj_kernel, n_chunk=1024),
        grid=(bsz, s // tm),
        in_specs=[
            pl.BlockSpec((None, tm, d), lambda b, t: (b, t, 0)),
            per_batch, per_batch,
            pl.BlockSpec((1, d), lambda b, t: (0, 0)),
            pl.BlockSpec((d, D_IN), lambda b, t: (0, 0), pipeline_mode=pl.Buffered(1)),
        ],
        out_specs=pl.BlockSpec((None, tm, D_IN), lambda b, t: (b, t, 0)),
        out_shape=jax.ShapeDtypeStruct((bsz, s, D_IN), F32),
        compiler_params=pltpu.CompilerParams(
            dimension_semantics=("arbitrary", "arbitrary"), vmem_limit_bytes=VMEM_LIMIT),
        name="in_proj",
    )(x, sc, sh, g, w_bf)


def _shift_rows(x, prev8, d, row):
    rolled = pltpu.roll(x, d, 0)
    head = jnp.where(row[:SUBLANES] >= d, rolled[:SUBLANES], pltpu.roll(prev8, d, 0))
    return jnp.concatenate([head, rolled[SUBLANES:]], axis=0)


def _head_rms(y, p_ref, g):
    sq = y * y
    hi = sq.astype(BF16)
    lo = (sq - hi.astype(F32)).astype(BF16)
    ms = _dot(hi, p_ref[...]) + _dot(lo, p_ref[...])
    return y * lax.rsqrt(ms + EPS) * g


def _gelu_tanh(x):
    return 0.5 * x * (1.0 + jnp.tanh(0.7978845608028654 * (x + 0.044715 * (x * x * x))))


def _mix_kernel(p_ref, w3_ref, w4_ref, b4_ref, wax_ref, ba_ref, bx_ref, lam_ref, ga_ref, gb_ref,
                pa_ref, pb_ref, y_ref, ua_carry, xb_carry, h_carry, *, ts):
    @pl.when(pl.program_id(1) == 0)
    def _():
        ua_carry[...] = jnp.zeros_like(ua_carry)
        xb_carry[...] = jnp.zeros_like(xb_carry)
        h_carry[...] = jnp.zeros_like(h_carry)

    row = lax.broadcasted_iota(jnp.int32, (ts, LANES), 0)

    for k in range(D_CONV // LANES):
        cols = slice(k * LANES, (k + 1) * LANES)
        b_a = p_ref[:, k * LANES:(k + 1) * LANES]
        c_a = p_ref[:, D_CONV + k * LANES:D_CONV + (k + 1) * LANES]
        x_a = p_ref[:, 2 * D_CONV + k * LANES:2 * D_CONV + (k + 1) * LANES]
        u = c_a * x_a
        prev = ua_carry[:, cols]
        conv = (w3_ref[2:3, cols] * u
                + w3_ref[1:2, cols] * _shift_rows(u, prev, 1, row)
                + w3_ref[0:1, cols] * _shift_rows(u, prev, 2, row))
        ua_carry[:, cols] = u[ts - SUBLANES:, :]
        y_a = b_a * conv
        y_ref[:, cols] = _head_rms(y_a, pa_ref, ga_ref[:, cols]).astype(BF16)

        g_b = p_ref[:, 3 * D_CONV + k * LANES:3 * D_CONV + (k + 1) * LANES]
        x_b = p_ref[:, 3 * D_CONV + D_LRU + k * LANES:3 * D_CONV + D_LRU + (k + 1) * LANES]
        prevb = xb_carry[:, cols]
        xc = (w4_ref[3:4, cols] * x_b
              + w4_ref[2:3, cols] * _shift_rows(x_b, prevb, 1, row)
              + w4_ref[1:2, cols] * _shift_rows(x_b, prevb, 2, row)
              + w4_ref[0:1, cols] * _shift_rows(x_b, prevb, 3, row)
              + b4_ref[:, cols])
        xb_carry[:, cols] = x_b[ts - SUBLANES:, :]
        gates = _dot(xc.astype(BF16), wax_ref[k])
        r = jax.nn.sigmoid(gates[:, :LANES] + ba_ref[:, cols])
        i = jax.nn.sigmoid(gates[:, LANES:] + bx_ref[:, cols])
        nlam = -lam_ref[:, cols]
        softplus = jnp.maximum(nlam, 0.0) + jnp.log1p(jnp.exp(-jnp.abs(nlam)))
        log_a = (-LRU_C) * r * softplus
        a = jnp.exp(log_a)
        mult = jnp.sqrt(-jnp.tanh(log_a) * (a * a + 1.0))
        v = mult * i * xc
        d = 1
        while d < ts:
            keep = row >= d
            a_sh = jnp.where(keep, pltpu.roll(a, d, 0), 1.0)
            v_sh = jnp.where(keep, pltpu.roll(v, d, 0), 0.0)
            v = v + a * v_sh
            a = a * a_sh
            d *= 2
        hs = v + a * h_carry[:, cols]
        h_carry[:, cols] = hs[ts - 1:ts, :]
        y_b = hs * _gelu_tanh(g_b)
        y_ref[:, D_CONV + k * LANES:D_CONV + (k + 1) * LANES] = _head_rms(
            y_b, pb_ref, gb_ref[:, cols]).astype(BF16)


def _mix(proj, w3, w4, b4, wax_bf, b_a, b_x, lam, g_a, g_b):
    bsz, s, _ = proj.shape
    ts = 256
    lane = jnp.arange(LANES)
    p_a = ((lane[:, None] // CONV_HEAD_DIM) == (lane[None, :] // CONV_HEAD_DIM)).astype(BF16) / CONV_HEAD_DIM
    p_b = jnp.full((LANES, LANES), 1.0 / LRU_HEAD_DIM, BF16)
    full = lambda shape: pl.BlockSpec(shape, lambda b, t: (0,) * len(shape))
    row = lambda a: a.reshape(1, -1)
    return pl.pallas_call(
        functools.partial(_mix_kernel, ts=ts),
        grid=(bsz, s // ts),
        in_specs=[
            pl.BlockSpec((None, ts, D_IN), lambda b, t: (b, t, 0)),
            full((3, D_CONV)), full((4, D_LRU)), full((1, D_LRU)),
            full((LRU_HEADS, LRU_HEAD_DIM, 2 * LRU_HEAD_DIM)),
            full((1, D_LRU)), full((1, D_LRU)), full((1, D_LRU)),
            full((1, D_CONV)), full((1, D_LRU)),
            full((LANES, LANES)), full((LANES, LANES)),
        ],
        out_specs=pl.BlockSpec((None, ts, D_MIX), lambda b, t: (b, t, 0)),
        out_shape=jax.ShapeDtypeStruct((bsz, s, D_MIX), BF16),
        scratch_shapes=[
            pltpu.VMEM((SUBLANES, D_CONV), F32),
            pltpu.VMEM((SUBLANES, D_LRU), F32),
            pltpu.VMEM((1, D_LRU), F32),
        ],
        compiler_params=pltpu.CompilerParams(
            dimension_semantics=("arbitrary", "arbitrary"), vmem_limit_bytes=VMEM_LIMIT),
        name="mix",
    )(proj, w3, w4, row(b4), wax_bf, row(b_a), row(b_x), row(lam), row(g_a), row(g_b), p_a, p_b)


def _route(logits, tri, carry):
    lane = lax.broadcasted_iota(jnp.int32, logits.shape, 1)
    neg = -jnp.inf
    is_g = lane < N_GROUPS
    gl = jnp.where(is_g, logits, neg)
    ge = jnp.exp(gl - jnp.max(gl, axis=-1, keepdims=True))
    p_g = 1.0 / jnp.sum(ge, axis=-1, keepdims=True)
    g_idx = jnp.min(jnp.where(is_g & (ge >= 1.0), lane, LANES), axis=-1, keepdims=True)
    sel = (lane >= N_GROUPS) & (lane < N_GROUPS + N_EXPERTS) & (((lane - N_GROUPS) >> 3) == g_idx)
    el = jnp.where(sel, logits, neg)
    m1 = jnp.max(el, axis=-1, keepdims=True)
    i1 = jnp.min(jnp.where(sel & (el == m1), lane, LANES), axis=-1, keepdims=True)
    sel2 = sel & (lane != i1)
    el2 = jnp.where(sel2, logits, neg)
    m2 = jnp.max(el2, axis=-1, keepdims=True)
    i2 = jnp.min(jnp.where(sel2 & (el2 == m2), lane, LANES), axis=-1, keepdims=True)
    e2 = jnp.exp(m2 - m1)
    w1 = p_g * (1.0 / (1.0 + e2))
    w2 = p_g * (e2 / (1.0 + e2))
    out = jnp.where(lane == 0, (i1 - N_GROUPS).astype(F32), 0.0)
    out = jnp.where(lane == 1, (i2 - N_GROUPS).astype(F32), out)
    out = jnp.where(lane == 2, w1, out)
    out = jnp.where(lane == 3, w2, out)
    oh1 = lane == i1
    oh2 = lane == i2
    both = jnp.where(oh1 | oh2, 1.0, 0.0)
    before = _dot(tri, both.astype(BF16)) + carry
    out = jnp.where(lane == 4, jnp.sum(jnp.where(oh1, before, 0.0), axis=-1, keepdims=True), out)
    out = jnp.where(lane == 5, jnp.sum(jnp.where(oh2, before, 0.0), axis=-1, keepdims=True), out)
    return out, carry + jnp.sum(both, axis=0, keepdims=True)


def _pack_bf16_pair(lo, hi):
    lo_bits = lax.bitcast_convert_type(lo.astype(BF16).astype(F32), jnp.uint32)
    hi_bits = lax.bitcast_convert_type(hi.astype(BF16).astype(F32), jnp.uint32)
    return (lo_bits >> 16) | hi_bits


def _unpack_bf16_pair(w):
    lo = lax.bitcast_convert_type(w << 16, F32).astype(BF16)
    hi = lax.bitcast_convert_type(w & jnp.uint32(0xFFFF0000), F32).astype(BF16)
    return lo, hi


def _out_kernel(y_ref, x_ref, g1_ref, w_ref, sc_ref, sh_ref, g_ref, wr_ref, br_ref, tri_ref,
                x1_ref, h2p_ref, rt_ref, cnt_ref, carry):
    @pl.when((pl.program_id(0) == 0) & (pl.program_id(1) == 0))
    def _():
        carry[...] = jnp.zeros_like(carry)

    sub = tri_ref.shape[0]
    count = carry[...]
    for r0 in range(0, y_ref.shape[0], sub):
        rows = slice(r0, r0 + sub)
        mix = _dot(y_ref[rows, :], w_ref[...])
        x1 = x_ref[rows, :] + g1_ref[...] * mix
        x1_ref[rows, :] = x1
        h2 = _modulated_norm(x1, g_ref[...], sc_ref[...], sh_ref[...])
        half = h2.shape[1] // 2
        h2p_ref[rows, :] = _pack_bf16_pair(h2[:, :half], h2[:, half:])
        logits = _dot(h2.astype(BF16), wr_ref[...]) + br_ref[...]
        rt, count = _route(logits, tri_ref[...], count)
        rt_ref[rows, :] = rt
    carry[...] = count
    cnt_ref[...] = count


def _out(y, x, g1, w_bf, sc2, sh2, g, wr_bf, br):
    bsz, s, d = x.shape
    tm, sub = 512, 256
    per_batch = pl.BlockSpec((None, 1, d), lambda b, t: (b, 0, 0))
    tile = lambda w: pl.BlockSpec((None, tm, w), lambda b, t: (b, t, 0))
    const = lambda shape, **kw: pl.BlockSpec(shape, lambda b, t: (0, 0), **kw)
    return pl.pallas_call(
        _out_kernel,
        grid=(bsz, s // tm),
        in_specs=[
            tile(D_MIX), tile(d), per_batch,
            const((D_MIX, d), pipeline_mode=pl.Buffered(1)),
            per_batch, per_batch, const((1, d)),
            const((d, ROUTE_W)), const((1, ROUTE_W)), const((sub, sub)),
        ],
        out_specs=[tile(d), tile(d // 2), tile(ROUTE_W), const((1, ROUTE_W))],
        out_shape=[
            jax.ShapeDtypeStruct((bsz, s, d), F32),
            jax.ShapeDtypeStruct((bsz, s, d // 2), jnp.uint32),
            jax.ShapeDtypeStruct((bsz, s, ROUTE_W), F32),
            jax.ShapeDtypeStruct((1, ROUTE_W), F32),
        ],
        scratch_shapes=[pltpu.VMEM((1, ROUTE_W), F32)],
        compiler_params=pltpu.CompilerParams(
            dimension_semantics=("arbitrary", "arbitrary"), vmem_limit_bytes=VMEM_LIMIT),
        name="out_route",
    )(y, x, g1, w_bf, sc2, sh2, g, wr_bf, br, jnp.tril(jnp.ones((sub, sub), BF16), -1))


def _plan(rt, counts_f, n_tok):
    counts = counts_f[0, N_GROUPS:N_GROUPS + N_EXPERTS].astype(jnp.int32)
    padded = ((counts + MOE_BLOCK - 1) // MOE_BLOCK) * MOE_BLOCK
    padded_ends = jnp.cumsum(padded)
    padded_starts = padded_ends - padded
    n_rows = 2 * n_tok + N_EXPERTS * MOE_BLOCK
    n_blocks = n_rows // MOE_BLOCK
    expert_ids = jnp.arange(N_EXPERTS, dtype=jnp.int32)
    idx = rt[:, 0:6].astype(jnp.int32)
    start_of = jnp.sum(jnp.where(idx[:, 0:2, None] == expert_ids, padded_starts, 0), axis=-1)
    pos = start_of + idx[:, 4:6]
    block_start = jnp.arange(n_blocks, dtype=jnp.int32)[:, None] * MOE_BLOCK
    owns = (padded_starts <= block_start) & (block_start < padded_ends)
    n_valid = jnp.sum(jnp.where(owns, jnp.clip(counts - (block_start - padded_starts), 0, MOE_BLOCK), 0), axis=1)
    last_used = jnp.max(jnp.where(counts > 0, expert_ids, 0))
    block_expert = jnp.where(n_valid > 0, jnp.sum(jnp.where(owns, expert_ids, 0), axis=1), last_used)
    later_used = (expert_ids[None, :] > expert_ids[:, None]) & (counts[None, :] > 0)
    next_used = jnp.min(jnp.where(later_used, expert_ids[None, :], N_EXPERTS), axis=1)
    next_used = jnp.where(next_used == N_EXPERTS, -1, next_used)
    block_next = jnp.sum(jnp.where(owns, next_used, 0), axis=1)
    n_used = padded_ends[-1] // MOE_BLOCK
    block_index = jnp.minimum(jnp.arange(n_blocks, dtype=jnp.int32), n_used - 1)
    i32 = lambda a: a.astype(jnp.int32)
    return pos[:, 0], pos[:, 1], i32(block_expert), i32(n_valid), i32(block_index), i32(block_next), n_rows


def _dispatch_kernel(p0_ref, p1_ref, h_ref, zeros_hbm, xs_hbm, sem, *, tm):
    del zeros_hbm

    def issue(r, carry):
        src = h_ref.at[pl.ds(r, 1)]
        pltpu.make_async_copy(src, xs_hbm.at[pl.ds(p0_ref[0, r], 1)], sem).start()
        pltpu.make_async_copy(src, xs_hbm.at[pl.ds(p1_ref[0, r], 1)], sem).start()
        return carry

    lax.fori_loop(0, tm, issue, 0, unroll=DMA_ISSUE_UNROLL)
    for _ in range(2):
        pltpu.make_async_copy(h_ref, xs_hbm.at[pl.ds(0, tm)], sem).wait()


def _dispatch(pos0, pos1, h2p, n_rows):
    n, w = h2p.shape
    tm = 256
    idx_spec = pl.BlockSpec((None, 1, tm), lambda t: (t, 0, 0), memory_space=pltpu.SMEM)
    return pl.pallas_call(
        functools.partial(_dispatch_kernel, tm=tm),
        grid=(n // tm,),
        in_specs=[idx_spec, idx_spec, pl.BlockSpec((tm, w), lambda t: (t, 0)),
                  pl.BlockSpec(memory_space=pl.ANY)],
        out_specs=pl.BlockSpec(memory_space=pl.ANY),
        out_shape=jax.ShapeDtypeStruct((n_rows, w), h2p.dtype),
        input_output_aliases={3: 0},
        scratch_shapes=[pltpu.SemaphoreType.DMA(())],
        compiler_params=pltpu.CompilerParams(
            dimension_semantics=("arbitrary",), vmem_limit_bytes=VMEM_LIMIT,
            disable_bounds_checks=True),
        name="dispatch",
    )(pos0.reshape(n // tm, 1, tm), pos1.reshape(n // tm, 1, tm), h2p, jnp.zeros((n_rows, w), h2p.dtype))


def _moe_kernel(be_ref, nvalid_ref, bi_ref, nxt_ref, x_ref, wg_hbm, wu_hbm, wd_hbm, y_ref,
                wg_f, wu_f, wd_f, wg_b, wu_b, wd_b, sem):
    i = pl.program_id(0)
    n_valid = nvalid_ref[i]

    def fetch(e):
        return (pltpu.make_async_copy(wg_hbm.at[e], wg_f, sem.at[0]),
                pltpu.make_async_copy(wu_hbm.at[e], wu_f, sem.at[1]),
                pltpu.make_async_copy(wd_hbm.at[e], wd_f, sem.at[2]))

    @pl.when(n_valid > 0)
    def _():
        e = be_ref[i]

        @pl.when(i == 0)
        def _():
            for cp in fetch(e):
                cp.start()

        @pl.when((i == 0) | (e != be_ref[jnp.maximum(i - 1, 0)]))
        def _():
            for cp in fetch(e):
                cp.wait()
            wg_b[...] = wg_f[...].astype(BF16)
            wu_b[...] = wu_f[...].astype(BF16)
            wd_b[...] = wd_f[...].astype(BF16)
            nxt = nxt_ref[i]

            @pl.when(nxt >= 0)
            def _():
                for cp in fetch(nxt):
                    cp.start()

        lo, hi = _unpack_bf16_pair(x_ref[...])
        half = x_ref.shape[1]
        g = _dot(lo, wg_b[:half, :]) + _dot(hi, wg_b[half:, :])
        u = _dot(lo, wu_b[:half, :]) + _dot(hi, wu_b[half:, :])
        act = (g * jax.nn.sigmoid(g) * u).astype(BF16)
        y_ref[...] = _dot(act, wd_b[...])

    @pl.when(n_valid == 0)
    def _():
        y_ref[...] = jnp.zeros_like(y_ref)


def _moe(block_expert, n_valid, block_index, block_next, xs, w_gate, w_up, w_down):
    n_blocks = block_expert.shape[0]
    n_rows, half = xs.shape
    d = 2 * half
    hbm = pl.BlockSpec(memory_space=pl.ANY)
    grid_spec = pltpu.PrefetchScalarGridSpec(
        num_scalar_prefetch=4,
        grid=(n_blocks,),
        in_specs=[pl.BlockSpec((MOE_BLOCK, half), lambda i, be, nv, bi, nx: (bi[i], 0)), hbm, hbm, hbm],
        out_specs=pl.BlockSpec((MOE_BLOCK, d), lambda i, be, nv, bi, nx: (i, 0)),
        scratch_shapes=[
            pltpu.VMEM((d, D_EXPERT), F32),
            pltpu.VMEM((d, D_EXPERT), F32),
            pltpu.VMEM((D_EXPERT, d), F32),
            pltpu.VMEM((d, D_EXPERT), BF16),
            pltpu.VMEM((d, D_EXPERT), BF16),
            pltpu.VMEM((D_EXPERT, d), BF16),
            pltpu.SemaphoreType.DMA((3,)),
        ],
    )
    return pl.pallas_call(
        _moe_kernel,
        grid_spec=grid_spec,
        out_shape=jax.ShapeDtypeStruct((n_rows, d), F32),
        compiler_params=pltpu.CompilerParams(
            dimension_semantics=("arbitrary",), vmem_limit_bytes=VMEM_LIMIT),
        name="moe",
    )(block_expert, n_valid, block_index, block_next, xs, w_gate, w_up, w_down)


def _final_kernel(p0_ref, p1_ref, x1_ref, rt_ref, g2_ref, gf_ref, yb_hbm, o_ref, y0buf, y1buf, sem, *, tm):
    def issue(r, carry):
        pltpu.make_async_copy(yb_hbm.at[pl.ds(p0_ref[0, r], 1)], y0buf.at[pl.ds(r, 1)], sem).start()
        pltpu.make_async_copy(yb_hbm.at[pl.ds(p1_ref[0, r], 1)], y1buf.at[pl.ds(r, 1)], sem).start()
        return carry

    lax.fori_loop(0, tm, issue, 0, unroll=DMA_ISSUE_UNROLL)
    pltpu.make_async_copy(yb_hbm.at[pl.ds(0, tm)], y0buf, sem).wait()
    pltpu.make_async_copy(yb_hbm.at[pl.ds(0, tm)], y1buf, sem).wait()

    rt = rt_ref[...]
    y = rt[:, 2:3] * y0buf[...] + rt[:, 3:4] * y1buf[...]
    x2 = x1_ref[...] + g2_ref[...] * y
    ms = jnp.mean(x2 * x2, axis=-1, keepdims=True)
    o_ref[...] = x2 * lax.rsqrt(ms + EPS) * gf_ref[...]


def _final(pos0, pos1, x1, rt, g2, gf, yb, seq):
    n, d = x1.shape
    tm = 256
    per_seq = seq // tm
    idx_spec = pl.BlockSpec((None, 1, tm), lambda t: (t, 0, 0), memory_space=pltpu.SMEM)
    return pl.pallas_call(
        functools.partial(_final_kernel, tm=tm),
        grid=(n // tm,),
        in_specs=[
            idx_spec, idx_spec,
            pl.BlockSpec((tm, d), lambda t: (t, 0)),
            pl.BlockSpec((tm, ROUTE_W), lambda t: (t, 0)),
            pl.BlockSpec((None, 1, d), lambda t: (t // per_seq, 0, 0)),
            pl.BlockSpec((1, d), lambda t: (0, 0)),
            pl.BlockSpec(memory_space=pl.ANY),
        ],
        out_specs=pl.BlockSpec((tm, d), lambda t: (t, 0)),
        out_shape=jax.ShapeDtypeStruct((n, d), F32),
        scratch_shapes=[
            pltpu.VMEM((tm, d), F32),
            pltpu.VMEM((tm, d), F32),
            pltpu.SemaphoreType.DMA(()),
        ],
        compiler_params=pltpu.CompilerParams(
            dimension_semantics=("arbitrary",), vmem_limit_bytes=VMEM_LIMIT,
            disable_bounds_checks=True),
        name="final",
    )(pos0.reshape(n // tm, 1, tm), pos1.reshape(n // tm, 1, tm), x1, rt, g2, gf, yb)


def kernel(x, c, ada_w, ada_b, norm1_g, w_in, conv3_w, conv4_w, conv4_b, lru_w_a, lru_b_a, lru_w_x, lru_b_x, lru_lambda, head_norm_conv_g, head_norm_lru_g, w_out, norm2_g, route_w_group, route_b_group, route_w_expert, route_b_expert, w_e_gate, w_e_up, w_e_down, final_norm_g):
    bsz, s, d = x.shape
    n_tok = bsz * s
    l = 0

    mod = _ada(c, ada_w[l], ada_b[l]).reshape(bsz, 6, 1, d)
    sh1, sc1, g1, sh2, sc2, g2 = (mod[:, j] for j in range(6))

    proj = _in_proj(x, sc1, sh1, norm1_g[l].reshape(1, d), w_in[l].astype(BF16))

    wax = jnp.concatenate([lru_w_a[l], lru_w_x[l]], axis=-1).astype(BF16)
    y = _mix(proj, conv3_w[l], conv4_w[l], conv4_b[l], wax, lru_b_a[l], lru_b_x[l], lru_lambda[l],
             head_norm_conv_g[l], head_norm_lru_g[l])

    w_route = jnp.concatenate(
        [route_w_group[l], jnp.transpose(route_w_expert[l], (1, 0, 2)).reshape(d, N_EXPERTS),
         jnp.zeros((d, ROUTE_W - N_GROUPS - N_EXPERTS), F32)], axis=1).astype(BF16)
    b_route = jnp.concatenate(
        [route_b_group[l], route_b_expert[l].reshape(-1),
         jnp.zeros((ROUTE_W - N_GROUPS - N_EXPERTS,), F32)]).reshape(1, ROUTE_W)
    x1, h2p, rt, counts = _out(y, x, g1, w_out[l].astype(BF16), sc2, sh2, norm2_g[l].reshape(1, d),
                               w_route, b_route)

    rt = rt.reshape(n_tok, ROUTE_W)
    pos0, pos1, block_expert, n_valid, block_index, block_next, n_rows = _plan(rt, counts, n_tok)
    xs = _dispatch(pos0, pos1, h2p.reshape(n_tok, d // 2), n_rows)
    yb = _moe(block_expert, n_valid, block_index, block_next, xs, w_e_gate[l], w_e_up[l], w_e_down[l])
    out = _final(pos0, pos1, x1.reshape(n_tok, d), rt, g2, final_norm_g.reshape(1, d), yb, s)
    return out.reshape(bsz, s, d)
```

```python
import functools

import jax
import jax.numpy as jnp
from jax import lax
from jax.experimental import pallas as pl
from jax.experimental.pallas import tpu as pltpu

D_MODEL = 2048
D_CONV = 1024
CONV_HEAD_DIM = 64
D_LRU = 1024
LRU_HEADS = 8
LRU_HEAD_DIM = 128
LRU_C = 8.0
D_MIX = D_CONV + D_LRU
D_IN = 3 * D_CONV + 2 * D_LRU
N_GROUPS = 8
EXPERTS_PER_GROUP = 8
N_EXPERTS = 64
D_EXPERT = 512
EPS = 1e-6

LANES = 128
SUBLANES = 8
ROUTE_W = LANES
MOE_BLOCK = 256
DMA_ISSUE_UNROLL = 8
VMEM_LIMIT = 56 * 1024 * 1024

F32 = jnp.float32
BF16 = jnp.bfloat16


def _dot(a, b):
    return jnp.dot(a, b, preferred_element_type=F32)


def _ada_kernel(c_ref, w_ref, b_ref, o_ref):
    c = c_ref[...]
    s = (c * jax.nn.sigmoid(c)).astype(BF16)
    o_ref[...] = _dot(s, w_ref[...].astype(BF16)) + b_ref[...]


def _ada(c, w, b):
    bsz, d = c.shape
    n = w.shape[1]
    tn = 1024
    return pl.pallas_call(
        _ada_kernel,
        grid=(n // tn,),
        in_specs=[
            pl.BlockSpec((bsz, d), lambda j: (0, 0)),
            pl.BlockSpec((d, tn), lambda j: (0, j)),
            pl.BlockSpec((1, tn), lambda j: (0, j)),
        ],
        out_specs=pl.BlockSpec((bsz, tn), lambda j: (0, j)),
        out_shape=jax.ShapeDtypeStruct((bsz, n), F32),
        compiler_params=pltpu.CompilerParams(
            dimension_semantics=("arbitrary",), vmem_limit_bytes=VMEM_LIMIT),
        name="ada",
    )(c, w, b.reshape(1, n))


def _modulated_norm(x, g, sc, sh):
    ms = jnp.mean(x * x, axis=-1, keepdims=True)
    return x * lax.rsqrt(ms + EPS) * (g * (1.0 + sc)) + sh


def _in_proj_kernel(x_ref, sc_ref, sh_ref, g_ref, w_ref, o_ref, *, n_chunk):
    hb = _modulated_norm(x_ref[...], g_ref[...], sc_ref[...], sh_ref[...]).astype(BF16)
    for j in range(D_IN // n_chunk):
        cols = slice(j * n_chunk, (j + 1) * n_chunk)
        o_ref[:, cols] = _dot(hb, w_ref[:, cols])


def _in_proj(x, sc, sh, g, w_bf):
    bsz, s, d = x.shape
    tm = 256
    per_batch = pl.BlockSpec((None, 1, d), lambda b, t: (b, 0, 0))
    return pl.pallas_call(
        functools.partial(_in_proj_kernel, n_chunk=1024),
        grid=(bsz, s // tm),
        in_specs=[
            pl.BlockSpec((None, tm, d), lambda b, t: (b, t, 0)),
            per_batch, per_batch,
            pl.BlockSpec((1, d), lambda b, t: (0, 0)),
            pl.BlockSpec((d, D_IN), lambda b, t: (0, 0), pipeline_mode=pl.Buffered(1)),
        ],
        out_specs=pl.BlockSpec((None, tm, D_IN), lambda b, t: (b, t, 0)),
        out_shape=jax.ShapeDtypeStruct((bsz, s, D_IN), F32),
        compiler_params=pltpu.CompilerParams(
            dimension_semantics=("arbitrary", "arbitrary"), vmem_limit_bytes=VMEM_LIMIT),
        name="in_proj",
    )(x, sc, sh, g, w_bf)


def _shift_rows(x, prev8, d, row):
    rolled = pltpu.roll(x, d, 0)
    head = jnp.where(row[:SUBLANES] >= d, rolled[:SUBLANES], pltpu.roll(prev8, d, 0))
    return jnp.concatenate([head, rolled[SUBLANES:]], axis=0)


def _head_rms(y, p_ref, g):
    sq = y * y
    hi = sq.astype(BF16)
    lo = (sq - hi.astype(F32)).astype(BF16)
    ms = _dot(hi, p_ref[...]) + _dot(lo, p_ref[...])
    return y * lax.rsqrt(ms + EPS) * g


def _gelu_tanh(x):
    return 0.5 * x * (1.0 + jnp.tanh(0.7978845608028654 * (x + 0.044715 * (x * x * x))))


def _mix_kernel(p_ref, w3_ref, w4_ref, b4_ref, wax_ref, ba_ref, bx_ref, lam_ref, ga_ref, gb_ref,
                pa_ref, pb_ref, y_ref, ua_carry, xb_carry, h_carry, *, ts):
    @pl.when(pl.program_id(1) == 0)
    def _():
        ua_carry[...] = jnp.zeros_like(ua_carry)
        xb_carry[...] = jnp.zeros_like(xb_carry)
        h_carry[...] = jnp.zeros_like(h_carry)

    row = lax.broadcasted_iota(jnp.int32, (ts, LANES), 0)

    for k in range(D_CONV // LANES):
        cols = slice(k * LANES, (k + 1) * LANES)
        b_a = p_ref[:, k * LANES:(k + 1) * LANES]
        c_a = p_ref[:, D_CONV + k * LANES:D_CONV + (k + 1) * LANES]
        x_a = p_ref[:, 2 * D_CONV + k * LANES:2 * D_CONV + (k + 1) * LANES]
        u = c_a * x_a
        prev = ua_carry[:, cols]
        conv = (w3_ref[2:3, cols] * u
                + w3_ref[1:2, cols] * _shift_rows(u, prev, 1, row)
                + w3_ref[0:1, cols] * _shift_rows(u, prev, 2, row))
        ua_carry[:, cols] = u[ts - SUBLANES:, :]
        y_a = b_a * conv
        y_ref[:, cols] = _head_rms(y_a, pa_ref, ga_ref[:, cols]).astype(BF16)

        g_b = p_ref[:, 3 * D_CONV + k * LANES:3 * D_CONV + (k + 1) * LANES]
        x_b = p_ref[:, 3 * D_CONV + D_LRU + k * LANES:3 * D_CONV + D_LRU + (k + 1) * LANES]
        prevb = xb_carry[:, cols]
        xc = (w4_ref[3:4, cols] * x_b
              + w4_ref[2:3, cols] * _shift_rows(x_b, prevb, 1, row)
              + w4_ref[1:2, cols] * _shift_rows(x_b, prevb, 2, row)
              + w4_ref[0:1, cols] * _shift_rows(x_b, prevb, 3, row)
              + b4_ref[:, cols])
        xb_carry[:, cols] = x_b[ts - SUBLANES:, :]
        gates = _dot(xc.astype(BF16), wax_ref[k])
        r = jax.nn.sigmoid(gates[:, :LANES] + ba_ref[:, cols])
        i = jax.nn.sigmoid(gates[:, LANES:] + bx_ref[:, cols])
        nlam = -lam_ref[:, cols]
        softplus = jnp.maximum(nlam, 0.0) + jnp.log1p(jnp.exp(-jnp.abs(nlam)))
        log_a = (-LRU_C) * r * softplus
        a = jnp.exp(log_a)
        mult = jnp.sqrt(-jnp.tanh(log_a) * (a * a + 1.0))
        v = mult * i * xc
        d = 1
        while d < ts:
            keep = row >= d
            a_sh = jnp.where(keep, pltpu.roll(a, d, 0), 1.0)
            v_sh = jnp.where(keep, pltpu.roll(v, d, 0), 0.0)
            v = v + a * v_sh
            a = a * a_sh
            d *= 2
        hs = v + a * h_carry[:, cols]
        h_carry[:, cols] = hs[ts - 1:ts, :]
        y_b = hs * _gelu_tanh(g_b)
        y_ref[:, D_CONV + k * LANES:D_CONV + (k + 1) * LANES] = _head_rms(
            y_b, pb_ref, gb_ref[:, cols]).astype(BF16)


def _mix(proj, w3, w4, b4, wax_bf, b_a, b_x, lam, g_a, g_b):
    bsz, s, _ = proj.shape
    ts = 256
    lane = jnp.arange(LANES)
    p_a = ((lane[:, None] // CONV_HEAD_DIM) == (lane[None, :] // CONV_HEAD_DIM)).astype(BF16) / CONV_HEAD_DIM
    p_b = jnp.full((LANES, LANES), 1.0 / LRU_HEAD_DIM, BF16)
    full = lambda shape: pl.BlockSpec(shape, lambda b, t: (0,) * len(shape))
    row = lambda a: a.reshape(1, -1)
    return pl.pallas_call(
        functools.partial(_mix_kernel, ts=ts),
        grid=(bsz, s // ts),
        in_specs=[
            pl.BlockSpec((None, ts, D_IN), lambda b, t: (b, t, 0)),
            full((3, D_CONV)), full((4, D_LRU)), full((1, D_LRU)),
            full((LRU_HEADS, LRU_HEAD_DIM, 2 * LRU_HEAD_DIM)),
            full((1, D_LRU)), full((1, D_LRU)), full((1, D_LRU)),
            full((1, D_CONV)), full((1, D_LRU)),
            full((LANES, LANES)), full((LANES, LANES)),
        ],
        out_specs=pl.BlockSpec((None, ts, D_MIX), lambda b, t: (b, t, 0)),
        out_shape=jax.ShapeDtypeStruct((bsz, s, D_MIX), BF16),
        scratch_shapes=[
            pltpu.VMEM((SUBLANES, D_CONV), F32),
            pltpu.VMEM((SUBLANES, D_LRU), F32),
            pltpu.VMEM((1, D_LRU), F32),
        ],
        compiler_params=pltpu.CompilerParams(
            dimension_semantics=("arbitrary", "arbitrary"), vmem_limit_bytes=VMEM_LIMIT),
        name="mix",
    )(proj, w3, w4, row(b4), wax_bf, row(b_a), row(b_x), row(lam), row(g_a), row(g_b), p_a, p_b)


def _route(logits, tri, carry):
    lane = lax.broadcasted_iota(jnp.int32, logits.shape, 1)
    neg = -jnp.inf
    is_g = lane < N_GROUPS
    gl = jnp.where(is_g, logits, neg)
    ge = jnp.exp(gl - jnp.max(gl, axis=-1, keepdims=True))
    p_g = 1.0 / jnp.sum(ge, axis=-1, keepdims=True)
    g_idx = jnp.min(jnp.where(is_g & (ge >= 1.0), lane, LANES), axis=-1, keepdims=True)
    sel = (lane >= N_GROUPS) & (lane < N_GROUPS + N_EXPERTS) & (((lane - N_GROUPS) >> 3) == g_idx)
    el = jnp.where(sel, logits, neg)
    m1 = jnp.max(el, axis=-1, keepdims=True)
    i1 = jnp.min(jnp.where(sel & (el == m1), lane, LANES), axis=-1, keepdims=True)
    sel2 = sel & (lane != i1)
    el2 = jnp.where(sel2, logits, neg)
    m2 = jnp.max(el2, axis=-1, keepdims=True)
    i2 = jnp.min(jnp.where(sel2 & (el2 == m2), lane, LANES), axis=-1, keepdims=True)
    e2 = jnp.exp(m2 - m1)
    w1 = p_g * (1.0 / (1.0 + e2))
    w2 = p_g * (e2 / (1.0 + e2))
    out = jnp.where(lane == 0, (i1 - N_GROUPS).astype(F32), 0.0)
    out = jnp.where(lane == 1, (i2 - N_GROUPS).astype(F32), out)
    out = jnp.where(lane == 2, w1, out)
    out = jnp.where(lane == 3, w2, out)
    oh1 = lane == i1
    oh2 = lane == i2
    both = jnp.where(oh1 | oh2, 1.0, 0.0)
    before = _dot(tri, both.astype(BF16)) + carry
    out = jnp.where(lane == 4, jnp.sum(jnp.where(oh1, before, 0.0), axis=-1, keepdims=True), out)
    out = jnp.where(lane == 5, jnp.sum(jnp.where(oh2, before, 0.0), axis=-1, keepdims=True), out)
    return out, carry + jnp.sum(both, axis=0, keepdims=True)


def _pack_bf16_pair(lo, hi):
    lo_bits = lax.bitcast_convert_type(lo.astype(BF16).astype(F32), jnp.uint32)
    hi_bits = lax.bitcast_convert_type(hi.astype(BF16).astype(F32), jnp.uint32)
    return (lo_bits >> 16) | hi_bits


def _unpack_bf16_pair(w):
    lo = lax.bitcast_convert_type(w << 16, F32).astype(BF16)
    hi = lax.bitcast_convert_type(w & jnp.uint32(0xFFFF0000), F32).astype(BF16)
    return lo, hi


def _out_kernel(y_ref, x_ref, g1_ref, w_ref, sc_ref, sh_ref, g_ref, wr_ref, br_ref, tri_ref,
                x1_ref, h2p_ref, rt_ref, cnt_ref, carry):
    @pl.when((pl.program_id(0) == 0) & (pl.program_id(1) == 0))
    def _():
        carry[...] = jnp.zeros_like(carry)

    sub = tri_ref.shape[0]
    count = carry[...]
    tiles = [slice(r0, r0 + sub) for r0 in range(0, y_ref.shape[0], sub)]
    mixes = [_dot(y_ref[rows, :], w_ref[...]) for rows in tiles]
    for rows, mix in zip(tiles, mixes):
        x1 = x_ref[rows, :] + g1_ref[...] * mix
        x1_ref[rows, :] = x1
        h2 = _modulated_norm(x1, g_ref[...], sc_ref[...], sh_ref[...])
        half = h2.shape[1] // 2
        h2p_ref[rows, :] = _pack_bf16_pair(h2[:, :half], h2[:, half:])
        logits = _dot(h2.astype(BF16), wr_ref[...]) + br_ref[...]
        rt, count = _route(logits, tri_ref[...], count)
        rt_ref[rows, :] = rt
    carry[...] = count
    cnt_ref[...] = count


def _out(y, x, g1, w_bf, sc2, sh2, g, wr_bf, br):
    bsz, s, d = x.shape
    tm, sub = 512, 256
    per_batch = pl.BlockSpec((None, 1, d), lambda b, t: (b, 0, 0))
    tile = lambda w: pl.BlockSpec((None, tm, w), lambda b, t: (b, t, 0))
    const = lambda shape, **kw: pl.BlockSpec(shape, lambda b, t: (0, 0), **kw)
    return pl.pallas_call(
        _out_kernel,
        grid=(bsz, s // tm),
        in_specs=[
            tile(D_MIX), tile(d), per_batch,
            const((D_MIX, d), pipeline_mode=pl.Buffered(1)),
            per_batch, per_batch, const((1, d)),
            const((d, ROUTE_W)), const((1, ROUTE_W)), const((sub, sub)),
        ],
        out_specs=[tile(d), tile(d // 2), tile(ROUTE_W), const((1, ROUTE_W))],
        out_shape=[
            jax.ShapeDtypeStruct((bsz, s, d), F32),
            jax.ShapeDtypeStruct((bsz, s, d // 2), jnp.uint32),
            jax.ShapeDtypeStruct((bsz, s, ROUTE_W), F32),
            jax.ShapeDtypeStruct((1, ROUTE_W), F32),
        ],
        scratch_shapes=[pltpu.VMEM((1, ROUTE_W), F32)],
        compiler_params=pltpu.CompilerParams(
            dimension_semantics=("arbitrary", "arbitrary"), vmem_limit_bytes=VMEM_LIMIT),
        name="out_route",
    )(y, x, g1, w_bf, sc2, sh2, g, wr_bf, br, jnp.tril(jnp.ones((sub, sub), BF16), -1))


def _plan(rt, counts_f, n_tok):
    counts = counts_f[0, N_GROUPS:N_GROUPS + N_EXPERTS].astype(jnp.int32)
    padded = ((counts + MOE_BLOCK - 1) // MOE_BLOCK) * MOE_BLOCK
    padded_ends = jnp.cumsum(padded)
    padded_starts = padded_ends - padded
    n_rows = 2 * n_tok + N_EXPERTS * MOE_BLOCK
    n_blocks = n_rows // MOE_BLOCK
    expert_ids = jnp.arange(N_EXPERTS, dtype=jnp.int32)
    idx = rt[:, 0:6].astype(jnp.int32)
    start_of = jnp.sum(jnp.where(idx[:, 0:2, None] == expert_ids, padded_starts, 0), axis=-1)
    pos = start_of + idx[:, 4:6]
    block_start = jnp.arange(n_blocks, dtype=jnp.int32)[:, None] * MOE_BLOCK
    owns = (padded_starts <= block_start) & (block_start < padded_ends)
    n_valid = jnp.sum(jnp.where(owns, jnp.clip(counts - (block_start - padded_starts), 0, MOE_BLOCK), 0), axis=1)
    last_used = jnp.max(jnp.where(counts > 0, expert_ids, 0))
    block_expert = jnp.where(n_valid > 0, jnp.sum(jnp.where(owns, expert_ids, 0), axis=1), last_used)
    later_used = (expert_ids[None, :] > expert_ids[:, None]) & (counts[None, :] > 0)
    next_used = jnp.min(jnp.where(later_used, expert_ids[None, :], N_EXPERTS), axis=1)
    next_used = jnp.where(next_used == N_EXPERTS, -1, next_used)
    block_next = jnp.sum(jnp.where(owns, next_used, 0), axis=1)
    i32 = lambda a: a.astype(jnp.int32)
    return pos[:, 0], pos[:, 1], i32(block_expert), i32(n_valid), i32(block_next), n_rows


def _invert_kernel(p0_ref, p1_ref, code_ref, *, tm, n_rows, n_tok):
    t0 = pl.program_id(0) * tm

    @pl.when(pl.program_id(0) == 0)
    def _():
        def clear(r, carry):
            code_ref[r] = 0
            return carry

        lax.fori_loop(0, n_rows, clear, 0, unroll=DMA_ISSUE_UNROLL)

    def fill(r, carry):
        code_ref[p0_ref[0, r]] = t0 + r
        code_ref[p1_ref[0, r]] = n_tok + t0 + r
        return carry

    lax.fori_loop(0, tm, fill, 0, unroll=DMA_ISSUE_UNROLL)


def _invert(pos0, pos1, n_rows):
    n = pos0.shape[0]
    tm = 2048
    idx_spec = pl.BlockSpec((None, 1, tm), lambda t: (t, 0, 0), memory_space=pltpu.SMEM)
    return pl.pallas_call(
        functools.partial(_invert_kernel, tm=tm, n_rows=n_rows, n_tok=n),
        grid=(n // tm,),
        in_specs=[idx_spec, idx_spec],
        out_specs=pl.BlockSpec(memory_space=pltpu.SMEM),
        out_shape=jax.ShapeDtypeStruct((n_rows,), jnp.int32),
        compiler_params=pltpu.CompilerParams(dimension_semantics=("arbitrary",)),
        name="invert",
    )(pos0.reshape(n // tm, 1, tm), pos1.reshape(n // tm, 1, tm))


def _wait_rows(copy_of, n):
    n_full = pl.multiple_of((n // SUBLANES) * SUBLANES, SUBLANES)

    @pl.when(n_full > 0)
    def _():
        copy_of(n_full).wait()

    def one(r, carry):
        copy_of(1).wait()
        return carry

    lax.fori_loop(0, n - n_full, one, 0)


def _moe_kernel(be_ref, nvalid_ref, nxt_ref, code_ref, code_next_ref, h_hbm, wg_hbm, wu_hbm, wd_hbm, y2_hbm,
                wg_f, wu_f, wd_f, wg_b, wu_b, wd_b, xbuf, ybuf, wsem, gsem, ssem, *, n_tok, n_blocks):
    i = pl.program_id(0)
    slot = i & 1
    n_valid = nvalid_ref[i]

    def gather_row(codes, r, s):
        tok = codes[0, r] & (n_tok - 1)
        return pltpu.make_async_copy(h_hbm.at[pl.ds(tok, 1)], xbuf.at[s].at[pl.ds(r, 1)], gsem.at[s])

    def gather_all(s):
        return pltpu.make_async_copy(h_hbm.at[pl.ds(0, MOE_BLOCK)], xbuf.at[s], gsem.at[s])

    def scatter_row(r, s):
        return pltpu.make_async_copy(ybuf.at[s].at[pl.ds(r, 1)], y2_hbm.at[pl.ds(code_ref[0, r], 1)], ssem.at[s])

    def scatter_rows(s):
        return lambda k: pltpu.make_async_copy(ybuf.at[s].at[pl.ds(0, k)], y2_hbm.at[pl.ds(0, k)], ssem.at[s])

    def fetch(e):
        return (pltpu.make_async_copy(wg_hbm.at[e], wg_f, wsem.at[0]),
                pltpu.make_async_copy(wu_hbm.at[e], wu_f, wsem.at[1]),
                pltpu.make_async_copy(wd_hbm.at[e], wd_f, wsem.at[2]))

    def rolled(n, start_row):
        def body(r, carry):
            start_row(r)
            return carry

        lax.fori_loop(0, n, body, 0, unroll=DMA_ISSUE_UNROLL if isinstance(n, int) else 1)

    @pl.when(i == 0)
    def _():
        rolled(MOE_BLOCK, lambda r: gather_row(code_ref, r, 0).start())

    gather_all(slot).wait()

    @pl.when(i >= 2)
    def _():
        _wait_rows(scatter_rows(slot), nvalid_ref[jnp.maximum(i - 2, 0)])

    @pl.when(n_valid > 0)
    def _():
        e = be_ref[i]

        @pl.when(i == 0)
        def _():
            for cp in fetch(e):
                cp.start()

        @pl.when((i == 0) | (e != be_ref[jnp.maximum(i - 1, 0)]))
        def _():
            for cp in fetch(e):
                cp.wait()
            wg_b[...] = wg_f[...].astype(BF16)
            wu_b[...] = wu_f[...].astype(BF16)
            wd_b[...] = wd_f[...].astype(BF16)
            nxt = nxt_ref[i]

            @pl.when(nxt >= 0)
            def _():
                for cp in fetch(nxt):
                    cp.start()

    def expert_block(s, static_scatter):
        for r in range(MOE_BLOCK):
            gather_row(code_next_ref, r, 1 - s).start()
        lo, hi = _unpack_bf16_pair(xbuf[s])
        half = xbuf.shape[2]
        g = _dot(lo, wg_b[:half, :]) + _dot(hi, wg_b[half:, :])
        u = _dot(lo, wu_b[:half, :]) + _dot(hi, wu_b[half:, :])
        act = (g * jax.nn.sigmoid(g) * u).astype(BF16)
        ybuf[s] = _dot(act, wd_b[...])
        if static_scatter:
            for r in range(MOE_BLOCK):
                scatter_row(r, s).start()
        else:
            rolled(n_valid, lambda r: scatter_row(r, s).start())

    for s in (0, 1):
        pl.when((slot == s) & (n_valid == MOE_BLOCK))(functools.partial(expert_block, s, True))
        pl.when((slot == s) & (n_valid > 0) & (n_valid < MOE_BLOCK))(functools.partial(expert_block, s, False))

    @pl.when(n_valid == 0)
    def _():
        rolled(MOE_BLOCK, lambda r: gather_row(code_next_ref, r, 1 - slot).start())

    @pl.when(i == n_blocks - 1)
    def _():
        gather_all(1 - slot).wait()
        _wait_rows(scatter_rows(slot), n_valid)
        _wait_rows(scatter_rows(1 - slot), nvalid_ref[jnp.maximum(i - 1, 0)])


def _moe(block_expert, n_valid, block_next, row_code, h2p, w_gate, w_up, w_down):
    n_blocks = block_expert.shape[0]
    n_tok, half = h2p.shape
    assert n_tok & (n_tok - 1) == 0, "the token is read from the low bits of the row code"
    d = 2 * half
    hbm = pl.BlockSpec(memory_space=pl.ANY)
    code_spec = lambda index: pl.BlockSpec((None, 1, MOE_BLOCK), index, memory_space=pltpu.SMEM)
    grid_spec = pltpu.PrefetchScalarGridSpec(
        num_scalar_prefetch=3,
        grid=(n_blocks,),
        in_specs=[
            code_spec(lambda i, be, nv, nx: (i, 0, 0)),
            code_spec(lambda i, be, nv, nx: (jnp.minimum(i + 1, n_blocks - 1), 0, 0)),
            hbm, hbm, hbm, hbm,
        ],
        out_specs=hbm,
        scratch_shapes=[
            pltpu.VMEM((d, D_EXPERT), F32),
            pltpu.VMEM((d, D_EXPERT), F32),
            pltpu.VMEM((D_EXPERT, d), F32),
            pltpu.VMEM((d, D_EXPERT), BF16),
            pltpu.VMEM((d, D_EXPERT), BF16),
            pltpu.VMEM((D_EXPERT, d), BF16),
            pltpu.VMEM((2, MOE_BLOCK, half), jnp.uint32),
            pltpu.VMEM((2, MOE_BLOCK, d), F32),
            pltpu.SemaphoreType.DMA((3,)),
            pltpu.SemaphoreType.DMA((2,)),
            pltpu.SemaphoreType.DMA((2,)),
        ],
    )
    codes = row_code.reshape(n_blocks, 1, MOE_BLOCK)
    return pl.pallas_call(
        functools.partial(_moe_kernel, n_tok=n_tok, n_blocks=n_blocks),
        grid_spec=grid_spec,
        out_shape=jax.ShapeDtypeStruct((2 * n_tok, d), F32),
        compiler_params=pltpu.CompilerParams(
            dimension_semantics=("arbitrary",), vmem_limit_bytes=VMEM_LIMIT,
            disable_bounds_checks=True),
        name="moe",
    )(block_expert, n_valid, block_next, codes, codes, h2p, w_gate, w_up, w_down)


def _final_kernel(x1_ref, y0_ref, y1_ref, rt_ref, g2_ref, gf_ref, o_ref):
    rt = rt_ref[...]
    y = rt[:, 2:3] * y0_ref[...] + rt[:, 3:4] * y1_ref[...]
    x2 = x1_ref[...] + g2_ref[...] * y
    ms = jnp.mean(x2 * x2, axis=-1, keepdims=True)
    o_ref[...] = x2 * lax.rsqrt(ms + EPS) * gf_ref[...]


def _final(x1, y2, rt, g2, gf, seq):
    n, d = x1.shape
    tm = 512
    per_seq = seq // tm
    n_t = n // tm
    return pl.pallas_call(
        _final_kernel,
        grid=(n_t,),
        in_specs=[
            pl.BlockSpec((tm, d), lambda t: (t, 0)),
            pl.BlockSpec((tm, d), lambda t: (t, 0)),
            pl.BlockSpec((tm, d), lambda t: (t + n_t, 0)),
            pl.BlockSpec((tm, ROUTE_W), lambda t: (t, 0)),
            pl.BlockSpec((None, 1, d), lambda t: (t // per_seq, 0, 0)),
            pl.BlockSpec((1, d), lambda t: (0, 0)),
        ],
        out_specs=pl.BlockSpec((tm, d), lambda t: (t, 0)),
        out_shape=jax.ShapeDtypeStruct((n, d), F32),
        compiler_params=pltpu.CompilerParams(
            dimension_semantics=("arbitrary",), vmem_limit_bytes=VMEM_LIMIT),
        name="final",
    )(x1, y2, y2, rt, g2, gf)


def kernel(x, c, ada_w, ada_b, norm1_g, w_in, conv3_w, conv4_w, conv4_b, lru_w_a, lru_b_a, lru_w_x, lru_b_x, lru_lambda, head_norm_conv_g, head_norm_lru_g, w_out, norm2_g, route_w_group, route_b_group, route_w_expert, route_b_expert, w_e_gate, w_e_up, w_e_down, final_norm_g):
    bsz, s, d = x.shape
    n_tok = bsz * s
    l = 0

    mod = _ada(c, ada_w[l], ada_b[l]).reshape(bsz, 6, 1, d)
    sh1, sc1, g1, sh2, sc2, g2 = (mod[:, j] for j in range(6))

    proj = _in_proj(x, sc1, sh1, norm1_g[l].reshape(1, d), w_in[l].astype(BF16))

    wax = jnp.concatenate([lru_w_a[l], lru_w_x[l]], axis=-1).astype(BF16)
    y = _mix(proj, conv3_w[l], conv4_w[l], conv4_b[l], wax, lru_b_a[l], lru_b_x[l], lru_lambda[l],
             head_norm_conv_g[l], head_norm_lru_g[l])

    w_route = jnp.concatenate(
        [route_w_group[l], jnp.transpose(route_w_expert[l], (1, 0, 2)).reshape(d, N_EXPERTS),
         jnp.zeros((d, ROUTE_W - N_GROUPS - N_EXPERTS), F32)], axis=1).astype(BF16)
    b_route = jnp.concatenate(
        [route_b_group[l], route_b_expert[l].reshape(-1),
         jnp.zeros((ROUTE_W - N_GROUPS - N_EXPERTS,), F32)]).reshape(1, ROUTE_W)
    x1, h2p, rt, counts = _out(y, x, g1, w_out[l].astype(BF16), sc2, sh2, norm2_g[l].reshape(1, d),
                               w_route, b_route)

    rt = rt.reshape(n_tok, ROUTE_W)
    pos0, pos1, block_expert, n_valid, block_next, n_rows = _plan(rt, counts, n_tok)
    row_code = _invert(pos0, pos1, n_rows)
    y2 = _moe(block_expert, n_valid, block_next, row_code, h2p.reshape(n_tok, d // 2),
              w_e_gate[l], w_e_up[l], w_e_down[l])
    out = _final(x1.reshape(n_tok, d), y2, rt, g2, final_norm_g.reshape(1, d), s)
    return out.reshape(bsz, s, d)
```

```python
import functools

import jax
import jax.numpy as jnp
from jax import lax
from jax.experimental import pallas as pl
from jax.experimental.pallas import tpu as pltpu

D_MODEL = 2048
D_CONV = 1024
CONV_HEAD_DIM = 64
D_LRU = 1024
LRU_HEADS = 8
LRU_HEAD_DIM = 128
LRU_C = 8.0
D_MIX = D_CONV + D_LRU
D_IN = 3 * D_CONV + 2 * D_LRU
N_GROUPS = 8
EXPERTS_PER_GROUP = 8
N_EXPERTS = 64
D_EXPERT = 512
EPS = 1e-6

LANES = 128
SUBLANES = 8
ROUTE_W = LANES
MOE_BLOCK = 256
DMA_ISSUE_UNROLL = 8
VMEM_LIMIT = 56 * 1024 * 1024

F32 = jnp.float32
BF16 = jnp.bfloat16


def _dot(a, b):
    return jnp.dot(a, b, preferred_element_type=F32)


def _ada_kernel(c_ref, w_ref, b_ref, o_ref):
    c = c_ref[...]
    s = (c * jax.nn.sigmoid(c)).astype(BF16)
    o_ref[...] = _dot(s, w_ref[...].astype(BF16)) + b_ref[...]


def _ada(c, w, b):
    bsz, d = c.shape
    n = w.shape[1]
    tn = 1024
    return pl.pallas_call(
        _ada_kernel,
        grid=(n // tn,),
        in_specs=[
            pl.BlockSpec((bsz, d), lambda j: (0, 0)),
            pl.BlockSpec((d, tn), lambda j: (0, j)),
            pl.BlockSpec((1, tn), lambda j: (0, j)),
        ],
        out_specs=pl.BlockSpec((bsz, tn), lambda j: (0, j)),
        out_shape=jax.ShapeDtypeStruct((bsz, n), F32),
        compiler_params=pltpu.CompilerParams(
            dimension_semantics=("arbitrary",), vmem_limit_bytes=VMEM_LIMIT),
        name="ada",
    )(c, w, b.reshape(1, n))


def _modulated_norm(x, g, sc, sh):
    ms = jnp.mean(x * x, axis=-1, keepdims=True)
    return x * lax.rsqrt(ms + EPS) * (g * (1.0 + sc)) + sh


def _in_proj_kernel(x_ref, sc_ref, sh_ref, g_ref, w_ref, o_ref, *, n_chunk):
    hb = _modulated_norm(x_ref[...], g_ref[...], sc_ref[...], sh_ref[...]).astype(BF16)
    for j in range(D_IN // n_chunk):
        cols = slice(j * n_chunk, (j + 1) * n_chunk)
        o_ref[:, cols] = _dot(hb, w_ref[:, cols])


def _in_proj(x, sc, sh, g, w_bf):
    bsz, s, d = x.shape
    tm = 256
    per_batch = pl.BlockSpec((None, 1, d), lambda b, t: (b, 0, 0))
    return pl.pallas_call(
        functools.partial(_in_proj_kernel, n_chunk=1024),
        grid=(bsz, s // tm),
        in_specs=[
            pl.BlockSpec((None, tm, d), lambda b, t: (b, t, 0)),
            per_batch, per_batch,
            pl.BlockSpec((1, d), lambda b, t: (0, 0)),
            pl.BlockSpec((d, D_IN), lambda b, t: (0, 0), pipeline_mode=pl.Buffered(1)),
        ],
        out_specs=pl.BlockSpec((None, tm, D_IN), lambda b, t: (b, t, 0)),
        out_shape=jax.ShapeDtypeStruct((bsz, s, D_IN), F32),
        compiler_params=pltpu.CompilerParams(
            dimension_semantics=("arbitrary", "arbitrary"), vmem_limit_bytes=VMEM_LIMIT),
        name="in_proj",
    )(x, sc, sh, g, w_bf)


def _shift_rows(x, prev8, d, row):
    rolled = pltpu.roll(x, d, 0)
    head = jnp.where(row[:SUBLANES] >= d, rolled[:SUBLANES], pltpu.roll(prev8, d, 0))
    return jnp.concatenate([head, rolled[SUBLANES:]], axis=0)


def _head_rms(y, p_ref, g):
    sq = y * y
    hi = sq.astype(BF16)
    lo = (sq - hi.astype(F32)).astype(BF16)
    ms = _dot(hi, p_ref[...]) + _dot(lo, p_ref[...])
    return y * lax.rsqrt(ms + EPS) * g


def _gelu_tanh(x):
    return 0.5 * x * (1.0 + jnp.tanh(0.7978845608028654 * (x + 0.044715 * (x * x * x))))


def _mix_kernel(p_ref, w3_ref, w4_ref, b4_ref, wax_ref, ba_ref, bx_ref, lam_ref, ga_ref, gb_ref,
                pa_ref, pb_ref, y_ref, ua_carry, xb_carry, h_carry, *, ts):
    @pl.when(pl.program_id(1) == 0)
    def _():
        ua_carry[...] = jnp.zeros_like(ua_carry)
        xb_carry[...] = jnp.zeros_like(xb_carry)
        h_carry[...] = jnp.zeros_like(h_carry)

    row = lax.broadcasted_iota(jnp.int32, (ts, LANES), 0)

    for k in range(D_CONV // LANES):
        cols = slice(k * LANES, (k + 1) * LANES)
        b_a = p_ref[:, k * LANES:(k + 1) * LANES]
        c_a = p_ref[:, D_CONV + k * LANES:D_CONV + (k + 1) * LANES]
        x_a = p_ref[:, 2 * D_CONV + k * LANES:2 * D_CONV + (k + 1) * LANES]
        u = c_a * x_a
        prev = ua_carry[:, cols]
        conv = (w3_ref[2:3, cols] * u
                + w3_ref[1:2, cols] * _shift_rows(u, prev, 1, row)
                + w3_ref[0:1, cols] * _shift_rows(u, prev, 2, row))
        ua_carry[:, cols] = u[ts - SUBLANES:, :]
        y_a = b_a * conv
        y_ref[:, cols] = _head_rms(y_a, pa_ref, ga_ref[:, cols]).astype(BF16)

        g_b = p_ref[:, 3 * D_CONV + k * LANES:3 * D_CONV + (k + 1) * LANES]
        x_b = p_ref[:, 3 * D_CONV + D_LRU + k * LANES:3 * D_CONV + D_LRU + (k + 1) * LANES]
        prevb = xb_carry[:, cols]
        xc = (w4_ref[3:4, cols] * x_b
              + w4_ref[2:3, cols] * _shift_rows(x_b, prevb, 1, row)
              + w4_ref[1:2, cols] * _shift_rows(x_b, prevb, 2, row)
              + w4_ref[0:1, cols] * _shift_rows(x_b, prevb, 3, row)
              + b4_ref[:, cols])
        xb_carry[:, cols] = x_b[ts - SUBLANES:, :]
        gates = _dot(xc.astype(BF16), wax_ref[k])
        r = jax.nn.sigmoid(gates[:, :LANES] + ba_ref[:, cols])
        i = jax.nn.sigmoid(gates[:, LANES:] + bx_ref[:, cols])
        nlam = -lam_ref[:, cols]
        softplus = jnp.maximum(nlam, 0.0) + jnp.log1p(jnp.exp(-jnp.abs(nlam)))
        log_a = (-LRU_C) * r * softplus
        a = jnp.exp(log_a)
        mult = jnp.sqrt(-jnp.tanh(log_a) * (a * a + 1.0))
        v = mult * i * xc
        d = 1
        while d < ts:
            keep = row >= d
            a_sh = jnp.where(keep, pltpu.roll(a, d, 0), 1.0)
            v_sh = jnp.where(keep, pltpu.roll(v, d, 0), 0.0)
            v = v + a * v_sh
            a = a * a_sh
            d *= 2
        hs = v + a * h_carry[:, cols]
        h_carry[:, cols] = hs[ts - 1:ts, :]
        y_b = hs * _gelu_tanh(g_b)
        y_ref[:, D_CONV + k * LANES:D_CONV + (k + 1) * LANES] = _head_rms(
            y_b, pb_ref, gb_ref[:, cols]).astype(BF16)


def _mix(proj, w3, w4, b4, wax_bf, b_a, b_x, lam, g_a, g_b):
    bsz, s, _ = proj.shape
    ts = 256
    lane = jnp.arange(LANES)
    p_a = ((lane[:, None] // CONV_HEAD_DIM) == (lane[None, :] // CONV_HEAD_DIM)).astype(BF16) / CONV_HEAD_DIM
    p_b = jnp.full((LANES, LANES), 1.0 / LRU_HEAD_DIM, BF16)
    full = lambda shape: pl.BlockSpec(shape, lambda b, t: (0,) * len(shape))
    row = lambda a: a.reshape(1, -1)
    return pl.pallas_call(
        functools.partial(_mix_kernel, ts=ts),
        grid=(bsz, s // ts),
        in_specs=[
            pl.BlockSpec((None, ts, D_IN), lambda b, t: (b, t, 0)),
            full((3, D_CONV)), full((4, D_LRU)), full((1, D_LRU)),
            full((LRU_HEADS, LRU_HEAD_DIM, 2 * LRU_HEAD_DIM)),
            full((1, D_LRU)), full((1, D_LRU)), full((1, D_LRU)),
            full((1, D_CONV)), full((1, D_LRU)),
            full((LANES, LANES)), full((LANES, LANES)),
        ],
        out_specs=pl.BlockSpec((None, ts, D_MIX), lambda b, t: (b, t, 0)),
        out_shape=jax.ShapeDtypeStruct((bsz, s, D_MIX), BF16),
        scratch_shapes=[
            pltpu.VMEM((SUBLANES, D_CONV), F32),
            pltpu.VMEM((SUBLANES, D_LRU), F32),
            pltpu.VMEM((1, D_LRU), F32),
        ],
        compiler_params=pltpu.CompilerParams(
            dimension_semantics=("arbitrary", "arbitrary"), vmem_limit_bytes=VMEM_LIMIT),
        name="mix",
    )(proj, w3, w4, row(b4), wax_bf, row(b_a), row(b_x), row(lam), row(g_a), row(g_b), p_a, p_b)


def _route(logits, tri, carry):
    lane = lax.broadcasted_iota(jnp.int32, logits.shape, 1)
    neg = -jnp.inf
    is_g = lane < N_GROUPS
    gl = jnp.where(is_g, logits, neg)
    ge = jnp.exp(gl - jnp.max(gl, axis=-1, keepdims=True))
    p_g = 1.0 / jnp.sum(ge, axis=-1, keepdims=True)
    g_idx = jnp.min(jnp.where(is_g & (ge >= 1.0), lane, LANES), axis=-1, keepdims=True)
    sel = (lane >= N_GROUPS) & (lane < N_GROUPS + N_EXPERTS) & (((lane - N_GROUPS) >> 3) == g_idx)
    el = jnp.where(sel, logits, neg)
    m1 = jnp.max(el, axis=-1, keepdims=True)
    i1 = jnp.min(jnp.where(sel & (el == m1), lane, LANES), axis=-1, keepdims=True)
    sel2 = sel & (lane != i1)
    el2 = jnp.where(sel2, logits, neg)
    m2 = jnp.max(el2, axis=-1, keepdims=True)
    i2 = jnp.min(jnp.where(sel2 & (el2 == m2), lane, LANES), axis=-1, keepdims=True)
    e2 = jnp.exp(m2 - m1)
    w1 = p_g * (1.0 / (1.0 + e2))
    w2 = p_g * (e2 / (1.0 + e2))
    out = jnp.where(lane == 0, (i1 - N_GROUPS).astype(F32), 0.0)
    out = jnp.where(lane == 1, (i2 - N_GROUPS).astype(F32), out)
    out = jnp.where(lane == 2, w1, out)
    out = jnp.where(lane == 3, w2, out)
    oh1 = lane == i1
    oh2 = lane == i2
    both = jnp.where(oh1 | oh2, 1.0, 0.0)
    before = _dot(tri, both.astype(BF16)) + carry
    out = jnp.where(lane == 4, jnp.sum(jnp.where(oh1, before, 0.0), axis=-1, keepdims=True), out)
    out = jnp.where(lane == 5, jnp.sum(jnp.where(oh2, before, 0.0), axis=-1, keepdims=True), out)
    return out, carry + jnp.sum(both, axis=0, keepdims=True)


def _pack_bf16_pair(lo, hi):
    lo_bits = lax.bitcast_convert_type(lo.astype(BF16).astype(F32), jnp.uint32)
    hi_bits = lax.bitcast_convert_type(hi.astype(BF16).astype(F32), jnp.uint32)
    return (lo_bits >> 16) | hi_bits


def _unpack_bf16_pair(w):
    lo = lax.bitcast_convert_type(w << 16, F32).astype(BF16)
    hi = lax.bitcast_convert_type(w & jnp.uint32(0xFFFF0000), F32).astype(BF16)
    return lo, hi


def _out_kernel(y_ref, x_ref, g1_ref, w_ref, sc_ref, sh_ref, g_ref, wr_ref, br_ref, tri_ref,
                x1_ref, h2p_ref, rt_ref, cnt_ref, carry):
    @pl.when((pl.program_id(0) == 0) & (pl.program_id(1) == 0))
    def _():
        carry[...] = jnp.zeros_like(carry)

    sub = tri_ref.shape[0]
    count = carry[...]
    tiles = [slice(r0, r0 + sub) for r0 in range(0, y_ref.shape[0], sub)]
    mixes = [_dot(y_ref[rows, :], w_ref[...]) for rows in tiles]
    for rows, mix in zip(tiles, mixes):
        x1 = x_ref[rows, :] + g1_ref[...] * mix
        x1_ref[rows, :] = x1
        h2 = _modulated_norm(x1, g_ref[...], sc_ref[...], sh_ref[...])
        half = h2.shape[1] // 2
        h2p_ref[rows, :] = _pack_bf16_pair(h2[:, :half], h2[:, half:])
        logits = _dot(h2.astype(BF16), wr_ref[...]) + br_ref[...]
        rt, count = _route(logits, tri_ref[...], count)
        rt_ref[rows, :] = rt
    carry[...] = count
    cnt_ref[...] = count


def _out(y, x, g1, w_bf, sc2, sh2, g, wr_bf, br):
    bsz, s, d = x.shape
    tm, sub = 512, 256
    per_batch = pl.BlockSpec((None, 1, d), lambda b, t: (b, 0, 0))
    tile = lambda w: pl.BlockSpec((None, tm, w), lambda b, t: (b, t, 0))
    const = lambda shape, **kw: pl.BlockSpec(shape, lambda b, t: (0, 0), **kw)
    return pl.pallas_call(
        _out_kernel,
        grid=(bsz, s // tm),
        in_specs=[
            tile(D_MIX), tile(d), per_batch,
            const((D_MIX, d), pipeline_mode=pl.Buffered(1)),
            per_batch, per_batch, const((1, d)),
            const((d, ROUTE_W)), const((1, ROUTE_W)), const((sub, sub)),
        ],
        out_specs=[tile(d), tile(d // 2), tile(ROUTE_W), const((1, ROUTE_W))],
        out_shape=[
            jax.ShapeDtypeStruct((bsz, s, d), F32),
            jax.ShapeDtypeStruct((bsz, s, d // 2), jnp.uint32),
            jax.ShapeDtypeStruct((bsz, s, ROUTE_W), F32),
            jax.ShapeDtypeStruct((1, ROUTE_W), F32),
        ],
        scratch_shapes=[pltpu.VMEM((1, ROUTE_W), F32)],
        compiler_params=pltpu.CompilerParams(
            dimension_semantics=("arbitrary", "arbitrary"), vmem_limit_bytes=VMEM_LIMIT),
        name="out_route",
    )(y, x, g1, w_bf, sc2, sh2, g, wr_bf, br, jnp.tril(jnp.ones((sub, sub), BF16), -1))


def _plan(rt, counts_f, n_tok):
    counts = counts_f[0, N_GROUPS:N_GROUPS + N_EXPERTS].astype(jnp.int32)
    padded = ((counts + MOE_BLOCK - 1) // MOE_BLOCK) * MOE_BLOCK
    padded_ends = jnp.cumsum(padded)
    padded_starts = padded_ends - padded
    n_rows = 2 * n_tok + N_EXPERTS * MOE_BLOCK
    n_blocks = n_rows // MOE_BLOCK
    expert_ids = jnp.arange(N_EXPERTS, dtype=jnp.int32)
    idx = rt[:, 0:6].astype(jnp.int32)
    start_of = jnp.sum(jnp.where(idx[:, 0:2, None] == expert_ids, padded_starts, 0), axis=-1)
    pos = start_of + idx[:, 4:6]
    block_start = jnp.arange(n_blocks, dtype=jnp.int32)[:, None] * MOE_BLOCK
    owns = (padded_starts <= block_start) & (block_start < padded_ends)
    n_valid = jnp.sum(jnp.where(owns, jnp.clip(counts - (block_start - padded_starts), 0, MOE_BLOCK), 0), axis=1)
    last_used = jnp.max(jnp.where(counts > 0, expert_ids, 0))
    block_expert = jnp.where(n_valid > 0, jnp.sum(jnp.where(owns, expert_ids, 0), axis=1), last_used)
    later_used = (expert_ids[None, :] > expert_ids[:, None]) & (counts[None, :] > 0)
    next_used = jnp.min(jnp.where(later_used, expert_ids[None, :], N_EXPERTS), axis=1)
    next_used = jnp.where(next_used == N_EXPERTS, -1, next_used)
    block_next = jnp.sum(jnp.where(owns, next_used, 0), axis=1)
    n_used = padded_ends[-1] // MOE_BLOCK
    block_index = jnp.minimum(jnp.arange(n_blocks, dtype=jnp.int32), n_used - 1)
    i32 = lambda a: a.astype(jnp.int32)
    return pos[:, 0], pos[:, 1], i32(block_expert), i32(n_valid), i32(block_index), i32(block_next), n_rows


def _dispatch_kernel(p0_ref, p1_ref, h_ref, zeros_hbm, xs_hbm, sem, *, tm):
    del zeros_hbm

    def issue(c, carry):
        for j in range(DMA_ISSUE_UNROLL):
            r = c * DMA_ISSUE_UNROLL + j
            src = h_ref.at[pl.ds(r, 1)]
            pltpu.make_async_copy(src, xs_hbm.at[pl.ds(p0_ref[0, r], 1)], sem).start(priority=0)
            pltpu.make_async_copy(src, xs_hbm.at[pl.ds(p1_ref[0, r], 1)], sem).start(priority=1)
        return carry

    lax.fori_loop(0, tm // DMA_ISSUE_UNROLL, issue, 0)
    for _ in range(2):
        pltpu.make_async_copy(h_ref, xs_hbm.at[pl.ds(0, tm)], sem).wait()


def _dispatch(pos0, pos1, h2p, n_rows):
    n, w = h2p.shape
    tm = 256
    idx_spec = pl.BlockSpec((None, 1, tm), lambda t: (t, 0, 0), memory_space=pltpu.SMEM)
    return pl.pallas_call(
        functools.partial(_dispatch_kernel, tm=tm),
        grid=(n // tm,),
        in_specs=[idx_spec, idx_spec, pl.BlockSpec((tm, w), lambda t: (t, 0)),
                  pl.BlockSpec(memory_space=pl.ANY)],
        out_specs=pl.BlockSpec(memory_space=pl.ANY),
        out_shape=jax.ShapeDtypeStruct((n_rows, w), h2p.dtype),
        input_output_aliases={3: 0},
        scratch_shapes=[pltpu.SemaphoreType.DMA(())],
        compiler_params=pltpu.CompilerParams(
            dimension_semantics=("arbitrary",), vmem_limit_bytes=VMEM_LIMIT,
            disable_bounds_checks=True),
        name="dispatch",
    )(pos0.reshape(n // tm, 1, tm), pos1.reshape(n // tm, 1, tm), h2p, jnp.zeros((n_rows, w), h2p.dtype))


def _moe_kernel(be_ref, nvalid_ref, bi_ref, nxt_ref, x_ref, wg_hbm, wu_hbm, wd_hbm, y_ref,
                wg_f, wu_f, wd_f, wg_b, wu_b, wd_b, sem):
    i = pl.program_id(0)
    n_valid = nvalid_ref[i]

    def fetch(e):
        return (pltpu.make_async_copy(wg_hbm.at[e], wg_f, sem.at[0]),
                pltpu.make_async_copy(wu_hbm.at[e], wu_f, sem.at[1]),
                pltpu.make_async_copy(wd_hbm.at[e], wd_f, sem.at[2]))

    @pl.when(n_valid > 0)
    def _():
        e = be_ref[i]

        @pl.when(i == 0)
        def _():
            for cp in fetch(e):
                cp.start()

        @pl.when((i == 0) | (e != be_ref[jnp.maximum(i - 1, 0)]))
        def _():
            for cp in fetch(e):
                cp.wait()
            wg_b[...] = wg_f[...].astype(BF16)
            wu_b[...] = wu_f[...].astype(BF16)
            wd_b[...] = wd_f[...].astype(BF16)
            nxt = nxt_ref[i]

            @pl.when(nxt >= 0)
            def _():
                for cp in fetch(nxt):
                    cp.start()

        lo, hi = _unpack_bf16_pair(x_ref[...])
        half = x_ref.shape[1]
        g = _dot(lo, wg_b[:half, :]) + _dot(hi, wg_b[half:, :])
        u = _dot(lo, wu_b[:half, :]) + _dot(hi, wu_b[half:, :])
        act = (g * jax.nn.sigmoid(g) * u).astype(BF16)
        y_ref[...] = _dot(act, wd_b[...])

    @pl.when(n_valid == 0)
    def _():
        y_ref[...] = jnp.zeros_like(y_ref)


def _moe(block_expert, n_valid, block_index, block_next, xs, w_gate, w_up, w_down):
    n_blocks = block_expert.shape[0]
    n_rows, half = xs.shape
    d = 2 * half
    hbm = pl.BlockSpec(memory_space=pl.ANY)
    grid_spec = pltpu.PrefetchScalarGridSpec(
        num_scalar_prefetch=4,
        grid=(n_blocks,),
        in_specs=[pl.BlockSpec((MOE_BLOCK, half), lambda i, be, nv, bi, nx: (bi[i], 0)), hbm, hbm, hbm],
        out_specs=pl.BlockSpec((MOE_BLOCK, d), lambda i, be, nv, bi, nx: (i, 0)),
        scratch_shapes=[
            pltpu.VMEM((d, D_EXPERT), F32),
            pltpu.VMEM((d, D_EXPERT), F32),
            pltpu.VMEM((D_EXPERT, d), F32),
            pltpu.VMEM((d, D_EXPERT), BF16),
            pltpu.VMEM((d, D_EXPERT), BF16),
            pltpu.VMEM((D_EXPERT, d), BF16),
            pltpu.SemaphoreType.DMA((3,)),
        ],
    )
    return pl.pallas_call(
        _moe_kernel,
        grid_spec=grid_spec,
        out_shape=jax.ShapeDtypeStruct((n_rows, d), F32),
        compiler_params=pltpu.CompilerParams(
            dimension_semantics=("arbitrary",), vmem_limit_bytes=VMEM_LIMIT),
        name="moe",
    )(block_expert, n_valid, block_index, block_next, xs, w_gate, w_up, w_down)


def _final_kernel(p0_ref, p1_ref, q0_ref, q1_ref, x1_ref, rt_ref, g2_ref, gf_ref, yb_hbm, o_ref,
                  y0buf, y1buf, sem, *, tm, n_tiles):
    t = pl.program_id(0)
    slot = t & 1

    def issue(pa_ref, pb_ref, s):
        def body(c, carry):
            for j in range(DMA_ISSUE_UNROLL):
                r = c * DMA_ISSUE_UNROLL + j
                pltpu.make_async_copy(yb_hbm.at[pl.ds(pa_ref[0, r], 1)], y0buf.at[s].at[pl.ds(r, 1)],
                                      sem.at[s]).start(priority=0)
                pltpu.make_async_copy(yb_hbm.at[pl.ds(pb_ref[0, r], 1)], y1buf.at[s].at[pl.ds(r, 1)],
                                      sem.at[s]).start(priority=1)
            return carry

        lax.fori_loop(0, tm // DMA_ISSUE_UNROLL, body, 0)

    @pl.when(t == 0)
    def _():
        issue(p0_ref, p1_ref, 0)

    @pl.when(t + 1 < n_tiles)
    def _():
        issue(q0_ref, q1_ref, 1 - slot)

    pltpu.make_async_copy(yb_hbm.at[pl.ds(0, tm)], y0buf.at[slot], sem.at[slot]).wait()
    pltpu.make_async_copy(yb_hbm.at[pl.ds(0, tm)], y1buf.at[slot], sem.at[slot]).wait()

    rt = rt_ref[...]
    y = rt[:, 2:3] * y0buf[slot] + rt[:, 3:4] * y1buf[slot]
    x2 = x1_ref[...] + g2_ref[...] * y
    ms = jnp.mean(x2 * x2, axis=-1, keepdims=True)
    o_ref[...] = x2 * lax.rsqrt(ms + EPS) * gf_ref[...]


def _final(pos0, pos1, x1, rt, g2, gf, yb, seq):
    n, d = x1.shape
    tm = 256
    n_tiles = n // tm
    per_seq = seq // tm
    this_tile = pl.BlockSpec((None, 1, tm), lambda t: (t, 0, 0), memory_space=pltpu.SMEM)
    next_tile = pl.BlockSpec((None, 1, tm), lambda t: (jnp.minimum(t + 1, n_tiles - 1), 0, 0),
                             memory_space=pltpu.SMEM)
    p0 = pos0.reshape(n_tiles, 1, tm)
    p1 = pos1.reshape(n_tiles, 1, tm)
    return pl.pallas_call(
        functools.partial(_final_kernel, tm=tm, n_tiles=n_tiles),
        grid=(n_tiles,),
        in_specs=[
            this_tile, this_tile, next_tile, next_tile,
            pl.BlockSpec((tm, d), lambda t: (t, 0)),
            pl.BlockSpec((tm, ROUTE_W), lambda t: (t, 0)),
            pl.BlockSpec((None, 1, d), lambda t: (t // per_seq, 0, 0)),
            pl.BlockSpec((1, d), lambda t: (0, 0)),
            pl.BlockSpec(memory_space=pl.ANY),
        ],
        out_specs=pl.BlockSpec((tm, d), lambda t: (t, 0)),
        out_shape=jax.ShapeDtypeStruct((n, d), F32),
        scratch_shapes=[
            pltpu.VMEM((2, tm, d), F32),
            pltpu.VMEM((2, tm, d), F32),
            pltpu.SemaphoreType.DMA((2,)),
        ],
        compiler_params=pltpu.CompilerParams(
            dimension_semantics=("arbitrary",), vmem_limit_bytes=VMEM_LIMIT,
            disable_bounds_checks=True),
        name="final",
    )(p0, p1, p0, p1, x1, rt, g2, gf, yb)


def kernel(x, c, ada_w, ada_b, norm1_g, w_in, conv3_w, conv4_w, conv4_b, lru_w_a, lru_b_a, lru_w_x, lru_b_x, lru_lambda, head_norm_conv_g, head_norm_lru_g, w_out, norm2_g, route_w_group, route_b_group, route_w_expert, route_b_expert, w_e_gate, w_e_up, w_e_down, final_norm_g):
    bsz, s, d = x.shape
    n_tok = bsz * s
    l = 0

    mod = _ada(c, ada_w[l], ada_b[l]).reshape(bsz, 6, 1, d)
    sh1, sc1, g1, sh2, sc2, g2 = (mod[:, j] for j in range(6))

    proj = _in_proj(x, sc1, sh1, norm1_g[l].reshape(1, d), w_in[l].astype(BF16))

    wax = jnp.concatenate([lru_w_a[l], lru_w_x[l]], axis=-1).astype(BF16)
    y = _mix(proj, conv3_w[l], conv4_w[l], conv4_b[l], wax, lru_b_a[l], lru_b_x[l], lru_lambda[l],
             head_norm_conv_g[l], head_norm_lru_g[l])

    w_route = jnp.concatenate(
        [route_w_group[l], jnp.transpose(route_w_expert[l], (1, 0, 2)).reshape(d, N_EXPERTS),
         jnp.zeros((d, ROUTE_W - N_GROUPS - N_EXPERTS), F32)], axis=1).astype(BF16)
    b_route = jnp.concatenate(
        [route_b_group[l], route_b_expert[l].reshape(-1),
         jnp.zeros((ROUTE_W - N_GROUPS - N_EXPERTS,), F32)]).reshape(1, ROUTE_W)
    x1, h2p, rt, counts = _out(y, x, g1, w_out[l].astype(BF16), sc2, sh2, norm2_g[l].reshape(1, d),
                               w_route, b_route)

    rt = rt.reshape(n_tok, ROUTE_W)
    pos0, pos1, block_expert, n_valid, block_index, block_next, n_rows = _plan(rt, counts, n_tok)
    xs = _dispatch(pos0, pos1, h2p.reshape(n_tok, d // 2), n_rows)
    yb = _moe(block_expert, n_valid, block_index, block_next, xs, w_e_gate[l], w_e_up[l], w_e_down[l])
    out = _final(pos0, pos1, x1.reshape(n_tok, d), rt, g2, final_norm_g.reshape(1, d), yb, s)
    return out.reshape(bsz, s, d)
```

```python
import functools

import jax
import jax.numpy as jnp
from jax import lax
from jax.experimental import pallas as pl
from jax.experimental.pallas import tpu as pltpu

D_MODEL = 2048
D_CONV = 1024
CONV_HEAD_DIM = 64
D_LRU = 1024
LRU_HEADS = 8
LRU_HEAD_DIM = 128
LRU_C = 8.0
D_MIX = D_CONV + D_LRU
D_IN = 3 * D_CONV + 2 * D_LRU
N_GROUPS = 8
EXPERTS_PER_GROUP = 8
N_EXPERTS = 64
D_EXPERT = 512
EPS = 1e-6

LANES = 128
SUBLANES = 8
ROUTE_W = LANES
MOE_BLOCK = 256
DMA_ISSUE_UNROLL = 8
VMEM_LIMIT = 56 * 1024 * 1024

F32 = jnp.float32
BF16 = jnp.bfloat16


def _dot(a, b):
    return jnp.dot(a, b, preferred_element_type=F32)


def _ada_kernel(c_ref, w_ref, b_ref, o_ref):
    c = c_ref[...]
    s = (c * jax.nn.sigmoid(c)).astype(BF16)
    o_ref[...] = _dot(s, w_ref[...].astype(BF16)) + b_ref[...]


def _ada(c, w, b):
    bsz, d = c.shape
    n = w.shape[1]
    tn = 1024
    return pl.pallas_call(
        _ada_kernel,
        grid=(n // tn,),
        in_specs=[
            pl.BlockSpec((bsz, d), lambda j: (0, 0)),
            pl.BlockSpec((d, tn), lambda j: (0, j)),
            pl.BlockSpec((1, tn), lambda j: (0, j)),
        ],
        out_specs=pl.BlockSpec((bsz, tn), lambda j: (0, j)),
        out_shape=jax.ShapeDtypeStruct((bsz, n), F32),
        compiler_params=pltpu.CompilerParams(
            dimension_semantics=("arbitrary",), vmem_limit_bytes=VMEM_LIMIT),
        name="ada",
    )(c, w, b.reshape(1, n))


def _modulated_norm(x, g, sc, sh):
    ms = jnp.mean(x * x, axis=-1, keepdims=True)
    return x * lax.rsqrt(ms + EPS) * (g * (1.0 + sc)) + sh


def _in_proj_kernel(x_ref, sc_ref, sh_ref, g_ref, w_ref, o_ref, *, n_chunk):
    hb = _modulated_norm(x_ref[...], g_ref[...], sc_ref[...], sh_ref[...]).astype(BF16)
    for j in range(D_IN // n_chunk):
        cols = slice(j * n_chunk, (j + 1) * n_chunk)
        o_ref[:, cols] = _dot(hb, w_ref[:, cols])


def _in_proj(x, sc, sh, g, w_bf):
    bsz, s, d = x.shape
    tm = 256
    per_batch = pl.BlockSpec((None, 1, d), lambda b, t: (b, 0, 0))
    return pl.pallas_call(
        functools.partial(_in_proj_kernel, n_chunk=1024),
        grid=(bsz, s // tm),
        in_specs=[
            pl.BlockSpec((None, tm, d), lambda b, t: (b, t, 0)),
            per_batch, per_batch,
            pl.BlockSpec((1, d), lambda b, t: (0, 0)),
            pl.BlockSpec((d, D_IN), lambda b, t: (0, 0), pipeline_mode=pl.Buffered(1)),
        ],
        out_specs=pl.BlockSpec((None, tm, D_IN), lambda b, t: (b, t, 0)),
        out_shape=jax.ShapeDtypeStruct((bsz, s, D_IN), F32),
        compiler_params=pltpu.CompilerParams(
            dimension_semantics=("arbitrary", "arbitrary"), vmem_limit_bytes=VMEM_LIMIT),
        name="in_proj",
    )(x, sc, sh, g, w_bf)


def _shift_rows(x, prev8, d, row):
    rolled = pltpu.roll(x, d, 0)
    head = jnp.where(row[:SUBLANES] >= d, rolled[:SUBLANES], pltpu.roll(prev8, d, 0))
    return jnp.concatenate([head, rolled[SUBLANES:]], axis=0)


def _head_rms(y, p_ref, g):
    sq = y * y
    hi = sq.astype(BF16)
    lo = (sq - hi.astype(F32)).astype(BF16)
    ms = _dot(hi, p_ref[...]) + _dot(lo, p_ref[...])
    return y * lax.rsqrt(ms + EPS) * g


def _linear_scan(a, v, h0, sub):
    groups = []
    for r0 in range(0, a.shape[0], SUBLANES):
        ag, vg = a[r0:r0 + SUBLANES], v[r0:r0 + SUBLANES]
        d = 1
        while d < SUBLANES:
            keep = sub >= d
            vg = vg + ag * jnp.where(keep, pltpu.roll(vg, d, 0), 0.0)
            ag = ag * jnp.where(keep, pltpu.roll(ag, d, 0), 1.0)
            d *= 2
        groups.append((ag, vg))
    out, carry = [], h0
    for ag, vg in groups:
        hg = vg + ag * carry
        carry = hg[SUBLANES - 1:, :]
        out.append(hg)
    return jnp.concatenate(out, axis=0), carry


def _gelu_tanh(x):
    c = 0.7978845608028654
    half_x = 0.5 * x
    return half_x + half_x * jnp.tanh(x * (c + (c * 0.044715) * (x * x)))


def _mix_kernel(p_ref, w3_ref, w4_ref, b4_ref, wax_ref, ba_ref, bx_ref, lam_ref, ga_ref, gb_ref,
                pa_ref, pb_ref, y_ref, ua_carry, xb_carry, h_carry, *, ts):
    @pl.when(pl.program_id(1) == 0)
    def _():
        ua_carry[...] = jnp.zeros_like(ua_carry)
        xb_carry[...] = jnp.zeros_like(xb_carry)
        h_carry[...] = jnp.zeros_like(h_carry)

    row = lax.broadcasted_iota(jnp.int32, (ts, LANES), 0)

    for k in range(D_CONV // LANES):
        cols = slice(k * LANES, (k + 1) * LANES)
        b_a = p_ref[:, k * LANES:(k + 1) * LANES]
        c_a = p_ref[:, D_CONV + k * LANES:D_CONV + (k + 1) * LANES]
        x_a = p_ref[:, 2 * D_CONV + k * LANES:2 * D_CONV + (k + 1) * LANES]
        u = c_a * x_a
        prev = ua_carry[:, cols]
        conv = (w3_ref[2:3, cols] * u
                + w3_ref[1:2, cols] * _shift_rows(u, prev, 1, row)
                + w3_ref[0:1, cols] * _shift_rows(u, prev, 2, row))
        ua_carry[:, cols] = u[ts - SUBLANES:, :]
        y_a = b_a * conv
        y_ref[:, cols] = _head_rms(y_a, pa_ref, ga_ref[:, cols]).astype(BF16)

        g_b = p_ref[:, 3 * D_CONV + k * LANES:3 * D_CONV + (k + 1) * LANES]
        x_b = p_ref[:, 3 * D_CONV + D_LRU + k * LANES:3 * D_CONV + D_LRU + (k + 1) * LANES]
        prevb = xb_carry[:, cols]
        xc = (w4_ref[3:4, cols] * x_b
              + w4_ref[2:3, cols] * _shift_rows(x_b, prevb, 1, row)
              + w4_ref[1:2, cols] * _shift_rows(x_b, prevb, 2, row)
              + w4_ref[0:1, cols] * _shift_rows(x_b, prevb, 3, row)
              + b4_ref[:, cols])
        xb_carry[:, cols] = x_b[ts - SUBLANES:, :]
        gates = _dot(xc.astype(BF16), wax_ref[k])
        r = jax.nn.sigmoid(gates[:, :LANES] + ba_ref[:, cols])
        i = jax.nn.sigmoid(gates[:, LANES:] + bx_ref[:, cols])
        nlam = -lam_ref[:, cols]
        softplus = jnp.maximum(nlam, 0.0) + jnp.log1p(jnp.exp(-jnp.abs(nlam)))
        log_a = r * ((-LRU_C) * softplus)
        a = jnp.exp(log_a)
        mult = jnp.sqrt(-jnp.tanh(log_a) * (a * a + 1.0))
        v = mult * i * xc
        hs, h_last = _linear_scan(a, v, h_carry[:, cols], row[:SUBLANES])
        h_carry[:, cols] = h_last
        y_b = hs * _gelu_tanh(g_b)
        y_ref[:, D_CONV + k * LANES:D_CONV + (k + 1) * LANES] = _head_rms(
            y_b, pb_ref, gb_ref[:, cols]).astype(BF16)


def _mix(proj, w3, w4, b4, wax_bf, b_a, b_x, lam, g_a, g_b):
    bsz, s, _ = proj.shape
    ts = 256
    lane = jnp.arange(LANES)
    p_a = ((lane[:, None] // CONV_HEAD_DIM) == (lane[None, :] // CONV_HEAD_DIM)).astype(BF16) / CONV_HEAD_DIM
    p_b = jnp.full((LANES, LANES), 1.0 / LRU_HEAD_DIM, BF16)
    full = lambda shape: pl.BlockSpec(shape, lambda b, t: (0,) * len(shape))
    row = lambda a: a.reshape(1, -1)
    return pl.pallas_call(
        functools.partial(_mix_kernel, ts=ts),
        grid=(bsz, s // ts),
        in_specs=[
            pl.BlockSpec((None, ts, D_IN), lambda b, t: (b, t, 0)),
            full((3, D_CONV)), full((4, D_LRU)), full((1, D_LRU)),
            full((LRU_HEADS, LRU_HEAD_DIM, 2 * LRU_HEAD_DIM)),
            full((1, D_LRU)), full((1, D_LRU)), full((1, D_LRU)),
            full((1, D_CONV)), full((1, D_LRU)),
            full((LANES, LANES)), full((LANES, LANES)),
        ],
        out_specs=pl.BlockSpec((None, ts, D_MIX), lambda b, t: (b, t, 0)),
        out_shape=jax.ShapeDtypeStruct((bsz, s, D_MIX), BF16),
        scratch_shapes=[
            pltpu.VMEM((SUBLANES, D_CONV), F32),
            pltpu.VMEM((SUBLANES, D_LRU), F32),
            pltpu.VMEM((1, D_LRU), F32),
        ],
        compiler_params=pltpu.CompilerParams(
            dimension_semantics=("arbitrary", "arbitrary"), vmem_limit_bytes=VMEM_LIMIT),
        name="mix",
    )(proj, w3, w4, row(b4), wax_bf, row(b_a), row(b_x), row(lam), row(g_a), row(g_b), p_a, p_b)


def _route(logits, tri, carry):
    lane = lax.broadcasted_iota(jnp.int32, logits.shape, 1)
    neg = -jnp.inf
    is_g = lane < N_GROUPS
    gl = jnp.where(is_g, logits, neg)
    ge = jnp.exp(gl - jnp.max(gl, axis=-1, keepdims=True))
    p_g = 1.0 / jnp.sum(ge, axis=-1, keepdims=True)
    g_idx = jnp.min(jnp.where(is_g & (ge >= 1.0), lane, LANES), axis=-1, keepdims=True)
    sel = (lane >= N_GROUPS) & (lane < N_GROUPS + N_EXPERTS) & (((lane - N_GROUPS) >> 3) == g_idx)
    el = jnp.where(sel, logits, neg)
    m1 = jnp.max(el, axis=-1, keepdims=True)
    i1 = jnp.min(jnp.where(sel & (el == m1), lane, LANES), axis=-1, keepdims=True)
    sel2 = sel & (lane != i1)
    el2 = jnp.where(sel2, logits, neg)
    m2 = jnp.max(el2, axis=-1, keepdims=True)
    i2 = jnp.min(jnp.where(sel2 & (el2 == m2), lane, LANES), axis=-1, keepdims=True)
    e2 = jnp.exp(m2 - m1)
    w1 = p_g * (1.0 / (1.0 + e2))
    w2 = p_g * (e2 / (1.0 + e2))
    out = jnp.where(lane == 0, (i1 - N_GROUPS).astype(F32), 0.0)
    out = jnp.where(lane == 1, (i2 - N_GROUPS).astype(F32), out)
    out = jnp.where(lane == 2, w1, out)
    out = jnp.where(lane == 3, w2, out)
    oh1 = lane == i1
    oh2 = lane == i2
    both = jnp.where(oh1 | oh2, 1.0, 0.0)
    before = _dot(tri, both.astype(BF16)) + carry
    out = jnp.where(lane == 4, jnp.sum(jnp.where(oh1, before, 0.0), axis=-1, keepdims=True), out)
    out = jnp.where(lane == 5, jnp.sum(jnp.where(oh2, before, 0.0), axis=-1, keepdims=True), out)
    return out, carry + jnp.sum(both, axis=0, keepdims=True)


def _pack_bf16_pair(lo, hi):
    lo_bits = lax.bitcast_convert_type(lo.astype(BF16).astype(F32), jnp.uint32)
    hi_bits = lax.bitcast_convert_type(hi.astype(BF16).astype(F32), jnp.uint32)
    return (lo_bits >> 16) | hi_bits


def _unpack_bf16_pair(w):
    lo = lax.bitcast_convert_type(w << 16, F32).astype(BF16)
    hi = lax.bitcast_convert_type(w & jnp.uint32(0xFFFF0000), F32).astype(BF16)
    return lo, hi


def _out_kernel(y_ref, x_ref, g1_ref, w_ref, sc_ref, sh_ref, g_ref, wr_ref, br_ref, tri_ref,
                x1_ref, h2p_ref, rt_ref, cnt_ref, carry):
    @pl.when((pl.program_id(0) == 0) & (pl.program_id(1) == 0))
    def _():
        carry[...] = jnp.zeros_like(carry)

    sub = tri_ref.shape[0]
    count = carry[...]
    tiles = [slice(r0, r0 + sub) for r0 in range(0, y_ref.shape[0], sub)]
    mixes = [_dot(y_ref[rows, :], w_ref[...]) for rows in tiles]
    for rows, mix in zip(tiles, mixes):
        x1 = x_ref[rows, :] + g1_ref[...] * mix
        x1_ref[rows, :] = x1
        h2 = _modulated_norm(x1, g_ref[...], sc_ref[...], sh_ref[...])
        half = h2.shape[1] // 2
        h2p_ref[rows, :] = _pack_bf16_pair(h2[:, :half], h2[:, half:])
        logits = _dot(h2.astype(BF16), wr_ref[...]) + br_ref[...]
        rt, count = _route(logits, tri_ref[...], count)
        rt_ref[rows, :] = rt
    carry[...] = count
    cnt_ref[...] = count


def _out(y, x, g1, w_bf, sc2, sh2, g, wr_bf, br):
    bsz, s, d = x.shape
    tm, sub = 512, 256
    per_batch = pl.BlockSpec((None, 1, d), lambda b, t: (b, 0, 0))
    tile = lambda w: pl.BlockSpec((None, tm, w), lambda b, t: (b, t, 0))
    const = lambda shape, **kw: pl.BlockSpec(shape, lambda b, t: (0, 0), **kw)
    return pl.pallas_call(
        _out_kernel,
        grid=(bsz, s // tm),
        in_specs=[
            tile(D_MIX), tile(d), per_batch,
            const((D_MIX, d), pipeline_mode=pl.Buffered(1)),
            per_batch, per_batch, const((1, d)),
            const((d, ROUTE_W)), const((1, ROUTE_W)), const((sub, sub)),
        ],
        out_specs=[tile(d), tile(d // 2), tile(ROUTE_W), const((1, ROUTE_W))],
        out_shape=[
            jax.ShapeDtypeStruct((bsz, s, d), F32),
            jax.ShapeDtypeStruct((bsz, s, d // 2), jnp.uint32),
            jax.ShapeDtypeStruct((bsz, s, ROUTE_W), F32),
            jax.ShapeDtypeStruct((1, ROUTE_W), F32),
        ],
        scratch_shapes=[pltpu.VMEM((1, ROUTE_W), F32)],
        compiler_params=pltpu.CompilerParams(
            dimension_semantics=("arbitrary", "arbitrary"), vmem_limit_bytes=VMEM_LIMIT),
        name="out_route",
    )(y, x, g1, w_bf, sc2, sh2, g, wr_bf, br, jnp.tril(jnp.ones((sub, sub), BF16), -1))


def _plan(rt, counts_f, n_tok):
    counts = counts_f[0, N_GROUPS:N_GROUPS + N_EXPERTS].astype(jnp.int32)
    padded = ((counts + MOE_BLOCK - 1) // MOE_BLOCK) * MOE_BLOCK
    padded_ends = jnp.cumsum(padded)
    padded_starts = padded_ends - padded
    n_rows = 2 * n_tok + N_EXPERTS * MOE_BLOCK
    n_blocks = n_rows // MOE_BLOCK
    expert_ids = jnp.arange(N_EXPERTS, dtype=jnp.int32)
    idx = rt[:, 0:6].astype(jnp.int32)
    start_of = jnp.sum(jnp.where(idx[:, 0:2, None] == expert_ids, padded_starts, 0), axis=-1)
    pos = start_of + idx[:, 4:6]
    block_start = jnp.arange(n_blocks, dtype=jnp.int32)[:, None] * MOE_BLOCK
    owns = (padded_starts <= block_start) & (block_start < padded_ends)
    n_valid = jnp.sum(jnp.where(owns, jnp.clip(counts - (block_start - padded_starts), 0, MOE_BLOCK), 0), axis=1)
    last_used = jnp.max(jnp.where(counts > 0, expert_ids, 0))
    block_expert = jnp.where(n_valid > 0, jnp.sum(jnp.where(owns, expert_ids, 0), axis=1), last_used)
    later_used = (expert_ids[None, :] > expert_ids[:, None]) & (counts[None, :] > 0)
    next_used = jnp.min(jnp.where(later_used, expert_ids[None, :], N_EXPERTS), axis=1)
    next_used = jnp.where(next_used == N_EXPERTS, -1, next_used)
    block_next = jnp.sum(jnp.where(owns, next_used, 0), axis=1)
    n_used = padded_ends[-1] // MOE_BLOCK
    block_index = jnp.minimum(jnp.arange(n_blocks, dtype=jnp.int32), n_used - 1)
    i32 = lambda a: a.astype(jnp.int32)
    return pos[:, 0], pos[:, 1], i32(block_expert), i32(n_valid), i32(block_index), i32(block_next), n_rows


def _dispatch_kernel(p0_ref, p1_ref, h_ref, zeros_hbm, xs_hbm, sem, *, tm):
    del zeros_hbm

    def issue(c, carry):
        for j in range(DMA_ISSUE_UNROLL):
            r = c * DMA_ISSUE_UNROLL + j
            src = h_ref.at[pl.ds(r, 1)]
            pltpu.make_async_copy(src, xs_hbm.at[pl.ds(p0_ref[0, r], 1)], sem).start(priority=0)
            pltpu.make_async_copy(src, xs_hbm.at[pl.ds(p1_ref[0, r], 1)], sem).start(priority=1)
        return carry

    lax.fori_loop(0, tm // DMA_ISSUE_UNROLL, issue, 0)
    for _ in range(2):
        pltpu.make_async_copy(h_ref, xs_hbm.at[pl.ds(0, tm)], sem).wait()


def _dispatch(pos0, pos1, h2p, n_rows):
    n, w = h2p.shape
    tm = 256
    idx_spec = pl.BlockSpec((None, 1, tm), lambda t: (t, 0, 0), memory_space=pltpu.SMEM)
    return pl.pallas_call(
        functools.partial(_dispatch_kernel, tm=tm),
        grid=(n // tm,),
        in_specs=[idx_spec, idx_spec, pl.BlockSpec((tm, w), lambda t: (t, 0)),
                  pl.BlockSpec(memory_space=pl.ANY)],
        out_specs=pl.BlockSpec(memory_space=pl.ANY),
        out_shape=jax.ShapeDtypeStruct((n_rows, w), h2p.dtype),
        input_output_aliases={3: 0},
        scratch_shapes=[pltpu.SemaphoreType.DMA(())],
        compiler_params=pltpu.CompilerParams(
            dimension_semantics=("arbitrary",), vmem_limit_bytes=VMEM_LIMIT,
            disable_bounds_checks=True),
        name="dispatch",
    )(pos0.reshape(n // tm, 1, tm), pos1.reshape(n // tm, 1, tm), h2p, jnp.zeros((n_rows, w), h2p.dtype))


def _moe_kernel(be_ref, nvalid_ref, bi_ref, nxt_ref, x_ref, wg_hbm, wu_hbm, wd_hbm, y_ref,
                wg_f, wu_f, wd_f, wg_b, wu_b, wd_b, sem):
    i = pl.program_id(0)
    n_valid = nvalid_ref[i]

    def fetch(e):
        return (pltpu.make_async_copy(wg_hbm.at[e], wg_f, sem.at[0]),
                pltpu.make_async_copy(wu_hbm.at[e], wu_f, sem.at[1]),
                pltpu.make_async_copy(wd_hbm.at[e], wd_f, sem.at[2]))

    @pl.when(n_valid > 0)
    def _():
        e = be_ref[i]

        @pl.when(i == 0)
        def _():
            for cp in fetch(e):
                cp.start()

        @pl.when((i == 0) | (e != be_ref[jnp.maximum(i - 1, 0)]))
        def _():
            for cp in fetch(e):
                cp.wait()
            wg_b[...] = wg_f[...].astype(BF16)
            wu_b[...] = wu_f[...].astype(BF16)
            wd_b[...] = wd_f[...].astype(BF16)
            nxt = nxt_ref[i]

            @pl.when(nxt >= 0)
            def _():
                for cp in fetch(nxt):
                    cp.start()

        lo, hi = _unpack_bf16_pair(x_ref[...])
        half = x_ref.shape[1]
        g = _dot(lo, wg_b[:half, :]) + _dot(hi, wg_b[half:, :])
        u = _dot(lo, wu_b[:half, :]) + _dot(hi, wu_b[half:, :])
        act = (g * jax.nn.sigmoid(g) * u).astype(BF16)
        y_ref[...] = _dot(act, wd_b[...])

    @pl.when(n_valid == 0)
    def _():
        y_ref[...] = jnp.zeros_like(y_ref)


def _moe(block_expert, n_valid, block_index, block_next, xs, w_gate, w_up, w_down):
    n_blocks = block_expert.shape[0]
    n_rows, half = xs.shape
    d = 2 * half
    hbm = pl.BlockSpec(memory_space=pl.ANY)
    grid_spec = pltpu.PrefetchScalarGridSpec(
        num_scalar_prefetch=4,
        grid=(n_blocks,),
        in_specs=[pl.BlockSpec((MOE_BLOCK, half), lambda i, be, nv, bi, nx: (bi[i], 0)), hbm, hbm, hbm],
        out_specs=pl.BlockSpec((MOE_BLOCK, d), lambda i, be, nv, bi, nx: (i, 0)),
        scratch_shapes=[
            pltpu.VMEM((d, D_EXPERT), F32),
            pltpu.VMEM((d, D_EXPERT), F32),
            pltpu.VMEM((D_EXPERT, d), F32),
            pltpu.VMEM((d, D_EXPERT), BF16),
            pltpu.VMEM((d, D_EXPERT), BF16),
            pltpu.VMEM((D_EXPERT, d), BF16),
            pltpu.SemaphoreType.DMA((3,)),
        ],
    )
    return pl.pallas_call(
        _moe_kernel,
        grid_spec=grid_spec,
        out_shape=jax.ShapeDtypeStruct((n_rows, d), F32),
        compiler_params=pltpu.CompilerParams(
            dimension_semantics=("arbitrary",), vmem_limit_bytes=VMEM_LIMIT),
        name="moe",
    )(block_expert, n_valid, block_index, block_next, xs, w_gate, w_up, w_down)


def _final_kernel(p0_ref, p1_ref, q0_ref, q1_ref, x1_ref, rt_ref, g2_ref, gf_ref, yb_hbm, o_ref,
                  y0buf, y1buf, sem, *, tm, n_tiles):
    t = pl.program_id(0)
    slot = t & 1

    def issue(pa_ref, pb_ref, s):
        def body(c, carry):
            for j in range(DMA_ISSUE_UNROLL):
                r = c * DMA_ISSUE_UNROLL + j
                pltpu.make_async_copy(yb_hbm.at[pl.ds(pa_ref[0, r], 1)], y0buf.at[s].at[pl.ds(r, 1)],
                                      sem.at[s]).start(priority=0)
                pltpu.make_async_copy(yb_hbm.at[pl.ds(pb_ref[0, r], 1)], y1buf.at[s].at[pl.ds(r, 1)],
                                      sem.at[s]).start(priority=1)
            return carry

        lax.fori_loop(0, tm // DMA_ISSUE_UNROLL, body, 0)

    @pl.when(t == 0)
    def _():
        issue(p0_ref, p1_ref, 0)

    @pl.when(t + 1 < n_tiles)
    def _():
        issue(q0_ref, q1_ref, 1 - slot)

    pltpu.make_async_copy(yb_hbm.at[pl.ds(0, tm)], y0buf.at[slot], sem.at[slot]).wait()
    pltpu.make_async_copy(yb_hbm.at[pl.ds(0, tm)], y1buf.at[slot], sem.at[slot]).wait()

    rt = rt_ref[...]
    y = rt[:, 2:3] * y0buf[slot] + rt[:, 3:4] * y1buf[slot]
    x2 = x1_ref[...] + g2_ref[...] * y
    ms = jnp.mean(x2 * x2, axis=-1, keepdims=True)
    o_ref[...] = x2 * lax.rsqrt(ms + EPS) * gf_ref[...]


def _final(pos0, pos1, x1, rt, g2, gf, yb, seq):
    n, d = x1.shape
    tm = 256
    n_tiles = n // tm
    per_seq = seq // tm
    this_tile = pl.BlockSpec((None, 1, tm), lambda t: (t, 0, 0), memory_space=pltpu.SMEM)
    next_tile = pl.BlockSpec((None, 1, tm), lambda t: (jnp.minimum(t + 1, n_tiles - 1), 0, 0),
                             memory_space=pltpu.SMEM)
    p0 = pos0.reshape(n_tiles, 1, tm)
    p1 = pos1.reshape(n_tiles, 1, tm)
    return pl.pallas_call(
        functools.partial(_final_kernel, tm=tm, n_tiles=n_tiles),
        grid=(n_tiles,),
        in_specs=[
            this_tile, this_tile, next_tile, next_tile,
            pl.BlockSpec((tm, d), lambda t: (t, 0)),
            pl.BlockSpec((tm, ROUTE_W), lambda t: (t, 0)),
            pl.BlockSpec((None, 1, d), lambda t: (t // per_seq, 0, 0)),
            pl.BlockSpec((1, d), lambda t: (0, 0)),
            pl.BlockSpec(memory_space=pl.ANY),
        ],
        out_specs=pl.BlockSpec((tm, d), lambda t: (t, 0)),
        out_shape=jax.ShapeDtypeStruct((n, d), F32),
        scratch_shapes=[
            pltpu.VMEM((2, tm, d), F32),
            pltpu.VMEM((2, tm, d), F32),
            pltpu.SemaphoreType.DMA((2,)),
        ],
        compiler_params=pltpu.CompilerParams(
            dimension_semantics=("arbitrary",), vmem_limit_bytes=VMEM_LIMIT,
            disable_bounds_checks=True),
        name="final",
    )(p0, p1, p0, p1, x1, rt, g2, gf, yb)


def kernel(x, c, ada_w, ada_b, norm1_g, w_in, conv3_w, conv4_w, conv4_b, lru_w_a, lru_b_a, lru_w_x, lru_b_x, lru_lambda, head_norm_conv_g, head_norm_lru_g, w_out, norm2_g, route_w_group, route_b_group, route_w_expert, route_b_expert, w_e_gate, w_e_up, w_e_down, final_norm_g):
    bsz, s, d = x.shape
    n_tok = bsz * s
    l = 0

    mod = _ada(c, ada_w[l], ada_b[l]).reshape(bsz, 6, 1, d)
    sh1, sc1, g1, sh2, sc2, g2 = (mod[:, j] for j in range(6))

    proj = _in_proj(x, sc1, sh1, norm1_g[l].reshape(1, d), w_in[l].astype(BF16))

    wax = jnp.concatenate([lru_w_a[l], lru_w_x[l]], axis=-1).astype(BF16)
    y = _mix(proj, conv3_w[l], conv4_w[l], conv4_b[l], wax, lru_b_a[l], lru_b_x[l], lru_lambda[l],
             head_norm_conv_g[l], head_norm_lru_g[l])

    w_route = jnp.concatenate(
        [route_w_group[l], jnp.transpose(route_w_expert[l], (1, 0, 2)).reshape(d, N_EXPERTS),
         jnp.zeros((d, ROUTE_W - N_GROUPS - N_EXPERTS), F32)], axis=1).astype(BF16)
    b_route = jnp.concatenate(
        [route_b_group[l], route_b_expert[l].reshape(-1),
         jnp.zeros((ROUTE_W - N_GROUPS - N_EXPERTS,), F32)]).reshape(1, ROUTE_W)
    x1, h2p, rt, counts = _out(y, x, g1, w_out[l].astype(BF16), sc2, sh2, norm2_g[l].reshape(1, d),
                               w_route, b_route)

    rt = rt.reshape(n_tok, ROUTE_W)
    pos0, pos1, block_expert, n_valid, block_index, block_next, n_rows = _plan(rt, counts, n_tok)
    xs = _dispatch(pos0, pos1, h2p.reshape(n_tok, d // 2), n_rows)
    yb = _moe(block_expert, n_valid, block_index, block_next, xs, w_e_gate[l], w_e_up[l], w_e_down[l])
    out = _final(pos0, pos1, x1.reshape(n_tok, d), rt, g2, final_norm_g.reshape(1, d), yb, s)
    return out.reshape(bsz, s, d)
```

```python
import functools

import jax
import jax.numpy as jnp
from jax import lax
from jax.experimental import pallas as pl
from jax.experimental.pallas import tpu as pltpu

D_MODEL = 2048
D_CONV = 1024
CONV_HEAD_DIM = 64
D_LRU = 1024
LRU_HEADS = 8
LRU_HEAD_DIM = 128
LRU_C = 8.0
D_MIX = D_CONV + D_LRU
D_IN = 3 * D_CONV + 2 * D_LRU
N_GROUPS = 8
EXPERTS_PER_GROUP = 8
N_EXPERTS = 64
D_EXPERT = 512
EPS = 1e-6

LANES = 128
SUBLANES = 8
ROUTE_W = LANES
MOE_BLOCK = 256
DMA_ISSUE_UNROLL = 8
VMEM_LIMIT = 56 * 1024 * 1024

F32 = jnp.float32
BF16 = jnp.bfloat16


def _dot(a, b):
    return jnp.dot(a, b, preferred_element_type=F32)


def _ada_kernel(c_ref, w_ref, b_ref, o_ref):
    c = c_ref[...]
    s = (c * jax.nn.sigmoid(c)).astype(BF16)
    o_ref[...] = _dot(s, w_ref[...].astype(BF16)) + b_ref[...]


def _ada(c, w, b):
    bsz, d = c.shape
    n = w.shape[1]
    tn = 1024
    return pl.pallas_call(
        _ada_kernel,
        grid=(n // tn,),
        in_specs=[
            pl.BlockSpec((bsz, d), lambda j: (0, 0)),
            pl.BlockSpec((d, tn), lambda j: (0, j)),
            pl.BlockSpec((1, tn), lambda j: (0, j)),
        ],
        out_specs=pl.BlockSpec((bsz, tn), lambda j: (0, j)),
        out_shape=jax.ShapeDtypeStruct((bsz, n), F32),
        compiler_params=pltpu.CompilerParams(
            dimension_semantics=("arbitrary",), vmem_limit_bytes=VMEM_LIMIT),
        name="ada",
    )(c, w, b.reshape(1, n))


def _modulated_norm(x, g, sc, sh):
    ms = jnp.mean(x * x, axis=-1, keepdims=True)
    return x * lax.rsqrt(ms + EPS) * (g * (1.0 + sc)) + sh


def _in_proj_kernel(x_ref, sc_ref, sh_ref, g_ref, w_ref, o_ref, *, n_chunk):
    hb = _modulated_norm(x_ref[...], g_ref[...], sc_ref[...], sh_ref[...]).astype(BF16)
    for j in range(D_IN // n_chunk):
        cols = slice(j * n_chunk, (j + 1) * n_chunk)
        o_ref[:, cols] = _dot(hb, w_ref[:, cols])


def _in_proj(x, sc, sh, g, w_bf):
    bsz, s, d = x.shape
    tm = 256
    per_batch = pl.BlockSpec((None, 1, d), lambda b, t: (b, 0, 0))
    return pl.pallas_call(
        functools.partial(_in_proj_kernel, n_chunk=1024),
        grid=(bsz, s // tm),
        in_specs=[
            pl.BlockSpec((None, tm, d), lambda b, t: (b, t, 0)),
            per_batch, per_batch,
            pl.BlockSpec((1, d), lambda b, t: (0, 0)),
            pl.BlockSpec((d, D_IN), lambda b, t: (0, 0), pipeline_mode=pl.Buffered(1)),
        ],
        out_specs=pl.BlockSpec((None, tm, D_IN), lambda b, t: (b, t, 0)),
        out_shape=jax.ShapeDtypeStruct((bsz, s, D_IN), F32),
        compiler_params=pltpu.CompilerParams(
            dimension_semantics=("arbitrary", "arbitrary"), vmem_limit_bytes=VMEM_LIMIT),
        name="in_proj",
    )(x, sc, sh, g, w_bf)


def _shift_rows(x, prev8, d, row):
    rolled = pltpu.roll(x, d, 0)
    head = jnp.where(row[:SUBLANES] >= d, rolled[:SUBLANES], pltpu.roll(prev8, d, 0))
    return jnp.concatenate([head, rolled[SUBLANES:]], axis=0)


def _head_rms(y, p_ref, g):
    sq = y * y
    hi = sq.astype(BF16)
    lo = (sq - hi.astype(F32)).astype(BF16)
    ms = _dot(hi, p_ref[...]) + _dot(lo, p_ref[...])
    return y * lax.rsqrt(ms + EPS) * g


def _linear_scan(a, v, h0, sub):
    groups = []
    for r0 in range(0, a.shape[0], SUBLANES):
        ag, vg = a[r0:r0 + SUBLANES], v[r0:r0 + SUBLANES]
        d = 1
        while d < SUBLANES:
            keep = sub >= d
            vg = vg + ag * jnp.where(keep, pltpu.roll(vg, d, 0), 0.0)
            ag = ag * jnp.where(keep, pltpu.roll(ag, d, 0), 1.0)
            d *= 2
        groups.append((ag, vg))
    out, carry = [], h0
    for ag, vg in groups:
        hg = vg + ag * carry
        carry = hg[SUBLANES - 1:, :]
        out.append(hg)
    return jnp.concatenate(out, axis=0), carry


def _gelu_tanh(x):
    c = 0.7978845608028654
    half_x = 0.5 * x
    return half_x + half_x * jnp.tanh(x * (c + (c * 0.044715) * (x * x)))


def _mix_kernel(p_ref, w3_ref, w4_ref, b4_ref, wax_ref, ba_ref, bx_ref, lam_ref, ga_ref, gb_ref,
                pa_ref, pb_ref, y_ref, ua_carry, xb_carry, h_carry, *, ts):
    @pl.when(pl.program_id(1) == 0)
    def _():
        ua_carry[...] = jnp.zeros_like(ua_carry)
        xb_carry[...] = jnp.zeros_like(xb_carry)
        h_carry[...] = jnp.zeros_like(h_carry)

    row = lax.broadcasted_iota(jnp.int32, (ts, LANES), 0)

    for k in range(D_CONV // LANES):
        cols = slice(k * LANES, (k + 1) * LANES)
        b_a = p_ref[:, k * LANES:(k + 1) * LANES]
        c_a = p_ref[:, D_CONV + k * LANES:D_CONV + (k + 1) * LANES]
        x_a = p_ref[:, 2 * D_CONV + k * LANES:2 * D_CONV + (k + 1) * LANES]
        u = c_a * x_a
        prev = ua_carry[:, cols]
        conv = (w3_ref[2:3, cols] * u
                + w3_ref[1:2, cols] * _shift_rows(u, prev, 1, row)
                + w3_ref[0:1, cols] * _shift_rows(u, prev, 2, row))
        ua_carry[:, cols] = u[ts - SUBLANES:, :]
        y_a = b_a * conv
        y_ref[:, cols] = _head_rms(y_a, pa_ref, ga_ref[:, cols]).astype(BF16)

        g_b = p_ref[:, 3 * D_CONV + k * LANES:3 * D_CONV + (k + 1) * LANES]
        x_b = p_ref[:, 3 * D_CONV + D_LRU + k * LANES:3 * D_CONV + D_LRU + (k + 1) * LANES]
        prevb = xb_carry[:, cols]
        xc = (w4_ref[3:4, cols] * x_b
              + w4_ref[2:3, cols] * _shift_rows(x_b, prevb, 1, row)
              + w4_ref[1:2, cols] * _shift_rows(x_b, prevb, 2, row)
              + w4_ref[0:1, cols] * _shift_rows(x_b, prevb, 3, row)
              + b4_ref[:, cols])
        xb_carry[:, cols] = x_b[ts - SUBLANES:, :]
        gates = _dot(xc.astype(BF16), wax_ref[k])
        r = jax.nn.sigmoid(gates[:, :LANES] + ba_ref[:, cols])
        i = jax.nn.sigmoid(gates[:, LANES:] + bx_ref[:, cols])
        nlam = -lam_ref[:, cols]
        softplus = jnp.maximum(nlam, 0.0) + jnp.log1p(jnp.exp(-jnp.abs(nlam)))
        log_a = r * ((-LRU_C) * softplus)
        a = jnp.exp(log_a)
        mult = jnp.sqrt(-jnp.tanh(log_a) * (a * a + 1.0))
        v = mult * i * xc
        hs, h_last = _linear_scan(a, v, h_carry[:, cols], row[:SUBLANES])
        h_carry[:, cols] = h_last
        y_b = hs * _gelu_tanh(g_b)
        y_ref[:, D_CONV + k * LANES:D_CONV + (k + 1) * LANES] = _head_rms(
            y_b, pb_ref, gb_ref[:, cols]).astype(BF16)


def _mix(proj, w3, w4, b4, wax_bf, b_a, b_x, lam, g_a, g_b):
    bsz, s, _ = proj.shape
    ts = 256
    lane = jnp.arange(LANES)
    p_a = ((lane[:, None] // CONV_HEAD_DIM) == (lane[None, :] // CONV_HEAD_DIM)).astype(BF16) / CONV_HEAD_DIM
    p_b = jnp.full((LANES, LANES), 1.0 / LRU_HEAD_DIM, BF16)
    full = lambda shape: pl.BlockSpec(shape, lambda b, t: (0,) * len(shape))
    row = lambda a: a.reshape(1, -1)
    return pl.pallas_call(
        functools.partial(_mix_kernel, ts=ts),
        grid=(bsz, s // ts),
        in_specs=[
            pl.BlockSpec((None, ts, D_IN), lambda b, t: (b, t, 0)),
            full((3, D_CONV)), full((4, D_LRU)), full((1, D_LRU)),
            full((LRU_HEADS, LRU_HEAD_DIM, 2 * LRU_HEAD_DIM)),
            full((1, D_LRU)), full((1, D_LRU)), full((1, D_LRU)),
            full((1, D_CONV)), full((1, D_LRU)),
            full((LANES, LANES)), full((LANES, LANES)),
        ],
        out_specs=pl.BlockSpec((None, ts, D_MIX), lambda b, t: (b, t, 0)),
        out_shape=jax.ShapeDtypeStruct((bsz, s, D_MIX), BF16),
        scratch_shapes=[
            pltpu.VMEM((SUBLANES, D_CONV), F32),
            pltpu.VMEM((SUBLANES, D_LRU), F32),
            pltpu.VMEM((1, D_LRU), F32),
        ],
        compiler_params=pltpu.CompilerParams(
            dimension_semantics=("arbitrary", "arbitrary"), vmem_limit_bytes=VMEM_LIMIT),
        name="mix",
    )(proj, w3, w4, row(b4), wax_bf, row(b_a), row(b_x), row(lam), row(g_a), row(g_b), p_a, p_b)


def _route(logits, tri, carry):
    lane = lax.broadcasted_iota(jnp.int32, logits.shape, 1)
    neg = -jnp.inf
    is_g = lane < N_GROUPS
    gl = jnp.where(is_g, logits, neg)
    ge = jnp.exp(gl - jnp.max(gl, axis=-1, keepdims=True))
    p_g = 1.0 / jnp.sum(ge, axis=-1, keepdims=True)
    g_idx = jnp.min(jnp.where(is_g & (ge >= 1.0), lane, LANES), axis=-1, keepdims=True)
    sel = (lane >= N_GROUPS) & (lane < N_GROUPS + N_EXPERTS) & (((lane - N_GROUPS) >> 3) == g_idx)
    el = jnp.where(sel, logits, neg)
    m1 = jnp.max(el, axis=-1, keepdims=True)
    i1 = jnp.min(jnp.where(sel & (el == m1), lane, LANES), axis=-1, keepdims=True)
    sel2 = sel & (lane != i1)
    el2 = jnp.where(sel2, logits, neg)
    m2 = jnp.max(el2, axis=-1, keepdims=True)
    i2 = jnp.min(jnp.where(sel2 & (el2 == m2), lane, LANES), axis=-1, keepdims=True)
    e2 = jnp.exp(m2 - m1)
    w1 = p_g * (1.0 / (1.0 + e2))
    w2 = p_g * (e2 / (1.0 + e2))
    out = jnp.where(lane == 0, (i1 - N_GROUPS).astype(F32), 0.0)
    out = jnp.where(lane == 1, (i2 - N_GROUPS).astype(F32), out)
    out = jnp.where(lane == 2, w1, out)
    out = jnp.where(lane == 3, w2, out)
    oh1 = lane == i1
    oh2 = lane == i2
    both = jnp.where(oh1 | oh2, 1.0, 0.0)
    before = _dot(tri, both.astype(BF16)) + carry
    out = jnp.where(lane == 4, jnp.sum(jnp.where(oh1, before, 0.0), axis=-1, keepdims=True), out)
    out = jnp.where(lane == 5, jnp.sum(jnp.where(oh2, before, 0.0), axis=-1, keepdims=True), out)
    return out, carry + jnp.sum(both, axis=0, keepdims=True)


def _pack_bf16_pair(lo, hi):
    lo_bits = lax.bitcast_convert_type(lo.astype(BF16).astype(F32), jnp.uint32)
    hi_bits = lax.bitcast_convert_type(hi.astype(BF16).astype(F32), jnp.uint32)
    return (lo_bits >> 16) | hi_bits


def _unpack_bf16_pair(w):
    lo = lax.bitcast_convert_type(w << 16, F32).astype(BF16)
    hi = lax.bitcast_convert_type(w & jnp.uint32(0xFFFF0000), F32).astype(BF16)
    return lo, hi


def _out_kernel(y_ref, x_ref, g1_ref, w_ref, sc_ref, sh_ref, g_ref, wr_ref, br_ref, tri_ref,
                x1_ref, h2p_ref, rt_ref, cnt_ref, carry):
    @pl.when((pl.program_id(0) == 0) & (pl.program_id(1) == 0))
    def _():
        carry[...] = jnp.zeros_like(carry)

    sub = tri_ref.shape[0]
    count = carry[...]
    tiles = [slice(r0, r0 + sub) for r0 in range(0, y_ref.shape[0], sub)]
    mixes = [_dot(y_ref[rows, :], w_ref[...]) for rows in tiles]
    for rows, mix in zip(tiles, mixes):
        x1 = x_ref[rows, :] + g1_ref[...] * mix
        x1_ref[rows, :] = x1
        h2 = _modulated_norm(x1, g_ref[...], sc_ref[...], sh_ref[...])
        half = h2.shape[1] // 2
        words = _pack_bf16_pair(h2[:, :half], h2[:, half:])
        for c in range(half // LANES):
            h2p_ref[pl.ds(rows.start * SUBLANES + c, sub, stride=SUBLANES), :] = words[:, c * LANES:(c + 1) * LANES]
        logits = _dot(h2.astype(BF16), wr_ref[...]) + br_ref[...]
        rt, count = _route(logits, tri_ref[...], count)
        rt_ref[rows, :] = rt
    carry[...] = count
    cnt_ref[...] = count


def _out(y, x, g1, w_bf, sc2, sh2, g, wr_bf, br):
    bsz, s, d = x.shape
    tm, sub = 512, 256
    per_batch = pl.BlockSpec((None, 1, d), lambda b, t: (b, 0, 0))
    tile = lambda w: pl.BlockSpec((None, tm, w), lambda b, t: (b, t, 0))
    const = lambda shape, **kw: pl.BlockSpec(shape, lambda b, t: (0, 0), **kw)
    return pl.pallas_call(
        _out_kernel,
        grid=(bsz, s // tm),
        in_specs=[
            tile(D_MIX), tile(d), per_batch,
            const((D_MIX, d), pipeline_mode=pl.Buffered(1)),
            per_batch, per_batch, const((1, d)),
            const((d, ROUTE_W)), const((1, ROUTE_W)), const((sub, sub)),
        ],
        out_specs=[tile(d), pl.BlockSpec((None, tm * SUBLANES, LANES), lambda b, t: (b, t, 0)),
                   tile(ROUTE_W), const((1, ROUTE_W))],
        out_shape=[
            jax.ShapeDtypeStruct((bsz, s, d), F32),
            jax.ShapeDtypeStruct((bsz, s * SUBLANES, LANES), jnp.uint32),
            jax.ShapeDtypeStruct((bsz, s, ROUTE_W), F32),
            jax.ShapeDtypeStruct((1, ROUTE_W), F32),
        ],
        scratch_shapes=[pltpu.VMEM((1, ROUTE_W), F32)],
        compiler_params=pltpu.CompilerParams(
            dimension_semantics=("arbitrary", "arbitrary"), vmem_limit_bytes=VMEM_LIMIT),
        name="out_route",
    )(y, x, g1, w_bf, sc2, sh2, g, wr_bf, br, jnp.tril(jnp.ones((sub, sub), BF16), -1))


def _plan(rt, counts_f, n_tok):
    counts = counts_f[0, N_GROUPS:N_GROUPS + N_EXPERTS].astype(jnp.int32)
    padded = ((counts + MOE_BLOCK - 1) // MOE_BLOCK) * MOE_BLOCK
    padded_ends = jnp.cumsum(padded)
    padded_starts = padded_ends - padded
    n_rows = 2 * n_tok + N_EXPERTS * MOE_BLOCK
    n_blocks = n_rows // MOE_BLOCK
    expert_ids = jnp.arange(N_EXPERTS, dtype=jnp.int32)
    idx = rt[:, 0:6].astype(jnp.int32)
    start_of = jnp.sum(jnp.where(idx[:, 0:2, None] == expert_ids, padded_starts, 0), axis=-1)
    pos = start_of + idx[:, 4:6]
    block_start = jnp.arange(n_blocks, dtype=jnp.int32)[:, None] * MOE_BLOCK
    owns = (padded_starts <= block_start) & (block_start < padded_ends)
    n_valid = jnp.sum(jnp.where(owns, jnp.clip(counts - (block_start - padded_starts), 0, MOE_BLOCK), 0), axis=1)
    last_used = jnp.max(jnp.where(counts > 0, expert_ids, 0))
    block_expert = jnp.where(n_valid > 0, jnp.sum(jnp.where(owns, expert_ids, 0), axis=1), last_used)
    later_used = (expert_ids[None, :] > expert_ids[:, None]) & (counts[None, :] > 0)
    next_used = jnp.min(jnp.where(later_used, expert_ids[None, :], N_EXPERTS), axis=1)
    next_used = jnp.where(next_used == N_EXPERTS, -1, next_used)
    block_next = jnp.sum(jnp.where(owns, next_used, 0), axis=1)
    n_used = padded_ends[-1] // MOE_BLOCK
    block_index = jnp.minimum(jnp.arange(n_blocks, dtype=jnp.int32), n_used - 1)
    i32 = lambda a: a.astype(jnp.int32)
    return pos[:, 0], pos[:, 1], i32(block_expert), i32(n_valid), i32(block_index), i32(block_next), n_rows


def _dispatch_kernel(nvalid_ref, p0_ref, p1_ref, h_ref, xs_hbm, zbuf, sem, zsem, *, tm, n_blocks):
    @pl.when(pl.program_id(0) == 0)
    def _():
        zbuf[...] = jnp.zeros_like(zbuf)

        def zero_block(i):
            return pltpu.make_async_copy(zbuf, xs_hbm.at[pl.ds(i * MOE_BLOCK, MOE_BLOCK)], zsem)

        def start(i, carry):
            pl.when(nvalid_ref[i] < MOE_BLOCK)(lambda: zero_block(i).start())
            return carry

        def wait(i, carry):
            pl.when(nvalid_ref[i] < MOE_BLOCK)(lambda: zero_block(i).wait())
            return carry

        lax.fori_loop(0, n_blocks, start, 0)
        lax.fori_loop(0, n_blocks, wait, 0)

    def issue(c, carry):
        for j in range(DMA_ISSUE_UNROLL):
            r = c * DMA_ISSUE_UNROLL + j
            src = h_ref.at[r]
            pltpu.make_async_copy(src, xs_hbm.at[p0_ref[0, r]], sem).start(priority=0)
            pltpu.make_async_copy(src, xs_hbm.at[p1_ref[0, r]], sem).start(priority=1)
        return carry

    lax.fori_loop(0, tm // DMA_ISSUE_UNROLL, issue, 0)
    for _ in range(2):
        pltpu.make_async_copy(h_ref, xs_hbm.at[pl.ds(0, tm)], sem).wait()


def _dispatch(n_valid, pos0, pos1, h2p, n_rows):
    n = h2p.shape[0]
    tm = 256
    n_blocks = n_valid.shape[0]
    idx_spec = pl.BlockSpec((None, 1, tm), lambda t, nv: (t, 0, 0), memory_space=pltpu.SMEM)
    grid_spec = pltpu.PrefetchScalarGridSpec(
        num_scalar_prefetch=1,
        grid=(n // tm,),
        in_specs=[idx_spec, idx_spec, pl.BlockSpec((tm, SUBLANES, LANES), lambda t, nv: (t, 0, 0))],
        out_specs=pl.BlockSpec(memory_space=pl.ANY),
        scratch_shapes=[
            pltpu.VMEM((MOE_BLOCK, SUBLANES, LANES), h2p.dtype),
            pltpu.SemaphoreType.DMA(()),
            pltpu.SemaphoreType.DMA(()),
        ],
    )
    return pl.pallas_call(
        functools.partial(_dispatch_kernel, tm=tm, n_blocks=n_blocks),
        grid_spec=grid_spec,
        out_shape=jax.ShapeDtypeStruct((n_rows, SUBLANES, LANES), h2p.dtype),
        compiler_params=pltpu.CompilerParams(
            dimension_semantics=("arbitrary",), vmem_limit_bytes=VMEM_LIMIT,
            disable_bounds_checks=True),
        name="dispatch",
    )(n_valid, pos0.reshape(n // tm, 1, tm), pos1.reshape(n // tm, 1, tm), h2p)


def _moe_kernel(be_ref, nvalid_ref, bi_ref, nxt_ref, x_ref, wg_hbm, wu_hbm, wd_hbm, y_ref,
                wg_f, wu_f, wd_f, wg_b, wu_b, wd_b, sem):
    i = pl.program_id(0)
    n_valid = nvalid_ref[i]

    def fetch(e):
        return (pltpu.make_async_copy(wg_hbm.at[e], wg_f, sem.at[0]),
                pltpu.make_async_copy(wu_hbm.at[e], wu_f, sem.at[1]),
                pltpu.make_async_copy(wd_hbm.at[e], wd_f, sem.at[2]))

    @pl.when(n_valid > 0)
    def _():
        e = be_ref[i]

        @pl.when(i == 0)
        def _():
            for cp in fetch(e):
                cp.start()

        @pl.when((i == 0) | (e != be_ref[jnp.maximum(i - 1, 0)]))
        def _():
            for cp in fetch(e):
                cp.wait()
            wg_b[...] = wg_f[...].astype(BF16)
            wu_b[...] = wu_f[...].astype(BF16)
            wd_b[...] = wd_f[...].astype(BF16)
            nxt = nxt_ref[i]

            @pl.when(nxt >= 0)
            def _():
                for cp in fetch(nxt):
                    cp.start()

        words = jnp.concatenate(
            [x_ref[pl.ds(c, MOE_BLOCK, stride=SUBLANES), :] for c in range(SUBLANES)], axis=1)
        lo, hi = _unpack_bf16_pair(words)
        half = words.shape[1]
        g = _dot(lo, wg_b[:half, :]) + _dot(hi, wg_b[half:, :])
        u = _dot(lo, wu_b[:half, :]) + _dot(hi, wu_b[half:, :])
        act = (g * jax.nn.sigmoid(g) * u).astype(BF16)
        y_ref[...] = _dot(act, wd_b[...])

    @pl.when(n_valid == 0)
    def _():
        y_ref[...] = jnp.zeros_like(y_ref)


def _moe(block_expert, n_valid, block_index, block_next, xs, w_gate, w_up, w_down):
    n_blocks = block_expert.shape[0]
    n_rows = xs.shape[0]
    d = 2 * SUBLANES * LANES
    hbm = pl.BlockSpec(memory_space=pl.ANY)
    grid_spec = pltpu.PrefetchScalarGridSpec(
        num_scalar_prefetch=4,
        grid=(n_blocks,),
        in_specs=[pl.BlockSpec((MOE_BLOCK * SUBLANES, LANES), lambda i, be, nv, bi, nx: (bi[i], 0)),
                  hbm, hbm, hbm],
        out_specs=pl.BlockSpec((MOE_BLOCK, d), lambda i, be, nv, bi, nx: (i, 0)),
        scratch_shapes=[
            pltpu.VMEM((d, D_EXPERT), F32),
            pltpu.VMEM((d, D_EXPERT), F32),
            pltpu.VMEM((D_EXPERT, d), F32),
            pltpu.VMEM((d, D_EXPERT), BF16),
            pltpu.VMEM((d, D_EXPERT), BF16),
            pltpu.VMEM((D_EXPERT, d), BF16),
            pltpu.SemaphoreType.DMA((3,)),
        ],
    )
    return pl.pallas_call(
        _moe_kernel,
        grid_spec=grid_spec,
        out_shape=jax.ShapeDtypeStruct((n_rows, d), F32),
        compiler_params=pltpu.CompilerParams(
            dimension_semantics=("arbitrary",), vmem_limit_bytes=VMEM_LIMIT),
        name="moe",
    )(block_expert, n_valid, block_index, block_next, xs.reshape(n_rows * SUBLANES, LANES), w_gate, w_up, w_down)


def _final_kernel(p0_ref, p1_ref, q0_ref, q1_ref, x1_ref, rt_ref, g2_ref, gf_ref, yb_hbm, o_ref,
                  y0buf, y1buf, sem, *, tm, n_tiles):
    t = pl.program_id(0)
    slot = t & 1

    def issue(pa_ref, pb_ref, s):
        def body(c, carry):
            for j in range(DMA_ISSUE_UNROLL):
                r = c * DMA_ISSUE_UNROLL + j
                pltpu.make_async_copy(yb_hbm.at[pl.ds(pa_ref[0, r], 1)], y0buf.at[s].at[pl.ds(r, 1)],
                                      sem.at[s]).start(priority=0)
                pltpu.make_async_copy(yb_hbm.at[pl.ds(pb_ref[0, r], 1)], y1buf.at[s].at[pl.ds(r, 1)],
                                      sem.at[s]).start(priority=1)
            return carry

        lax.fori_loop(0, tm // DMA_ISSUE_UNROLL, body, 0)

    @pl.when(t == 0)
    def _():
        issue(p0_ref, p1_ref, 0)

    @pl.when(t + 1 < n_tiles)
    def _():
        issue(q0_ref, q1_ref, 1 - slot)

    pltpu.make_async_copy(yb_hbm.at[pl.ds(0, tm)], y0buf.at[slot], sem.at[slot]).wait()
    pltpu.make_async_copy(yb_hbm.at[pl.ds(0, tm)], y1buf.at[slot], sem.at[slot]).wait()

    rt = rt_ref[...]
    y = rt[:, 2:3] * y0buf[slot] + rt[:, 3:4] * y1buf[slot]
    x2 = x1_ref[...] + g2_ref[...] * y
    ms = jnp.mean(x2 * x2, axis=-1, keepdims=True)
    o_ref[...] = x2 * lax.rsqrt(ms + EPS) * gf_ref[...]


def _final(pos0, pos1, x1, rt, g2, gf, yb, seq):
    n, d = x1.shape
    tm = 256
    n_tiles = n // tm
    per_seq = seq // tm
    this_tile = pl.BlockSpec((None, 1, tm), lambda t: (t, 0, 0), memory_space=pltpu.SMEM)
    next_tile = pl.BlockSpec((None, 1, tm), lambda t: (jnp.minimum(t + 1, n_tiles - 1), 0, 0),
                             memory_space=pltpu.SMEM)
    p0 = pos0.reshape(n_tiles, 1, tm)
    p1 = pos1.reshape(n_tiles, 1, tm)
    return pl.pallas_call(
        functools.partial(_final_kernel, tm=tm, n_tiles=n_tiles),
        grid=(n_tiles,),
        in_specs=[
            this_tile, this_tile, next_tile, next_tile,
            pl.BlockSpec((tm, d), lambda t: (t, 0)),
            pl.BlockSpec((tm, ROUTE_W), lambda t: (t, 0)),
            pl.BlockSpec((None, 1, d), lambda t: (t // per_seq, 0, 0)),
            pl.BlockSpec((1, d), lambda t: (0, 0)),
            pl.BlockSpec(memory_space=pl.ANY),
        ],
        out_specs=pl.BlockSpec((tm, d), lambda t: (t, 0)),
        out_shape=jax.ShapeDtypeStruct((n, d), F32),
        scratch_shapes=[
            pltpu.VMEM((2, tm, d), F32),
            pltpu.VMEM((2, tm, d), F32),
            pltpu.SemaphoreType.DMA((2,)),
        ],
        compiler_params=pltpu.CompilerParams(
            dimension_semantics=("arbitrary",), vmem_limit_bytes=VMEM_LIMIT,
            disable_bounds_checks=True),
        name="final",
    )(p0, p1, p0, p1, x1, rt, g2, gf, yb)


def kernel(x, c, ada_w, ada_b, norm1_g, w_in, conv3_w, conv4_w, conv4_b, lru_w_a, lru_b_a, lru_w_x, lru_b_x, lru_lambda, head_norm_conv_g, head_norm_lru_g, w_out, norm2_g, route_w_group, route_b_group, route_w_expert, route_b_expert, w_e_gate, w_e_up, w_e_down, final_norm_g):
    bsz, s, d = x.shape
    n_tok = bsz * s
    l = 0

    mod = _ada(c, ada_w[l], ada_b[l]).reshape(bsz, 6, 1, d)
    sh1, sc1, g1, sh2, sc2, g2 = (mod[:, j] for j in range(6))

    proj = _in_proj(x, sc1, sh1, norm1_g[l].reshape(1, d), w_in[l].astype(BF16))

    wax = jnp.concatenate([lru_w_a[l], lru_w_x[l]], axis=-1).astype(BF16)
    y = _mix(proj, conv3_w[l], conv4_w[l], conv4_b[l], wax, lru_b_a[l], lru_b_x[l], lru_lambda[l],
             head_norm_conv_g[l], head_norm_lru_g[l])

    w_route = jnp.concatenate(
        [route_w_group[l], jnp.transpose(route_w_expert[l], (1, 0, 2)).reshape(d, N_EXPERTS),
         jnp.zeros((d, ROUTE_W - N_GROUPS - N_EXPERTS), F32)], axis=1).astype(BF16)
    b_route = jnp.concatenate(
        [route_b_group[l], route_b_expert[l].reshape(-1),
         jnp.zeros((ROUTE_W - N_GROUPS - N_EXPERTS,), F32)]).reshape(1, ROUTE_W)
    x1, h2p, rt, counts = _out(y, x, g1, w_out[l].astype(BF16), sc2, sh2, norm2_g[l].reshape(1, d),
                               w_route, b_route)

    rt = rt.reshape(n_tok, ROUTE_W)
    pos0, pos1, block_expert, n_valid, block_index, block_next, n_rows = _plan(rt, counts, n_tok)
    assert d // 2 == SUBLANES * LANES, "a token's packed words must fill exactly one (8, 128) tile"
    xs = _dispatch(n_valid, pos0, pos1, h2p.reshape(n_tok, SUBLANES, LANES), n_rows)
    yb = _moe(block_expert, n_valid, block_index, block_next, xs, w_e_gate[l], w_e_up[l], w_e_down[l])
    out = _final(pos0, pos1, x1.reshape(n_tok, d), rt, g2, final_norm_g.reshape(1, d), yb, s)
    return out.reshape(bsz, s, d)
```

```python
import functools

import jax
import jax.numpy as jnp
from jax import lax
from jax.experimental import pallas as pl
from jax.experimental.pallas import tpu as pltpu

D_MODEL = 2048
D_CONV = 1024
CONV_HEAD_DIM = 64
D_LRU = 1024
LRU_HEADS = 8
LRU_HEAD_DIM = 128
LRU_C = 8.0
D_MIX = D_CONV + D_LRU
D_IN = 3 * D_CONV + 2 * D_LRU
N_GROUPS = 8
EXPERTS_PER_GROUP = 8
N_EXPERTS = 64
D_EXPERT = 512
EPS = 1e-6

LANES = 128
SUBLANES = 8
ROUTE_W = LANES
MOE_BLOCK = 256
X_RING = 3
DMA_ISSUE_UNROLL = 8
VMEM_LIMIT = 56 * 1024 * 1024

F32 = jnp.float32
BF16 = jnp.bfloat16


def _dot(a, b):
    return jnp.dot(a, b, preferred_element_type=F32)


def _ada_kernel(c_ref, w_ref, b_ref, o_ref):
    c = c_ref[...]
    s = (c * jax.nn.sigmoid(c)).astype(BF16)
    o_ref[...] = _dot(s, w_ref[...].astype(BF16)) + b_ref[...]


def _ada(c, w, b):
    bsz, d = c.shape
    n = w.shape[1]
    tn = 1024
    return pl.pallas_call(
        _ada_kernel,
        grid=(n // tn,),
        in_specs=[
            pl.BlockSpec((bsz, d), lambda j: (0, 0)),
            pl.BlockSpec((d, tn), lambda j: (0, j)),
            pl.BlockSpec((1, tn), lambda j: (0, j)),
        ],
        out_specs=pl.BlockSpec((bsz, tn), lambda j: (0, j)),
        out_shape=jax.ShapeDtypeStruct((bsz, n), F32),
        compiler_params=pltpu.CompilerParams(
            dimension_semantics=("arbitrary",), vmem_limit_bytes=VMEM_LIMIT),
        name="ada",
    )(c, w, b.reshape(1, n))


def _modulated_norm(x, g, sc, sh):
    ms = jnp.mean(x * x, axis=-1, keepdims=True)
    return x * lax.rsqrt(ms + EPS) * (g * (1.0 + sc)) + sh


def _in_proj_kernel(x_ref, sc_ref, sh_ref, g_ref, w_ref, o_ref, *, n_chunk):
    hb = _modulated_norm(x_ref[...], g_ref[...], sc_ref[...], sh_ref[...]).astype(BF16)
    for j in range(D_IN // n_chunk):
        cols = slice(j * n_chunk, (j + 1) * n_chunk)
        o_ref[:, cols] = _dot(hb, w_ref[:, cols])


def _in_proj(x, sc, sh, g, w_bf):
    bsz, s, d = x.shape
    tm = 256
    per_batch = pl.BlockSpec((None, 1, d), lambda b, t: (b, 0, 0))
    return pl.pallas_call(
        functools.partial(_in_proj_kernel, n_chunk=1024),
        grid=(bsz, s // tm),
        in_specs=[
            pl.BlockSpec((None, tm, d), lambda b, t: (b, t, 0)),
            per_batch, per_batch,
            pl.BlockSpec((1, d), lambda b, t: (0, 0)),
            pl.BlockSpec((d, D_IN), lambda b, t: (0, 0), pipeline_mode=pl.Buffered(1)),
        ],
        out_specs=pl.BlockSpec((None, tm, D_IN), lambda b, t: (b, t, 0)),
        out_shape=jax.ShapeDtypeStruct((bsz, s, D_IN), F32),
        compiler_params=pltpu.CompilerParams(
            dimension_semantics=("arbitrary", "arbitrary"), vmem_limit_bytes=VMEM_LIMIT),
        name="in_proj",
    )(x, sc, sh, g, w_bf)


def _shift_rows(x, prev8, d, row):
    rolled = pltpu.roll(x, d, 0)
    head = jnp.where(row[:SUBLANES] >= d, rolled[:SUBLANES], pltpu.roll(prev8, d, 0))
    return jnp.concatenate([head, rolled[SUBLANES:]], axis=0)


def _head_rms(y, p_ref, g):
    sq = y * y
    hi = sq.astype(BF16)
    lo = (sq - hi.astype(F32)).astype(BF16)
    ms = _dot(hi, p_ref[...]) + _dot(lo, p_ref[...])
    return y * lax.rsqrt(ms + EPS) * g


def _linear_scan(a, v, h0, sub):
    groups = []
    for r0 in range(0, a.shape[0], SUBLANES):
        ag, vg = a[r0:r0 + SUBLANES], v[r0:r0 + SUBLANES]
        d = 1
        while d < SUBLANES:
            keep = sub >= d
            vg = vg + ag * jnp.where(keep, pltpu.roll(vg, d, 0), 0.0)
            ag = ag * jnp.where(keep, pltpu.roll(ag, d, 0), 1.0)
            d *= 2
        groups.append((ag, vg))
    out, carry = [], h0
    for ag, vg in groups:
        hg = vg + ag * carry
        carry = hg[SUBLANES - 1:, :]
        out.append(hg)
    return jnp.concatenate(out, axis=0), carry


def _gelu_tanh(x):
    c = 0.7978845608028654
    half_x = 0.5 * x
    return half_x + half_x * jnp.tanh(x * (c + (c * 0.044715) * (x * x)))


def _mix_kernel(p_ref, w3_ref, w4_ref, b4_ref, wax_ref, ba_ref, bx_ref, lam_ref, ga_ref, gb_ref,
                pa_ref, pb_ref, y_ref, ua_carry, xb_carry, h_carry, *, ts):
    @pl.when(pl.program_id(1) == 0)
    def _():
        ua_carry[...] = jnp.zeros_like(ua_carry)
        xb_carry[...] = jnp.zeros_like(xb_carry)
        h_carry[...] = jnp.zeros_like(h_carry)

    row = lax.broadcasted_iota(jnp.int32, (ts, LANES), 0)

    for k in range(D_CONV // LANES):
        cols = slice(k * LANES, (k + 1) * LANES)
        b_a = p_ref[:, k * LANES:(k + 1) * LANES]
        c_a = p_ref[:, D_CONV + k * LANES:D_CONV + (k + 1) * LANES]
        x_a = p_ref[:, 2 * D_CONV + k * LANES:2 * D_CONV + (k + 1) * LANES]
        u = c_a * x_a
        prev = ua_carry[:, cols]
        conv = (w3_ref[2:3, cols] * u
                + w3_ref[1:2, cols] * _shift_rows(u, prev, 1, row)
                + w3_ref[0:1, cols] * _shift_rows(u, prev, 2, row))
        ua_carry[:, cols] = u[ts - SUBLANES:, :]
        y_a = b_a * conv
        y_ref[:, cols] = _head_rms(y_a, pa_ref, ga_ref[:, cols]).astype(BF16)

        g_b = p_ref[:, 3 * D_CONV + k * LANES:3 * D_CONV + (k + 1) * LANES]
        x_b = p_ref[:, 3 * D_CONV + D_LRU + k * LANES:3 * D_CONV + D_LRU + (k + 1) * LANES]
        prevb = xb_carry[:, cols]
        xc = (w4_ref[3:4, cols] * x_b
              + w4_ref[2:3, cols] * _shift_rows(x_b, prevb, 1, row)
              + w4_ref[1:2, cols] * _shift_rows(x_b, prevb, 2, row)
              + w4_ref[0:1, cols] * _shift_rows(x_b, prevb, 3, row)
              + b4_ref[:, cols])
        xb_carry[:, cols] = x_b[ts - SUBLANES:, :]
        gates = _dot(xc.astype(BF16), wax_ref[k])
        r = jax.nn.sigmoid(gates[:, :LANES] + ba_ref[:, cols])
        i = jax.nn.sigmoid(gates[:, LANES:] + bx_ref[:, cols])
        nlam = -lam_ref[:, cols]
        softplus = jnp.maximum(nlam, 0.0) + jnp.log1p(jnp.exp(-jnp.abs(nlam)))
        log_a = r * ((-LRU_C) * softplus)
        a = jnp.exp(log_a)
        mult = jnp.sqrt(-jnp.tanh(log_a) * (a * a + 1.0))
        v = mult * i * xc
        hs, h_last = _linear_scan(a, v, h_carry[:, cols], row[:SUBLANES])
        h_carry[:, cols] = h_last
        y_b = hs * _gelu_tanh(g_b)
        y_ref[:, D_CONV + k * LANES:D_CONV + (k + 1) * LANES] = _head_rms(
            y_b, pb_ref, gb_ref[:, cols]).astype(BF16)


def _mix(proj, w3, w4, b4, wax_bf, b_a, b_x, lam, g_a, g_b):
    bsz, s, _ = proj.shape
    ts = 256
    lane = jnp.arange(LANES)
    p_a = ((lane[:, None] // CONV_HEAD_DIM) == (lane[None, :] // CONV_HEAD_DIM)).astype(BF16) / CONV_HEAD_DIM
    p_b = jnp.full((LANES, LANES), 1.0 / LRU_HEAD_DIM, BF16)
    full = lambda shape: pl.BlockSpec(shape, lambda b, t: (0,) * len(shape))
    row = lambda a: a.reshape(1, -1)
    return pl.pallas_call(
        functools.partial(_mix_kernel, ts=ts),
        grid=(bsz, s // ts),
        in_specs=[
            pl.BlockSpec((None, ts, D_IN), lambda b, t: (b, t, 0)),
            full((3, D_CONV)), full((4, D_LRU)), full((1, D_LRU)),
            full((LRU_HEADS, LRU_HEAD_DIM, 2 * LRU_HEAD_DIM)),
            full((1, D_LRU)), full((1, D_LRU)), full((1, D_LRU)),
            full((1, D_CONV)), full((1, D_LRU)),
            full((LANES, LANES)), full((LANES, LANES)),
        ],
        out_specs=pl.BlockSpec((None, ts, D_MIX), lambda b, t: (b, t, 0)),
        out_shape=jax.ShapeDtypeStruct((bsz, s, D_MIX), BF16),
        scratch_shapes=[
            pltpu.VMEM((SUBLANES, D_CONV), F32),
            pltpu.VMEM((SUBLANES, D_LRU), F32),
            pltpu.VMEM((1, D_LRU), F32),
        ],
        compiler_params=pltpu.CompilerParams(
            dimension_semantics=("arbitrary", "arbitrary"), vmem_limit_bytes=VMEM_LIMIT),
        name="mix",
    )(proj, w3, w4, row(b4), wax_bf, row(b_a), row(b_x), row(lam), row(g_a), row(g_b), p_a, p_b)


def _route(logits, tri, carry):
    lane = lax.broadcasted_iota(jnp.int32, logits.shape, 1)
    neg = -jnp.inf
    is_g = lane < N_GROUPS
    gl = jnp.where(is_g, logits, neg)
    ge = jnp.exp(gl - jnp.max(gl, axis=-1, keepdims=True))
    p_g = 1.0 / jnp.sum(ge, axis=-1, keepdims=True)
    g_idx = jnp.min(jnp.where(is_g & (ge >= 1.0), lane, LANES), axis=-1, keepdims=True)
    sel = (lane >= N_GROUPS) & (lane < N_GROUPS + N_EXPERTS) & (((lane - N_GROUPS) >> 3) == g_idx)
    el = jnp.where(sel, logits, neg)
    m1 = jnp.max(el, axis=-1, keepdims=True)
    i1 = jnp.min(jnp.where(sel & (el == m1), lane, LANES), axis=-1, keepdims=True)
    sel2 = sel & (lane != i1)
    el2 = jnp.where(sel2, logits, neg)
    m2 = jnp.max(el2, axis=-1, keepdims=True)
    i2 = jnp.min(jnp.where(sel2 & (el2 == m2), lane, LANES), axis=-1, keepdims=True)
    e2 = jnp.exp(m2 - m1)
    w1 = p_g * (1.0 / (1.0 + e2))
    w2 = p_g * (e2 / (1.0 + e2))
    out = jnp.where(lane == 0, (i1 - N_GROUPS).astype(F32), 0.0)
    out = jnp.where(lane == 1, (i2 - N_GROUPS).astype(F32), out)
    out = jnp.where(lane == 2, w1, out)
    out = jnp.where(lane == 3, w2, out)
    oh1 = lane == i1
    oh2 = lane == i2
    both = jnp.where(oh1 | oh2, 1.0, 0.0)
    before = _dot(tri, both.astype(BF16)) + carry
    out = jnp.where(lane == 4, jnp.sum(jnp.where(oh1, before, 0.0), axis=-1, keepdims=True), out)
    out = jnp.where(lane == 5, jnp.sum(jnp.where(oh2, before, 0.0), axis=-1, keepdims=True), out)
    return out, carry + jnp.sum(both, axis=0, keepdims=True)


def _pack_bf16_pair(lo, hi):
    lo_bits = lax.bitcast_convert_type(lo.astype(BF16).astype(F32), jnp.uint32)
    hi_bits = lax.bitcast_convert_type(hi.astype(BF16).astype(F32), jnp.uint32)
    return (lo_bits >> 16) | hi_bits


def _unpack_bf16_pair(w):
    lo = lax.bitcast_convert_type(w << 16, F32).astype(BF16)
    hi = lax.bitcast_convert_type(w & jnp.uint32(0xFFFF0000), F32).astype(BF16)
    return lo, hi


def _out_kernel(y_ref, x_ref, g1_ref, w_ref, sc_ref, sh_ref, g_ref, wr_ref, br_ref, tri_ref,
                x1_ref, h2p_ref, rt_ref, cnt_ref, carry):
    @pl.when((pl.program_id(0) == 0) & (pl.program_id(1) == 0))
    def _():
        carry[...] = jnp.zeros_like(carry)

    sub = tri_ref.shape[0]
    count = carry[...]
    tiles = [slice(r0, r0 + sub) for r0 in range(0, y_ref.shape[0], sub)]
    mixes = [_dot(y_ref[rows, :], w_ref[...]) for rows in tiles]
    for rows, mix in zip(tiles, mixes):
        x1 = x_ref[rows, :] + g1_ref[...] * mix
        x1_ref[rows, :] = x1
        h2 = _modulated_norm(x1, g_ref[...], sc_ref[...], sh_ref[...])
        half = h2.shape[1] // 2
        words = _pack_bf16_pair(h2[:, :half], h2[:, half:])
        for c in range(half // LANES):
            h2p_ref[pl.ds(rows.start * SUBLANES + c, sub, stride=SUBLANES), :] = words[:, c * LANES:(c + 1) * LANES]
        logits = _dot(h2.astype(BF16), wr_ref[...]) + br_ref[...]
        rt, count = _route(logits, tri_ref[...], count)
        rt_ref[rows, :] = rt
    carry[...] = count
    cnt_ref[...] = count


def _out(y, x, g1, w_bf, sc2, sh2, g, wr_bf, br):
    bsz, s, d = x.shape
    tm, sub = 512, 256
    per_batch = pl.BlockSpec((None, 1, d), lambda b, t: (b, 0, 0))
    tile = lambda w: pl.BlockSpec((None, tm, w), lambda b, t: (b, t, 0))
    const = lambda shape, **kw: pl.BlockSpec(shape, lambda b, t: (0, 0), **kw)
    return pl.pallas_call(
        _out_kernel,
        grid=(bsz, s // tm),
        in_specs=[
            tile(D_MIX), tile(d), per_batch,
            const((D_MIX, d), pipeline_mode=pl.Buffered(1)),
            per_batch, per_batch, const((1, d)),
            const((d, ROUTE_W)), const((1, ROUTE_W)), const((sub, sub)),
        ],
        out_specs=[tile(d), pl.BlockSpec((None, tm * SUBLANES, LANES), lambda b, t: (b, t, 0)),
                   tile(ROUTE_W), const((1, ROUTE_W))],
        out_shape=[
            jax.ShapeDtypeStruct((bsz, s, d), F32),
            jax.ShapeDtypeStruct((bsz, s * SUBLANES, LANES), jnp.uint32),
            jax.ShapeDtypeStruct((bsz, s, ROUTE_W), F32),
            jax.ShapeDtypeStruct((1, ROUTE_W), F32),
        ],
        scratch_shapes=[pltpu.VMEM((1, ROUTE_W), F32)],
        compiler_params=pltpu.CompilerParams(
            dimension_semantics=("arbitrary", "arbitrary"), vmem_limit_bytes=VMEM_LIMIT),
        name="out_route",
    )(y, x, g1, w_bf, sc2, sh2, g, wr_bf, br, jnp.tril(jnp.ones((sub, sub), BF16), -1))


def _plan(rt, counts_f, n_tok):
    counts = counts_f[0, N_GROUPS:N_GROUPS + N_EXPERTS].astype(jnp.int32)
    padded = ((counts + MOE_BLOCK - 1) // MOE_BLOCK) * MOE_BLOCK
    padded_ends = jnp.cumsum(padded)
    padded_starts = padded_ends - padded
    n_rows = 2 * n_tok + N_EXPERTS * MOE_BLOCK
    n_blocks = n_rows // MOE_BLOCK
    expert_ids = jnp.arange(N_EXPERTS, dtype=jnp.int32)
    idx = rt[:, 0:6].astype(jnp.int32)
    start_of = jnp.sum(jnp.where(idx[:, 0:2, None] == expert_ids, padded_starts, 0), axis=-1)
    pos = start_of + idx[:, 4:6]
    block_start = jnp.arange(n_blocks, dtype=jnp.int32)[:, None] * MOE_BLOCK
    owns = (padded_starts <= block_start) & (block_start < padded_ends)
    n_valid = jnp.sum(jnp.where(owns, jnp.clip(counts - (block_start - padded_starts), 0, MOE_BLOCK), 0), axis=1)
    last_used = jnp.max(jnp.where(counts > 0, expert_ids, 0))
    block_expert = jnp.where(n_valid > 0, jnp.sum(jnp.where(owns, expert_ids, 0), axis=1), last_used)
    later_used = (expert_ids[None, :] > expert_ids[:, None]) & (counts[None, :] > 0)
    next_used = jnp.min(jnp.where(later_used, expert_ids[None, :], N_EXPERTS), axis=1)
    next_used = jnp.where(next_used == N_EXPERTS, -1, next_used)
    block_next = jnp.sum(jnp.where(owns, next_used, 0), axis=1)
    n_used = padded_ends[-1] // MOE_BLOCK
    block_index = jnp.minimum(jnp.arange(n_blocks, dtype=jnp.int32), n_used - 1)
    i32 = lambda a: a.astype(jnp.int32)
    return pos[:, 0], pos[:, 1], i32(block_expert), i32(n_valid), i32(block_index), i32(block_next), n_rows


def _dispatch_kernel(nvalid_ref, p0_ref, p1_ref, h_ref, xs_hbm, zbuf, sem, zsem, *, tm, n_blocks):
    @pl.when(pl.program_id(0) == 0)
    def _():
        zbuf[...] = jnp.zeros_like(zbuf)

        def zero_block(i):
            return pltpu.make_async_copy(zbuf, xs_hbm.at[pl.ds(i * MOE_BLOCK, MOE_BLOCK)], zsem)

        def start(i, carry):
            pl.when(nvalid_ref[i] < MOE_BLOCK)(lambda: zero_block(i).start())
            return carry

        def wait(i, carry):
            pl.when(nvalid_ref[i] < MOE_BLOCK)(lambda: zero_block(i).wait())
            return carry

        lax.fori_loop(0, n_blocks, start, 0)
        lax.fori_loop(0, n_blocks, wait, 0)

    def issue(c, carry):
        for j in range(DMA_ISSUE_UNROLL):
            r = c * DMA_ISSUE_UNROLL + j
            src = h_ref.at[r]
            pltpu.make_async_copy(src, xs_hbm.at[p0_ref[0, r]], sem).start(priority=0)
            pltpu.make_async_copy(src, xs_hbm.at[p1_ref[0, r]], sem).start(priority=1)
        return carry

    lax.fori_loop(0, tm // DMA_ISSUE_UNROLL, issue, 0)
    for _ in range(2):
        pltpu.make_async_copy(h_ref, xs_hbm.at[pl.ds(0, tm)], sem).wait()


def _dispatch(n_valid, pos0, pos1, h2p, n_rows):
    n = h2p.shape[0]
    tm = 256
    n_blocks = n_valid.shape[0]
    idx_spec = pl.BlockSpec((None, 1, tm), lambda t, nv: (t, 0, 0), memory_space=pltpu.SMEM)
    grid_spec = pltpu.PrefetchScalarGridSpec(
        num_scalar_prefetch=1,
        grid=(n // tm,),
        in_specs=[idx_spec, idx_spec, pl.BlockSpec((tm, SUBLANES, LANES), lambda t, nv: (t, 0, 0))],
        out_specs=pl.BlockSpec(memory_space=pl.ANY),
        scratch_shapes=[
            pltpu.VMEM((MOE_BLOCK, SUBLANES, LANES), h2p.dtype),
            pltpu.SemaphoreType.DMA(()),
            pltpu.SemaphoreType.DMA(()),
        ],
    )
    return pl.pallas_call(
        functools.partial(_dispatch_kernel, tm=tm, n_blocks=n_blocks),
        grid_spec=grid_spec,
        out_shape=jax.ShapeDtypeStruct((n_rows, SUBLANES, LANES), h2p.dtype),
        compiler_params=pltpu.CompilerParams(
            dimension_semantics=("arbitrary",), vmem_limit_bytes=VMEM_LIMIT,
            disable_bounds_checks=True),
        name="dispatch",
    )(n_valid, pos0.reshape(n // tm, 1, tm), pos1.reshape(n // tm, 1, tm), h2p)


def _moe_kernel(be_ref, nvalid_ref, bi_ref, nxt_ref, xs_hbm, wg_hbm, wu_hbm, wd_hbm, y_ref,
                wg_f, wu_f, wd_f, wg_b, wu_b, wd_b, xbuf, sem, xsem, *, n_blocks):
    i = pl.program_id(0)
    n_valid = nvalid_ref[i]
    block_rows = MOE_BLOCK * SUBLANES

    def x_copy(j):
        start = pl.multiple_of(bi_ref[j] * block_rows, block_rows)
        return pltpu.make_async_copy(xs_hbm.at[pl.ds(start, block_rows)], xbuf.at[j % X_RING], xsem.at[j % X_RING])

    @pl.when(i == 0)
    def _():
        for j in range(X_RING - 1):
            x_copy(j).start()

    @pl.when(i + X_RING - 1 < n_blocks)
    def _():
        x_copy(i + X_RING - 1).start()

    x_copy(i).wait()
    x_ref = xbuf.at[i % X_RING]

    def fetch(e):
        return (pltpu.make_async_copy(wg_hbm.at[e], wg_f, sem.at[0]),
                pltpu.make_async_copy(wu_hbm.at[e], wu_f, sem.at[1]),
                pltpu.make_async_copy(wd_hbm.at[e], wd_f, sem.at[2]))

    @pl.when(n_valid > 0)
    def _():
        e = be_ref[i]

        @pl.when(i == 0)
        def _():
            for cp in fetch(e):
                cp.start(priority=1)

        @pl.when((i == 0) | (e != be_ref[jnp.maximum(i - 1, 0)]))
        def _():
            for cp in fetch(e):
                cp.wait()
            wg_b[...] = wg_f[...].astype(BF16)
            wu_b[...] = wu_f[...].astype(BF16)
            wd_b[...] = wd_f[...].astype(BF16)
            nxt = nxt_ref[i]

            @pl.when(nxt >= 0)
            def _():
                for cp in fetch(nxt):
                    cp.start(priority=1)

        words = jnp.concatenate(
            [x_ref[pl.ds(c, MOE_BLOCK, stride=SUBLANES), :] for c in range(SUBLANES)], axis=1)
        lo, hi = _unpack_bf16_pair(words)
        half = words.shape[1]
        g = _dot(lo, wg_b[:half, :]) + _dot(hi, wg_b[half:, :])
        u = _dot(lo, wu_b[:half, :]) + _dot(hi, wu_b[half:, :])
        act = (g * jax.nn.sigmoid(g) * u).astype(BF16)
        y_ref[...] = _dot(act, wd_b[...])

    @pl.when(n_valid == 0)
    def _():
        y_ref[...] = jnp.zeros_like(y_ref)


def _moe(block_expert, n_valid, block_index, block_next, xs, w_gate, w_up, w_down):
    n_blocks = block_expert.shape[0]
    n_rows = xs.shape[0]
    d = 2 * SUBLANES * LANES
    hbm = pl.BlockSpec(memory_space=pl.ANY)
    grid_spec = pltpu.PrefetchScalarGridSpec(
        num_scalar_prefetch=4,
        grid=(n_blocks,),
        in_specs=[hbm, hbm, hbm, hbm],
        out_specs=pl.BlockSpec((MOE_BLOCK, d), lambda i, be, nv, bi, nx: (i, 0)),
        scratch_shapes=[
            pltpu.VMEM((d, D_EXPERT), F32),
            pltpu.VMEM((d, D_EXPERT), F32),
            pltpu.VMEM((D_EXPERT, d), F32),
            pltpu.VMEM((d, D_EXPERT), BF16),
            pltpu.VMEM((d, D_EXPERT), BF16),
            pltpu.VMEM((D_EXPERT, d), BF16),
            pltpu.VMEM((X_RING, MOE_BLOCK * SUBLANES, LANES), jnp.uint32),
            pltpu.SemaphoreType.DMA((3,)),
            pltpu.SemaphoreType.DMA((X_RING,)),
        ],
    )
    return pl.pallas_call(
        functools.partial(_moe_kernel, n_blocks=n_blocks),
        grid_spec=grid_spec,
        out_shape=jax.ShapeDtypeStruct((n_rows, d), F32),
        compiler_params=pltpu.CompilerParams(
            dimension_semantics=("arbitrary",), vmem_limit_bytes=VMEM_LIMIT),
        name="moe",
    )(block_expert, n_valid, block_index, block_next, xs.reshape(n_rows * SUBLANES, LANES), w_gate, w_up, w_down)


def _final_kernel(p0_ref, p1_ref, q0_ref, q1_ref, x1_ref, rt_ref, g2_ref, gf_ref, yb_hbm, o_ref,
                  y0buf, y1buf, sem, *, tm, n_tiles):
    t = pl.program_id(0)
    slot = t & 1

    def issue(pa_ref, pb_ref, s):
        def body(c, carry):
            for j in range(DMA_ISSUE_UNROLL):
                r = c * DMA_ISSUE_UNROLL + j
                pltpu.make_async_copy(yb_hbm.at[pl.ds(pa_ref[0, r], 1)], y0buf.at[s].at[pl.ds(r, 1)],
                                      sem.at[s]).start(priority=0)
                pltpu.make_async_copy(yb_hbm.at[pl.ds(pb_ref[0, r], 1)], y1buf.at[s].at[pl.ds(r, 1)],
                                      sem.at[s]).start(priority=1)
            return carry

        lax.fori_loop(0, tm // DMA_ISSUE_UNROLL, body, 0)

    @pl.when(t == 0)
    def _():
        issue(p0_ref, p1_ref, 0)

    @pl.when(t + 1 < n_tiles)
    def _():
        issue(q0_ref, q1_ref, 1 - slot)

    pltpu.make_async_copy(yb_hbm.at[pl.ds(0, tm)], y0buf.at[slot], sem.at[slot]).wait()
    pltpu.make_async_copy(yb_hbm.at[pl.ds(0, tm)], y1buf.at[slot], sem.at[slot]).wait()

    rt = rt_ref[...]
    y = rt[:, 2:3] * y0buf[slot] + rt[:, 3:4] * y1buf[slot]
    x2 = x1_ref[...] + g2_ref[...] * y
    ms = jnp.mean(x2 * x2, axis=-1, keepdims=True)
    o_ref[...] = x2 * lax.rsqrt(ms + EPS) * gf_ref[...]


def _final(pos0, pos1, x1, rt, g2, gf, yb, seq):
    n, d = x1.shape
    tm = 256
    n_tiles = n // tm
    per_seq = seq // tm
    this_tile = pl.BlockSpec((None, 1, tm), lambda t: (t, 0, 0), memory_space=pltpu.SMEM)
    next_tile = pl.BlockSpec((None, 1, tm), lambda t: (jnp.minimum(t + 1, n_tiles - 1), 0, 0),
                             memory_space=pltpu.SMEM)
    p0 = pos0.reshape(n_tiles, 1, tm)
    p1 = pos1.reshape(n_tiles, 1, tm)
    return pl.pallas_call(
        functools.partial(_final_kernel, tm=tm, n_tiles=n_tiles),
        grid=(n_tiles,),
        in_specs=[
            this_tile, this_tile, next_tile, next_tile,
            pl.BlockSpec((tm, d), lambda t: (t, 0)),
            pl.BlockSpec((tm, ROUTE_W), lambda t: (t, 0)),
            pl.BlockSpec((None, 1, d), lambda t: (t // per_seq, 0, 0)),
            pl.BlockSpec((1, d), lambda t: (0, 0)),
            pl.BlockSpec(memory_space=pl.ANY),
        ],
        out_specs=pl.BlockSpec((tm, d), lambda t: (t, 0)),
        out_shape=jax.ShapeDtypeStruct((n, d), F32),
        scratch_shapes=[
            pltpu.VMEM((2, tm, d), F32),
            pltpu.VMEM((2, tm, d), F32),
            pltpu.SemaphoreType.DMA((2,)),
        ],
        compiler_params=pltpu.CompilerParams(
            dimension_semantics=("arbitrary",), vmem_limit_bytes=VMEM_LIMIT,
            disable_bounds_checks=True),
        name="final",
    )(p0, p1, p0, p1, x1, rt, g2, gf, yb)


def kernel(x, c, ada_w, ada_b, norm1_g, w_in, conv3_w, conv4_w, conv4_b, lru_w_a, lru_b_a, lru_w_x, lru_b_x, lru_lambda, head_norm_conv_g, head_norm_lru_g, w_out, norm2_g, route_w_group, route_b_group, route_w_expert, route_b_expert, w_e_gate, w_e_up, w_e_down, final_norm_g):
    bsz, s, d = x.shape
    n_tok = bsz * s
    l = 0

    mod = _ada(c, ada_w[l], ada_b[l]).reshape(bsz, 6, 1, d)
    sh1, sc1, g1, sh2, sc2, g2 = (mod[:, j] for j in range(6))

    proj = _in_proj(x, sc1, sh1, norm1_g[l].reshape(1, d), w_in[l].astype(BF16))

    wax = jnp.concatenate([lru_w_a[l], lru_w_x[l]], axis=-1).astype(BF16)
    y = _mix(proj, conv3_w[l], conv4_w[l], conv4_b[l], wax, lru_b_a[l], lru_b_x[l], lru_lambda[l],
             head_norm_conv_g[l], head_norm_lru_g[l])

    w_route = jnp.concatenate(
        [route_w_group[l], jnp.transpose(route_w_expert[l], (1, 0, 2)).reshape(d, N_EXPERTS),
         jnp.zeros((d, ROUTE_W - N_GROUPS - N_EXPERTS), F32)], axis=1).astype(BF16)
    b_route = jnp.concatenate(
        [route_b_group[l], route_b_expert[l].reshape(-1),
         jnp.zeros((ROUTE_W - N_GROUPS - N_EXPERTS,), F32)]).reshape(1, ROUTE_W)
    x1, h2p, rt, counts = _out(y, x, g1, w_out[l].astype(BF16), sc2, sh2, norm2_g[l].reshape(1, d),
                               w_route, b_route)

    rt = rt.reshape(n_tok, ROUTE_W)
    pos0, pos1, block_expert, n_valid, block_index, block_next, n_rows = _plan(rt, counts, n_tok)
    assert d // 2 == SUBLANES * LANES, "a token's packed words must fill exactly one (8, 128) tile"
    xs = _dispatch(n_valid, pos0, pos1, h2p.reshape(n_tok, SUBLANES, LANES), n_rows)
    yb = _moe(block_expert, n_valid, block_index, block_next, xs, w_e_gate[l], w_e_up[l], w_e_down[l])
    out = _final(pos0, pos1, x1.reshape(n_tok, d), rt, g2, final_norm_g.reshape(1, d), yb, s)
    return out.reshape(bsz, s, d)
```

```python
import functools

import jax
import jax.numpy as jnp
from jax import lax
from jax.experimental import pallas as pl
from jax.experimental.pallas import tpu as pltpu

D_MODEL = 2048
D_CONV = 1024
CONV_HEAD_DIM = 64
D_LRU = 1024
LRU_HEADS = 8
LRU_HEAD_DIM = 128
LRU_C = 8.0
D_MIX = D_CONV + D_LRU
D_IN = 3 * D_CONV + 2 * D_LRU
N_GROUPS = 8
EXPERTS_PER_GROUP = 8
N_EXPERTS = 64
D_EXPERT = 512
EPS = 1e-6

LANES = 128
SUBLANES = 8
ROUTE_W = LANES
MOE_BLOCK = 256
X_RING = 3
DMA_ISSUE_UNROLL = 8
VMEM_LIMIT = 56 * 1024 * 1024

F32 = jnp.float32
BF16 = jnp.bfloat16


def _dot(a, b):
    return jnp.dot(a, b, preferred_element_type=F32)


def _ada_kernel(c_ref, w_ref, b_ref, o_ref):
    c = c_ref[...]
    s = (c * jax.nn.sigmoid(c)).astype(BF16)
    o_ref[...] = _dot(s, w_ref[...].astype(BF16)) + b_ref[...]


def _ada(c, w, b):
    bsz, d = c.shape
    n = w.shape[1]
    tn = 1024
    return pl.pallas_call(
        _ada_kernel,
        grid=(n // tn,),
        in_specs=[
            pl.BlockSpec((bsz, d), lambda j: (0, 0)),
            pl.BlockSpec((d, tn), lambda j: (0, j)),
            pl.BlockSpec((1, tn), lambda j: (0, j)),
        ],
        out_specs=pl.BlockSpec((bsz, tn), lambda j: (0, j)),
        out_shape=jax.ShapeDtypeStruct((bsz, n), F32),
        compiler_params=pltpu.CompilerParams(
            dimension_semantics=("arbitrary",), vmem_limit_bytes=VMEM_LIMIT),
        name="ada",
    )(c, w, b.reshape(1, n))


def _modulated_norm(x, g, sc, sh):
    ms = jnp.mean(x * x, axis=-1, keepdims=True)
    return x * lax.rsqrt(ms + EPS) * (g * (1.0 + sc)) + sh


def _in_proj_kernel(x_ref, sc_ref, sh_ref, g_ref, w_ref, o_ref, *, n_chunk):
    hb = _modulated_norm(x_ref[...], g_ref[...], sc_ref[...], sh_ref[...]).astype(BF16)
    for j in range(D_IN // n_chunk):
        cols = slice(j * n_chunk, (j + 1) * n_chunk)
        o_ref[:, cols] = _dot(hb, w_ref[:, cols])


def _in_proj(x, sc, sh, g, w_bf):
    bsz, s, d = x.shape
    tm = 256
    per_batch = pl.BlockSpec((None, 1, d), lambda b, t: (b, 0, 0))
    return pl.pallas_call(
        functools.partial(_in_proj_kernel, n_chunk=1024),
        grid=(bsz, s // tm),
        in_specs=[
            pl.BlockSpec((None, tm, d), lambda b, t: (b, t, 0)),
            per_batch, per_batch,
            pl.BlockSpec((1, d), lambda b, t: (0, 0)),
            pl.BlockSpec((d, D_IN), lambda b, t: (0, 0), pipeline_mode=pl.Buffered(1)),
        ],
        out_specs=pl.BlockSpec((None, tm, D_IN), lambda b, t: (b, t, 0)),
        out_shape=jax.ShapeDtypeStruct((bsz, s, D_IN), F32),
        compiler_params=pltpu.CompilerParams(
            dimension_semantics=("arbitrary", "arbitrary"), vmem_limit_bytes=VMEM_LIMIT),
        name="in_proj",
    )(x, sc, sh, g, w_bf)


def _shift_rows(x, prev8, d, row):
    rolled = pltpu.roll(x, d, 0)
    head = jnp.where(row[:SUBLANES] >= d, rolled[:SUBLANES], pltpu.roll(prev8, d, 0))
    return jnp.concatenate([head, rolled[SUBLANES:]], axis=0)


def _head_rms(y, p_ref, g):
    sq = y * y
    hi = sq.astype(BF16)
    lo = (sq - hi.astype(F32)).astype(BF16)
    ms = _dot(hi, p_ref[...]) + _dot(lo, p_ref[...])
    return y * lax.rsqrt(ms + EPS) * g


def _linear_scan(a, v, h0, sub):
    groups = []
    for r0 in range(0, a.shape[0], SUBLANES):
        ag, vg = a[r0:r0 + SUBLANES], v[r0:r0 + SUBLANES]
        d = 1
        while d < SUBLANES:
            keep = sub >= d
            vg = vg + ag * jnp.where(keep, pltpu.roll(vg, d, 0), 0.0)
            ag = ag * jnp.where(keep, pltpu.roll(ag, d, 0), 1.0)
            d *= 2
        groups.append((ag, vg))
    out, carry = [], h0
    for ag, vg in groups:
        hg = vg + ag * carry
        carry = hg[SUBLANES - 1:, :]
        out.append(hg)
    return jnp.concatenate(out, axis=0), carry


def _gelu_tanh(x):
    c = 0.7978845608028654
    half_x = 0.5 * x
    return half_x + half_x * jnp.tanh(x * (c + (c * 0.044715) * (x * x)))


def _mix_kernel(p_ref, w3_ref, w4_ref, b4_ref, wax_ref, ba_ref, bx_ref, lam_ref, ga_ref, gb_ref,
                pa_ref, pb_ref, y_ref, ua_carry, xb_carry, h_carry, *, ts):
    @pl.when(pl.program_id(1) == 0)
    def _():
        ua_carry[...] = jnp.zeros_like(ua_carry)
        xb_carry[...] = jnp.zeros_like(xb_carry)
        h_carry[...] = jnp.zeros_like(h_carry)

    row = lax.broadcasted_iota(jnp.int32, (ts, LANES), 0)

    for k in range(D_CONV // LANES):
        cols = slice(k * LANES, (k + 1) * LANES)
        b_a = p_ref[:, k * LANES:(k + 1) * LANES]
        c_a = p_ref[:, D_CONV + k * LANES:D_CONV + (k + 1) * LANES]
        x_a = p_ref[:, 2 * D_CONV + k * LANES:2 * D_CONV + (k + 1) * LANES]
        u = c_a * x_a
        prev = ua_carry[:, cols]
        conv = (w3_ref[2:3, cols] * u
                + w3_ref[1:2, cols] * _shift_rows(u, prev, 1, row)
                + w3_ref[0:1, cols] * _shift_rows(u, prev, 2, row))
        ua_carry[:, cols] = u[ts - SUBLANES:, :]
        y_a = b_a * conv
        y_ref[:, cols] = _head_rms(y_a, pa_ref, ga_ref[:, cols]).astype(BF16)

        g_b = p_ref[:, 3 * D_CONV + k * LANES:3 * D_CONV + (k + 1) * LANES]
        x_b = p_ref[:, 3 * D_CONV + D_LRU + k * LANES:3 * D_CONV + D_LRU + (k + 1) * LANES]
        prevb = xb_carry[:, cols]
        xc = (w4_ref[3:4, cols] * x_b
              + w4_ref[2:3, cols] * _shift_rows(x_b, prevb, 1, row)
              + w4_ref[1:2, cols] * _shift_rows(x_b, prevb, 2, row)
              + w4_ref[0:1, cols] * _shift_rows(x_b, prevb, 3, row)
              + b4_ref[:, cols])
        xb_carry[:, cols] = x_b[ts - SUBLANES:, :]
        gates = _dot(xc.astype(BF16), wax_ref[k])
        r = jax.nn.sigmoid(gates[:, :LANES] + ba_ref[:, cols])
        i = jax.nn.sigmoid(gates[:, LANES:] + bx_ref[:, cols])
        nlam = -lam_ref[:, cols]
        softplus = jnp.maximum(nlam, 0.0) + jnp.log1p(jnp.exp(-jnp.abs(nlam)))
        log_a = r * ((-LRU_C) * softplus)
        a = jnp.exp(log_a)
        mult = jnp.sqrt(-jnp.tanh(log_a) * (a * a + 1.0))
        v = mult * i * xc
        hs, h_last = _linear_scan(a, v, h_carry[:, cols], row[:SUBLANES])
        h_carry[:, cols] = h_last
        y_b = hs * _gelu_tanh(g_b)
        y_ref[:, D_CONV + k * LANES:D_CONV + (k + 1) * LANES] = _head_rms(
            y_b, pb_ref, gb_ref[:, cols]).astype(BF16)


def _mix(proj, w3, w4, b4, wax_bf, b_a, b_x, lam, g_a, g_b):
    bsz, s, _ = proj.shape
    ts = 256
    lane = jnp.arange(LANES)
    p_a = ((lane[:, None] // CONV_HEAD_DIM) == (lane[None, :] // CONV_HEAD_DIM)).astype(BF16) / CONV_HEAD_DIM
    p_b = jnp.full((LANES, LANES), 1.0 / LRU_HEAD_DIM, BF16)
    full = lambda shape: pl.BlockSpec(shape, lambda b, t: (0,) * len(shape))
    row = lambda a: a.reshape(1, -1)
    return pl.pallas_call(
        functools.partial(_mix_kernel, ts=ts),
        grid=(bsz, s // ts),
        in_specs=[
            pl.BlockSpec((None, ts, D_IN), lambda b, t: (b, t, 0)),
            full((3, D_CONV)), full((4, D_LRU)), full((1, D_LRU)),
            full((LRU_HEADS, LRU_HEAD_DIM, 2 * LRU_HEAD_DIM)),
            full((1, D_LRU)), full((1, D_LRU)), full((1, D_LRU)),
            full((1, D_CONV)), full((1, D_LRU)),
            full((LANES, LANES)), full((LANES, LANES)),
        ],
        out_specs=pl.BlockSpec((None, ts, D_MIX), lambda b, t: (b, t, 0)),
        out_shape=jax.ShapeDtypeStruct((bsz, s, D_MIX), BF16),
        scratch_shapes=[
            pltpu.VMEM((SUBLANES, D_CONV), F32),
            pltpu.VMEM((SUBLANES, D_LRU), F32),
            pltpu.VMEM((1, D_LRU), F32),
        ],
        compiler_params=pltpu.CompilerParams(
            dimension_semantics=("arbitrary", "arbitrary"), vmem_limit_bytes=VMEM_LIMIT),
        name="mix",
    )(proj, w3, w4, row(b4), wax_bf, row(b_a), row(b_x), row(lam), row(g_a), row(g_b), p_a, p_b)


def _route(logits, tri, carry):
    lane = lax.broadcasted_iota(jnp.int32, logits.shape, 1)
    neg = -jnp.inf
    is_g = lane < N_GROUPS
    gl = jnp.where(is_g, logits, neg)
    ge = jnp.exp(gl - jnp.max(gl, axis=-1, keepdims=True))
    p_g = 1.0 / jnp.sum(ge, axis=-1, keepdims=True)
    g_idx = jnp.min(jnp.where(is_g & (ge >= 1.0), lane, LANES), axis=-1, keepdims=True)
    sel = (lane >= N_GROUPS) & (lane < N_GROUPS + N_EXPERTS) & (((lane - N_GROUPS) >> 3) == g_idx)
    el = jnp.where(sel, logits, neg)
    m1 = jnp.max(el, axis=-1, keepdims=True)
    i1 = jnp.min(jnp.where(sel & (el == m1), lane, LANES), axis=-1, keepdims=True)
    sel2 = sel & (lane != i1)
    el2 = jnp.where(sel2, logits, neg)
    m2 = jnp.max(el2, axis=-1, keepdims=True)
    i2 = jnp.min(jnp.where(sel2 & (el2 == m2), lane, LANES), axis=-1, keepdims=True)
    e2 = jnp.exp(m2 - m1)
    w1 = p_g * (1.0 / (1.0 + e2))
    w2 = p_g * (e2 / (1.0 + e2))
    out = jnp.where(lane == 0, (i1 - N_GROUPS).astype(F32), 0.0)
    out = jnp.where(lane == 1, (i2 - N_GROUPS).astype(F32), out)
    out = jnp.where(lane == 2, w1, out)
    out = jnp.where(lane == 3, w2, out)
    oh1 = lane == i1
    oh2 = lane == i2
    both = jnp.where(oh1 | oh2, 1.0, 0.0)
    before = _dot(tri, both.astype(BF16)) + carry
    out = jnp.where(lane == 4, jnp.sum(jnp.where(oh1, before, 0.0), axis=-1, keepdims=True), out)
    out = jnp.where(lane == 5, jnp.sum(jnp.where(oh2, before, 0.0), axis=-1, keepdims=True), out)
    return out, carry + jnp.sum(both, axis=0, keepdims=True)


def _pack_bf16_pair(lo, hi):
    lo_bits = lax.bitcast_convert_type(lo.astype(BF16).astype(F32), jnp.uint32)
    hi_bits = lax.bitcast_convert_type(hi.astype(BF16).astype(F32), jnp.uint32)
    return (lo_bits >> 16) | hi_bits


def _unpack_bf16_pair(w):
    lo = lax.bitcast_convert_type(w << 16, F32).astype(BF16)
    hi = lax.bitcast_convert_type(w & jnp.uint32(0xFFFF0000), F32).astype(BF16)
    return lo, hi


def _out_kernel(y_ref, x_ref, g1_ref, w_ref, sc_ref, sh_ref, g_ref, wr_ref, br_ref, tri_ref,
                x1_ref, h2p_ref, rt_ref, cnt_ref, carry):
    @pl.when((pl.program_id(0) == 0) & (pl.program_id(1) == 0))
    def _():
        carry[...] = jnp.zeros_like(carry)

    sub = tri_ref.shape[0]
    count = carry[...]
    tiles = [slice(r0, r0 + sub) for r0 in range(0, y_ref.shape[0], sub)]
    mixes = [_dot(y_ref[rows, :], w_ref[...]) for rows in tiles]
    for rows, mix in zip(tiles, mixes):
        x1 = x_ref[rows, :] + g1_ref[...] * mix
        x1_ref[rows, :] = x1
        h2 = _modulated_norm(x1, g_ref[...], sc_ref[...], sh_ref[...])
        half = h2.shape[1] // 2
        words = _pack_bf16_pair(h2[:, :half], h2[:, half:])
        for c in range(half // LANES):
            h2p_ref[pl.ds(rows.start * SUBLANES + c, sub, stride=SUBLANES), :] = words[:, c * LANES:(c + 1) * LANES]
        logits = _dot(h2.astype(BF16), wr_ref[...]) + br_ref[...]
        rt, count = _route(logits, tri_ref[...], count)
        rt_ref[rows, :] = rt
    carry[...] = count
    cnt_ref[...] = count


def _out(y, x, g1, w_bf, sc2, sh2, g, wr_bf, br):
    bsz, s, d = x.shape
    tm, sub = 512, 256
    per_batch = pl.BlockSpec((None, 1, d), lambda b, t: (b, 0, 0))
    tile = lambda w: pl.BlockSpec((None, tm, w), lambda b, t: (b, t, 0))
    const = lambda shape, **kw: pl.BlockSpec(shape, lambda b, t: (0, 0), **kw)
    return pl.pallas_call(
        _out_kernel,
        grid=(bsz, s // tm),
        in_specs=[
            tile(D_MIX), tile(d), per_batch,
            const((D_MIX, d), pipeline_mode=pl.Buffered(1)),
            per_batch, per_batch, const((1, d)),
            const((d, ROUTE_W)), const((1, ROUTE_W)), const((sub, sub)),
        ],
        out_specs=[tile(d), pl.BlockSpec((None, tm * SUBLANES, LANES), lambda b, t: (b, t, 0)),
                   tile(ROUTE_W), const((1, ROUTE_W))],
        out_shape=[
            jax.ShapeDtypeStruct((bsz, s, d), F32),
            jax.ShapeDtypeStruct((bsz, s * SUBLANES, LANES), jnp.uint32),
            jax.ShapeDtypeStruct((bsz, s, ROUTE_W), F32),
            jax.ShapeDtypeStruct((1, ROUTE_W), F32),
        ],
        scratch_shapes=[pltpu.VMEM((1, ROUTE_W), F32)],
        compiler_params=pltpu.CompilerParams(
            dimension_semantics=("arbitrary", "arbitrary"), vmem_limit_bytes=VMEM_LIMIT),
        name="out_route",
    )(y, x, g1, w_bf, sc2, sh2, g, wr_bf, br, jnp.tril(jnp.ones((sub, sub), BF16), -1))


def _plan(rt, counts_f, n_tok):
    counts = counts_f[0, N_GROUPS:N_GROUPS + N_EXPERTS].astype(jnp.int32)
    padded = ((counts + MOE_BLOCK - 1) // MOE_BLOCK) * MOE_BLOCK
    padded_ends = jnp.cumsum(padded)
    padded_starts = padded_ends - padded
    n_rows = 2 * n_tok + N_EXPERTS * MOE_BLOCK
    n_blocks = n_rows // MOE_BLOCK
    expert_ids = jnp.arange(N_EXPERTS, dtype=jnp.int32)
    idx = rt[:, 0:6].astype(jnp.int32)
    start_of = jnp.sum(jnp.where(idx[:, 0:2, None] == expert_ids, padded_starts, 0), axis=-1)
    pos = start_of + idx[:, 4:6]
    block_start = jnp.arange(n_blocks, dtype=jnp.int32)[:, None] * MOE_BLOCK
    owns = (padded_starts <= block_start) & (block_start < padded_ends)
    n_valid = jnp.sum(jnp.where(owns, jnp.clip(counts - (block_start - padded_starts), 0, MOE_BLOCK), 0), axis=1)
    last_used = jnp.max(jnp.where(counts > 0, expert_ids, 0))
    block_expert = jnp.where(n_valid > 0, jnp.sum(jnp.where(owns, expert_ids, 0), axis=1), last_used)
    later_used = (expert_ids[None, :] > expert_ids[:, None]) & (counts[None, :] > 0)
    next_used = jnp.min(jnp.where(later_used, expert_ids[None, :], N_EXPERTS), axis=1)
    next_used = jnp.where(next_used == N_EXPERTS, -1, next_used)
    block_next = jnp.sum(jnp.where(owns, next_used, 0), axis=1)
    n_used = padded_ends[-1] // MOE_BLOCK
    block_index = jnp.minimum(jnp.arange(n_blocks, dtype=jnp.int32), n_used - 1)
    i32 = lambda a: a.astype(jnp.int32)
    return pos[:, 0], pos[:, 1], i32(block_expert), i32(n_valid), i32(block_index), i32(block_next), n_rows


def _dispatch_kernel(nvalid_ref, p0_ref, p1_ref, h_ref, xs_hbm, zbuf, sem, zsem, *, tm, n_blocks):
    @pl.when(pl.program_id(0) == 0)
    def _():
        zbuf[...] = jnp.zeros_like(zbuf)

        def zero_block(i):
            return pltpu.make_async_copy(zbuf, xs_hbm.at[pl.ds(i * MOE_BLOCK, MOE_BLOCK)], zsem)

        def start(i, carry):
            pl.when(nvalid_ref[i] < MOE_BLOCK)(lambda: zero_block(i).start())
            return carry

        def wait(i, carry):
            pl.when(nvalid_ref[i] < MOE_BLOCK)(lambda: zero_block(i).wait())
            return carry

        lax.fori_loop(0, n_blocks, start, 0)
        lax.fori_loop(0, n_blocks, wait, 0)

    def issue(c, carry):
        for j in range(DMA_ISSUE_UNROLL):
            r = c * DMA_ISSUE_UNROLL + j
            src = h_ref.at[r]
            pltpu.make_async_copy(src, xs_hbm.at[p0_ref[0, r]], sem).start(priority=0)
            pltpu.make_async_copy(src, xs_hbm.at[p1_ref[0, r]], sem).start(priority=1)
        return carry

    lax.fori_loop(0, tm // DMA_ISSUE_UNROLL, issue, 0)
    for _ in range(2):
        pltpu.make_async_copy(h_ref, xs_hbm.at[pl.ds(0, tm)], sem).wait()


def _dispatch(n_valid, pos0, pos1, h2p, n_rows):
    n = h2p.shape[0]
    tm = 256
    n_blocks = n_valid.shape[0]
    idx_spec = pl.BlockSpec((None, 1, tm), lambda t, nv: (t, 0, 0), memory_space=pltpu.SMEM)
    grid_spec = pltpu.PrefetchScalarGridSpec(
        num_scalar_prefetch=1,
        grid=(n // tm,),
        in_specs=[idx_spec, idx_spec, pl.BlockSpec((tm, SUBLANES, LANES), lambda t, nv: (t, 0, 0))],
        out_specs=pl.BlockSpec(memory_space=pl.ANY),
        scratch_shapes=[
            pltpu.VMEM((MOE_BLOCK, SUBLANES, LANES), h2p.dtype),
            pltpu.SemaphoreType.DMA(()),
            pltpu.SemaphoreType.DMA(()),
        ],
    )
    return pl.pallas_call(
        functools.partial(_dispatch_kernel, tm=tm, n_blocks=n_blocks),
        grid_spec=grid_spec,
        out_shape=jax.ShapeDtypeStruct((n_rows, SUBLANES, LANES), h2p.dtype),
        compiler_params=pltpu.CompilerParams(
            dimension_semantics=("arbitrary",), vmem_limit_bytes=VMEM_LIMIT,
            disable_bounds_checks=True),
        name="dispatch",
    )(n_valid, pos0.reshape(n // tm, 1, tm), pos1.reshape(n // tm, 1, tm), h2p)


def _moe_kernel(be_ref, nvalid_ref, bi_ref, nxt_ref, xs_hbm, wg_hbm, wu_hbm, wd_hbm, y_ref,
                wg_f, wu_f, wd_f, wg_b, wu_b, wd_b, xbuf, sem, xsem, *, n_blocks):
    i = pl.program_id(0)
    n_valid = nvalid_ref[i]
    block_rows = MOE_BLOCK * SUBLANES

    def x_copy(j):
        start = pl.multiple_of(bi_ref[j] * block_rows, block_rows)
        return pltpu.make_async_copy(xs_hbm.at[pl.ds(start, block_rows)], xbuf.at[j % X_RING], xsem.at[j % X_RING])

    @pl.when(i == 0)
    def _():
        for j in range(X_RING - 1):
            x_copy(j).start()

    @pl.when(i + X_RING - 1 < n_blocks)
    def _():
        x_copy(i + X_RING - 1).start()

    x_copy(i).wait()
    x_ref = xbuf.at[i % X_RING]

    def fetch(e):
        return (pltpu.make_async_copy(wg_hbm.at[e], wg_f, sem.at[0]),
                pltpu.make_async_copy(wu_hbm.at[e], wu_f, sem.at[1]),
                pltpu.make_async_copy(wd_hbm.at[e], wd_f, sem.at[2]))

    @pl.when(n_valid > 0)
    def _():
        e = be_ref[i]

        @pl.when(i == 0)
        def _():
            for cp in fetch(e):
                cp.start(priority=1)

        @pl.when((i == 0) | (e != be_ref[jnp.maximum(i - 1, 0)]))
        def _():
            for cp in fetch(e):
                cp.wait()
            wg_b[...] = wg_f[...].astype(BF16)
            wu_b[...] = wu_f[...].astype(BF16)
            wd_b[...] = wd_f[...].astype(BF16)
            nxt = nxt_ref[i]

            @pl.when(nxt >= 0)
            def _():
                for cp in fetch(nxt):
                    cp.start(priority=1)

        words = jnp.concatenate(
            [x_ref[pl.ds(c, MOE_BLOCK, stride=SUBLANES), :] for c in range(SUBLANES)], axis=1)
        lo, hi = _unpack_bf16_pair(words)
        half = words.shape[1]
        g = _dot(lo, wg_b[:half, :]) + _dot(hi, wg_b[half:, :])
        u = _dot(lo, wu_b[:half, :]) + _dot(hi, wu_b[half:, :])
        act = (g * jax.nn.sigmoid(g) * u).astype(BF16)
        y = _dot(act, wd_b[...])
        y_words = _pack_bf16_pair(y[:, :half], y[:, half:])
        for c in range(SUBLANES):
            y_ref[pl.ds(c, MOE_BLOCK, stride=SUBLANES), :] = y_words[:, c * LANES:(c + 1) * LANES]

    @pl.when(n_valid == 0)
    def _():
        y_ref[...] = jnp.zeros_like(y_ref)


def _moe(block_expert, n_valid, block_index, block_next, xs, w_gate, w_up, w_down):
    n_blocks = block_expert.shape[0]
    n_rows = xs.shape[0]
    d = 2 * SUBLANES * LANES
    hbm = pl.BlockSpec(memory_space=pl.ANY)
    grid_spec = pltpu.PrefetchScalarGridSpec(
        num_scalar_prefetch=4,
        grid=(n_blocks,),
        in_specs=[hbm, hbm, hbm, hbm],
        out_specs=pl.BlockSpec((MOE_BLOCK * SUBLANES, LANES), lambda i, be, nv, bi, nx: (i, 0)),
        scratch_shapes=[
            pltpu.VMEM((d, D_EXPERT), F32),
            pltpu.VMEM((d, D_EXPERT), F32),
            pltpu.VMEM((D_EXPERT, d), F32),
            pltpu.VMEM((d, D_EXPERT), BF16),
            pltpu.VMEM((d, D_EXPERT), BF16),
            pltpu.VMEM((D_EXPERT, d), BF16),
            pltpu.VMEM((X_RING, MOE_BLOCK * SUBLANES, LANES), jnp.uint32),
            pltpu.SemaphoreType.DMA((3,)),
            pltpu.SemaphoreType.DMA((X_RING,)),
        ],
    )
    return pl.pallas_call(
        functools.partial(_moe_kernel, n_blocks=n_blocks),
        grid_spec=grid_spec,
        out_shape=jax.ShapeDtypeStruct((n_rows * SUBLANES, LANES), jnp.uint32),
        compiler_params=pltpu.CompilerParams(
            dimension_semantics=("arbitrary",), vmem_limit_bytes=VMEM_LIMIT),
        name="moe",
    )(block_expert, n_valid, block_index, block_next, xs.reshape(n_rows * SUBLANES, LANES), w_gate, w_up, w_down)


def _final_kernel(p0_ref, p1_ref, q0_ref, q1_ref, x1_ref, rt_ref, g2_ref, gf_ref, yb_hbm, yb_flat_hbm, o_ref,
                  y0buf, y1buf, sem, *, tm, n_tiles):
    t = pl.program_id(0)
    slot = t & 1

    def issue(pa_ref, pb_ref, s):
        def body(c, carry):
            for j in range(DMA_ISSUE_UNROLL):
                r = c * DMA_ISSUE_UNROLL + j
                dst = pl.ds(pl.multiple_of(r * SUBLANES, SUBLANES), SUBLANES)
                pltpu.make_async_copy(yb_hbm.at[pa_ref[0, r]], y0buf.at[s].at[dst], sem.at[s]).start(priority=0)
                pltpu.make_async_copy(yb_hbm.at[pb_ref[0, r]], y1buf.at[s].at[dst], sem.at[s]).start(priority=1)
            return carry

        lax.fori_loop(0, tm // DMA_ISSUE_UNROLL, body, 0)

    @pl.when(t == 0)
    def _():
        issue(p0_ref, p1_ref, 0)

    @pl.when(t + 1 < n_tiles)
    def _():
        issue(q0_ref, q1_ref, 1 - slot)

    pltpu.make_async_copy(yb_flat_hbm.at[pl.ds(0, tm * SUBLANES)], y0buf.at[slot], sem.at[slot]).wait()
    pltpu.make_async_copy(yb_flat_hbm.at[pl.ds(0, tm * SUBLANES)], y1buf.at[slot], sem.at[slot]).wait()

    def rows_of(buf):
        tiles = buf.at[slot]
        words = jnp.concatenate([tiles[pl.ds(c, tm, stride=SUBLANES), :] for c in range(SUBLANES)], axis=1)
        lo = lax.bitcast_convert_type(words << 16, F32)
        hi = lax.bitcast_convert_type(words & jnp.uint32(0xFFFF0000), F32)
        return jnp.concatenate([lo, hi], axis=1)

    rt = rt_ref[...]
    y = rt[:, 2:3] * rows_of(y0buf) + rt[:, 3:4] * rows_of(y1buf)
    x2 = x1_ref[...] + g2_ref[...] * y
    ms = jnp.mean(x2 * x2, axis=-1, keepdims=True)
    o_ref[...] = x2 * lax.rsqrt(ms + EPS) * gf_ref[...]


def _final(pos0, pos1, x1, rt, g2, gf, yb, seq):
    n, d = x1.shape
    tm = 256
    n_tiles = n // tm
    per_seq = seq // tm
    this_tile = pl.BlockSpec((None, 1, tm), lambda t: (t, 0, 0), memory_space=pltpu.SMEM)
    next_tile = pl.BlockSpec((None, 1, tm), lambda t: (jnp.minimum(t + 1, n_tiles - 1), 0, 0),
                             memory_space=pltpu.SMEM)
    p0 = pos0.reshape(n_tiles, 1, tm)
    p1 = pos1.reshape(n_tiles, 1, tm)
    return pl.pallas_call(
        functools.partial(_final_kernel, tm=tm, n_tiles=n_tiles),
        grid=(n_tiles,),
        in_specs=[
            this_tile, this_tile, next_tile, next_tile,
            pl.BlockSpec((tm, d), lambda t: (t, 0)),
            pl.BlockSpec((tm, ROUTE_W), lambda t: (t, 0)),
            pl.BlockSpec((None, 1, d), lambda t: (t // per_seq, 0, 0)),
            pl.BlockSpec((1, d), lambda t: (0, 0)),
            pl.BlockSpec(memory_space=pl.ANY),
            pl.BlockSpec(memory_space=pl.ANY),
        ],
        out_specs=pl.BlockSpec((tm, d), lambda t: (t, 0)),
        out_shape=jax.ShapeDtypeStruct((n, d), F32),
        scratch_shapes=[
            pltpu.VMEM((2, tm * SUBLANES, LANES), jnp.uint32),
            pltpu.VMEM((2, tm * SUBLANES, LANES), jnp.uint32),
            pltpu.SemaphoreType.DMA((2,)),
        ],
        compiler_params=pltpu.CompilerParams(
            dimension_semantics=("arbitrary",), vmem_limit_bytes=VMEM_LIMIT,
            disable_bounds_checks=True),
        name="final",
    )(p0, p1, p0, p1, x1, rt, g2, gf, yb.reshape(-1, SUBLANES, LANES), yb)


def kernel(x, c, ada_w, ada_b, norm1_g, w_in, conv3_w, conv4_w, conv4_b, lru_w_a, lru_b_a, lru_w_x, lru_b_x, lru_lambda, head_norm_conv_g, head_norm_lru_g, w_out, norm2_g, route_w_group, route_b_group, route_w_expert, route_b_expert, w_e_gate, w_e_up, w_e_down, final_norm_g):
    bsz, s, d = x.shape
    n_tok = bsz * s
    l = 0

    mod = _ada(c, ada_w[l], ada_b[l]).reshape(bsz, 6, 1, d)
    sh1, sc1, g1, sh2, sc2, g2 = (mod[:, j] for j in range(6))

    proj = _in_proj(x, sc1, sh1, norm1_g[l].reshape(1, d), w_in[l].astype(BF16))

    wax = jnp.concatenate([lru_w_a[l], lru_w_x[l]], axis=-1).astype(BF16)
    y = _mix(proj, conv3_w[l], conv4_w[l], conv4_b[l], wax, lru_b_a[l], lru_b_x[l], lru_lambda[l],
             head_norm_conv_g[l], head_norm_lru_g[l])

    w_route = jnp.concatenate(
        [route_w_group[l], jnp.transpose(route_w_expert[l], (1, 0, 2)).reshape(d, N_EXPERTS),
         jnp.zeros((d, ROUTE_W - N_GROUPS - N_EXPERTS), F32)], axis=1).astype(BF16)
    b_route = jnp.concatenate(
        [route_b_group[l], route_b_expert[l].reshape(-1),
         jnp.zeros((ROUTE_W - N_GROUPS - N_EXPERTS,), F32)]).reshape(1, ROUTE_W)
    x1, h2p, rt, counts = _out(y, x, g1, w_out[l].astype(BF16), sc2, sh2, norm2_g[l].reshape(1, d),
                               w_route, b_route)

    rt = rt.reshape(n_tok, ROUTE_W)
    pos0, pos1, block_expert, n_valid, block_index, block_next, n_rows = _plan(rt, counts, n_tok)
    assert d // 2 == SUBLANES * LANES, "a token's packed words must fill exactly one (8, 128) tile"
    xs = _dispatch(n_valid, pos0, pos1, h2p.reshape(n_tok, SUBLANES, LANES), n_rows)
    yb = _moe(block_expert, n_valid, block_index, block_next, xs, w_e_gate[l], w_e_up[l], w_e_down[l])
    out = _final(pos0, pos1, x1.reshape(n_tok, d), rt, g2, final_norm_g.reshape(1, d), yb, s)
    return out.reshape(bsz, s, d)
```

```python
import functools

import jax
import jax.numpy as jnp
from jax import lax
from jax.experimental import pallas as pl
from jax.experimental.pallas import tpu as pltpu

D_MODEL = 2048
D_CONV = 1024
CONV_HEAD_DIM = 64
D_LRU = 1024
LRU_HEADS = 8
LRU_HEAD_DIM = 128
LRU_C = 8.0
D_MIX = D_CONV + D_LRU
D_IN = 3 * D_CONV + 2 * D_LRU
N_GROUPS = 8
EXPERTS_PER_GROUP = 8
N_EXPERTS = 64
D_EXPERT = 512
EPS = 1e-6

LANES = 128
SUBLANES = 8
ROUTE_W = LANES
MOE_BLOCK = 256
X_RING = 3
DMA_ISSUE_UNROLL = 8
VMEM_LIMIT = 56 * 1024 * 1024

F32 = jnp.float32
BF16 = jnp.bfloat16


def _dot(a, b):
    return jnp.dot(a, b, preferred_element_type=F32)


def _ada_kernel(c_ref, w_ref, b_ref, o_ref):
    c = c_ref[...]
    s = (c * jax.nn.sigmoid(c)).astype(BF16)
    o_ref[...] = _dot(s, w_ref[...].astype(BF16)) + b_ref[...]


def _ada(c, w, b):
    bsz, d = c.shape
    n = w.shape[1]
    tn = 1024
    return pl.pallas_call(
        _ada_kernel,
        grid=(n // tn,),
        in_specs=[
            pl.BlockSpec((bsz, d), lambda j: (0, 0)),
            pl.BlockSpec((d, tn), lambda j: (0, j)),
            pl.BlockSpec((1, tn), lambda j: (0, j)),
        ],
        out_specs=pl.BlockSpec((bsz, tn), lambda j: (0, j)),
        out_shape=jax.ShapeDtypeStruct((bsz, n), F32),
        compiler_params=pltpu.CompilerParams(
            dimension_semantics=("arbitrary",), vmem_limit_bytes=VMEM_LIMIT),
        name="ada",
    )(c, w, b.reshape(1, n))


def _modulated_norm(x, g, sc, sh):
    ms = jnp.mean(x * x, axis=-1, keepdims=True)
    return x * lax.rsqrt(ms + EPS) * (g * (1.0 + sc)) + sh


def _in_proj_kernel(x_ref, sc_ref, sh_ref, g_ref, w_ref, o_ref, *, n_chunk):
    hb = _modulated_norm(x_ref[...], g_ref[...], sc_ref[...], sh_ref[...]).astype(BF16)
    for j in range(D_IN // n_chunk):
        cols = slice(j * n_chunk, (j + 1) * n_chunk)
        o_ref[:, cols] = _dot(hb, w_ref[:, cols])


def _in_proj(x, sc, sh, g, w_bf):
    bsz, s, d = x.shape
    tm = 512
    per_batch = pl.BlockSpec((None, 1, d), lambda b, t: (b, 0, 0))
    return pl.pallas_call(
        functools.partial(_in_proj_kernel, n_chunk=1024),
        grid=(bsz, s // tm),
        in_specs=[
            pl.BlockSpec((None, tm, d), lambda b, t: (b, t, 0)),
            per_batch, per_batch,
            pl.BlockSpec((1, d), lambda b, t: (0, 0)),
            pl.BlockSpec((d, D_IN), lambda b, t: (0, 0), pipeline_mode=pl.Buffered(1)),
        ],
        out_specs=pl.BlockSpec((None, tm, D_IN), lambda b, t: (b, t, 0)),
        out_shape=jax.ShapeDtypeStruct((bsz, s, D_IN), F32),
        compiler_params=pltpu.CompilerParams(
            dimension_semantics=("arbitrary", "arbitrary"), vmem_limit_bytes=VMEM_LIMIT),
        name="in_proj",
    )(x, sc, sh, g, w_bf)


def _shift_rows(x, prev8, d, row):
    rolled = pltpu.roll(x, d, 0)
    head = jnp.where(row[:SUBLANES] >= d, rolled[:SUBLANES], pltpu.roll(prev8, d, 0))
    return jnp.concatenate([head, rolled[SUBLANES:]], axis=0)


def _head_rms(y, p_ref, g):
    sq = y * y
    hi = sq.astype(BF16)
    lo = (sq - hi.astype(F32)).astype(BF16)
    ms = _dot(hi, p_ref[...]) + _dot(lo, p_ref[...])
    return y * lax.rsqrt(ms + EPS) * g


def _linear_scan(a, v, h0, sub):
    groups = []
    for r0 in range(0, a.shape[0], SUBLANES):
        ag, vg = a[r0:r0 + SUBLANES], v[r0:r0 + SUBLANES]
        d = 1
        while d < SUBLANES:
            keep = sub >= d
            vg = vg + ag * jnp.where(keep, pltpu.roll(vg, d, 0), 0.0)
            ag = ag * jnp.where(keep, pltpu.roll(ag, d, 0), 1.0)
            d *= 2
        groups.append((ag, vg))
    out, carry = [], h0
    for ag, vg in groups:
        hg = vg + ag * carry
        carry = hg[SUBLANES - 1:, :]
        out.append(hg)
    return jnp.concatenate(out, axis=0), carry


def _gelu_tanh(x):
    c = 0.7978845608028654
    half_x = 0.5 * x
    return half_x + half_x * jnp.tanh(x * (c + (c * 0.044715) * (x * x)))


def _mix_kernel(p_ref, w3_ref, w4_ref, b4_ref, wax_ref, ba_ref, bx_ref, lam_ref, ga_ref, gb_ref,
                pa_ref, pb_ref, y_ref, ua_carry, xb_carry, h_carry, *, ts):
    @pl.when(pl.program_id(1) == 0)
    def _():
        ua_carry[...] = jnp.zeros_like(ua_carry)
        xb_carry[...] = jnp.zeros_like(xb_carry)
        h_carry[...] = jnp.zeros_like(h_carry)

    row = lax.broadcasted_iota(jnp.int32, (ts, LANES), 0)

    for k in range(D_CONV // LANES):
        cols = slice(k * LANES, (k + 1) * LANES)
        b_a = p_ref[:, k * LANES:(k + 1) * LANES]
        c_a = p_ref[:, D_CONV + k * LANES:D_CONV + (k + 1) * LANES]
        x_a = p_ref[:, 2 * D_CONV + k * LANES:2 * D_CONV + (k + 1) * LANES]
        u = c_a * x_a
        prev = ua_carry[:, cols]
        conv = (w3_ref[2:3, cols] * u
                + w3_ref[1:2, cols] * _shift_rows(u, prev, 1, row)
                + w3_ref[0:1, cols] * _shift_rows(u, prev, 2, row))
        ua_carry[:, cols] = u[ts - SUBLANES:, :]
        y_a = b_a * conv
        y_ref[:, cols] = _head_rms(y_a, pa_ref, ga_ref[:, cols]).astype(BF16)

        g_b = p_ref[:, 3 * D_CONV + k * LANES:3 * D_CONV + (k + 1) * LANES]
        x_b = p_ref[:, 3 * D_CONV + D_LRU + k * LANES:3 * D_CONV + D_LRU + (k + 1) * LANES]
        prevb = xb_carry[:, cols]
        xc = (w4_ref[3:4, cols] * x_b
              + w4_ref[2:3, cols] * _shift_rows(x_b, prevb, 1, row)
              + w4_ref[1:2, cols] * _shift_rows(x_b, prevb, 2, row)
              + w4_ref[0:1, cols] * _shift_rows(x_b, prevb, 3, row)
              + b4_ref[:, cols])
        xb_carry[:, cols] = x_b[ts - SUBLANES:, :]
        gates = _dot(xc.astype(BF16), wax_ref[k])
        r = jax.nn.sigmoid(gates[:, :LANES] + ba_ref[:, cols])
        i = jax.nn.sigmoid(gates[:, LANES:] + bx_ref[:, cols])
        nlam = -lam_ref[:, cols]
        softplus = jnp.maximum(nlam, 0.0) + jnp.log1p(jnp.exp(-jnp.abs(nlam)))
        log_a = r * ((-LRU_C) * softplus)
        a = jnp.exp(log_a)
        mult = jnp.sqrt(-jnp.tanh(log_a) * (a * a + 1.0))
        v = mult * i * xc
        hs, h_last = _linear_scan(a, v, h_carry[:, cols], row[:SUBLANES])
        h_carry[:, cols] = h_last
        y_b = hs * _gelu_tanh(g_b)
        y_ref[:, D_CONV + k * LANES:D_CONV + (k + 1) * LANES] = _head_rms(
            y_b, pb_ref, gb_ref[:, cols]).astype(BF16)


def _mix(proj, w3, w4, b4, wax_bf, b_a, b_x, lam, g_a, g_b):
    bsz, s, _ = proj.shape
    ts = 256
    lane = jnp.arange(LANES)
    p_a = ((lane[:, None] // CONV_HEAD_DIM) == (lane[None, :] // CONV_HEAD_DIM)).astype(BF16) / CONV_HEAD_DIM
    p_b = jnp.full((LANES, LANES), 1.0 / LRU_HEAD_DIM, BF16)
    full = lambda shape: pl.BlockSpec(shape, lambda b, t: (0,) * len(shape))
    row = lambda a: a.reshape(1, -1)
    return pl.pallas_call(
        functools.partial(_mix_kernel, ts=ts),
        grid=(bsz, s // ts),
        in_specs=[
            pl.BlockSpec((None, ts, D_IN), lambda b, t: (b, t, 0)),
            full((3, D_CONV)), full((4, D_LRU)), full((1, D_LRU)),
            full((LRU_HEADS, LRU_HEAD_DIM, 2 * LRU_HEAD_DIM)),
            full((1, D_LRU)), full((1, D_LRU)), full((1, D_LRU)),
            full((1, D_CONV)), full((1, D_LRU)),
            full((LANES, LANES)), full((LANES, LANES)),
        ],
        out_specs=pl.BlockSpec((None, ts, D_MIX), lambda b, t: (b, t, 0)),
        out_shape=jax.ShapeDtypeStruct((bsz, s, D_MIX), BF16),
        scratch_shapes=[
            pltpu.VMEM((SUBLANES, D_CONV), F32),
            pltpu.VMEM((SUBLANES, D_LRU), F32),
            pltpu.VMEM((1, D_LRU), F32),
        ],
        compiler_params=pltpu.CompilerParams(
            dimension_semantics=("arbitrary", "arbitrary"), vmem_limit_bytes=VMEM_LIMIT),
        name="mix",
    )(proj, w3, w4, row(b4), wax_bf, row(b_a), row(b_x), row(lam), row(g_a), row(g_b), p_a, p_b)


def _route(logits, tri, carry):
    lane = lax.broadcasted_iota(jnp.int32, logits.shape, 1)
    neg = -jnp.inf
    is_g = lane < N_GROUPS
    gl = jnp.where(is_g, logits, neg)
    ge = jnp.exp(gl - jnp.max(gl, axis=-1, keepdims=True))
    p_g = 1.0 / jnp.sum(ge, axis=-1, keepdims=True)
    g_idx = jnp.min(jnp.where(is_g & (ge >= 1.0), lane, LANES), axis=-1, keepdims=True)
    sel = (lane >= N_GROUPS) & (lane < N_GROUPS + N_EXPERTS) & (((lane - N_GROUPS) >> 3) == g_idx)
    el = jnp.where(sel, logits, neg)
    m1 = jnp.max(el, axis=-1, keepdims=True)
    i1 = jnp.min(jnp.where(sel & (el == m1), lane, LANES), axis=-1, keepdims=True)
    sel2 = sel & (lane != i1)
    el2 = jnp.where(sel2, logits, neg)
    m2 = jnp.max(el2, axis=-1, keepdims=True)
    i2 = jnp.min(jnp.where(sel2 & (el2 == m2), lane, LANES), axis=-1, keepdims=True)
    e2 = jnp.exp(m2 - m1)
    w1 = p_g * (1.0 / (1.0 + e2))
    w2 = p_g * (e2 / (1.0 + e2))
    out = jnp.where(lane == 0, (i1 - N_GROUPS).astype(F32), 0.0)
    out = jnp.where(lane == 1, (i2 - N_GROUPS).astype(F32), out)
    out = jnp.where(lane == 2, w1, out)
    out = jnp.where(lane == 3, w2, out)
    oh1 = lane == i1
    oh2 = lane == i2
    both = jnp.where(oh1 | oh2, 1.0, 0.0)
    before = _dot(tri, both.astype(BF16)) + carry
    out = jnp.where(lane == 4, jnp.sum(jnp.where(oh1, before, 0.0), axis=-1, keepdims=True), out)
    out = jnp.where(lane == 5, jnp.sum(jnp.where(oh2, before, 0.0), axis=-1, keepdims=True), out)
    return out, carry + jnp.sum(both, axis=0, keepdims=True)


def _pack_bf16_pair(lo, hi):
    lo_bits = lax.bitcast_convert_type(lo.astype(BF16).astype(F32), jnp.uint32)
    hi_bits = lax.bitcast_convert_type(hi.astype(BF16).astype(F32), jnp.uint32)
    return (lo_bits >> 16) | hi_bits


def _unpack_bf16_pair(w):
    lo = lax.bitcast_convert_type(w << 16, F32).astype(BF16)
    hi = lax.bitcast_convert_type(w & jnp.uint32(0xFFFF0000), F32).astype(BF16)
    return lo, hi


def _out_kernel(y_ref, x_ref, g1_ref, w_ref, sc_ref, sh_ref, g_ref, wr_ref, br_ref, tri_ref,
                x1_ref, h2p_ref, rt_ref, cnt_ref, carry):
    @pl.when((pl.program_id(0) == 0) & (pl.program_id(1) == 0))
    def _():
        carry[...] = jnp.zeros_like(carry)

    sub = tri_ref.shape[0]
    count = carry[...]
    tiles = [slice(r0, r0 + sub) for r0 in range(0, y_ref.shape[0], sub)]
    mixes = [_dot(y_ref[rows, :], w_ref[...]) for rows in tiles]
    for rows, mix in zip(tiles, mixes):
        x1 = x_ref[rows, :] + g1_ref[...] * mix
        x1_ref[rows, :] = x1
        h2 = _modulated_norm(x1, g_ref[...], sc_ref[...], sh_ref[...])
        half = h2.shape[1] // 2
        words = _pack_bf16_pair(h2[:, :half], h2[:, half:])
        for c in range(half // LANES):
            h2p_ref[pl.ds(rows.start * SUBLANES + c, sub, stride=SUBLANES), :] = words[:, c * LANES:(c + 1) * LANES]
        logits = _dot(h2.astype(BF16), wr_ref[...]) + br_ref[...]
        rt, count = _route(logits, tri_ref[...], count)
        rt_ref[rows, :] = rt
    carry[...] = count
    cnt_ref[...] = count


def _out(y, x, g1, w_bf, sc2, sh2, g, wr_bf, br):
    bsz, s, d = x.shape
    tm, sub = 512, 256
    per_batch = pl.BlockSpec((None, 1, d), lambda b, t: (b, 0, 0))
    tile = lambda w: pl.BlockSpec((None, tm, w), lambda b, t: (b, t, 0))
    const = lambda shape, **kw: pl.BlockSpec(shape, lambda b, t: (0, 0), **kw)
    return pl.pallas_call(
        _out_kernel,
        grid=(bsz, s // tm),
        in_specs=[
            tile(D_MIX), tile(d), per_batch,
            const((D_MIX, d), pipeline_mode=pl.Buffered(1)),
            per_batch, per_batch, const((1, d)),
            const((d, ROUTE_W)), const((1, ROUTE_W)), const((sub, sub)),
        ],
        out_specs=[tile(d), pl.BlockSpec((None, tm * SUBLANES, LANES), lambda b, t: (b, t, 0)),
                   tile(ROUTE_W), const((1, ROUTE_W))],
        out_shape=[
            jax.ShapeDtypeStruct((bsz, s, d), F32),
            jax.ShapeDtypeStruct((bsz, s * SUBLANES, LANES), jnp.uint32),
            jax.ShapeDtypeStruct((bsz, s, ROUTE_W), F32),
            jax.ShapeDtypeStruct((1, ROUTE_W), F32),
        ],
        scratch_shapes=[pltpu.VMEM((1, ROUTE_W), F32)],
        compiler_params=pltpu.CompilerParams(
            dimension_semantics=("arbitrary", "arbitrary"), vmem_limit_bytes=VMEM_LIMIT),
        name="out_route",
    )(y, x, g1, w_bf, sc2, sh2, g, wr_bf, br, jnp.tril(jnp.ones((sub, sub), BF16), -1))


def _plan(rt, counts_f, n_tok):
    counts = counts_f[0, N_GROUPS:N_GROUPS + N_EXPERTS].astype(jnp.int32)
    padded = ((counts + MOE_BLOCK - 1) // MOE_BLOCK) * MOE_BLOCK
    padded_ends = jnp.cumsum(padded)
    padded_starts = padded_ends - padded
    n_rows = 2 * n_tok + N_EXPERTS * MOE_BLOCK
    n_blocks = n_rows // MOE_BLOCK
    expert_ids = jnp.arange(N_EXPERTS, dtype=jnp.int32)
    idx = rt[:, 0:6].astype(jnp.int32)
    start_of = jnp.sum(jnp.where(idx[:, 0:2, None] == expert_ids, padded_starts, 0), axis=-1)
    pos = start_of + idx[:, 4:6]
    block_start = jnp.arange(n_blocks, dtype=jnp.int32)[:, None] * MOE_BLOCK
    owns = (padded_starts <= block_start) & (block_start < padded_ends)
    n_valid = jnp.sum(jnp.where(owns, jnp.clip(counts - (block_start - padded_starts), 0, MOE_BLOCK), 0), axis=1)
    last_used = jnp.max(jnp.where(counts > 0, expert_ids, 0))
    block_expert = jnp.where(n_valid > 0, jnp.sum(jnp.where(owns, expert_ids, 0), axis=1), last_used)
    later_used = (expert_ids[None, :] > expert_ids[:, None]) & (counts[None, :] > 0)
    next_used = jnp.min(jnp.where(later_used, expert_ids[None, :], N_EXPERTS), axis=1)
    next_used = jnp.where(next_used == N_EXPERTS, -1, next_used)
    block_next = jnp.sum(jnp.where(owns, next_used, 0), axis=1)
    n_used = padded_ends[-1] // MOE_BLOCK
    block_index = jnp.minimum(jnp.arange(n_blocks, dtype=jnp.int32), n_used - 1)
    i32 = lambda a: a.astype(jnp.int32)
    return pos[:, 0], pos[:, 1], i32(block_expert), i32(n_valid), i32(block_index), i32(block_next), n_rows


def _dispatch_kernel(nvalid_ref, p0_ref, p1_ref, h_ref, xs_hbm, zbuf, sem, zsem, *, tm, n_blocks):
    @pl.when(pl.program_id(0) == 0)
    def _():
        zbuf[...] = jnp.zeros_like(zbuf)

        def zero_block(i):
            return pltpu.make_async_copy(zbuf, xs_hbm.at[pl.ds(i * MOE_BLOCK, MOE_BLOCK)], zsem)

        def start(i, carry):
            pl.when(nvalid_ref[i] < MOE_BLOCK)(lambda: zero_block(i).start())
            return carry

        def wait(i, carry):
            pl.when(nvalid_ref[i] < MOE_BLOCK)(lambda: zero_block(i).wait())
            return carry

        lax.fori_loop(0, n_blocks, start, 0)
        lax.fori_loop(0, n_blocks, wait, 0)

    def issue(c, carry):
        for j in range(DMA_ISSUE_UNROLL):
            r = c * DMA_ISSUE_UNROLL + j
            src = h_ref.at[r]
            pltpu.make_async_copy(src, xs_hbm.at[p0_ref[0, r]], sem).start(priority=0)
            pltpu.make_async_copy(src, xs_hbm.at[p1_ref[0, r]], sem).start(priority=1)
        return carry

    lax.fori_loop(0, tm // DMA_ISSUE_UNROLL, issue, 0)
    for _ in range(2):
        pltpu.make_async_copy(h_ref, xs_hbm.at[pl.ds(0, tm)], sem).wait()


def _dispatch(n_valid, pos0, pos1, h2p, n_rows):
    n = h2p.shape[0]
    tm = 1024
    n_blocks = n_valid.shape[0]
    idx_spec = pl.BlockSpec((None, 1, tm), lambda t, nv: (t, 0, 0), memory_space=pltpu.SMEM)
    grid_spec = pltpu.PrefetchScalarGridSpec(
        num_scalar_prefetch=1,
        grid=(n // tm,),
        in_specs=[idx_spec, idx_spec, pl.BlockSpec((tm, SUBLANES, LANES), lambda t, nv: (t, 0, 0))],
        out_specs=pl.BlockSpec(memory_space=pl.ANY),
        scratch_shapes=[
            pltpu.VMEM((MOE_BLOCK, SUBLANES, LANES), h2p.dtype),
            pltpu.SemaphoreType.DMA(()),
            pltpu.SemaphoreType.DMA(()),
        ],
    )
    return pl.pallas_call(
        functools.partial(_dispatch_kernel, tm=tm, n_blocks=n_blocks),
        grid_spec=grid_spec,
        out_shape=jax.ShapeDtypeStruct((n_rows, SUBLANES, LANES), h2p.dtype),
        compiler_params=pltpu.CompilerParams(
            dimension_semantics=("arbitrary",), vmem_limit_bytes=VMEM_LIMIT,
            disable_bounds_checks=True),
        name="dispatch",
    )(n_valid, pos0.reshape(n // tm, 1, tm), pos1.reshape(n // tm, 1, tm), h2p)


def _moe_kernel(be_ref, nvalid_ref, bi_ref, nxt_ref, xs_hbm, wg_hbm, wu_hbm, wd_hbm, y_ref,
                wg_f, wu_f, wd_f, wg_b, wu_b, wd_b, xbuf, sem, xsem, *, n_blocks):
    i = pl.program_id(0)
    n_valid = nvalid_ref[i]
    block_rows = MOE_BLOCK * SUBLANES

    def x_copy(j):
        start = pl.multiple_of(bi_ref[j] * block_rows, block_rows)
        return pltpu.make_async_copy(xs_hbm.at[pl.ds(start, block_rows)], xbuf.at[j % X_RING], xsem.at[j % X_RING])

    @pl.when(i == 0)
    def _():
        for j in range(X_RING - 1):
            x_copy(j).start()

    @pl.when(i + X_RING - 1 < n_blocks)
    def _():
        x_copy(i + X_RING - 1).start()

    x_copy(i).wait()
    x_ref = xbuf.at[i % X_RING]

    def fetch(e):
        return (pltpu.make_async_copy(wg_hbm.at[e], wg_f, sem.at[0]),
                pltpu.make_async_copy(wu_hbm.at[e], wu_f, sem.at[1]),
                pltpu.make_async_copy(wd_hbm.at[e], wd_f, sem.at[2]))

    @pl.when(n_valid > 0)
    def _():
        e = be_ref[i]

        @pl.when(i == 0)
        def _():
            for cp in fetch(e):
                cp.start(priority=1)

        @pl.when((i == 0) | (e != be_ref[jnp.maximum(i - 1, 0)]))
        def _():
            for cp in fetch(e):
                cp.wait()
            wg_b[...] = wg_f[...].astype(BF16)
            wu_b[...] = wu_f[...].astype(BF16)
            wd_b[...] = wd_f[...].astype(BF16)
            nxt = nxt_ref[i]

            @pl.when(nxt >= 0)
            def _():
                for cp in fetch(nxt):
                    cp.start(priority=1)

        words = jnp.concatenate(
            [x_ref[pl.ds(c, MOE_BLOCK, stride=SUBLANES), :] for c in range(SUBLANES)], axis=1)
        lo, hi = _unpack_bf16_pair(words)
        half = words.shape[1]
        g = _dot(lo, wg_b[:half, :]) + _dot(hi, wg_b[half:, :])
        u = _dot(lo, wu_b[:half, :]) + _dot(hi, wu_b[half:, :])
        act = (g * jax.nn.sigmoid(g) * u).astype(BF16)
        y = _dot(act, wd_b[...])
        y_words = _pack_bf16_pair(y[:, :half], y[:, half:])
        for c in range(SUBLANES):
            y_ref[pl.ds(c, MOE_BLOCK, stride=SUBLANES), :] = y_words[:, c * LANES:(c + 1) * LANES]

    @pl.when(n_valid == 0)
    def _():
        y_ref[...] = jnp.zeros_like(y_ref)


def _moe(block_expert, n_valid, block_index, block_next, xs, w_gate, w_up, w_down):
    n_blocks = block_expert.shape[0]
    n_rows = xs.shape[0]
    d = 2 * SUBLANES * LANES
    hbm = pl.BlockSpec(memory_space=pl.ANY)
    grid_spec = pltpu.PrefetchScalarGridSpec(
        num_scalar_prefetch=4,
        grid=(n_blocks,),
        in_specs=[hbm, hbm, hbm, hbm],
        out_specs=pl.BlockSpec((MOE_BLOCK * SUBLANES, LANES), lambda i, be, nv, bi, nx: (i, 0)),
        scratch_shapes=[
            pltpu.VMEM((d, D_EXPERT), F32),
            pltpu.VMEM((d, D_EXPERT), F32),
            pltpu.VMEM((D_EXPERT, d), F32),
            pltpu.VMEM((d, D_EXPERT), BF16),
            pltpu.VMEM((d, D_EXPERT), BF16),
            pltpu.VMEM((D_EXPERT, d), BF16),
            pltpu.VMEM((X_RING, MOE_BLOCK * SUBLANES, LANES), jnp.uint32),
            pltpu.SemaphoreType.DMA((3,)),
            pltpu.SemaphoreType.DMA((X_RING,)),
        ],
    )
    return pl.pallas_call(
        functools.partial(_moe_kernel, n_blocks=n_blocks),
        grid_spec=grid_spec,
        out_shape=jax.ShapeDtypeStruct((n_rows * SUBLANES, LANES), jnp.uint32),
        compiler_params=pltpu.CompilerParams(
            dimension_semantics=("arbitrary",), vmem_limit_bytes=VMEM_LIMIT),
        name="moe",
    )(block_expert, n_valid, block_index, block_next, xs.reshape(n_rows * SUBLANES, LANES), w_gate, w_up, w_down)


def _final_kernel(p0_ref, p1_ref, q0_ref, q1_ref, x1_ref, rt_ref, g2_ref, gf_ref, yb_hbm, yb_flat_hbm, o_ref,
                  y0buf, y1buf, sem, *, tm, n_tiles):
    t = pl.program_id(0)
    slot = t & 1

    def issue(pa_ref, pb_ref, s):
        def body(c, carry):
            for j in range(DMA_ISSUE_UNROLL):
                r = c * DMA_ISSUE_UNROLL + j
                dst = pl.ds(pl.multiple_of(r * SUBLANES, SUBLANES), SUBLANES)
                pltpu.make_async_copy(yb_hbm.at[pa_ref[0, r]], y0buf.at[s].at[dst], sem.at[s]).start(priority=0)
                pltpu.make_async_copy(yb_hbm.at[pb_ref[0, r]], y1buf.at[s].at[dst], sem.at[s]).start(priority=1)
            return carry

        lax.fori_loop(0, tm // DMA_ISSUE_UNROLL, body, 0)

    @pl.when(t == 0)
    def _():
        issue(p0_ref, p1_ref, 0)

    @pl.when(t + 1 < n_tiles)
    def _():
        issue(q0_ref, q1_ref, 1 - slot)

    pltpu.make_async_copy(yb_flat_hbm.at[pl.ds(0, tm * SUBLANES)], y0buf.at[slot], sem.at[slot]).wait()
    pltpu.make_async_copy(yb_flat_hbm.at[pl.ds(0, tm * SUBLANES)], y1buf.at[slot], sem.at[slot]).wait()

    def rows_of(buf):
        tiles = buf.at[slot]
        words = jnp.concatenate([tiles[pl.ds(c, tm, stride=SUBLANES), :] for c in range(SUBLANES)], axis=1)
        lo = lax.bitcast_convert_type(words << 16, F32)
        hi = lax.bitcast_convert_type(words & jnp.uint32(0xFFFF0000), F32)
        return jnp.concatenate([lo, hi], axis=1)

    rt = rt_ref[...]
    y = rt[:, 2:3] * rows_of(y0buf) + rt[:, 3:4] * rows_of(y1buf)
    x2 = x1_ref[...] + g2_ref[...] * y
    ms = jnp.mean(x2 * x2, axis=-1, keepdims=True)
    o_ref[...] = x2 * lax.rsqrt(ms + EPS) * gf_ref[...]


def _final(pos0, pos1, x1, rt, g2, gf, yb, seq):
    n, d = x1.shape
    tm = 512
    n_tiles = n // tm
    per_seq = seq // tm
    this_tile = pl.BlockSpec((None, 1, tm), lambda t: (t, 0, 0), memory_space=pltpu.SMEM)
    next_tile = pl.BlockSpec((None, 1, tm), lambda t: (jnp.minimum(t + 1, n_tiles - 1), 0, 0),
                             memory_space=pltpu.SMEM)
    p0 = pos0.reshape(n_tiles, 1, tm)
    p1 = pos1.reshape(n_tiles, 1, tm)
    return pl.pallas_call(
        functools.partial(_final_kernel, tm=tm, n_tiles=n_tiles),
        grid=(n_tiles,),
        in_specs=[
            this_tile, this_tile, next_tile, next_tile,
            pl.BlockSpec((tm, d), lambda t: (t, 0)),
            pl.BlockSpec((tm, ROUTE_W), lambda t: (t, 0)),
            pl.BlockSpec((None, 1, d), lambda t: (t // per_seq, 0, 0)),
            pl.BlockSpec((1, d), lambda t: (0, 0)),
            pl.BlockSpec(memory_space=pl.ANY),
            pl.BlockSpec(memory_space=pl.ANY),
        ],
        out_specs=pl.BlockSpec((tm, d), lambda t: (t, 0)),
        out_shape=jax.ShapeDtypeStruct((n, d), F32),
        scratch_shapes=[
            pltpu.VMEM((2, tm * SUBLANES, LANES), jnp.uint32),
            pltpu.VMEM((2, tm * SUBLANES, LANES), jnp.uint32),
            pltpu.SemaphoreType.DMA((2,)),
        ],
        compiler_params=pltpu.CompilerParams(
            dimension_semantics=("arbitrary",), vmem_limit_bytes=VMEM_LIMIT,
            disable_bounds_checks=True),
        name="final",
    )(p0, p1, p0, p1, x1, rt, g2, gf, yb.reshape(-1, SUBLANES, LANES), yb)


def kernel(x, c, ada_w, ada_b, norm1_g, w_in, conv3_w, conv4_w, conv4_b, lru_w_a, lru_b_a, lru_w_x, lru_b_x, lru_lambda, head_norm_conv_g, head_norm_lru_g, w_out, norm2_g, route_w_group, route_b_group, route_w_expert, route_b_expert, w_e_gate, w_e_up, w_e_down, final_norm_g):
    bsz, s, d = x.shape
    n_tok = bsz * s
    l = 0

    mod = _ada(c, ada_w[l], ada_b[l]).reshape(bsz, 6, 1, d)
    sh1, sc1, g1, sh2, sc2, g2 = (mod[:, j] for j in range(6))

    proj = _in_proj(x, sc1, sh1, norm1_g[l].reshape(1, d), w_in[l].astype(BF16))

    wax = jnp.concatenate([lru_w_a[l], lru_w_x[l]], axis=-1).astype(BF16)
    y = _mix(proj, conv3_w[l], conv4_w[l], conv4_b[l], wax, lru_b_a[l], lru_b_x[l], lru_lambda[l],
             head_norm_conv_g[l], head_norm_lru_g[l])

    w_route = jnp.concatenate(
        [route_w_group[l], jnp.transpose(route_w_expert[l], (1, 0, 2)).reshape(d, N_EXPERTS),
         jnp.zeros((d, ROUTE_W - N_GROUPS - N_EXPERTS), F32)], axis=1).astype(BF16)
    b_route = jnp.concatenate(
        [route_b_group[l], route_b_expert[l].reshape(-1),
         jnp.zeros((ROUTE_W - N_GROUPS - N_EXPERTS,), F32)]).reshape(1, ROUTE_W)
    x1, h2p, rt, counts = _out(y, x, g1, w_out[l].astype(BF16), sc2, sh2, norm2_g[l].reshape(1, d),
                               w_route, b_route)

    rt = rt.reshape(n_tok, ROUTE_W)
    pos0, pos1, block_expert, n_valid, block_index, block_next, n_rows = _plan(rt, counts, n_tok)
    assert d // 2 == SUBLANES * LANES, "a token's packed words must fill exactly one (8, 128) tile"
    xs = _dispatch(n_valid, pos0, pos1, h2p.reshape(n_tok, SUBLANES, LANES), n_rows)
    yb = _moe(block_expert, n_valid, block_index, block_next, xs, w_e_gate[l], w_e_up[l], w_e_down[l])
    out = _final(pos0, pos1, x1.reshape(n_tok, d), rt, g2, final_norm_g.reshape(1, d), yb, s)
    return out.reshape(bsz, s, d)
```

```python
import functools

import jax
import jax.numpy as jnp
from jax import lax
from jax.experimental import pallas as pl
from jax.experimental.pallas import tpu as pltpu

D_MODEL = 2048
D_CONV = 1024
CONV_HEAD_DIM = 64
D_LRU = 1024
LRU_HEADS = 8
LRU_HEAD_DIM = 128
LRU_C = 8.0
D_MIX = D_CONV + D_LRU
D_IN = 3 * D_CONV + 2 * D_LRU
N_GROUPS = 8
EXPERTS_PER_GROUP = 8
N_EXPERTS = 64
D_EXPERT = 512
EPS = 1e-6

LANES = 128
SUBLANES = 8
ROUTE_W = LANES
MOE_BLOCK = 256
X_RING = 3
DMA_ISSUE_UNROLL = 8
VMEM_LIMIT = 56 * 1024 * 1024

F32 = jnp.float32
BF16 = jnp.bfloat16


def _dot(a, b):
    return jnp.dot(a, b, preferred_element_type=F32)


def _ada_kernel(c_ref, w_ref, b_ref, o_ref):
    c = c_ref[...]
    s = (c * jax.nn.sigmoid(c)).astype(BF16)
    o_ref[...] = _dot(s, w_ref[...].astype(BF16)) + b_ref[...]


def _ada(c, w, b):
    bsz, d = c.shape
    n = w.shape[1]
    tn = 1024
    return pl.pallas_call(
        _ada_kernel,
        grid=(n // tn,),
        in_specs=[
            pl.BlockSpec((bsz, d), lambda j: (0, 0)),
            pl.BlockSpec((d, tn), lambda j: (0, j)),
            pl.BlockSpec((1, tn), lambda j: (0, j)),
        ],
        out_specs=pl.BlockSpec((bsz, tn), lambda j: (0, j)),
        out_shape=jax.ShapeDtypeStruct((bsz, n), F32),
        compiler_params=pltpu.CompilerParams(
            dimension_semantics=("arbitrary",), vmem_limit_bytes=VMEM_LIMIT),
        name="ada",
    )(c, w, b.reshape(1, n))


def _modulated_norm(x, g, sc, sh):
    ms = jnp.mean(x * x, axis=-1, keepdims=True)
    return x * lax.rsqrt(ms + EPS) * (g * (1.0 + sc)) + sh


def _in_proj_kernel(x_ref, sc_ref, sh_ref, g_ref, w_ref, o_ref, *, n_chunk):
    hb = _modulated_norm(x_ref[...], g_ref[...], sc_ref[...], sh_ref[...]).astype(BF16)
    for j in range(D_IN // n_chunk):
        cols = slice(j * n_chunk, (j + 1) * n_chunk)
        o_ref[:, cols] = _dot(hb, w_ref[:, cols])


def _in_proj(x, sc, sh, g, w_bf):
    bsz, s, d = x.shape
    tm = 512
    per_batch = pl.BlockSpec((None, 1, d), lambda b, t: (b, 0, 0))
    return pl.pallas_call(
        functools.partial(_in_proj_kernel, n_chunk=1024),
        grid=(bsz, s // tm),
        in_specs=[
            pl.BlockSpec((None, tm, d), lambda b, t: (b, t, 0)),
            per_batch, per_batch,
            pl.BlockSpec((1, d), lambda b, t: (0, 0)),
            pl.BlockSpec((d, D_IN), lambda b, t: (0, 0), pipeline_mode=pl.Buffered(1)),
        ],
        out_specs=pl.BlockSpec((None, tm, D_IN), lambda b, t: (b, t, 0)),
        out_shape=jax.ShapeDtypeStruct((bsz, s, D_IN), F32),
        compiler_params=pltpu.CompilerParams(
            dimension_semantics=("arbitrary", "arbitrary"), vmem_limit_bytes=VMEM_LIMIT),
        name="in_proj",
    )(x, sc, sh, g, w_bf)


def _shift_rows(x, prev8, d, row):
    rolled = pltpu.roll(x, d, 0)
    head = jnp.where(row[:SUBLANES] >= d, rolled[:SUBLANES], pltpu.roll(prev8, d, 0))
    return jnp.concatenate([head, rolled[SUBLANES:]], axis=0)


def _head_rms(y, p_ref, g):
    sq = y * y
    hi = sq.astype(BF16)
    lo = (sq - hi.astype(F32)).astype(BF16)
    ms = _dot(hi, p_ref[...]) + _dot(lo, p_ref[...])
    return y * lax.rsqrt(ms + EPS) * g


def _linear_scan(a, v, h0, sub):
    groups = []
    for r0 in range(0, a.shape[0], SUBLANES):
        ag, vg = a[r0:r0 + SUBLANES], v[r0:r0 + SUBLANES]
        d = 1
        while d < SUBLANES:
            keep = sub >= d
            vg = vg + ag * jnp.where(keep, pltpu.roll(vg, d, 0), 0.0)
            ag = ag * jnp.where(keep, pltpu.roll(ag, d, 0), 1.0)
            d *= 2
        groups.append((ag, vg))
    out, carry = [], h0
    for ag, vg in groups:
        hg = vg + ag * carry
        carry = hg[SUBLANES - 1:, :]
        out.append(hg)
    return jnp.concatenate(out, axis=0), carry


def _gelu_tanh(x):
    c = 0.7978845608028654
    half_x = 0.5 * x
    return half_x + half_x * jnp.tanh(x * (c + (c * 0.044715) * (x * x)))


def _mix_kernel(p_ref, w3_ref, w4_ref, b4_ref, wax_ref, ba_ref, bx_ref, lam_ref, ga_ref, gb_ref,
                pa_ref, pb_ref, y_ref, ua_carry, xb_carry, h_carry, *, ts):
    @pl.when(pl.program_id(1) == 0)
    def _():
        ua_carry[...] = jnp.zeros_like(ua_carry)
        xb_carry[...] = jnp.zeros_like(xb_carry)
        h_carry[...] = jnp.zeros_like(h_carry)

    row = lax.broadcasted_iota(jnp.int32, (ts, LANES), 0)

    for k in range(D_CONV // LANES):
        cols = slice(k * LANES, (k + 1) * LANES)
        b_a = p_ref[:, k * LANES:(k + 1) * LANES]
        c_a = p_ref[:, D_CONV + k * LANES:D_CONV + (k + 1) * LANES]
        x_a = p_ref[:, 2 * D_CONV + k * LANES:2 * D_CONV + (k + 1) * LANES]
        u = c_a * x_a
        prev = ua_carry[:, cols]
        conv = (w3_ref[2:3, cols] * u
                + w3_ref[1:2, cols] * _shift_rows(u, prev, 1, row)
                + w3_ref[0:1, cols] * _shift_rows(u, prev, 2, row))
        ua_carry[:, cols] = u[ts - SUBLANES:, :]
        y_a = b_a * conv
        y_ref[:, cols] = _head_rms(y_a, pa_ref, ga_ref[:, cols]).astype(BF16)

        g_b = p_ref[:, 3 * D_CONV + k * LANES:3 * D_CONV + (k + 1) * LANES]
        x_b = p_ref[:, 3 * D_CONV + D_LRU + k * LANES:3 * D_CONV + D_LRU + (k + 1) * LANES]
        prevb = xb_carry[:, cols]
        xc = (w4_ref[3:4, cols] * x_b
              + w4_ref[2:3, cols] * _shift_rows(x_b, prevb, 1, row)
              + w4_ref[1:2, cols] * _shift_rows(x_b, prevb, 2, row)
              + w4_ref[0:1, cols] * _shift_rows(x_b, prevb, 3, row)
              + b4_ref[:, cols])
        xb_carry[:, cols] = x_b[ts - SUBLANES:, :]
        gates = _dot(xc.astype(BF16), wax_ref[k])
        r = jax.nn.sigmoid(gates[:, :LANES] + ba_ref[:, cols])
        i = jax.nn.sigmoid(gates[:, LANES:] + bx_ref[:, cols])
        nlam = -lam_ref[:, cols]
        softplus = jnp.maximum(nlam, 0.0) + jnp.log1p(jnp.exp(-jnp.abs(nlam)))
        log_a = r * ((-LRU_C) * softplus)
        a = jnp.exp(log_a)
        mult = jnp.sqrt(-jnp.tanh(log_a) * (a * a + 1.0))
        v = mult * i * xc
        hs, h_last = _linear_scan(a, v, h_carry[:, cols], row[:SUBLANES])
        h_carry[:, cols] = h_last
        y_b = hs * _gelu_tanh(g_b)
        y_ref[:, D_CONV + k * LANES:D_CONV + (k + 1) * LANES] = _head_rms(
            y_b, pb_ref, gb_ref[:, cols]).astype(BF16)


def _mix(proj, w3, w4, b4, wax_bf, b_a, b_x, lam, g_a, g_b):
    bsz, s, _ = proj.shape
    ts = 256
    lane = jnp.arange(LANES)
    p_a = ((lane[:, None] // CONV_HEAD_DIM) == (lane[None, :] // CONV_HEAD_DIM)).astype(BF16) / CONV_HEAD_DIM
    p_b = jnp.full((LANES, LANES), 1.0 / LRU_HEAD_DIM, BF16)
    full = lambda shape: pl.BlockSpec(shape, lambda b, t: (0,) * len(shape))
    row = lambda a: a.reshape(1, -1)
    return pl.pallas_call(
        functools.partial(_mix_kernel, ts=ts),
        grid=(bsz, s // ts),
        in_specs=[
            pl.BlockSpec((None, ts, D_IN), lambda b, t: (b, t, 0)),
            full((3, D_CONV)), full((4, D_LRU)), full((1, D_LRU)),
            full((LRU_HEADS, LRU_HEAD_DIM, 2 * LRU_HEAD_DIM)),
            full((1, D_LRU)), full((1, D_LRU)), full((1, D_LRU)),
            full((1, D_CONV)), full((1, D_LRU)),
            full((LANES, LANES)), full((LANES, LANES)),
        ],
        out_specs=pl.BlockSpec((None, ts, D_MIX), lambda b, t: (b, t, 0)),
        out_shape=jax.ShapeDtypeStruct((bsz, s, D_MIX), BF16),
        scratch_shapes=[
            pltpu.VMEM((SUBLANES, D_CONV), F32),
            pltpu.VMEM((SUBLANES, D_LRU), F32),
            pltpu.VMEM((1, D_LRU), F32),
        ],
        compiler_params=pltpu.CompilerParams(
            dimension_semantics=("arbitrary", "arbitrary"), vmem_limit_bytes=VMEM_LIMIT),
        name="mix",
    )(proj, w3, w4, row(b4), wax_bf, row(b_a), row(b_x), row(lam), row(g_a), row(g_b), p_a, p_b)


def _route(logits, tri, carry):
    lane = lax.broadcasted_iota(jnp.int32, logits.shape, 1)
    neg = -jnp.inf
    is_g = lane < N_GROUPS
    gl = jnp.where(is_g, logits, neg)
    ge = jnp.exp(gl - jnp.max(gl, axis=-1, keepdims=True))
    p_g = 1.0 / jnp.sum(ge, axis=-1, keepdims=True)
    g_idx = jnp.min(jnp.where(is_g & (ge >= 1.0), lane, LANES), axis=-1, keepdims=True)
    sel = (lane >= N_GROUPS) & (lane < N_GROUPS + N_EXPERTS) & (((lane - N_GROUPS) >> 3) == g_idx)
    el = jnp.where(sel, logits, neg)
    m1 = jnp.max(el, axis=-1, keepdims=True)
    i1 = jnp.min(jnp.where(sel & (el == m1), lane, LANES), axis=-1, keepdims=True)
    sel2 = sel & (lane != i1)
    el2 = jnp.where(sel2, logits, neg)
    m2 = jnp.max(el2, axis=-1, keepdims=True)
    i2 = jnp.min(jnp.where(sel2 & (el2 == m2), lane, LANES), axis=-1, keepdims=True)
    e2 = jnp.exp(m2 - m1)
    w1 = p_g * (1.0 / (1.0 + e2))
    w2 = p_g * (e2 / (1.0 + e2))
    out = jnp.where(lane == 0, (i1 - N_GROUPS).astype(F32), 0.0)
    out = jnp.where(lane == 1, (i2 - N_GROUPS).astype(F32), out)
    out = jnp.where(lane == 2, w1, out)
    out = jnp.where(lane == 3, w2, out)
    oh1 = lane == i1
    oh2 = lane == i2
    both = jnp.where(oh1 | oh2, 1.0, 0.0)
    before = _dot(tri, both.astype(BF16)) + carry
    out = jnp.where(lane == 4, jnp.sum(jnp.where(oh1, before, 0.0), axis=-1, keepdims=True), out)
    out = jnp.where(lane == 5, jnp.sum(jnp.where(oh2, before, 0.0), axis=-1, keepdims=True), out)
    return out, carry + jnp.sum(both, axis=0, keepdims=True)


def _pack_bf16_pair(lo, hi):
    lo_bits = lax.bitcast_convert_type(lo.astype(BF16).astype(F32), jnp.uint32)
    hi_bits = lax.bitcast_convert_type(hi.astype(BF16).astype(F32), jnp.uint32)
    return (lo_bits >> 16) | hi_bits


def _unpack_bf16_pair(w):
    lo = lax.bitcast_convert_type(w << 16, F32).astype(BF16)
    hi = lax.bitcast_convert_type(w & jnp.uint32(0xFFFF0000), F32).astype(BF16)
    return lo, hi


def _out_kernel(y_ref, x_ref, g1_ref, w_ref, sc_ref, sh_ref, g_ref, wr_ref, br_ref, tri_ref,
                x1_ref, h2p_ref, rt_ref, cnt_ref, carry):
    @pl.when((pl.program_id(0) == 0) & (pl.program_id(1) == 0))
    def _():
        carry[...] = jnp.zeros_like(carry)

    sub = tri_ref.shape[0]
    count = carry[...]
    tiles = [slice(r0, r0 + sub) for r0 in range(0, y_ref.shape[0], sub)]
    mixes = [_dot(y_ref[rows, :], w_ref[...]) for rows in tiles]
    for rows, mix in zip(tiles, mixes):
        x1 = x_ref[rows, :] + g1_ref[...] * mix
        x1_ref[rows, :] = x1
        h2 = _modulated_norm(x1, g_ref[...], sc_ref[...], sh_ref[...])
        half = h2.shape[1] // 2
        words = _pack_bf16_pair(h2[:, :half], h2[:, half:])
        for c in range(half // LANES):
            h2p_ref[pl.ds(rows.start * SUBLANES + c, sub, stride=SUBLANES), :] = words[:, c * LANES:(c + 1) * LANES]
        logits = _dot(h2.astype(BF16), wr_ref[...]) + br_ref[...]
        rt, count = _route(logits, tri_ref[...], count)
        rt_ref[rows, :] = rt
    carry[...] = count
    cnt_ref[...] = count


def _out(y, x, g1, w_bf, sc2, sh2, g, wr_bf, br):
    bsz, s, d = x.shape
    tm, sub = 512, 256
    per_batch = pl.BlockSpec((None, 1, d), lambda b, t: (b, 0, 0))
    tile = lambda w: pl.BlockSpec((None, tm, w), lambda b, t: (b, t, 0))
    const = lambda shape, **kw: pl.BlockSpec(shape, lambda b, t: (0, 0), **kw)
    return pl.pallas_call(
        _out_kernel,
        grid=(bsz, s // tm),
        in_specs=[
            tile(D_MIX), tile(d), per_batch,
            const((D_MIX, d), pipeline_mode=pl.Buffered(1)),
            per_batch, per_batch, const((1, d)),
            const((d, ROUTE_W)), const((1, ROUTE_W)), const((sub, sub)),
        ],
        out_specs=[tile(d), pl.BlockSpec((None, tm * SUBLANES, LANES), lambda b, t: (b, t, 0)),
                   tile(ROUTE_W), const((1, ROUTE_W))],
        out_shape=[
            jax.ShapeDtypeStruct((bsz, s, d), F32),
            jax.ShapeDtypeStruct((bsz, s * SUBLANES, LANES), jnp.uint32),
            jax.ShapeDtypeStruct((bsz, s, ROUTE_W), F32),
            jax.ShapeDtypeStruct((1, ROUTE_W), F32),
        ],
        scratch_shapes=[pltpu.VMEM((1, ROUTE_W), F32)],
        compiler_params=pltpu.CompilerParams(
            dimension_semantics=("arbitrary", "arbitrary"), vmem_limit_bytes=VMEM_LIMIT),
        name="out_route",
    )(y, x, g1, w_bf, sc2, sh2, g, wr_bf, br, jnp.tril(jnp.ones((sub, sub), BF16), -1))


def _plan(rt, counts_f, n_tok):
    counts = counts_f[0, N_GROUPS:N_GROUPS + N_EXPERTS].astype(jnp.int32)
    padded = ((counts + MOE_BLOCK - 1) // MOE_BLOCK) * MOE_BLOCK
    padded_ends = jnp.cumsum(padded)
    padded_starts = padded_ends - padded
    n_rows = 2 * n_tok + N_EXPERTS * MOE_BLOCK
    n_blocks = n_rows // MOE_BLOCK
    expert_ids = jnp.arange(N_EXPERTS, dtype=jnp.int32)
    idx = rt[:, 0:6].astype(jnp.int32)
    start_of = jnp.sum(jnp.where(idx[:, 0:2, None] == expert_ids, padded_starts, 0), axis=-1)
    pos = start_of + idx[:, 4:6]
    block_start = jnp.arange(n_blocks, dtype=jnp.int32)[:, None] * MOE_BLOCK
    owns = (padded_starts <= block_start) & (block_start < padded_ends)
    n_valid = jnp.sum(jnp.where(owns, jnp.clip(counts - (block_start - padded_starts), 0, MOE_BLOCK), 0), axis=1)
    last_used = jnp.max(jnp.where(counts > 0, expert_ids, 0))
    block_expert = jnp.where(n_valid > 0, jnp.sum(jnp.where(owns, expert_ids, 0), axis=1), last_used)
    later_used = (expert_ids[None, :] > expert_ids[:, None]) & (counts[None, :] > 0)
    next_used = jnp.min(jnp.where(later_used, expert_ids[None, :], N_EXPERTS), axis=1)
    next_used = jnp.where(next_used == N_EXPERTS, -1, next_used)
    block_next = jnp.sum(jnp.where(owns, next_used, 0), axis=1)
    n_used = padded_ends[-1] // MOE_BLOCK
    block_index = jnp.minimum(jnp.arange(n_blocks, dtype=jnp.int32), n_used - 1)
    i32 = lambda a: a.astype(jnp.int32)
    return pos[:, 0], pos[:, 1], i32(block_expert), i32(n_valid), i32(block_index), i32(block_next), n_rows


def _dispatch_kernel(nvalid_ref, p0_ref, p1_ref, h_ref, xs_hbm, zbuf, sem, zsem, *, tm, n_blocks):
    @pl.when(pl.program_id(0) == 0)
    def _():
        zbuf[...] = jnp.zeros_like(zbuf)

        def zero_block(i):
            return pltpu.make_async_copy(zbuf, xs_hbm.at[pl.ds(i * MOE_BLOCK, MOE_BLOCK)], zsem)

        def start(i, carry):
            pl.when(nvalid_ref[i] < MOE_BLOCK)(lambda: zero_block(i).start())
            return carry

        def wait(i, carry):
            pl.when(nvalid_ref[i] < MOE_BLOCK)(lambda: zero_block(i).wait())
            return carry

        lax.fori_loop(0, n_blocks, start, 0)
        lax.fori_loop(0, n_blocks, wait, 0)

    def issue(c, carry):
        for j in range(DMA_ISSUE_UNROLL):
            r = c * DMA_ISSUE_UNROLL + j
            src = h_ref.at[r]
            pltpu.make_async_copy(src, xs_hbm.at[p0_ref[0, r]], sem).start(priority=0)
            pltpu.make_async_copy(src, xs_hbm.at[p1_ref[0, r]], sem).start(priority=1)
        return carry

    lax.fori_loop(0, tm // DMA_ISSUE_UNROLL, issue, 0)
    for _ in range(2):
        pltpu.make_async_copy(h_ref, xs_hbm.at[pl.ds(0, tm)], sem).wait()


def _dispatch(n_valid, pos0, pos1, h2p, n_rows):
    n = h2p.shape[0]
    tm = 2048
    n_blocks = n_valid.shape[0]
    idx_spec = pl.BlockSpec((None, 1, tm), lambda t, nv: (t, 0, 0), memory_space=pltpu.SMEM)
    grid_spec = pltpu.PrefetchScalarGridSpec(
        num_scalar_prefetch=1,
        grid=(n // tm,),
        in_specs=[idx_spec, idx_spec, pl.BlockSpec((tm, SUBLANES, LANES), lambda t, nv: (t, 0, 0))],
        out_specs=pl.BlockSpec(memory_space=pl.ANY),
        scratch_shapes=[
            pltpu.VMEM((MOE_BLOCK, SUBLANES, LANES), h2p.dtype),
            pltpu.SemaphoreType.DMA(()),
            pltpu.SemaphoreType.DMA(()),
        ],
    )
    return pl.pallas_call(
        functools.partial(_dispatch_kernel, tm=tm, n_blocks=n_blocks),
        grid_spec=grid_spec,
        out_shape=jax.ShapeDtypeStruct((n_rows, SUBLANES, LANES), h2p.dtype),
        compiler_params=pltpu.CompilerParams(
            dimension_semantics=("arbitrary",), vmem_limit_bytes=VMEM_LIMIT,
            disable_bounds_checks=True),
        name="dispatch",
    )(n_valid, pos0.reshape(n // tm, 1, tm), pos1.reshape(n // tm, 1, tm), h2p)


def _moe_kernel(be_ref, nvalid_ref, bi_ref, nxt_ref, xs_hbm, wg_hbm, wu_hbm, wd_hbm, y_ref,
                wg_f, wu_f, wd_f, wg_b, wu_b, wd_b, xbuf, sem, xsem, *, n_blocks):
    i = pl.program_id(0)
    n_valid = nvalid_ref[i]
    block_rows = MOE_BLOCK * SUBLANES

    def x_copy(j):
        start = pl.multiple_of(bi_ref[j] * block_rows, block_rows)
        return pltpu.make_async_copy(xs_hbm.at[pl.ds(start, block_rows)], xbuf.at[j % X_RING], xsem.at[j % X_RING])

    @pl.when(i == 0)
    def _():
        for j in range(X_RING - 1):
            x_copy(j).start()

    @pl.when(i + X_RING - 1 < n_blocks)
    def _():
        x_copy(i + X_RING - 1).start()

    x_copy(i).wait()
    x_ref = xbuf.at[i % X_RING]

    def fetch(e):
        return (pltpu.make_async_copy(wg_hbm.at[e], wg_f, sem.at[0]),
                pltpu.make_async_copy(wu_hbm.at[e], wu_f, sem.at[1]),
                pltpu.make_async_copy(wd_hbm.at[e], wd_f, sem.at[2]))

    @pl.when(n_valid > 0)
    def _():
        e = be_ref[i]

        @pl.when(i == 0)
        def _():
            for cp in fetch(e):
                cp.start(priority=1)

        @pl.when((i == 0) | (e != be_ref[jnp.maximum(i - 1, 0)]))
        def _():
            for cp in fetch(e):
                cp.wait()
            wg_b[...] = wg_f[...].astype(BF16)
            wu_b[...] = wu_f[...].astype(BF16)
            wd_b[...] = wd_f[...].astype(BF16)
            nxt = nxt_ref[i]

            @pl.when(nxt >= 0)
            def _():
                for cp in fetch(nxt):
                    cp.start(priority=1)

        words = jnp.concatenate(
            [x_ref[pl.ds(c, MOE_BLOCK, stride=SUBLANES), :] for c in range(SUBLANES)], axis=1)
        lo, hi = _unpack_bf16_pair(words)
        half = words.shape[1]
        g = _dot(lo, wg_b[:half, :]) + _dot(hi, wg_b[half:, :])
        u = _dot(lo, wu_b[:half, :]) + _dot(hi, wu_b[half:, :])
        act = (g * jax.nn.sigmoid(g) * u).astype(BF16)
        y = _dot(act, wd_b[...])
        y_words = _pack_bf16_pair(y[:, :half], y[:, half:])
        for c in range(SUBLANES):
            y_ref[pl.ds(c, MOE_BLOCK, stride=SUBLANES), :] = y_words[:, c * LANES:(c + 1) * LANES]

    @pl.when(n_valid == 0)
    def _():
        y_ref[...] = jnp.zeros_like(y_ref)


def _moe(block_expert, n_valid, block_index, block_next, xs, w_gate, w_up, w_down):
    n_blocks = block_expert.shape[0]
    n_rows = xs.shape[0]
    d = 2 * SUBLANES * LANES
    hbm = pl.BlockSpec(memory_space=pl.ANY)
    grid_spec = pltpu.PrefetchScalarGridSpec(
        num_scalar_prefetch=4,
        grid=(n_blocks,),
        in_specs=[hbm, hbm, hbm, hbm],
        out_specs=pl.BlockSpec((MOE_BLOCK * SUBLANES, LANES), lambda i, be, nv, bi, nx: (i, 0)),
        scratch_shapes=[
            pltpu.VMEM((d, D_EXPERT), F32),
            pltpu.VMEM((d, D_EXPERT), F32),
            pltpu.VMEM((D_EXPERT, d), F32),
            pltpu.VMEM((d, D_EXPERT), BF16),
            pltpu.VMEM((d, D_EXPERT), BF16),
            pltpu.VMEM((D_EXPERT, d), BF16),
            pltpu.VMEM((X_RING, MOE_BLOCK * SUBLANES, LANES), jnp.uint32),
            pltpu.SemaphoreType.DMA((3,)),
            pltpu.SemaphoreType.DMA((X_RING,)),
        ],
    )
    return pl.pallas_call(
        functools.partial(_moe_kernel, n_blocks=n_blocks),
        grid_spec=grid_spec,
        out_shape=jax.ShapeDtypeStruct((n_rows * SUBLANES, LANES), jnp.uint32),
        compiler_params=pltpu.CompilerParams(
            dimension_semantics=("arbitrary",), vmem_limit_bytes=VMEM_LIMIT),
        name="moe",
    )(block_expert, n_valid, block_index, block_next, xs.reshape(n_rows * SUBLANES, LANES), w_gate, w_up, w_down)


def _final_kernel(p0_ref, p1_ref, q0_ref, q1_ref, x1_ref, rt_ref, g2_ref, gf_ref, yb_hbm, yb_flat_hbm, o_ref,
                  y0buf, y1buf, sem, *, tm, n_tiles):
    t = pl.program_id(0)
    slot = t & 1

    def issue(pa_ref, pb_ref, s):
        def body(c, carry):
            for j in range(DMA_ISSUE_UNROLL):
                r = c * DMA_ISSUE_UNROLL + j
                dst = pl.ds(pl.multiple_of(r * SUBLANES, SUBLANES), SUBLANES)
                pltpu.make_async_copy(yb_hbm.at[pa_ref[0, r]], y0buf.at[s].at[dst], sem.at[s]).start(priority=0)
                pltpu.make_async_copy(yb_hbm.at[pb_ref[0, r]], y1buf.at[s].at[dst], sem.at[s]).start(priority=1)
            return carry

        lax.fori_loop(0, tm // DMA_ISSUE_UNROLL, body, 0)

    @pl.when(t == 0)
    def _():
        issue(p0_ref, p1_ref, 0)

    @pl.when(t + 1 < n_tiles)
    def _():
        issue(q0_ref, q1_ref, 1 - slot)

    pltpu.make_async_copy(yb_flat_hbm.at[pl.ds(0, tm * SUBLANES)], y0buf.at[slot], sem.at[slot]).wait()
    pltpu.make_async_copy(yb_flat_hbm.at[pl.ds(0, tm * SUBLANES)], y1buf.at[slot], sem.at[slot]).wait()

    def rows_of(buf):
        tiles = buf.at[slot]
        words = jnp.concatenate([tiles[pl.ds(c, tm, stride=SUBLANES), :] for c in range(SUBLANES)], axis=1)
        lo = lax.bitcast_convert_type(words << 16, F32)
        hi = lax.bitcast_convert_type(words & jnp.uint32(0xFFFF0000), F32)
        return jnp.concatenate([lo, hi], axis=1)

    rt = rt_ref[...]
    y = rt[:, 2:3] * rows_of(y0buf) + rt[:, 3:4] * rows_of(y1buf)
    x2 = x1_ref[...] + g2_ref[...] * y
    ms = jnp.mean(x2 * x2, axis=-1, keepdims=True)
    o_ref[...] = x2 * lax.rsqrt(ms + EPS) * gf_ref[...]


def _final(pos0, pos1, x1, rt, g2, gf, yb, seq):
    n, d = x1.shape
    tm = 256
    n_tiles = n // tm
    per_seq = seq // tm
    this_tile = pl.BlockSpec((None, 1, tm), lambda t: (t, 0, 0), memory_space=pltpu.SMEM)
    next_tile = pl.BlockSpec((None, 1, tm), lambda t: (jnp.minimum(t + 1, n_tiles - 1), 0, 0),
                             memory_space=pltpu.SMEM)
    p0 = pos0.reshape(n_tiles, 1, tm)
    p1 = pos1.reshape(n_tiles, 1, tm)
    return pl.pallas_call(
        functools.partial(_final_kernel, tm=tm, n_tiles=n_tiles),
        grid=(n_tiles,),
        in_specs=[
            this_tile, this_tile, next_tile, next_tile,
            pl.BlockSpec((tm, d), lambda t: (t, 0)),
            pl.BlockSpec((tm, ROUTE_W), lambda t: (t, 0)),
            pl.BlockSpec((None, 1, d), lambda t: (t // per_seq, 0, 0)),
            pl.BlockSpec((1, d), lambda t: (0, 0)),
            pl.BlockSpec(memory_space=pl.ANY),
            pl.BlockSpec(memory_space=pl.ANY),
        ],
        out_specs=pl.BlockSpec((tm, d), lambda t: (t, 0)),
        out_shape=jax.ShapeDtypeStruct((n, d), F32),
        scratch_shapes=[
            pltpu.VMEM((2, tm * SUBLANES, LANES), jnp.uint32),
            pltpu.VMEM((2, tm * SUBLANES, LANES), jnp.uint32),
            pltpu.SemaphoreType.DMA((2,)),
        ],
        compiler_params=pltpu.CompilerParams(
            dimension_semantics=("arbitrary",), vmem_limit_bytes=VMEM_LIMIT,
            disable_bounds_checks=True),
        name="final",
    )(p0, p1, p0, p1, x1, rt, g2, gf, yb.reshape(-1, SUBLANES, LANES), yb)


def kernel(x, c, ada_w, ada_b, norm1_g, w_in, conv3_w, conv4_w, conv4_b, lru_w_a, lru_b_a, lru_w_x, lru_b_x, lru_lambda, head_norm_conv_g, head_norm_lru_g, w_out, norm2_g, route_w_group, route_b_group, route_w_expert, route_b_expert, w_e_gate, w_e_up, w_e_down, final_norm_g):
    bsz, s, d = x.shape
    n_tok = bsz * s
    l = 0

    mod = _ada(c, ada_w[l], ada_b[l]).reshape(bsz, 6, 1, d)
    sh1, sc1, g1, sh2, sc2, g2 = (mod[:, j] for j in range(6))

    proj = _in_proj(x, sc1, sh1, norm1_g[l].reshape(1, d), w_in[l].astype(BF16))

    wax = jnp.concatenate([lru_w_a[l], lru_w_x[l]], axis=-1).astype(BF16)
    y = _mix(proj, conv3_w[l], conv4_w[l], conv4_b[l], wax, lru_b_a[l], lru_b_x[l], lru_lambda[l],
             head_norm_conv_g[l], head_norm_lru_g[l])

    w_route = jnp.concatenate(
        [route_w_group[l], jnp.transpose(route_w_expert[l], (1, 0, 2)).reshape(d, N_EXPERTS),
         jnp.zeros((d, ROUTE_W - N_GROUPS - N_EXPERTS), F32)], axis=1).astype(BF16)
    b_route = jnp.concatenate(
        [route_b_group[l], route_b_expert[l].reshape(-1),
         jnp.zeros((ROUTE_W - N_GROUPS - N_EXPERTS,), F32)]).reshape(1, ROUTE_W)
    x1, h2p, rt, counts = _out(y, x, g1, w_out[l].astype(BF16), sc2, sh2, norm2_g[l].reshape(1, d),
                               w_route, b_route)

    rt = rt.reshape(n_tok, ROUTE_W)
    pos0, pos1, block_expert, n_valid, block_index, block_next, n_rows = _plan(rt, counts, n_tok)
    assert d // 2 == SUBLANES * LANES, "a token's packed words must fill exactly one (8, 128) tile"
    xs = _dispatch(n_valid, pos0, pos1, h2p.reshape(n_tok, SUBLANES, LANES), n_rows)
    yb = _moe(block_expert, n_valid, block_index, block_next, xs, w_e_gate[l], w_e_up[l], w_e_down[l])
    out = _final(pos0, pos1, x1.reshape(n_tok, d), rt, g2, final_norm_g.reshape(1, d), yb, s)
    return out.reshape(bsz, s, d)
```

```python
import functools

import jax
import jax.numpy as jnp
from jax import lax
from jax.experimental import pallas as pl
from jax.experimental.pallas import tpu as pltpu

D_MODEL = 2048
D_CONV = 1024
CONV_HEAD_DIM = 64
D_LRU = 1024
LRU_HEADS = 8
LRU_HEAD_DIM = 128
LRU_C = 8.0
D_MIX = D_CONV + D_LRU
D_IN = 3 * D_CONV + 2 * D_LRU
N_GROUPS = 8
EXPERTS_PER_GROUP = 8
N_EXPERTS = 64
D_EXPERT = 512
EPS = 1e-6

LANES = 128
SUBLANES = 8
ROUTE_W = LANES
MOE_BLOCK = 256
X_RING = 3
DMA_ISSUE_UNROLL = 8
VMEM_LIMIT = 56 * 1024 * 1024

F32 = jnp.float32
BF16 = jnp.bfloat16


def _dot(a, b):
    return jnp.dot(a, b, preferred_element_type=F32)


def _ada_kernel(c_ref, w_ref, b_ref, o_ref):
    c = c_ref[...]
    s = (c * jax.nn.sigmoid(c)).astype(BF16)
    o_ref[...] = _dot(s, w_ref[...].astype(BF16)) + b_ref[...]


def _ada(c, w, b):
    bsz, d = c.shape
    n = w.shape[1]
    tn = 1024
    return pl.pallas_call(
        _ada_kernel,
        grid=(n // tn,),
        in_specs=[
            pl.BlockSpec((bsz, d), lambda j: (0, 0)),
            pl.BlockSpec((d, tn), lambda j: (0, j)),
            pl.BlockSpec((1, tn), lambda j: (0, j)),
        ],
        out_specs=pl.BlockSpec((bsz, tn), lambda j: (0, j)),
        out_shape=jax.ShapeDtypeStruct((bsz, n), F32),
        compiler_params=pltpu.CompilerParams(
            dimension_semantics=("arbitrary",), vmem_limit_bytes=VMEM_LIMIT),
        name="ada",
    )(c, w, b.reshape(1, n))


def _modulated_norm(x, g, sc, sh):
    ms = jnp.mean(x * x, axis=-1, keepdims=True)
    return x * lax.rsqrt(ms + EPS) * (g * (1.0 + sc)) + sh


def _in_proj_kernel(x_ref, sc_ref, sh_ref, g_ref, w_ref, o_ref, *, n_chunk):
    hb = _modulated_norm(x_ref[...], g_ref[...], sc_ref[...], sh_ref[...]).astype(BF16)
    for j in range(D_IN // n_chunk):
        cols = slice(j * n_chunk, (j + 1) * n_chunk)
        o_ref[:, cols] = _dot(hb, w_ref[:, cols])


def _in_proj(x, sc, sh, g, w_bf):
    bsz, s, d = x.shape
    tm = 512
    per_batch = pl.BlockSpec((None, 1, d), lambda b, t: (b, 0, 0))
    return pl.pallas_call(
        functools.partial(_in_proj_kernel, n_chunk=1024),
        grid=(bsz, s // tm),
        in_specs=[
            pl.BlockSpec((None, tm, d), lambda b, t: (b, t, 0)),
            per_batch, per_batch,
            pl.BlockSpec((1, d), lambda b, t: (0, 0)),
            pl.BlockSpec((d, D_IN), lambda b, t: (0, 0), pipeline_mode=pl.Buffered(1)),
        ],
        out_specs=pl.BlockSpec((None, tm, D_IN), lambda b, t: (b, t, 0)),
        out_shape=jax.ShapeDtypeStruct((bsz, s, D_IN), F32),
        compiler_params=pltpu.CompilerParams(
            dimension_semantics=("arbitrary", "arbitrary"), vmem_limit_bytes=VMEM_LIMIT),
        name="in_proj",
    )(x, sc, sh, g, w_bf)


def _shift_rows(x, prev8, d, row):
    rolled = pltpu.roll(x, d, 0)
    head = jnp.where(row[:SUBLANES] >= d, rolled[:SUBLANES], pltpu.roll(prev8, d, 0))
    return jnp.concatenate([head, rolled[SUBLANES:]], axis=0)


def _head_rms(y, p_ref, g):
    sq = y * y
    hi = sq.astype(BF16)
    lo = (sq - hi.astype(F32)).astype(BF16)
    ms = _dot(hi, p_ref[...]) + _dot(lo, p_ref[...])
    return y * lax.rsqrt(ms + EPS) * g


def _linear_scan(a, v, h0, sub):
    groups = []
    for r0 in range(0, a.shape[0], SUBLANES):
        ag, vg = a[r0:r0 + SUBLANES], v[r0:r0 + SUBLANES]
        d = 1
        while d < SUBLANES:
            keep = sub >= d
            vg = vg + ag * jnp.where(keep, pltpu.roll(vg, d, 0), 0.0)
            ag = ag * jnp.where(keep, pltpu.roll(ag, d, 0), 1.0)
            d *= 2
        groups.append((ag, vg))
    out, carry = [], h0
    for ag, vg in groups:
        hg = vg + ag * carry
        carry = hg[SUBLANES - 1:, :]
        out.append(hg)
    return jnp.concatenate(out, axis=0), carry


def _gelu_tanh(x):
    c = 0.7978845608028654
    half_x = 0.5 * x
    return half_x + half_x * jnp.tanh(x * (c + (c * 0.044715) * (x * x)))


def _mix_kernel(p_ref, w3_ref, w4_ref, b4_ref, wax_ref, ba_ref, bx_ref, lam_ref, ga_ref, gb_ref,
                pa_ref, pb_ref, y_ref, ua_carry, xb_carry, h_carry, *, ts):
    @pl.when(pl.program_id(1) == 0)
    def _():
        ua_carry[...] = jnp.zeros_like(ua_carry)
        xb_carry[...] = jnp.zeros_like(xb_carry)
        h_carry[...] = jnp.zeros_like(h_carry)

    row = lax.broadcasted_iota(jnp.int32, (ts, LANES), 0)

    for k in range(D_CONV // LANES):
        cols = slice(k * LANES, (k + 1) * LANES)
        b_a = p_ref[:, k * LANES:(k + 1) * LANES]
        c_a = p_ref[:, D_CONV + k * LANES:D_CONV + (k + 1) * LANES]
        x_a = p_ref[:, 2 * D_CONV + k * LANES:2 * D_CONV + (k + 1) * LANES]
        u = c_a * x_a
        prev = ua_carry[:, cols]
        conv = (w3_ref[2:3, cols] * u
                + w3_ref[1:2, cols] * _shift_rows(u, prev, 1, row)
                + w3_ref[0:1, cols] * _shift_rows(u, prev, 2, row))
        ua_carry[:, cols] = u[ts - SUBLANES:, :]
        y_a = b_a * conv
        y_ref[:, cols] = _head_rms(y_a, pa_ref, ga_ref[:, cols]).astype(BF16)

        g_b = p_ref[:, 3 * D_CONV + k * LANES:3 * D_CONV + (k + 1) * LANES]
        x_b = p_ref[:, 3 * D_CONV + D_LRU + k * LANES:3 * D_CONV + D_LRU + (k + 1) * LANES]
        prevb = xb_carry[:, cols]
        xc = (w4_ref[3:4, cols] * x_b
              + w4_ref[2:3, cols] * _shift_rows(x_b, prevb, 1, row)
              + w4_ref[1:2, cols] * _shift_rows(x_b, prevb, 2, row)
              + w4_ref[0:1, cols] * _shift_rows(x_b, prevb, 3, row)
              + b4_ref[:, cols])
        xb_carry[:, cols] = x_b[ts - SUBLANES:, :]
        gates = _dot(xc.astype(BF16), wax_ref[k])
        r = jax.nn.sigmoid(gates[:, :LANES] + ba_ref[:, cols])
        i = jax.nn.sigmoid(gates[:, LANES:] + bx_ref[:, cols])
        nlam = -lam_ref[:, cols]
        softplus = jnp.maximum(nlam, 0.0) + jnp.log1p(jnp.exp(-jnp.abs(nlam)))
        log_a = r * ((-LRU_C) * softplus)
        a = jnp.exp(log_a)
        mult = jnp.sqrt(-jnp.tanh(log_a) * (a * a + 1.0))
        v = mult * i * xc
        hs, h_last = _linear_scan(a, v, h_carry[:, cols], row[:SUBLANES])
        h_carry[:, cols] = h_last
        y_b = hs * _gelu_tanh(g_b)
        y_ref[:, D_CONV + k * LANES:D_CONV + (k + 1) * LANES] = _head_rms(
            y_b, pb_ref, gb_ref[:, cols]).astype(BF16)


def _mix(proj, w3, w4, b4, wax_bf, b_a, b_x, lam, g_a, g_b):
    bsz, s, _ = proj.shape
    ts = 256
    lane = jnp.arange(LANES)
    p_a = ((lane[:, None] // CONV_HEAD_DIM) == (lane[None, :] // CONV_HEAD_DIM)).astype(BF16) / CONV_HEAD_DIM
    p_b = jnp.full((LANES, LANES), 1.0 / LRU_HEAD_DIM, BF16)
    full = lambda shape: pl.BlockSpec(shape, lambda b, t: (0,) * len(shape))
    row = lambda a: a.reshape(1, -1)
    return pl.pallas_call(
        functools.partial(_mix_kernel, ts=ts),
        grid=(bsz, s // ts),
        in_specs=[
            pl.BlockSpec((None, ts, D_IN), lambda b, t: (b, t, 0)),
            full((3, D_CONV)), full((4, D_LRU)), full((1, D_LRU)),
            full((LRU_HEADS, LRU_HEAD_DIM, 2 * LRU_HEAD_DIM)),
            full((1, D_LRU)), full((1, D_LRU)), full((1, D_LRU)),
            full((1, D_CONV)), full((1, D_LRU)),
            full((LANES, LANES)), full((LANES, LANES)),
        ],
        out_specs=pl.BlockSpec((None, ts, D_MIX), lambda b, t: (b, t, 0)),
        out_shape=jax.ShapeDtypeStruct((bsz, s, D_MIX), BF16),
        scratch_shapes=[
            pltpu.VMEM((SUBLANES, D_CONV), F32),
            pltpu.VMEM((SUBLANES, D_LRU), F32),
            pltpu.VMEM((1, D_LRU), F32),
        ],
        compiler_params=pltpu.CompilerParams(
            dimension_semantics=("arbitrary", "arbitrary"), vmem_limit_bytes=VMEM_LIMIT),
        name="mix",
    )(proj, w3, w4, row(b4), wax_bf, row(b_a), row(b_x), row(lam), row(g_a), row(g_b), p_a, p_b)


def _route(logits, tri, carry):
    lane = lax.broadcasted_iota(jnp.int32, logits.shape, 1)
    neg = -jnp.inf
    is_g = lane < N_GROUPS
    gl = jnp.where(is_g, logits, neg)
    ge = jnp.exp(gl - jnp.max(gl, axis=-1, keepdims=True))
    p_g = 1.0 / jnp.sum(ge, axis=-1, keepdims=True)
    g_idx = jnp.min(jnp.where(is_g & (ge >= 1.0), lane, LANES), axis=-1, keepdims=True)
    sel = (lane >= N_GROUPS) & (lane < N_GROUPS + N_EXPERTS) & (((lane - N_GROUPS) >> 3) == g_idx)
    el = jnp.where(sel, logits, neg)
    m1 = jnp.max(el, axis=-1, keepdims=True)
    i1 = jnp.min(jnp.where(sel & (el == m1), lane, LANES), axis=-1, keepdims=True)
    sel2 = sel & (lane != i1)
    el2 = jnp.where(sel2, logits, neg)
    m2 = jnp.max(el2, axis=-1, keepdims=True)
    i2 = jnp.min(jnp.where(sel2 & (el2 == m2), lane, LANES), axis=-1, keepdims=True)
    e2 = jnp.exp(m2 - m1)
    w1 = p_g * (1.0 / (1.0 + e2))
    w2 = p_g * (e2 / (1.0 + e2))
    out = jnp.where(lane == 0, (i1 - N_GROUPS).astype(F32), 0.0)
    out = jnp.where(lane == 1, (i2 - N_GROUPS).astype(F32), out)
    out = jnp.where(lane == 2, w1, out)
    out = jnp.where(lane == 3, w2, out)
    oh1 = lane == i1
    oh2 = lane == i2
    both = jnp.where(oh1 | oh2, 1.0, 0.0)
    before = _dot(tri, both.astype(BF16)) + carry
    out = jnp.where(lane == 4, jnp.sum(jnp.where(oh1, before, 0.0), axis=-1, keepdims=True), out)
    out = jnp.where(lane == 5, jnp.sum(jnp.where(oh2, before, 0.0), axis=-1, keepdims=True), out)
    return out, carry + jnp.sum(both, axis=0, keepdims=True)


def _pack_bf16_pair(lo, hi):
    lo_bits = lax.bitcast_convert_type(lo.astype(BF16).astype(F32), jnp.uint32)
    hi_bits = lax.bitcast_convert_type(hi.astype(BF16).astype(F32), jnp.uint32)
    return (lo_bits >> 16) | hi_bits


def _unpack_bf16_pair(w):
    lo = lax.bitcast_convert_type(w << 16, F32).astype(BF16)
    hi = lax.bitcast_convert_type(w & jnp.uint32(0xFFFF0000), F32).astype(BF16)
    return lo, hi


def _out_kernel(y_ref, x_ref, g1_ref, w_ref, sc_ref, sh_ref, g_ref, wr_ref, br_ref, tri_ref,
                x1_ref, h2p_ref, rt_ref, cnt_ref, carry):
    @pl.when((pl.program_id(0) == 0) & (pl.program_id(1) == 0))
    def _():
        carry[...] = jnp.zeros_like(carry)

    sub = tri_ref.shape[0]
    count = carry[...]
    tiles = [slice(r0, r0 + sub) for r0 in range(0, y_ref.shape[0], sub)]
    mixes = [_dot(y_ref[rows, :], w_ref[...]) for rows in tiles]
    for rows, mix in zip(tiles, mixes):
        x1 = x_ref[rows, :] + g1_ref[...] * mix
        x1_ref[rows, :] = x1
        h2 = _modulated_norm(x1, g_ref[...], sc_ref[...], sh_ref[...])
        half = h2.shape[1] // 2
        words = _pack_bf16_pair(h2[:, :half], h2[:, half:])
        for c in range(half // LANES):
            h2p_ref[pl.ds(rows.start * SUBLANES + c, sub, stride=SUBLANES), :] = words[:, c * LANES:(c + 1) * LANES]
        logits = _dot(h2.astype(BF16), wr_ref[...]) + br_ref[...]
        rt, count = _route(logits, tri_ref[...], count)
        rt_ref[rows, :] = rt
    carry[...] = count
    cnt_ref[...] = count


def _out(y, x, g1, w_bf, sc2, sh2, g, wr_bf, br):
    bsz, s, d = x.shape
    tm, sub = 512, 256
    per_batch = pl.BlockSpec((None, 1, d), lambda b, t: (b, 0, 0))
    tile = lambda w: pl.BlockSpec((None, tm, w), lambda b, t: (b, t, 0))
    const = lambda shape, **kw: pl.BlockSpec(shape, lambda b, t: (0, 0), **kw)
    return pl.pallas_call(
        _out_kernel,
        grid=(bsz, s // tm),
        in_specs=[
            tile(D_MIX), tile(d), per_batch,
            const((D_MIX, d), pipeline_mode=pl.Buffered(1)),
            per_batch, per_batch, const((1, d)),
            const((d, ROUTE_W)), const((1, ROUTE_W)), const((sub, sub)),
        ],
        out_specs=[tile(d), pl.BlockSpec((None, tm * SUBLANES, LANES), lambda b, t: (b, t, 0)),
                   tile(ROUTE_W), const((1, ROUTE_W))],
        out_shape=[
            jax.ShapeDtypeStruct((bsz, s, d), F32),
            jax.ShapeDtypeStruct((bsz, s * SUBLANES, LANES), jnp.uint32),
            jax.ShapeDtypeStruct((bsz, s, ROUTE_W), F32),
            jax.ShapeDtypeStruct((1, ROUTE_W), F32),
        ],
        scratch_shapes=[pltpu.VMEM((1, ROUTE_W), F32)],
        compiler_params=pltpu.CompilerParams(
            dimension_semantics=("arbitrary", "arbitrary"), vmem_limit_bytes=VMEM_LIMIT),
        name="out_route",
    )(y, x, g1, w_bf, sc2, sh2, g, wr_bf, br, jnp.tril(jnp.ones((sub, sub), BF16), -1))


def _plan(rt, counts_f, n_tok):
    counts = counts_f[0, N_GROUPS:N_GROUPS + N_EXPERTS].astype(jnp.int32)
    padded = ((counts + MOE_BLOCK - 1) // MOE_BLOCK) * MOE_BLOCK
    padded_ends = jnp.cumsum(padded)
    padded_starts = padded_ends - padded
    n_rows = 2 * n_tok + N_EXPERTS * MOE_BLOCK
    n_blocks = n_rows // MOE_BLOCK
    expert_ids = jnp.arange(N_EXPERTS, dtype=jnp.int32)
    idx = rt[:, 0:6].astype(jnp.int32)
    start_of = jnp.sum(jnp.where(idx[:, 0:2, None] == expert_ids, padded_starts, 0), axis=-1)
    pos = start_of + idx[:, 4:6]
    block_start = jnp.arange(n_blocks, dtype=jnp.int32)[:, None] * MOE_BLOCK
    owns = (padded_starts <= block_start) & (block_start < padded_ends)
    n_valid = jnp.sum(jnp.where(owns, jnp.clip(counts - (block_start - padded_starts), 0, MOE_BLOCK), 0), axis=1)
    last_used = jnp.max(jnp.where(counts > 0, expert_ids, 0))
    block_expert = jnp.where(n_valid > 0, jnp.sum(jnp.where(owns, expert_ids, 0), axis=1), last_used)
    later_used = (expert_ids[None, :] > expert_ids[:, None]) & (counts[None, :] > 0)
    next_used = jnp.min(jnp.where(later_used, expert_ids[None, :], N_EXPERTS), axis=1)
    next_used = jnp.where(next_used == N_EXPERTS, -1, next_used)
    block_next = jnp.sum(jnp.where(owns, next_used, 0), axis=1)
    n_used = padded_ends[-1] // MOE_BLOCK
    block_index = jnp.minimum(jnp.arange(n_blocks, dtype=jnp.int32), n_used - 1)
    i32 = lambda a: a.astype(jnp.int32)
    return pos[:, 0], pos[:, 1], i32(block_expert), i32(n_valid), i32(block_index), i32(block_next), n_rows


def _dispatch_kernel(nvalid_ref, p0_ref, p1_ref, h_ref, xs_hbm, zbuf, sem, zsem, *, tm, n_blocks):
    @pl.when(pl.program_id(0) == 0)
    def _():
        zbuf[...] = jnp.zeros_like(zbuf)

        def zero_block(i):
            return pltpu.make_async_copy(zbuf, xs_hbm.at[pl.ds(i * MOE_BLOCK, MOE_BLOCK)], zsem)

        def start(i, carry):
            pl.when(nvalid_ref[i] < MOE_BLOCK)(lambda: zero_block(i).start())
            return carry

        def wait(i, carry):
            pl.when(nvalid_ref[i] < MOE_BLOCK)(lambda: zero_block(i).wait())
            return carry

        lax.fori_loop(0, n_blocks, start, 0)
        lax.fori_loop(0, n_blocks, wait, 0)

    def issue(c, carry):
        for j in range(DMA_ISSUE_UNROLL):
            r = c * DMA_ISSUE_UNROLL + j
            src = h_ref.at[r]
            pltpu.make_async_copy(src, xs_hbm.at[p0_ref[0, r]], sem).start(priority=0)
            pltpu.make_async_copy(src, xs_hbm.at[p1_ref[0, r]], sem).start(priority=1)
        return carry

    lax.fori_loop(0, tm // DMA_ISSUE_UNROLL, issue, 0)
    for _ in range(2):
        pltpu.make_async_copy(h_ref, xs_hbm.at[pl.ds(0, tm)], sem).wait()


def _dispatch(n_valid, pos0, pos1, h2p, n_rows):
    n = h2p.shape[0]
    tm = 2048
    n_blocks = n_valid.shape[0]
    idx_spec = pl.BlockSpec((None, 1, tm), lambda t, nv: (t, 0, 0), memory_space=pltpu.SMEM)
    grid_spec = pltpu.PrefetchScalarGridSpec(
        num_scalar_prefetch=1,
        grid=(n // tm,),
        in_specs=[idx_spec, idx_spec, pl.BlockSpec((tm, SUBLANES, LANES), lambda t, nv: (t, 0, 0))],
        out_specs=pl.BlockSpec(memory_space=pl.ANY),
        scratch_shapes=[
            pltpu.VMEM((MOE_BLOCK, SUBLANES, LANES), h2p.dtype),
            pltpu.SemaphoreType.DMA(()),
            pltpu.SemaphoreType.DMA(()),
        ],
    )
    return pl.pallas_call(
        functools.partial(_dispatch_kernel, tm=tm, n_blocks=n_blocks),
        grid_spec=grid_spec,
        out_shape=jax.ShapeDtypeStruct((n_rows, SUBLANES, LANES), h2p.dtype),
        compiler_params=pltpu.CompilerParams(
            dimension_semantics=("arbitrary",), vmem_limit_bytes=VMEM_LIMIT,
            disable_bounds_checks=True),
        name="dispatch",
    )(n_valid, pos0.reshape(n // tm, 1, tm), pos1.reshape(n // tm, 1, tm), h2p)


def _moe_kernel(be_ref, nvalid_ref, bi_ref, nxt_ref, xs_hbm, wg_hbm, wu_hbm, wd_hbm, y_ref,
                wg_f, wu_f, wd_f, wg_b, wu_b, wd_b, xbuf, sem, xsem, *, n_blocks):
    i = pl.program_id(0)
    n_valid = nvalid_ref[i]
    block_rows = MOE_BLOCK * SUBLANES

    def x_copy(j):
        start = pl.multiple_of(bi_ref[j] * block_rows, block_rows)
        return pltpu.make_async_copy(xs_hbm.at[pl.ds(start, block_rows)], xbuf.at[j % X_RING], xsem.at[j % X_RING])

    @pl.when(i == 0)
    def _():
        for j in range(X_RING - 1):
            x_copy(j).start()

    @pl.when(i + X_RING - 1 < n_blocks)
    def _():
        x_copy(i + X_RING - 1).start()

    x_copy(i).wait()
    x_ref = xbuf.at[i % X_RING]

    def fetch(e):
        return (pltpu.make_async_copy(wg_hbm.at[e], wg_f, sem.at[0]),
                pltpu.make_async_copy(wu_hbm.at[e], wu_f, sem.at[1]),
                pltpu.make_async_copy(wd_hbm.at[e], wd_f, sem.at[2]))

    @pl.when(n_valid > 0)
    def _():
        e = be_ref[i]

        @pl.when(i == 0)
        def _():
            for cp in fetch(e):
                cp.start(priority=1)

        @pl.when((i == 0) | (e != be_ref[jnp.maximum(i - 1, 0)]))
        def _():
            for cp in fetch(e):
                cp.wait()
            def cast_rows(src, dst, rows):
                def body(c, carry):
                    r = pl.multiple_of(c * rows, rows)
                    dst[pl.ds(r, rows), :] = src[pl.ds(r, rows), :].astype(BF16)
                    return carry

                lax.fori_loop(0, src.shape[0] // rows, body, 0)

            cast_rows(wg_f, wg_b, 128)
            cast_rows(wu_f, wu_b, 128)
            cast_rows(wd_f, wd_b, 32)
            nxt = nxt_ref[i]

            @pl.when(nxt >= 0)
            def _():
                for cp in fetch(nxt):
                    cp.start(priority=1)

        words = jnp.concatenate(
            [x_ref[pl.ds(c, MOE_BLOCK, stride=SUBLANES), :] for c in range(SUBLANES)], axis=1)
        lo, hi = _unpack_bf16_pair(words)
        half = words.shape[1]
        g = _dot(lo, wg_b[:half, :]) + _dot(hi, wg_b[half:, :])
        u = _dot(lo, wu_b[:half, :]) + _dot(hi, wu_b[half:, :])
        act = (g * jax.nn.sigmoid(g) * u).astype(BF16)
        y = _dot(act, wd_b[...])
        y_words = _pack_bf16_pair(y[:, :half], y[:, half:])
        for c in range(SUBLANES):
            y_ref[pl.ds(c, MOE_BLOCK, stride=SUBLANES), :] = y_words[:, c * LANES:(c + 1) * LANES]

    @pl.when(n_valid == 0)
    def _():
        y_ref[...] = jnp.zeros_like(y_ref)


def _moe(block_expert, n_valid, block_index, block_next, xs, w_gate, w_up, w_down):
    n_blocks = block_expert.shape[0]
    n_rows = xs.shape[0]
    d = 2 * SUBLANES * LANES
    hbm = pl.BlockSpec(memory_space=pl.ANY)
    grid_spec = pltpu.PrefetchScalarGridSpec(
        num_scalar_prefetch=4,
        grid=(n_blocks,),
        in_specs=[hbm, hbm, hbm, hbm],
        out_specs=pl.BlockSpec((MOE_BLOCK * SUBLANES, LANES), lambda i, be, nv, bi, nx: (i, 0)),
        scratch_shapes=[
            pltpu.VMEM((d, D_EXPERT), F32),
            pltpu.VMEM((d, D_EXPERT), F32),
            pltpu.VMEM((D_EXPERT, d), F32),
            pltpu.VMEM((d, D_EXPERT), BF16),
            pltpu.VMEM((d, D_EXPERT), BF16),
            pltpu.VMEM((D_EXPERT, d), BF16),
            pltpu.VMEM((X_RING, MOE_BLOCK * SUBLANES, LANES), jnp.uint32),
            pltpu.SemaphoreType.DMA((3,)),
            pltpu.SemaphoreType.DMA((X_RING,)),
        ],
    )
    return pl.pallas_call(
        functools.partial(_moe_kernel, n_blocks=n_blocks),
        grid_spec=grid_spec,
        out_shape=jax.ShapeDtypeStruct((n_rows * SUBLANES, LANES), jnp.uint32),
        compiler_params=pltpu.CompilerParams(
            dimension_semantics=("arbitrary",), vmem_limit_bytes=VMEM_LIMIT),
        name="moe",
    )(block_expert, n_valid, block_index, block_next, xs.reshape(n_rows * SUBLANES, LANES), w_gate, w_up, w_down)


def _final_kernel(p0_ref, p1_ref, q0_ref, q1_ref, x1_ref, rt_ref, g2_ref, gf_ref, yb_hbm, yb_flat_hbm, o_ref,
                  y0buf, y1buf, sem, *, tm, n_tiles):
    t = pl.program_id(0)
    slot = t & 1

    def issue(pa_ref, pb_ref, s):
        def body(c, carry):
            for j in range(DMA_ISSUE_UNROLL):
                r = c * DMA_ISSUE_UNROLL + j
                dst = pl.ds(pl.multiple_of(r * SUBLANES, SUBLANES), SUBLANES)
                pltpu.make_async_copy(yb_hbm.at[pa_ref[0, r]], y0buf.at[s].at[dst], sem.at[s]).start(priority=0)
                pltpu.make_async_copy(yb_hbm.at[pb_ref[0, r]], y1buf.at[s].at[dst], sem.at[s]).start(priority=1)
            return carry

        lax.fori_loop(0, tm // DMA_ISSUE_UNROLL, body, 0)

    @pl.when(t == 0)
    def _():
        issue(p0_ref, p1_ref, 0)

    @pl.when(t + 1 < n_tiles)
    def _():
        issue(q0_ref, q1_ref, 1 - slot)

    pltpu.make_async_copy(yb_flat_hbm.at[pl.ds(0, tm * SUBLANES)], y0buf.at[slot], sem.at[slot]).wait()
    pltpu.make_async_copy(yb_flat_hbm.at[pl.ds(0, tm * SUBLANES)], y1buf.at[slot], sem.at[slot]).wait()

    def rows_of(buf):
        tiles = buf.at[slot]
        words = jnp.concatenate([tiles[pl.ds(c, tm, stride=SUBLANES), :] for c in range(SUBLANES)], axis=1)
        lo = lax.bitcast_convert_type(words << 16, F32)
        hi = lax.bitcast_convert_type(words & jnp.uint32(0xFFFF0000), F32)
        return jnp.concatenate([lo, hi], axis=1)

    rt = rt_ref[...]
    y = rt[:, 2:3] * rows_of(y0buf) + rt[:, 3:4] * rows_of(y1buf)
    x2 = x1_ref[...] + g2_ref[...] * y
    ms = jnp.mean(x2 * x2, axis=-1, keepdims=True)
    o_ref[...] = x2 * lax.rsqrt(ms + EPS) * gf_ref[...]


def _final(pos0, pos1, x1, rt, g2, gf, yb, seq):
    n, d = x1.shape
    tm = 256
    n_tiles = n // tm
    per_seq = seq // tm
    this_tile = pl.BlockSpec((None, 1, tm), lambda t: (t, 0, 0), memory_space=pltpu.SMEM)
    next_tile = pl.BlockSpec((None, 1, tm), lambda t: (jnp.minimum(t + 1, n_tiles - 1), 0, 0),
                             memory_space=pltpu.SMEM)
    p0 = pos0.reshape(n_tiles, 1, tm)
    p1 = pos1.reshape(n_tiles, 1, tm)
    return pl.pallas_call(
        functools.partial(_final_kernel, tm=tm, n_tiles=n_tiles),
        grid=(n_tiles,),
        in_specs=[
            this_tile, this_tile, next_tile, next_tile,
            pl.BlockSpec((tm, d), lambda t: (t, 0)),
            pl.BlockSpec((tm, ROUTE_W), lambda t: (t, 0)),
            pl.BlockSpec((None, 1, d), lambda t: (t // per_seq, 0, 0)),
            pl.BlockSpec((1, d), lambda t: (0, 0)),
            pl.BlockSpec(memory_space=pl.ANY),
            pl.BlockSpec(memory_space=pl.ANY),
        ],
        out_specs=pl.BlockSpec((tm, d), lambda t: (t, 0)),
        out_shape=jax.ShapeDtypeStruct((n, d), F32),
        scratch_shapes=[
            pltpu.VMEM((2, tm * SUBLANES, LANES), jnp.uint32),
            pltpu.VMEM((2, tm * SUBLANES, LANES), jnp.uint32),
            pltpu.SemaphoreType.DMA((2,)),
        ],
        compiler_params=pltpu.CompilerParams(
            dimension_semantics=("arbitrary",), vmem_limit_bytes=VMEM_LIMIT,
            disable_bounds_checks=True),
        name="final",
    )(p0, p1, p0, p1, x1, rt, g2, gf, yb.reshape(-1, SUBLANES, LANES), yb)


def kernel(x, c, ada_w, ada_b, norm1_g, w_in, conv3_w, conv4_w, conv4_b, lru_w_a, lru_b_a, lru_w_x, lru_b_x, lru_lambda, head_norm_conv_g, head_norm_lru_g, w_out, norm2_g, route_w_group, route_b_group, route_w_expert, route_b_expert, w_e_gate, w_e_up, w_e_down, final_norm_g):
    bsz, s, d = x.shape
    n_tok = bsz * s
    l = 0

    mod = _ada(c, ada_w[l], ada_b[l]).reshape(bsz, 6, 1, d)
    sh1, sc1, g1, sh2, sc2, g2 = (mod[:, j] for j in range(6))

    proj = _in_proj(x, sc1, sh1, norm1_g[l].reshape(1, d), w_in[l].astype(BF16))

    wax = jnp.concatenate([lru_w_a[l], lru_w_x[l]], axis=-1).astype(BF16)
    y = _mix(proj, conv3_w[l], conv4_w[l], conv4_b[l], wax, lru_b_a[l], lru_b_x[l], lru_lambda[l],
             head_norm_conv_g[l], head_norm_lru_g[l])

    w_route = jnp.concatenate(
        [route_w_group[l], jnp.transpose(route_w_expert[l], (1, 0, 2)).reshape(d, N_EXPERTS),
         jnp.zeros((d, ROUTE_W - N_GROUPS - N_EXPERTS), F32)], axis=1).astype(BF16)
    b_route = jnp.concatenate(
        [route_b_group[l], route_b_expert[l].reshape(-1),
         jnp.zeros((ROUTE_W - N_GROUPS - N_EXPERTS,), F32)]).reshape(1, ROUTE_W)
    x1, h2p, rt, counts = _out(y, x, g1, w_out[l].astype(BF16), sc2, sh2, norm2_g[l].reshape(1, d),
                               w_route, b_route)

    rt = rt.reshape(n_tok, ROUTE_W)
    pos0, pos1, block_expert, n_valid, block_index, block_next, n_rows = _plan(rt, counts, n_tok)
    assert d // 2 == SUBLANES * LANES, "a token's packed words must fill exactly one (8, 128) tile"
    xs = _dispatch(n_valid, pos0, pos1, h2p.reshape(n_tok, SUBLANES, LANES), n_rows)
    yb = _moe(block_expert, n_valid, block_index, block_next, xs, w_e_gate[l], w_e_up[l], w_e_down[l])
    out = _final(pos0, pos1, x1.reshape(n_tok, d), rt, g2, final_norm_g.reshape(1, d), yb, s)
    return out.reshape(bsz, s, d)
```

```python
import functools

import jax
import jax.numpy as jnp
from jax import lax
from jax.experimental import pallas as pl
from jax.experimental.pallas import tpu as pltpu

D_MODEL = 2048
D_CONV = 1024
CONV_HEAD_DIM = 64
D_LRU = 1024
LRU_HEADS = 8
LRU_HEAD_DIM = 128
LRU_C = 8.0
D_MIX = D_CONV + D_LRU
D_IN = 3 * D_CONV + 2 * D_LRU
N_GROUPS = 8
EXPERTS_PER_GROUP = 8
N_EXPERTS = 64
D_EXPERT = 512
EPS = 1e-6

LANES = 128
SUBLANES = 8
ROUTE_W = LANES
MOE_BLOCK = 256
X_RING = 3
DMA_ISSUE_UNROLL = 8
VMEM_LIMIT = 56 * 1024 * 1024

F32 = jnp.float32
BF16 = jnp.bfloat16


def _dot(a, b):
    return jnp.dot(a, b, preferred_element_type=F32)


def _ada_kernel(c_ref, w_ref, b_ref, o_ref):
    c = c_ref[...]
    s = (c * jax.nn.sigmoid(c)).astype(BF16)
    o_ref[...] = _dot(s, w_ref[...].astype(BF16)) + b_ref[...]


def _ada(c, w, b):
    bsz, d = c.shape
    n = w.shape[1]
    tn = 1024
    return pl.pallas_call(
        _ada_kernel,
        grid=(n // tn,),
        in_specs=[
            pl.BlockSpec((bsz, d), lambda j: (0, 0)),
            pl.BlockSpec((d, tn), lambda j: (0, j)),
            pl.BlockSpec((1, tn), lambda j: (0, j)),
        ],
        out_specs=pl.BlockSpec((bsz, tn), lambda j: (0, j)),
        out_shape=jax.ShapeDtypeStruct((bsz, n), F32),
        compiler_params=pltpu.CompilerParams(
            dimension_semantics=("arbitrary",), vmem_limit_bytes=VMEM_LIMIT),
        name="ada",
    )(c, w, b.reshape(1, n))


def _modulated_norm(x, g, sc, sh):
    ms = jnp.mean(x * x, axis=-1, keepdims=True)
    return x * lax.rsqrt(ms + EPS) * (g * (1.0 + sc)) + sh


def _in_proj_kernel(x_ref, sc_ref, sh_ref, g_ref, w_ref, o_ref, *, n_chunk):
    hb = _modulated_norm(x_ref[...], g_ref[...], sc_ref[...], sh_ref[...]).astype(BF16)
    for j in range(D_IN // n_chunk):
        cols = slice(j * n_chunk, (j + 1) * n_chunk)
        o_ref[:, cols] = _dot(hb, w_ref[:, cols])


def _in_proj(x, sc, sh, g, w_bf):
    bsz, s, d = x.shape
    tm = 512
    per_batch = pl.BlockSpec((None, 1, d), lambda b, t: (b, 0, 0))
    return pl.pallas_call(
        functools.partial(_in_proj_kernel, n_chunk=1024),
        grid=(bsz, s // tm),
        in_specs=[
            pl.BlockSpec((None, tm, d), lambda b, t: (b, t, 0)),
            per_batch, per_batch,
            pl.BlockSpec((1, d), lambda b, t: (0, 0)),
            pl.BlockSpec((d, D_IN), lambda b, t: (0, 0), pipeline_mode=pl.Buffered(1)),
        ],
        out_specs=pl.BlockSpec((None, tm, D_IN), lambda b, t: (b, t, 0)),
        out_shape=jax.ShapeDtypeStruct((bsz, s, D_IN), F32),
        compiler_params=pltpu.CompilerParams(
            dimension_semantics=("arbitrary", "arbitrary"), vmem_limit_bytes=VMEM_LIMIT),
        name="in_proj",
    )(x, sc, sh, g, w_bf)


def _shift_rows(x, prev8, d, row):
    rolled = pltpu.roll(x, d, 0)
    head = jnp.where(row[:SUBLANES] >= d, rolled[:SUBLANES], pltpu.roll(prev8, d, 0))
    return jnp.concatenate([head, rolled[SUBLANES:]], axis=0)


def _head_rms(y, p_ref, g):
    sq = y * y
    hi = sq.astype(BF16)
    lo = (sq - hi.astype(F32)).astype(BF16)
    ms = _dot(hi, p_ref[...]) + _dot(lo, p_ref[...])
    return y * lax.rsqrt(ms + EPS) * g


def _linear_scan(a, v, h0, sub):
    groups = []
    for r0 in range(0, a.shape[0], SUBLANES):
        ag, vg = a[r0:r0 + SUBLANES], v[r0:r0 + SUBLANES]
        d = 1
        while d < SUBLANES:
            keep = sub >= d
            vg = vg + ag * jnp.where(keep, pltpu.roll(vg, d, 0), 0.0)
            ag = ag * jnp.where(keep, pltpu.roll(ag, d, 0), 1.0)
            d *= 2
        groups.append((ag, vg))
    out, carry = [], h0
    for ag, vg in groups:
        hg = vg + ag * carry
        carry = hg[SUBLANES - 1:, :]
        out.append(hg)
    return jnp.concatenate(out, axis=0), carry


def _gelu_tanh(x):
    c = 0.7978845608028654
    half_x = 0.5 * x
    return half_x + half_x * jnp.tanh(x * (c + (c * 0.044715) * (x * x)))


def _mix_kernel(p_ref, w3_ref, w4_ref, b4_ref, wax_ref, ba_ref, bx_ref, lam_ref, ga_ref, gb_ref,
                pa_ref, pb_ref, y_ref, ua_carry, xb_carry, h_carry, *, ts):
    @pl.when(pl.program_id(1) == 0)
    def _():
        ua_carry[...] = jnp.zeros_like(ua_carry)
        xb_carry[...] = jnp.zeros_like(xb_carry)
        h_carry[...] = jnp.zeros_like(h_carry)

    row = lax.broadcasted_iota(jnp.int32, (ts, LANES), 0)

    for k in range(D_CONV // LANES):
        cols = slice(k * LANES, (k + 1) * LANES)
        b_a = p_ref[:, k * LANES:(k + 1) * LANES]
        c_a = p_ref[:, D_CONV + k * LANES:D_CONV + (k + 1) * LANES]
        x_a = p_ref[:, 2 * D_CONV + k * LANES:2 * D_CONV + (k + 1) * LANES]
        u = c_a * x_a
        prev = ua_carry[:, cols]
        conv = (w3_ref[2:3, cols] * u
                + w3_ref[1:2, cols] * _shift_rows(u, prev, 1, row)
                + w3_ref[0:1, cols] * _shift_rows(u, prev, 2, row))
        ua_carry[:, cols] = u[ts - SUBLANES:, :]
        y_a = b_a * conv
        y_ref[:, cols] = _head_rms(y_a, pa_ref, ga_ref[:, cols]).astype(BF16)

        g_b = p_ref[:, 3 * D_CONV + k * LANES:3 * D_CONV + (k + 1) * LANES]
        x_b = p_ref[:, 3 * D_CONV + D_LRU + k * LANES:3 * D_CONV + D_LRU + (k + 1) * LANES]
        prevb = xb_carry[:, cols]
        xc = (w4_ref[3:4, cols] * x_b
              + w4_ref[2:3, cols] * _shift_rows(x_b, prevb, 1, row)
              + w4_ref[1:2, cols] * _shift_rows(x_b, prevb, 2, row)
              + w4_ref[0:1, cols] * _shift_rows(x_b, prevb, 3, row)
              + b4_ref[:, cols])
        xb_carry[:, cols] = x_b[ts - SUBLANES:, :]
        gates = _dot(xc.astype(BF16), wax_ref[k])
        r = jax.nn.sigmoid(gates[:, :LANES] + ba_ref[:, cols])
        i = jax.nn.sigmoid(gates[:, LANES:] + bx_ref[:, cols])
        nlam = -lam_ref[:, cols]
        softplus = jnp.maximum(nlam, 0.0) + jnp.log1p(jnp.exp(-jnp.abs(nlam)))
        log_a = r * ((-LRU_C) * softplus)
        a = jnp.exp(log_a)
        mult = jnp.sqrt(-jnp.tanh(log_a) * (a * a + 1.0))
        v = mult * i * xc
        hs, h_last = _linear_scan(a, v, h_carry[:, cols], row[:SUBLANES])
        h_carry[:, cols] = h_last
        y_b = hs * _gelu_tanh(g_b)
        y_ref[:, D_CONV + k * LANES:D_CONV + (k + 1) * LANES] = _head_rms(
            y_b, pb_ref, gb_ref[:, cols]).astype(BF16)


def _mix(proj, w3, w4, b4, wax_bf, b_a, b_x, lam, g_a, g_b):
    bsz, s, _ = proj.shape
    ts = 512
    lane = jnp.arange(LANES)
    p_a = ((lane[:, None] // CONV_HEAD_DIM) == (lane[None, :] // CONV_HEAD_DIM)).astype(BF16) / CONV_HEAD_DIM
    p_b = jnp.full((LANES, LANES), 1.0 / LRU_HEAD_DIM, BF16)
    full = lambda shape: pl.BlockSpec(shape, lambda b, t: (0,) * len(shape))
    row = lambda a: a.reshape(1, -1)
    return pl.pallas_call(
        functools.partial(_mix_kernel, ts=ts),
        grid=(bsz, s // ts),
        in_specs=[
            pl.BlockSpec((None, ts, D_IN), lambda b, t: (b, t, 0)),
            full((3, D_CONV)), full((4, D_LRU)), full((1, D_LRU)),
            full((LRU_HEADS, LRU_HEAD_DIM, 2 * LRU_HEAD_DIM)),
            full((1, D_LRU)), full((1, D_LRU)), full((1, D_LRU)),
            full((1, D_CONV)), full((1, D_LRU)),
            full((LANES, LANES)), full((LANES, LANES)),
        ],
        out_specs=pl.BlockSpec((None, ts, D_MIX), lambda b, t: (b, t, 0)),
        out_shape=jax.ShapeDtypeStruct((bsz, s, D_MIX), BF16),
        scratch_shapes=[
            pltpu.VMEM((SUBLANES, D_CONV), F32),
            pltpu.VMEM((SUBLANES, D_LRU), F32),
            pltpu.VMEM((1, D_LRU), F32),
        ],
        compiler_params=pltpu.CompilerParams(
            dimension_semantics=("arbitrary", "arbitrary"), vmem_limit_bytes=VMEM_LIMIT),
        name="mix",
    )(proj, w3, w4, row(b4), wax_bf, row(b_a), row(b_x), row(lam), row(g_a), row(g_b), p_a, p_b)


def _route(logits, tri, carry):
    lane = lax.broadcasted_iota(jnp.int32, logits.shape, 1)
    neg = -jnp.inf
    is_g = lane < N_GROUPS
    gl = jnp.where(is_g, logits, neg)
    ge = jnp.exp(gl - jnp.max(gl, axis=-1, keepdims=True))
    p_g = 1.0 / jnp.sum(ge, axis=-1, keepdims=True)
    g_idx = jnp.min(jnp.where(is_g & (ge >= 1.0), lane, LANES), axis=-1, keepdims=True)
    sel = (lane >= N_GROUPS) & (lane < N_GROUPS + N_EXPERTS) & (((lane - N_GROUPS) >> 3) == g_idx)
    el = jnp.where(sel, logits, neg)
    m1 = jnp.max(el, axis=-1, keepdims=True)
    i1 = jnp.min(jnp.where(sel & (el == m1), lane, LANES), axis=-1, keepdims=True)
    sel2 = sel & (lane != i1)
    el2 = jnp.where(sel2, logits, neg)
    m2 = jnp.max(el2, axis=-1, keepdims=True)
    i2 = jnp.min(jnp.where(sel2 & (el2 == m2), lane, LANES), axis=-1, keepdims=True)
    e2 = jnp.exp(m2 - m1)
    w1 = p_g * (1.0 / (1.0 + e2))
    w2 = p_g * (e2 / (1.0 + e2))
    out = jnp.where(lane == 0, (i1 - N_GROUPS).astype(F32), 0.0)
    out = jnp.where(lane == 1, (i2 - N_GROUPS).astype(F32), out)
    out = jnp.where(lane == 2, w1, out)
    out = jnp.where(lane == 3, w2, out)
    oh1 = lane == i1
    oh2 = lane == i2
    both = jnp.where(oh1 | oh2, 1.0, 0.0)
    before = _dot(tri, both.astype(BF16)) + carry
    out = jnp.where(lane == 4, jnp.sum(jnp.where(oh1, before, 0.0), axis=-1, keepdims=True), out)
    out = jnp.where(lane == 5, jnp.sum(jnp.where(oh2, before, 0.0), axis=-1, keepdims=True), out)
    return out, carry + jnp.sum(both, axis=0, keepdims=True)


def _pack_bf16_pair(lo, hi):
    lo_bits = lax.bitcast_convert_type(lo.astype(BF16).astype(F32), jnp.uint32)
    hi_bits = lax.bitcast_convert_type(hi.astype(BF16).astype(F32), jnp.uint32)
    return (lo_bits >> 16) | hi_bits


def _unpack_bf16_pair(w):
    lo = lax.bitcast_convert_type(w << 16, F32).astype(BF16)
    hi = lax.bitcast_convert_type(w & jnp.uint32(0xFFFF0000), F32).astype(BF16)
    return lo, hi


def _out_kernel(y_ref, x_ref, g1_ref, w_ref, sc_ref, sh_ref, g_ref, wr_ref, br_ref, tri_ref,
                x1_ref, h2p_ref, rt_ref, rtt_ref, cnt_ref, carry):
    @pl.when((pl.program_id(0) == 0) & (pl.program_id(1) == 0))
    def _():
        carry[...] = jnp.zeros_like(carry)

    sub = tri_ref.shape[0]
    count = carry[...]
    tiles = [slice(r0, r0 + sub) for r0 in range(0, y_ref.shape[0], sub)]
    mixes = [_dot(y_ref[rows, :], w_ref[...]) for rows in tiles]
    for rows, mix in zip(tiles, mixes):
        x1 = x_ref[rows, :] + g1_ref[...] * mix
        x1_ref[rows, :] = x1
        h2 = _modulated_norm(x1, g_ref[...], sc_ref[...], sh_ref[...])
        half = h2.shape[1] // 2
        words = _pack_bf16_pair(h2[:, :half], h2[:, half:])
        for c in range(half // LANES):
            h2p_ref[pl.ds(rows.start * SUBLANES + c, sub, stride=SUBLANES), :] = words[:, c * LANES:(c + 1) * LANES]
        logits = _dot(h2.astype(BF16), wr_ref[...]) + br_ref[...]
        rt, count = _route(logits, tri_ref[...], count)
        rt_ref[rows, :] = rt
        rtt_ref[:, rows] = rt.T[:SUBLANES, :]
    carry[...] = count
    cnt_ref[...] = count


def _out(y, x, g1, w_bf, sc2, sh2, g, wr_bf, br):
    bsz, s, d = x.shape
    tm, sub = 512, 256
    per_batch = pl.BlockSpec((None, 1, d), lambda b, t: (b, 0, 0))
    tile = lambda w: pl.BlockSpec((None, tm, w), lambda b, t: (b, t, 0))
    const = lambda shape, **kw: pl.BlockSpec(shape, lambda b, t: (0, 0), **kw)
    return pl.pallas_call(
        _out_kernel,
        grid=(bsz, s // tm),
        in_specs=[
            tile(D_MIX), tile(d), per_batch,
            const((D_MIX, d), pipeline_mode=pl.Buffered(1)),
            per_batch, per_batch, const((1, d)),
            const((d, ROUTE_W)), const((1, ROUTE_W)), const((sub, sub)),
        ],
        out_specs=[tile(d), pl.BlockSpec((None, tm * SUBLANES, LANES), lambda b, t: (b, t, 0)),
                   tile(ROUTE_W), pl.BlockSpec((SUBLANES, tm), lambda b, t: (0, b * (s // tm) + t)),
                   const((1, ROUTE_W))],
        out_shape=[
            jax.ShapeDtypeStruct((bsz, s, d), F32),
            jax.ShapeDtypeStruct((bsz, s * SUBLANES, LANES), jnp.uint32),
            jax.ShapeDtypeStruct((bsz, s, ROUTE_W), F32),
            jax.ShapeDtypeStruct((SUBLANES, bsz * s), F32),
            jax.ShapeDtypeStruct((1, ROUTE_W), F32),
        ],
        scratch_shapes=[pltpu.VMEM((1, ROUTE_W), F32)],
        compiler_params=pltpu.CompilerParams(
            dimension_semantics=("arbitrary", "arbitrary"), vmem_limit_bytes=VMEM_LIMIT),
        name="out_route",
    )(y, x, g1, w_bf, sc2, sh2, g, wr_bf, br, jnp.tril(jnp.ones((sub, sub), BF16), -1))


def _plan(rtt, counts_f, n_tok):
    counts = counts_f[0, N_GROUPS:N_GROUPS + N_EXPERTS].astype(jnp.int32)
    padded = ((counts + MOE_BLOCK - 1) // MOE_BLOCK) * MOE_BLOCK
    padded_ends = jnp.cumsum(padded)
    padded_starts = padded_ends - padded
    n_rows = 2 * n_tok + N_EXPERTS * MOE_BLOCK
    n_blocks = n_rows // MOE_BLOCK
    expert_ids = jnp.arange(N_EXPERTS, dtype=jnp.int32)
    idx = rtt.astype(jnp.int32)
    start_of = jnp.sum(jnp.where(idx[0:2, :, None] == expert_ids, padded_starts, 0), axis=-1)
    pos = start_of + idx[4:6]
    block_start = jnp.arange(n_blocks, dtype=jnp.int32)[:, None] * MOE_BLOCK
    owns = (padded_starts <= block_start) & (block_start < padded_ends)
    n_valid = jnp.sum(jnp.where(owns, jnp.clip(counts - (block_start - padded_starts), 0, MOE_BLOCK), 0), axis=1)
    last_used = jnp.max(jnp.where(counts > 0, expert_ids, 0))
    block_expert = jnp.where(n_valid > 0, jnp.sum(jnp.where(owns, expert_ids, 0), axis=1), last_used)
    later_used = (expert_ids[None, :] > expert_ids[:, None]) & (counts[None, :] > 0)
    next_used = jnp.min(jnp.where(later_used, expert_ids[None, :], N_EXPERTS), axis=1)
    next_used = jnp.where(next_used == N_EXPERTS, -1, next_used)
    block_next = jnp.sum(jnp.where(owns, next_used, 0), axis=1)
    n_used = padded_ends[-1] // MOE_BLOCK
    block_index = jnp.minimum(jnp.arange(n_blocks, dtype=jnp.int32), n_used - 1)
    i32 = lambda a: a.astype(jnp.int32)
    return pos[0], pos[1], i32(block_expert), i32(n_valid), i32(block_index), i32(block_next), n_rows


def _dispatch_kernel(nvalid_ref, p0_ref, p1_ref, h_ref, xs_hbm, zbuf, sem, zsem, *, tm, n_blocks):
    @pl.when(pl.program_id(0) == 0)
    def _():
        zbuf[...] = jnp.zeros_like(zbuf)

        def zero_block(i):
            return pltpu.make_async_copy(zbuf, xs_hbm.at[pl.ds(i * MOE_BLOCK, MOE_BLOCK)], zsem)

        def start(i, carry):
            pl.when(nvalid_ref[i] < MOE_BLOCK)(lambda: zero_block(i).start())
            return carry

        def wait(i, carry):
            pl.when(nvalid_ref[i] < MOE_BLOCK)(lambda: zero_block(i).wait())
            return carry

        lax.fori_loop(0, n_blocks, start, 0)
        lax.fori_loop(0, n_blocks, wait, 0)

    def issue(c, carry):
        for j in range(DMA_ISSUE_UNROLL):
            r = c * DMA_ISSUE_UNROLL + j
            src = h_ref.at[r]
            pltpu.make_async_copy(src, xs_hbm.at[p0_ref[0, r]], sem).start(priority=0)
            pltpu.make_async_copy(src, xs_hbm.at[p1_ref[0, r]], sem).start(priority=1)
        return carry

    lax.fori_loop(0, tm // DMA_ISSUE_UNROLL, issue, 0)
    for _ in range(2):
        pltpu.make_async_copy(h_ref, xs_hbm.at[pl.ds(0, tm)], sem).wait()


def _dispatch(n_valid, pos0, pos1, h2p, n_rows):
    n = h2p.shape[0]
    tm = 2048
    n_blocks = n_valid.shape[0]
    idx_spec = pl.BlockSpec((None, 1, tm), lambda t, nv: (t, 0, 0), memory_space=pltpu.SMEM)
    grid_spec = pltpu.PrefetchScalarGridSpec(
        num_scalar_prefetch=1,
        grid=(n // tm,),
        in_specs=[idx_spec, idx_spec, pl.BlockSpec((tm, SUBLANES, LANES), lambda t, nv: (t, 0, 0))],
        out_specs=pl.BlockSpec(memory_space=pl.ANY),
        scratch_shapes=[
            pltpu.VMEM((MOE_BLOCK, SUBLANES, LANES), h2p.dtype),
            pltpu.SemaphoreType.DMA(()),
            pltpu.SemaphoreType.DMA(()),
        ],
    )
    return pl.pallas_call(
        functools.partial(_dispatch_kernel, tm=tm, n_blocks=n_blocks),
        grid_spec=grid_spec,
        out_shape=jax.ShapeDtypeStruct((n_rows, SUBLANES, LANES), h2p.dtype),
        compiler_params=pltpu.CompilerParams(
            dimension_semantics=("arbitrary",), vmem_limit_bytes=VMEM_LIMIT,
            disable_bounds_checks=True),
        name="dispatch",
    )(n_valid, pos0.reshape(n // tm, 1, tm), pos1.reshape(n // tm, 1, tm), h2p)


def _moe_kernel(be_ref, nvalid_ref, bi_ref, nxt_ref, xs_hbm, wg_hbm, wu_hbm, wd_hbm, y_ref,
                wg_f, wu_f, wd_f, wg_b, wu_b, wd_b, xbuf, sem, xsem, *, n_blocks):
    i = pl.program_id(0)
    n_valid = nvalid_ref[i]
    block_rows = MOE_BLOCK * SUBLANES

    def x_copy(j):
        start = pl.multiple_of(bi_ref[j] * block_rows, block_rows)
        return pltpu.make_async_copy(xs_hbm.at[pl.ds(start, block_rows)], xbuf.at[j % X_RING], xsem.at[j % X_RING])

    @pl.when(i == 0)
    def _():
        for j in range(X_RING - 1):
            x_copy(j).start()

    @pl.when(i + X_RING - 1 < n_blocks)
    def _():
        x_copy(i + X_RING - 1).start()

    x_copy(i).wait()
    x_ref = xbuf.at[i % X_RING]

    def fetch(e):
        return (pltpu.make_async_copy(wg_hbm.at[e], wg_f, sem.at[0]),
                pltpu.make_async_copy(wu_hbm.at[e], wu_f, sem.at[1]),
                pltpu.make_async_copy(wd_hbm.at[e], wd_f, sem.at[2]))

    @pl.when(n_valid > 0)
    def _():
        e = be_ref[i]

        @pl.when(i == 0)
        def _():
            for cp in fetch(e):
                cp.start(priority=1)

        @pl.when((i == 0) | (e != be_ref[jnp.maximum(i - 1, 0)]))
        def _():
            for cp in fetch(e):
                cp.wait()
            def cast_rows(src, dst, rows):
                def body(c, carry):
                    r = pl.multiple_of(c * rows, rows)
                    dst[pl.ds(r, rows), :] = src[pl.ds(r, rows), :].astype(BF16)
                    return carry

                lax.fori_loop(0, src.shape[0] // rows, body, 0)

            cast_rows(wg_f, wg_b, 128)
            cast_rows(wu_f, wu_b, 128)
            cast_rows(wd_f, wd_b, 32)
            nxt = nxt_ref[i]

            @pl.when(nxt >= 0)
            def _():
                for cp in fetch(nxt):
                    cp.start(priority=1)

        words = jnp.concatenate(
            [x_ref[pl.ds(c, MOE_BLOCK, stride=SUBLANES), :] for c in range(SUBLANES)], axis=1)
        lo, hi = _unpack_bf16_pair(words)
        half = words.shape[1]
        g = _dot(lo, wg_b[:half, :]) + _dot(hi, wg_b[half:, :])
        u = _dot(lo, wu_b[:half, :]) + _dot(hi, wu_b[half:, :])
        act = (g * jax.nn.sigmoid(g) * u).astype(BF16)
        y = _dot(act, wd_b[...])
        y_words = _pack_bf16_pair(y[:, :half], y[:, half:])
        for c in range(SUBLANES):
            y_ref[pl.ds(c, MOE_BLOCK, stride=SUBLANES), :] = y_words[:, c * LANES:(c + 1) * LANES]

    @pl.when(n_valid == 0)
    def _():
        y_ref[...] = jnp.zeros_like(y_ref)


def _moe(block_expert, n_valid, block_index, block_next, xs, w_gate, w_up, w_down):
    n_blocks = block_expert.shape[0]
    n_rows = xs.shape[0]
    d = 2 * SUBLANES * LANES
    hbm = pl.BlockSpec(memory_space=pl.ANY)
    grid_spec = pltpu.PrefetchScalarGridSpec(
        num_scalar_prefetch=4,
        grid=(n_blocks,),
        in_specs=[hbm, hbm, hbm, hbm],
        out_specs=pl.BlockSpec((MOE_BLOCK * SUBLANES, LANES), lambda i, be, nv, bi, nx: (i, 0)),
        scratch_shapes=[
            pltpu.VMEM((d, D_EXPERT), F32),
            pltpu.VMEM((d, D_EXPERT), F32),
            pltpu.VMEM((D_EXPERT, d), F32),
            pltpu.VMEM((d, D_EXPERT), BF16),
            pltpu.VMEM((d, D_EXPERT), BF16),
            pltpu.VMEM((D_EXPERT, d), BF16),
            pltpu.VMEM((X_RING, MOE_BLOCK * SUBLANES, LANES), jnp.uint32),
            pltpu.SemaphoreType.DMA((3,)),
            pltpu.SemaphoreType.DMA((X_RING,)),
        ],
    )
    return pl.pallas_call(
        functools.partial(_moe_kernel, n_blocks=n_blocks),
        grid_spec=grid_spec,
        out_shape=jax.ShapeDtypeStruct((n_rows * SUBLANES, LANES), jnp.uint32),
        compiler_params=pltpu.CompilerParams(
            dimension_semantics=("arbitrary",), vmem_limit_bytes=VMEM_LIMIT),
        name="moe",
    )(block_expert, n_valid, block_index, block_next, xs.reshape(n_rows * SUBLANES, LANES), w_gate, w_up, w_down)


def _final_kernel(p0_ref, p1_ref, q0_ref, q1_ref, x1_ref, rt_ref, g2_ref, gf_ref, yb_hbm, yb_flat_hbm, o_ref,
                  y0buf, y1buf, sem, *, tm, n_tiles):
    t = pl.program_id(0)
    slot = t & 1

    def issue(pa_ref, pb_ref, s):
        def body(c, carry):
            for j in range(DMA_ISSUE_UNROLL):
                r = c * DMA_ISSUE_UNROLL + j
                dst = pl.ds(pl.multiple_of(r * SUBLANES, SUBLANES), SUBLANES)
                pltpu.make_async_copy(yb_hbm.at[pa_ref[0, r]], y0buf.at[s].at[dst], sem.at[s]).start(priority=0)
                pltpu.make_async_copy(yb_hbm.at[pb_ref[0, r]], y1buf.at[s].at[dst], sem.at[s]).start(priority=1)
            return carry

        lax.fori_loop(0, tm // DMA_ISSUE_UNROLL, body, 0)

    @pl.when(t == 0)
    def _():
        issue(p0_ref, p1_ref, 0)

    @pl.when(t + 1 < n_tiles)
    def _():
        issue(q0_ref, q1_ref, 1 - slot)

    pltpu.make_async_copy(yb_flat_hbm.at[pl.ds(0, tm * SUBLANES)], y0buf.at[slot], sem.at[slot]).wait()
    pltpu.make_async_copy(yb_flat_hbm.at[pl.ds(0, tm * SUBLANES)], y1buf.at[slot], sem.at[slot]).wait()

    def rows_of(buf):
        tiles = buf.at[slot]
        words = jnp.concatenate([tiles[pl.ds(c, tm, stride=SUBLANES), :] for c in range(SUBLANES)], axis=1)
        lo = lax.bitcast_convert_type(words << 16, F32)
        hi = lax.bitcast_convert_type(words & jnp.uint32(0xFFFF0000), F32)
        return jnp.concatenate([lo, hi], axis=1)

    rt = rt_ref[...]
    y = rt[:, 2:3] * rows_of(y0buf) + rt[:, 3:4] * rows_of(y1buf)
    x2 = x1_ref[...] + g2_ref[...] * y
    ms = jnp.mean(x2 * x2, axis=-1, keepdims=True)
    o_ref[...] = x2 * lax.rsqrt(ms + EPS) * gf_ref[...]


def _final(pos0, pos1, x1, rt, g2, gf, yb, seq):
    n, d = x1.shape
    tm = 256
    n_tiles = n // tm
    per_seq = seq // tm
    this_tile = pl.BlockSpec((None, 1, tm), lambda t: (t, 0, 0), memory_space=pltpu.SMEM)
    next_tile = pl.BlockSpec((None, 1, tm), lambda t: (jnp.minimum(t + 1, n_tiles - 1), 0, 0),
                             memory_space=pltpu.SMEM)
    p0 = pos0.reshape(n_tiles, 1, tm)
    p1 = pos1.reshape(n_tiles, 1, tm)
    return pl.pallas_call(
        functools.partial(_final_kernel, tm=tm, n_tiles=n_tiles),
        grid=(n_tiles,),
        in_specs=[
            this_tile, this_tile, next_tile, next_tile,
            pl.BlockSpec((tm, d), lambda t: (t, 0)),
            pl.BlockSpec((tm, ROUTE_W), lambda t: (t, 0)),
            pl.BlockSpec((None, 1, d), lambda t: (t // per_seq, 0, 0)),
            pl.BlockSpec((1, d), lambda t: (0, 0)),
            pl.BlockSpec(memory_space=pl.ANY),
            pl.BlockSpec(memory_space=pl.ANY),
        ],
        out_specs=pl.BlockSpec((tm, d), lambda t: (t, 0)),
        out_shape=jax.ShapeDtypeStruct((n, d), F32),
        scratch_shapes=[
            pltpu.VMEM((2, tm * SUBLANES, LANES), jnp.uint32),
            pltpu.VMEM((2, tm * SUBLANES, LANES), jnp.uint32),
            pltpu.SemaphoreType.DMA((2,)),
        ],
        compiler_params=pltpu.CompilerParams(
            dimension_semantics=("arbitrary",), vmem_limit_bytes=VMEM_LIMIT,
            disable_bounds_checks=True),
        name="final",
    )(p0, p1, p0, p1, x1, rt, g2, gf, yb.reshape(-1, SUBLANES, LANES), yb)


def kernel(x, c, ada_w, ada_b, norm1_g, w_in, conv3_w, conv4_w, conv4_b, lru_w_a, lru_b_a, lru_w_x, lru_b_x, lru_lambda, head_norm_conv_g, head_norm_lru_g, w_out, norm2_g, route_w_group, route_b_group, route_w_expert, route_b_expert, w_e_gate, w_e_up, w_e_down, final_norm_g):
    bsz, s, d = x.shape
    n_tok = bsz * s
    l = 0

    mod = _ada(c, ada_w[l], ada_b[l]).reshape(bsz, 6, 1, d)
    sh1, sc1, g1, sh2, sc2, g2 = (mod[:, j] for j in range(6))

    proj = _in_proj(x, sc1, sh1, norm1_g[l].reshape(1, d), w_in[l].astype(BF16))

    wax = jnp.concatenate([lru_w_a[l], lru_w_x[l]], axis=-1).astype(BF16)
    y = _mix(proj, conv3_w[l], conv4_w[l], conv4_b[l], wax, lru_b_a[l], lru_b_x[l], lru_lambda[l],
             head_norm_conv_g[l], head_norm_lru_g[l])

    w_route = jnp.concatenate(
        [route_w_group[l], jnp.transpose(route_w_expert[l], (1, 0, 2)).reshape(d, N_EXPERTS),
         jnp.zeros((d, ROUTE_W - N_GROUPS - N_EXPERTS), F32)], axis=1).astype(BF16)
    b_route = jnp.concatenate(
        [route_b_group[l], route_b_expert[l].reshape(-1),
         jnp.zeros((ROUTE_W - N_GROUPS - N_EXPERTS,), F32)]).reshape(1, ROUTE_W)
    x1, h2p, rt, rtt, counts = _out(y, x, g1, w_out[l].astype(BF16), sc2, sh2, norm2_g[l].reshape(1, d),
                                    w_route, b_route)

    rt = rt.reshape(n_tok, ROUTE_W)
    pos0, pos1, block_expert, n_valid, block_index, block_next, n_rows = _plan(rtt, counts, n_tok)
    assert d // 2 == SUBLANES * LANES, "a token's packed words must fill exactly one (8, 128) tile"
    xs = _dispatch(n_valid, pos0, pos1, h2p.reshape(n_tok, SUBLANES, LANES), n_rows)
    yb = _moe(block_expert, n_valid, block_index, block_next, xs, w_e_gate[l], w_e_up[l], w_e_down[l])
    out = _final(pos0, pos1, x1.reshape(n_tok, d), rt, g2, final_norm_g.reshape(1, d), yb, s)
    return out.reshape(bsz, s, d)
```

```python
import functools

import jax
import jax.numpy as jnp
from jax import lax
from jax.experimental import pallas as pl
from jax.experimental.pallas import tpu as pltpu

D_MODEL = 2048
D_CONV = 1024
CONV_HEAD_DIM = 64
D_LRU = 1024
LRU_HEADS = 8
LRU_HEAD_DIM = 128
LRU_C = 8.0
D_MIX = D_CONV + D_LRU
D_IN = 3 * D_CONV + 2 * D_LRU
N_GROUPS = 8
EXPERTS_PER_GROUP = 8
N_EXPERTS = 64
D_EXPERT = 512
EPS = 1e-6

LANES = 128
SUBLANES = 8
ROUTE_W = LANES
MOE_BLOCK = 256
X_RING = 3
DMA_ISSUE_UNROLL = 8
VMEM_LIMIT = 56 * 1024 * 1024

F32 = jnp.float32
BF16 = jnp.bfloat16


def _dot(a, b):
    return jnp.dot(a, b, preferred_element_type=F32)


def _ada_kernel(c_ref, w_ref, b_ref, o_ref):
    c = c_ref[...]
    s = (c * jax.nn.sigmoid(c)).astype(BF16)
    o_ref[...] = _dot(s, w_ref[...].astype(BF16)) + b_ref[...]


def _ada(c, w, b):
    bsz, d = c.shape
    n = w.shape[1]
    tn = 1024
    return pl.pallas_call(
        _ada_kernel,
        grid=(n // tn,),
        in_specs=[
            pl.BlockSpec((bsz, d), lambda j: (0, 0)),
            pl.BlockSpec((d, tn), lambda j: (0, j)),
            pl.BlockSpec((1, tn), lambda j: (0, j)),
        ],
        out_specs=pl.BlockSpec((bsz, tn), lambda j: (0, j)),
        out_shape=jax.ShapeDtypeStruct((bsz, n), F32),
        compiler_params=pltpu.CompilerParams(
            dimension_semantics=("arbitrary",), vmem_limit_bytes=VMEM_LIMIT),
        name="ada",
    )(c, w, b.reshape(1, n))


def _modulated_norm(x, g, sc, sh):
    ms = jnp.mean(x * x, axis=-1, keepdims=True)
    return x * lax.rsqrt(ms + EPS) * (g * (1.0 + sc)) + sh


def _in_proj_kernel(x_ref, sc_ref, sh_ref, g_ref, w_ref, o_ref, *, n_chunk):
    hb = _modulated_norm(x_ref[...], g_ref[...], sc_ref[...], sh_ref[...]).astype(BF16)
    for j in range(D_IN // n_chunk):
        cols = slice(j * n_chunk, (j + 1) * n_chunk)
        o_ref[:, cols] = _dot(hb, w_ref[:, cols])


def _in_proj(x, sc, sh, g, w_bf):
    bsz, s, d = x.shape
    tm = 512
    per_batch = pl.BlockSpec((None, 1, d), lambda b, t: (b, 0, 0))
    return pl.pallas_call(
        functools.partial(_in_proj_kernel, n_chunk=1024),
        grid=(bsz, s // tm),
        in_specs=[
            pl.BlockSpec((None, tm, d), lambda b, t: (b, t, 0)),
            per_batch, per_batch,
            pl.BlockSpec((1, d), lambda b, t: (0, 0)),
            pl.BlockSpec((d, D_IN), lambda b, t: (0, 0), pipeline_mode=pl.Buffered(1)),
        ],
        out_specs=pl.BlockSpec((None, tm, D_IN), lambda b, t: (b, t, 0)),
        out_shape=jax.ShapeDtypeStruct((bsz, s, D_IN), F32),
        compiler_params=pltpu.CompilerParams(
            dimension_semantics=("arbitrary", "arbitrary"), vmem_limit_bytes=VMEM_LIMIT),
        name="in_proj",
    )(x, sc, sh, g, w_bf)


def _shift_rows(x, prev8, d, row):
    rolled = pltpu.roll(x, d, 0)
    head = jnp.where(row[:SUBLANES] >= d, rolled[:SUBLANES], pltpu.roll(prev8, d, 0))
    return jnp.concatenate([head, rolled[SUBLANES:]], axis=0)


def _head_rms(y, p_ref, g):
    sq = y * y
    hi = sq.astype(BF16)
    lo = (sq - hi.astype(F32)).astype(BF16)
    ms = _dot(hi, p_ref[...]) + _dot(lo, p_ref[...])
    return y * lax.rsqrt(ms + EPS) * g


def _linear_scan(a, v, h0, sub):
    groups = []
    for r0 in range(0, a.shape[0], SUBLANES):
        ag, vg = a[r0:r0 + SUBLANES], v[r0:r0 + SUBLANES]
        d = 1
        while d < SUBLANES:
            keep = sub >= d
            vg = vg + ag * jnp.where(keep, pltpu.roll(vg, d, 0), 0.0)
            ag = ag * jnp.where(keep, pltpu.roll(ag, d, 0), 1.0)
            d *= 2
        groups.append((ag, vg))
    out, carry = [], h0
    for ag, vg in groups:
        hg = vg + ag * carry
        carry = hg[SUBLANES - 1:, :]
        out.append(hg)
    return jnp.concatenate(out, axis=0), carry


def _gelu_tanh(x):
    c = 0.7978845608028654
    half_x = 0.5 * x
    return half_x + half_x * jnp.tanh(x * (c + (c * 0.044715) * (x * x)))


def _mix_kernel(p_ref, w3_ref, w4_ref, b4_ref, wax_ref, ba_ref, bx_ref, lam_ref, ga_ref, gb_ref,
                pa_ref, pb_ref, y_ref, ua_carry, xb_carry, h_carry, *, ts):
    @pl.when(pl.program_id(1) == 0)
    def _():
        ua_carry[...] = jnp.zeros_like(ua_carry)
        xb_carry[...] = jnp.zeros_like(xb_carry)
        h_carry[...] = jnp.zeros_like(h_carry)

    row = lax.broadcasted_iota(jnp.int32, (ts, LANES), 0)

    for k in range(D_CONV // LANES):
        cols = slice(k * LANES, (k + 1) * LANES)
        b_a = p_ref[:, k * LANES:(k + 1) * LANES]
        c_a = p_ref[:, D_CONV + k * LANES:D_CONV + (k + 1) * LANES]
        x_a = p_ref[:, 2 * D_CONV + k * LANES:2 * D_CONV + (k + 1) * LANES]
        u = c_a * x_a
        prev = ua_carry[:, cols]
        conv = (w3_ref[2:3, cols] * u
                + w3_ref[1:2, cols] * _shift_rows(u, prev, 1, row)
                + w3_ref[0:1, cols] * _shift_rows(u, prev, 2, row))
        ua_carry[:, cols] = u[ts - SUBLANES:, :]
        y_a = b_a * conv
        y_ref[:, cols] = _head_rms(y_a, pa_ref, ga_ref[:, cols]).astype(BF16)

        g_b = p_ref[:, 3 * D_CONV + k * LANES:3 * D_CONV + (k + 1) * LANES]
        x_b = p_ref[:, 3 * D_CONV + D_LRU + k * LANES:3 * D_CONV + D_LRU + (k + 1) * LANES]
        prevb = xb_carry[:, cols]
        xc = (w4_ref[3:4, cols] * x_b
              + w4_ref[2:3, cols] * _shift_rows(x_b, prevb, 1, row)
              + w4_ref[1:2, cols] * _shift_rows(x_b, prevb, 2, row)
              + w4_ref[0:1, cols] * _shift_rows(x_b, prevb, 3, row)
              + b4_ref[:, cols])
        xb_carry[:, cols] = x_b[ts - SUBLANES:, :]
        gates = _dot(xc.astype(BF16), wax_ref[k])
        r = jax.nn.sigmoid(gates[:, :LANES] + ba_ref[:, cols])
        i = jax.nn.sigmoid(gates[:, LANES:] + bx_ref[:, cols])
        nlam = -lam_ref[:, cols]
        softplus = jnp.maximum(nlam, 0.0) + jnp.log1p(jnp.exp(-jnp.abs(nlam)))
        log_a = r * ((-LRU_C) * softplus)
        a = jnp.exp(log_a)
        mult = jnp.sqrt(-jnp.tanh(log_a) * (a * a + 1.0))
        v = mult * i * xc
        hs, h_last = _linear_scan(a, v, h_carry[:, cols], row[:SUBLANES])
        h_carry[:, cols] = h_last
        y_b = hs * _gelu_tanh(g_b)
        y_ref[:, D_CONV + k * LANES:D_CONV + (k + 1) * LANES] = _head_rms(
            y_b, pb_ref, gb_ref[:, cols]).astype(BF16)


def _mix(proj, w3, w4, b4, wax_bf, b_a, b_x, lam, g_a, g_b):
    bsz, s, _ = proj.shape
    ts = 512
    lane = jnp.arange(LANES)
    p_a = ((lane[:, None] // CONV_HEAD_DIM) == (lane[None, :] // CONV_HEAD_DIM)).astype(BF16) / CONV_HEAD_DIM
    p_b = jnp.full((LANES, LANES), 1.0 / LRU_HEAD_DIM, BF16)
    full = lambda shape: pl.BlockSpec(shape, lambda b, t: (0,) * len(shape))
    row = lambda a: a.reshape(1, -1)
    return pl.pallas_call(
        functools.partial(_mix_kernel, ts=ts),
        grid=(bsz, s // ts),
        in_specs=[
            pl.BlockSpec((None, ts, D_IN), lambda b, t: (b, t, 0)),
            full((3, D_CONV)), full((4, D_LRU)), full((1, D_LRU)),
            full((LRU_HEADS, LRU_HEAD_DIM, 2 * LRU_HEAD_DIM)),
            full((1, D_LRU)), full((1, D_LRU)), full((1, D_LRU)),
            full((1, D_CONV)), full((1, D_LRU)),
            full((LANES, LANES)), full((LANES, LANES)),
        ],
        out_specs=pl.BlockSpec((None, ts, D_MIX), lambda b, t: (b, t, 0)),
        out_shape=jax.ShapeDtypeStruct((bsz, s, D_MIX), BF16),
        scratch_shapes=[
            pltpu.VMEM((SUBLANES, D_CONV), F32),
            pltpu.VMEM((SUBLANES, D_LRU), F32),
            pltpu.VMEM((1, D_LRU), F32),
        ],
        compiler_params=pltpu.CompilerParams(
            dimension_semantics=("arbitrary", "arbitrary"), vmem_limit_bytes=VMEM_LIMIT),
        name="mix",
    )(proj, w3, w4, row(b4), wax_bf, row(b_a), row(b_x), row(lam), row(g_a), row(g_b), p_a, p_b)


def _route(logits, tri, carry):
    lane = lax.broadcasted_iota(jnp.int32, logits.shape, 1)
    neg = -jnp.inf
    is_g = lane < N_GROUPS
    gl = jnp.where(is_g, logits, neg)
    ge = jnp.exp(gl - jnp.max(gl, axis=-1, keepdims=True))
    p_g = 1.0 / jnp.sum(ge, axis=-1, keepdims=True)
    g_idx = jnp.min(jnp.where(is_g & (ge >= 1.0), lane, LANES), axis=-1, keepdims=True)
    sel = (lane >= N_GROUPS) & (lane < N_GROUPS + N_EXPERTS) & (((lane - N_GROUPS) >> 3) == g_idx)
    el = jnp.where(sel, logits, neg)
    m1 = jnp.max(el, axis=-1, keepdims=True)
    i1 = jnp.min(jnp.where(sel & (el == m1), lane, LANES), axis=-1, keepdims=True)
    sel2 = sel & (lane != i1)
    el2 = jnp.where(sel2, logits, neg)
    m2 = jnp.max(el2, axis=-1, keepdims=True)
    i2 = jnp.min(jnp.where(sel2 & (el2 == m2), lane, LANES), axis=-1, keepdims=True)
    e2 = jnp.exp(m2 - m1)
    w1 = p_g * (1.0 / (1.0 + e2))
    w2 = p_g * (e2 / (1.0 + e2))
    out = jnp.where(lane == 0, (i1 - N_GROUPS).astype(F32), 0.0)
    out = jnp.where(lane == 1, (i2 - N_GROUPS).astype(F32), out)
    out = jnp.where(lane == 2, w1, out)
    out = jnp.where(lane == 3, w2, out)
    oh1 = lane == i1
    oh2 = lane == i2
    both = jnp.where(oh1 | oh2, 1.0, 0.0)
    before = _dot(tri, both.astype(BF16)) + carry
    out = jnp.where(lane == 4, jnp.sum(jnp.where(oh1, before, 0.0), axis=-1, keepdims=True), out)
    out = jnp.where(lane == 5, jnp.sum(jnp.where(oh2, before, 0.0), axis=-1, keepdims=True), out)
    return out, carry + jnp.sum(both, axis=0, keepdims=True)


def _pack_bf16_pair(lo, hi):
    lo_bits = lax.bitcast_convert_type(lo.astype(BF16).astype(F32), jnp.uint32)
    hi_bits = lax.bitcast_convert_type(hi.astype(BF16).astype(F32), jnp.uint32)
    return (lo_bits >> 16) | hi_bits


def _unpack_bf16_pair(w):
    lo = lax.bitcast_convert_type(w << 16, F32).astype(BF16)
    hi = lax.bitcast_convert_type(w & jnp.uint32(0xFFFF0000), F32).astype(BF16)
    return lo, hi


def _out_kernel(y_ref, x_ref, g1_ref, w_ref, sc_ref, sh_ref, g_ref, wr_ref, br_ref, tri_ref,
                x1_ref, h2p_ref, rt_ref, rtt_ref, cnt_ref, carry):
    @pl.when((pl.program_id(0) == 0) & (pl.program_id(1) == 0))
    def _():
        carry[...] = jnp.zeros_like(carry)

    sub = tri_ref.shape[0]
    count = carry[...]
    tiles = [slice(r0, r0 + sub) for r0 in range(0, y_ref.shape[0], sub)]
    mixes = [_dot(y_ref[rows, :], w_ref[...]) for rows in tiles]
    for rows, mix in zip(tiles, mixes):
        x1 = x_ref[rows, :] + g1_ref[...] * mix
        x1_ref[rows, :] = x1
        h2 = _modulated_norm(x1, g_ref[...], sc_ref[...], sh_ref[...])
        half = h2.shape[1] // 2
        words = _pack_bf16_pair(h2[:, :half], h2[:, half:])
        for c in range(half // LANES):
            h2p_ref[pl.ds(rows.start * SUBLANES + c, sub, stride=SUBLANES), :] = words[:, c * LANES:(c + 1) * LANES]
        logits = _dot(h2.astype(BF16), wr_ref[...]) + br_ref[...]
        rt, count = _route(logits, tri_ref[...], count)
        rt_ref[rows, :] = rt
        rtt_ref[:, rows] = rt.T[:SUBLANES, :]
    carry[...] = count
    cnt_ref[...] = count


def _out(y, x, g1, w_bf, sc2, sh2, g, wr_bf, br):
    bsz, s, d = x.shape
    tm, sub = 512, 256
    per_batch = pl.BlockSpec((None, 1, d), lambda b, t: (b, 0, 0))
    tile = lambda w: pl.BlockSpec((None, tm, w), lambda b, t: (b, t, 0))
    const = lambda shape, **kw: pl.BlockSpec(shape, lambda b, t: (0, 0), **kw)
    return pl.pallas_call(
        _out_kernel,
        grid=(bsz, s // tm),
        in_specs=[
            tile(D_MIX), tile(d), per_batch,
            const((D_MIX, d), pipeline_mode=pl.Buffered(1)),
            per_batch, per_batch, const((1, d)),
            const((d, ROUTE_W)), const((1, ROUTE_W)), const((sub, sub)),
        ],
        out_specs=[tile(d), pl.BlockSpec((None, tm * SUBLANES, LANES), lambda b, t: (b, t, 0)),
                   tile(ROUTE_W), pl.BlockSpec((SUBLANES, tm), lambda b, t: (0, b * (s // tm) + t)),
                   const((1, ROUTE_W))],
        out_shape=[
            jax.ShapeDtypeStruct((bsz, s, d), F32),
            jax.ShapeDtypeStruct((bsz, s * SUBLANES, LANES), jnp.uint32),
            jax.ShapeDtypeStruct((bsz, s, ROUTE_W), F32),
            jax.ShapeDtypeStruct((SUBLANES, bsz * s), F32),
            jax.ShapeDtypeStruct((1, ROUTE_W), F32),
        ],
        scratch_shapes=[pltpu.VMEM((1, ROUTE_W), F32)],
        compiler_params=pltpu.CompilerParams(
            dimension_semantics=("arbitrary", "arbitrary"), vmem_limit_bytes=VMEM_LIMIT),
        name="out_route",
    )(y, x, g1, w_bf, sc2, sh2, g, wr_bf, br, jnp.tril(jnp.ones((sub, sub), BF16), -1))


def _plan(rtt, counts_f, n_tok):
    counts = counts_f[0, N_GROUPS:N_GROUPS + N_EXPERTS].astype(jnp.int32)
    padded = ((counts + MOE_BLOCK - 1) // MOE_BLOCK) * MOE_BLOCK
    padded_ends = jnp.cumsum(padded)
    padded_starts = padded_ends - padded
    n_rows = 2 * n_tok + N_EXPERTS * MOE_BLOCK
    n_blocks = n_rows // MOE_BLOCK
    expert_ids = jnp.arange(N_EXPERTS, dtype=jnp.int32)
    idx = rtt.astype(jnp.int32)
    start_of = jnp.sum(jnp.where(idx[0:2, :, None] == expert_ids, padded_starts, 0), axis=-1)
    pos = start_of + idx[4:6]
    block_start = jnp.arange(n_blocks, dtype=jnp.int32)[:, None] * MOE_BLOCK
    owns = (padded_starts <= block_start) & (block_start < padded_ends)
    n_valid = jnp.sum(jnp.where(owns, jnp.clip(counts - (block_start - padded_starts), 0, MOE_BLOCK), 0), axis=1)
    last_used = jnp.max(jnp.where(counts > 0, expert_ids, 0))
    block_expert = jnp.where(n_valid > 0, jnp.sum(jnp.where(owns, expert_ids, 0), axis=1), last_used)
    later_used = (expert_ids[None, :] > expert_ids[:, None]) & (counts[None, :] > 0)
    next_used = jnp.min(jnp.where(later_used, expert_ids[None, :], N_EXPERTS), axis=1)
    next_used = jnp.where(next_used == N_EXPERTS, -1, next_used)
    block_next = jnp.sum(jnp.where(owns, next_used, 0), axis=1)
    n_used = padded_ends[-1] // MOE_BLOCK
    block_index = jnp.minimum(jnp.arange(n_blocks, dtype=jnp.int32), n_used - 1)
    i32 = lambda a: a.astype(jnp.int32)
    return pos[0], pos[1], i32(block_expert), i32(n_valid), i32(block_index), i32(block_next), n_rows


def _dispatch_kernel(nvalid_ref, p0_ref, p1_ref, h_ref, xs_hbm, zbuf, sem, zsem, *, tm, n_blocks):
    @pl.when(pl.program_id(0) == 0)
    def _():
        zbuf[...] = jnp.zeros_like(zbuf)

        def zero_block(i):
            return pltpu.make_async_copy(zbuf, xs_hbm.at[pl.ds(i * MOE_BLOCK, MOE_BLOCK)], zsem)

        def start(i, carry):
            pl.when(nvalid_ref[i] < MOE_BLOCK)(lambda: zero_block(i).start())
            return carry

        def wait(i, carry):
            pl.when(nvalid_ref[i] < MOE_BLOCK)(lambda: zero_block(i).wait())
            return carry

        lax.fori_loop(0, n_blocks, start, 0)
        lax.fori_loop(0, n_blocks, wait, 0)

    def issue(c, carry):
        for j in range(DMA_ISSUE_UNROLL):
            r = c * DMA_ISSUE_UNROLL + j
            src = h_ref.at[r]
            pltpu.make_async_copy(src, xs_hbm.at[p0_ref[0, r]], sem).start(priority=0)
            pltpu.make_async_copy(src, xs_hbm.at[p1_ref[0, r]], sem).start(priority=1)
        return carry

    lax.fori_loop(0, tm // DMA_ISSUE_UNROLL, issue, 0)
    for _ in range(2):
        pltpu.make_async_copy(h_ref, xs_hbm.at[pl.ds(0, tm)], sem).wait()


def _dispatch(n_valid, pos0, pos1, h2p, n_rows):
    n = h2p.shape[0]
    tm = 2048
    n_blocks = n_valid.shape[0]
    idx_spec = pl.BlockSpec((None, 1, tm), lambda t, nv: (t, 0, 0), memory_space=pltpu.SMEM)
    grid_spec = pltpu.PrefetchScalarGridSpec(
        num_scalar_prefetch=1,
        grid=(n // tm,),
        in_specs=[idx_spec, idx_spec, pl.BlockSpec((tm, SUBLANES, LANES), lambda t, nv: (t, 0, 0))],
        out_specs=pl.BlockSpec(memory_space=pl.ANY),
        scratch_shapes=[
            pltpu.VMEM((MOE_BLOCK, SUBLANES, LANES), h2p.dtype),
            pltpu.SemaphoreType.DMA(()),
            pltpu.SemaphoreType.DMA(()),
        ],
    )
    return pl.pallas_call(
        functools.partial(_dispatch_kernel, tm=tm, n_blocks=n_blocks),
        grid_spec=grid_spec,
        out_shape=jax.ShapeDtypeStruct((n_rows, SUBLANES, LANES), h2p.dtype),
        compiler_params=pltpu.CompilerParams(
            dimension_semantics=("arbitrary",), vmem_limit_bytes=VMEM_LIMIT,
            disable_bounds_checks=True),
        name="dispatch",
    )(n_valid, pos0.reshape(n // tm, 1, tm), pos1.reshape(n // tm, 1, tm), h2p)


def _moe_kernel(be_ref, nvalid_ref, bi_ref, nxt_ref, xs_hbm, wg_hbm, wu_hbm, wd_hbm, y_ref,
                wg_f, wu_f, wd_f, wg_b, wu_b, wd_b, xbuf, sem, xsem, *, n_blocks):
    i = pl.program_id(0)
    n_valid = nvalid_ref[i]
    block_rows = MOE_BLOCK * SUBLANES

    def x_copy(j):
        start = pl.multiple_of(bi_ref[j] * block_rows, block_rows)
        return pltpu.make_async_copy(xs_hbm.at[pl.ds(start, block_rows)], xbuf.at[j % X_RING], xsem.at[j % X_RING])

    @pl.when(i == 0)
    def _():
        for j in range(X_RING - 1):
            x_copy(j).start()

    @pl.when(i + X_RING - 1 < n_blocks)
    def _():
        x_copy(i + X_RING - 1).start()

    x_copy(i).wait()
    x_ref = xbuf.at[i % X_RING]

    def fetch(e):
        return (pltpu.make_async_copy(wg_hbm.at[e], wg_f, sem.at[0]),
                pltpu.make_async_copy(wu_hbm.at[e], wu_f, sem.at[1]),
                pltpu.make_async_copy(wd_hbm.at[e], wd_f, sem.at[2]))

    @pl.when(n_valid > 0)
    def _():
        e = be_ref[i]

        @pl.when(i == 0)
        def _():
            for cp in fetch(e):
                cp.start()

        @pl.when((i == 0) | (e != be_ref[jnp.maximum(i - 1, 0)]))
        def _():
            for cp in fetch(e):
                cp.wait()
            def cast_rows(src, dst, rows):
                def body(c, carry):
                    r = pl.multiple_of(c * rows, rows)
                    dst[pl.ds(r, rows), :] = src[pl.ds(r, rows), :].astype(BF16)
                    return carry

                lax.fori_loop(0, src.shape[0] // rows, body, 0)

            cast_rows(wg_f, wg_b, 128)
            cast_rows(wu_f, wu_b, 128)
            cast_rows(wd_f, wd_b, 32)
            nxt = nxt_ref[i]

            @pl.when(nxt >= 0)
            def _():
                for cp in fetch(nxt):
                    cp.start()

        words = jnp.concatenate(
            [x_ref[pl.ds(c, MOE_BLOCK, stride=SUBLANES), :] for c in range(SUBLANES)], axis=1)
        lo, hi = _unpack_bf16_pair(words)
        half = words.shape[1]
        g = _dot(lo, wg_b[:half, :]) + _dot(hi, wg_b[half:, :])
        u = _dot(lo, wu_b[:half, :]) + _dot(hi, wu_b[half:, :])
        act = (g * jax.nn.sigmoid(g) * u).astype(BF16)
        y = _dot(act, wd_b[...])
        y_words = _pack_bf16_pair(y[:, :half], y[:, half:])
        for c in range(SUBLANES):
            y_ref[pl.ds(c, MOE_BLOCK, stride=SUBLANES), :] = y_words[:, c * LANES:(c + 1) * LANES]

    @pl.when(n_valid == 0)
    def _():
        y_ref[...] = jnp.zeros_like(y_ref)


def _moe(block_expert, n_valid, block_index, block_next, xs, w_gate, w_up, w_down):
    n_blocks = block_expert.shape[0]
    n_rows = xs.shape[0]
    d = 2 * SUBLANES * LANES
    hbm = pl.BlockSpec(memory_space=pl.ANY)
    grid_spec = pltpu.PrefetchScalarGridSpec(
        num_scalar_prefetch=4,
        grid=(n_blocks,),
        in_specs=[hbm, hbm, hbm, hbm],
        out_specs=pl.BlockSpec((MOE_BLOCK * SUBLANES, LANES), lambda i, be, nv, bi, nx: (i, 0)),
        scratch_shapes=[
            pltpu.VMEM((d, D_EXPERT), F32),
            pltpu.VMEM((d, D_EXPERT), F32),
            pltpu.VMEM((D_EXPERT, d), F32),
            pltpu.VMEM((d, D_EXPERT), BF16),
            pltpu.VMEM((d, D_EXPERT), BF16),
            pltpu.VMEM((D_EXPERT, d), BF16),
            pltpu.VMEM((X_RING, MOE_BLOCK * SUBLANES, LANES), jnp.uint32),
            pltpu.SemaphoreType.DMA((3,)),
            pltpu.SemaphoreType.DMA((X_RING,)),
        ],
    )
    return pl.pallas_call(
        functools.partial(_moe_kernel, n_blocks=n_blocks),
        grid_spec=grid_spec,
        out_shape=jax.ShapeDtypeStruct((n_rows * SUBLANES, LANES), jnp.uint32),
        compiler_params=pltpu.CompilerParams(
            dimension_semantics=("arbitrary",), vmem_limit_bytes=VMEM_LIMIT),
        name="moe",
    )(block_expert, n_valid, block_index, block_next, xs.reshape(n_rows * SUBLANES, LANES), w_gate, w_up, w_down)


def _final_kernel(p0_ref, p1_ref, q0_ref, q1_ref, x1_ref, rt_ref, g2_ref, gf_ref, yb_hbm, yb_flat_hbm, o_ref,
                  y0buf, y1buf, sem, *, tm, n_tiles):
    t = pl.program_id(0)
    slot = t & 1

    def issue(pa_ref, pb_ref, s):
        def body(c, carry):
            for j in range(DMA_ISSUE_UNROLL):
                r = c * DMA_ISSUE_UNROLL + j
                dst = pl.ds(pl.multiple_of(r * SUBLANES, SUBLANES), SUBLANES)
                pltpu.make_async_copy(yb_hbm.at[pa_ref[0, r]], y0buf.at[s].at[dst], sem.at[s]).start(priority=0)
                pltpu.make_async_copy(yb_hbm.at[pb_ref[0, r]], y1buf.at[s].at[dst], sem.at[s]).start(priority=1)
            return carry

        lax.fori_loop(0, tm // DMA_ISSUE_UNROLL, body, 0)

    @pl.when(t == 0)
    def _():
        issue(p0_ref, p1_ref, 0)

    @pl.when(t + 1 < n_tiles)
    def _():
        issue(q0_ref, q1_ref, 1 - slot)

    pltpu.make_async_copy(yb_flat_hbm.at[pl.ds(0, tm * SUBLANES)], y0buf.at[slot], sem.at[slot]).wait()
    pltpu.make_async_copy(yb_flat_hbm.at[pl.ds(0, tm * SUBLANES)], y1buf.at[slot], sem.at[slot]).wait()

    def rows_of(buf):
        tiles = buf.at[slot]
        words = jnp.concatenate([tiles[pl.ds(c, tm, stride=SUBLANES), :] for c in range(SUBLANES)], axis=1)
        lo = lax.bitcast_convert_type(words << 16, F32)
        hi = lax.bitcast_convert_type(words & jnp.uint32(0xFFFF0000), F32)
        return jnp.concatenate([lo, hi], axis=1)

    rt = rt_ref[...]
    y = rt[:, 2:3] * rows_of(y0buf) + rt[:, 3:4] * rows_of(y1buf)
    x2 = x1_ref[...] + g2_ref[...] * y
    ms = jnp.mean(x2 * x2, axis=-1, keepdims=True)
    o_ref[...] = x2 * lax.rsqrt(ms + EPS) * gf_ref[...]


def _final(pos0, pos1, x1, rt, g2, gf, yb, seq):
    n, d = x1.shape
    tm = 256
    n_tiles = n // tm
    per_seq = seq // tm
    this_tile = pl.BlockSpec((None, 1, tm), lambda t: (t, 0, 0), memory_space=pltpu.SMEM)
    next_tile = pl.BlockSpec((None, 1, tm), lambda t: (jnp.minimum(t + 1, n_tiles - 1), 0, 0),
                             memory_space=pltpu.SMEM)
    p0 = pos0.reshape(n_tiles, 1, tm)
    p1 = pos1.reshape(n_tiles, 1, tm)
    return pl.pallas_call(
        functools.partial(_final_kernel, tm=tm, n_tiles=n_tiles),
        grid=(n_tiles,),
        in_specs=[
            this_tile, this_tile, next_tile, next_tile,
            pl.BlockSpec((tm, d), lambda t: (t, 0)),
            pl.BlockSpec((tm, ROUTE_W), lambda t: (t, 0)),
            pl.BlockSpec((None, 1, d), lambda t: (t // per_seq, 0, 0)),
            pl.BlockSpec((1, d), lambda t: (0, 0)),
            pl.BlockSpec(memory_space=pl.ANY),
            pl.BlockSpec(memory_space=pl.ANY),
        ],
        out_specs=pl.BlockSpec((tm, d), lambda t: (t, 0)),
        out_shape=jax.ShapeDtypeStruct((n, d), F32),
        scratch_shapes=[
            pltpu.VMEM((2, tm * SUBLANES, LANES), jnp.uint32),
            pltpu.VMEM((2, tm * SUBLANES, LANES), jnp.uint32),
            pltpu.SemaphoreType.DMA((2,)),
        ],
        compiler_params=pltpu.CompilerParams(
            dimension_semantics=("arbitrary",), vmem_limit_bytes=VMEM_LIMIT,
            disable_bounds_checks=True),
        name="final",
    )(p0, p1, p0, p1, x1, rt, g2, gf, yb.reshape(-1, SUBLANES, LANES), yb)


def kernel(x, c, ada_w, ada_b, norm1_g, w_in, conv3_w, conv4_w, conv4_b, lru_w_a, lru_b_a, lru_w_x, lru_b_x, lru_lambda, head_norm_conv_g, head_norm_lru_g, w_out, norm2_g, route_w_group, route_b_group, route_w_expert, route_b_expert, w_e_gate, w_e_up, w_e_down, final_norm_g):
    bsz, s, d = x.shape
    n_tok = bsz * s
    l = 0

    mod = _ada(c, ada_w[l], ada_b[l]).reshape(bsz, 6, 1, d)
    sh1, sc1, g1, sh2, sc2, g2 = (mod[:, j] for j in range(6))

    proj = _in_proj(x, sc1, sh1, norm1_g[l].reshape(1, d), w_in[l].astype(BF16))

    wax = jnp.concatenate([lru_w_a[l], lru_w_x[l]], axis=-1).astype(BF16)
    y = _mix(proj, conv3_w[l], conv4_w[l], conv4_b[l], wax, lru_b_a[l], lru_b_x[l], lru_lambda[l],
             head_norm_conv_g[l], head_norm_lru_g[l])

    w_route = jnp.concatenate(
        [route_w_group[l], jnp.transpose(route_w_expert[l], (1, 0, 2)).reshape(d, N_EXPERTS),
         jnp.zeros((d, ROUTE_W - N_GROUPS - N_EXPERTS), F32)], axis=1).astype(BF16)
    b_route = jnp.concatenate(
        [route_b_group[l], route_b_expert[l].reshape(-1),
         jnp.zeros((ROUTE_W - N_GROUPS - N_EXPERTS,), F32)]).reshape(1, ROUTE_W)
    x1, h2p, rt, rtt, counts = _out(y, x, g1, w_out[l].astype(BF16), sc2, sh2, norm2_g[l].reshape(1, d),
                                    w_route, b_route)

    rt = rt.reshape(n_tok, ROUTE_W)
    pos0, pos1, block_expert, n_valid, block_index, block_next, n_rows = _plan(rtt, counts, n_tok)
    assert d // 2 == SUBLANES * LANES, "a token's packed words must fill exactly one (8, 128) tile"
    xs = _dispatch(n_valid, pos0, pos1, h2p.reshape(n_tok, SUBLANES, LANES), n_rows)
    yb = _moe(block_expert, n_valid, block_index, block_next, xs, w_e_gate[l], w_e_up[l], w_e_down[l])
    out = _final(pos0, pos1, x1.reshape(n_tok, d), rt, g2, final_norm_g.reshape(1, d), yb, s)
    return out.reshape(bsz, s, d)
```

```python
import functools

import jax
import jax.numpy as jnp
from jax import lax
from jax.experimental import pallas as pl
from jax.experimental.pallas import tpu as pltpu

D_MODEL = 2048
D_CONV = 1024
CONV_HEAD_DIM = 64
D_LRU = 1024
LRU_HEADS = 8
LRU_HEAD_DIM = 128
LRU_C = 8.0
D_MIX = D_CONV + D_LRU
D_IN = 3 * D_CONV + 2 * D_LRU
N_GROUPS = 8
EXPERTS_PER_GROUP = 8
N_EXPERTS = 64
D_EXPERT = 512
EPS = 1e-6

LANES = 128
SUBLANES = 8
ROUTE_W = LANES
MOE_BLOCK = 256
X_RING = 3
DMA_ISSUE_UNROLL = 8
VMEM_LIMIT = 56 * 1024 * 1024

F32 = jnp.float32
BF16 = jnp.bfloat16


def _dot(a, b):
    return jnp.dot(a, b, preferred_element_type=F32)


def _ada_kernel(c_ref, w_ref, b_ref, o_ref):
    c = c_ref[...]
    s = (c * jax.nn.sigmoid(c)).astype(BF16)
    o_ref[...] = _dot(s, w_ref[...].astype(BF16)) + b_ref[...]


def _ada(c, w, b):
    bsz, d = c.shape
    n = w.shape[1]
    tn = 1024
    return pl.pallas_call(
        _ada_kernel,
        grid=(n // tn,),
        in_specs=[
            pl.BlockSpec((bsz, d), lambda j: (0, 0)),
            pl.BlockSpec((d, tn), lambda j: (0, j)),
            pl.BlockSpec((1, tn), lambda j: (0, j)),
        ],
        out_specs=pl.BlockSpec((bsz, tn), lambda j: (0, j)),
        out_shape=jax.ShapeDtypeStruct((bsz, n), F32),
        compiler_params=pltpu.CompilerParams(
            dimension_semantics=("arbitrary",), vmem_limit_bytes=VMEM_LIMIT),
        name="ada",
    )(c, w, b.reshape(1, n))


def _modulated_norm(x, g, sc, sh):
    ms = jnp.mean(x * x, axis=-1, keepdims=True)
    return x * lax.rsqrt(ms + EPS) * (g * (1.0 + sc)) + sh


def _in_proj_kernel(x_ref, sc_ref, sh_ref, g_ref, w_ref, o_ref, *, n_chunk):
    hb = _modulated_norm(x_ref[...], g_ref[...], sc_ref[...], sh_ref[...]).astype(BF16)
    for j in range(D_IN // n_chunk):
        cols = slice(j * n_chunk, (j + 1) * n_chunk)
        o_ref[:, cols] = _dot(hb, w_ref[:, cols])


def _in_proj(x, sc, sh, g, w_bf):
    bsz, s, d = x.shape
    tm = 512
    per_batch = pl.BlockSpec((None, 1, d), lambda b, t: (b, 0, 0))
    return pl.pallas_call(
        functools.partial(_in_proj_kernel, n_chunk=1024),
        grid=(bsz, s // tm),
        in_specs=[
            pl.BlockSpec((None, tm, d), lambda b, t: (b, t, 0)),
            per_batch, per_batch,
            pl.BlockSpec((1, d), lambda b, t: (0, 0)),
            pl.BlockSpec((d, D_IN), lambda b, t: (0, 0), pipeline_mode=pl.Buffered(1)),
        ],
        out_specs=pl.BlockSpec((None, tm, D_IN), lambda b, t: (b, t, 0)),
        out_shape=jax.ShapeDtypeStruct((bsz, s, D_IN), F32),
        compiler_params=pltpu.CompilerParams(
            dimension_semantics=("arbitrary", "arbitrary"), vmem_limit_bytes=VMEM_LIMIT),
        name="in_proj",
    )(x, sc, sh, g, w_bf)


def _shift_rows(x, prev8, d, row):
    rolled = pltpu.roll(x, d, 0)
    head = jnp.where(row[:SUBLANES] >= d, rolled[:SUBLANES], pltpu.roll(prev8, d, 0))
    return jnp.concatenate([head, rolled[SUBLANES:]], axis=0)


def _head_rms(y, p_ref, g):
    sq = y * y
    hi = sq.astype(BF16)
    lo = (sq - hi.astype(F32)).astype(BF16)
    ms = _dot(hi, p_ref[...]) + _dot(lo, p_ref[...])
    return y * lax.rsqrt(ms + EPS) * g


def _linear_scan(a, v, h0, sub):
    groups = []
    for r0 in range(0, a.shape[0], SUBLANES):
        ag, vg = a[r0:r0 + SUBLANES], v[r0:r0 + SUBLANES]
        d = 1
        while d < SUBLANES:
            keep = sub >= d
            vg = vg + ag * jnp.where(keep, pltpu.roll(vg, d, 0), 0.0)
            ag = ag * jnp.where(keep, pltpu.roll(ag, d, 0), 1.0)
            d *= 2
        groups.append((ag, vg))
    out, carry = [], h0
    for ag, vg in groups:
        hg = vg + ag * carry
        carry = hg[SUBLANES - 1:, :]
        out.append(hg)
    return jnp.concatenate(out, axis=0), carry


def _gelu_tanh(x):
    c = 0.7978845608028654
    half_x = 0.5 * x
    return half_x + half_x * jnp.tanh(x * (c + (c * 0.044715) * (x * x)))


def _mix_kernel(p_ref, w3_ref, w4_ref, b4_ref, wax_ref, ba_ref, bx_ref, lam_ref, ga_ref, gb_ref,
                pa_ref, pb_ref, wo_ref, o_ref, ua_carry, xb_carry, h_carry, y_scr, *, ts, tiles_per_seq):
    n = pl.program_id(0)
    slot = n & 1

    @pl.when(n == 0)
    def _():
        y_scr[1] = jnp.zeros(y_scr.shape[1:], y_scr.dtype)

    @pl.when(n % tiles_per_seq == 0)
    def _():
        ua_carry[...] = jnp.zeros_like(ua_carry)
        xb_carry[...] = jnp.zeros_like(xb_carry)
        h_carry[...] = jnp.zeros_like(h_carry)

    row = lax.broadcasted_iota(jnp.int32, (ts, LANES), 0)
    y_ref = y_scr.at[slot]
    y_prev = y_scr.at[1 - slot]
    cw = o_ref.shape[1] // (D_CONV // LANES)

    for k in range(D_CONV // LANES):
        o_ref[:, k * cw:(k + 1) * cw] = _dot(y_prev[...], wo_ref[:, k * cw:(k + 1) * cw])
        cols = slice(k * LANES, (k + 1) * LANES)
        b_a = p_ref[:, k * LANES:(k + 1) * LANES]
        c_a = p_ref[:, D_CONV + k * LANES:D_CONV + (k + 1) * LANES]
        x_a = p_ref[:, 2 * D_CONV + k * LANES:2 * D_CONV + (k + 1) * LANES]
        u = c_a * x_a
        prev = ua_carry[:, cols]
        conv = (w3_ref[2:3, cols] * u
                + w3_ref[1:2, cols] * _shift_rows(u, prev, 1, row)
                + w3_ref[0:1, cols] * _shift_rows(u, prev, 2, row))
        ua_carry[:, cols] = u[ts - SUBLANES:, :]
        y_a = b_a * conv
        y_ref[:, cols] = _head_rms(y_a, pa_ref, ga_ref[:, cols]).astype(BF16)

        g_b = p_ref[:, 3 * D_CONV + k * LANES:3 * D_CONV + (k + 1) * LANES]
        x_b = p_ref[:, 3 * D_CONV + D_LRU + k * LANES:3 * D_CONV + D_LRU + (k + 1) * LANES]
        prevb = xb_carry[:, cols]
        xc = (w4_ref[3:4, cols] * x_b
              + w4_ref[2:3, cols] * _shift_rows(x_b, prevb, 1, row)
              + w4_ref[1:2, cols] * _shift_rows(x_b, prevb, 2, row)
              + w4_ref[0:1, cols] * _shift_rows(x_b, prevb, 3, row)
              + b4_ref[:, cols])
        xb_carry[:, cols] = x_b[ts - SUBLANES:, :]
        gates = _dot(xc.astype(BF16), wax_ref[k])
        r = jax.nn.sigmoid(gates[:, :LANES] + ba_ref[:, cols])
        i = jax.nn.sigmoid(gates[:, LANES:] + bx_ref[:, cols])
        nlam = -lam_ref[:, cols]
        softplus = jnp.maximum(nlam, 0.0) + jnp.log1p(jnp.exp(-jnp.abs(nlam)))
        log_a = r * ((-LRU_C) * softplus)
        a = jnp.exp(log_a)
        mult = jnp.sqrt(-jnp.tanh(log_a) * (a * a + 1.0))
        v = mult * i * xc
        hs, h_last = _linear_scan(a, v, h_carry[:, cols], row[:SUBLANES])
        h_carry[:, cols] = h_last
        y_b = hs * _gelu_tanh(g_b)
        y_ref[:, D_CONV + k * LANES:D_CONV + (k + 1) * LANES] = _head_rms(
            y_b, pb_ref, gb_ref[:, cols]).astype(BF16)


def _mix(proj, w3, w4, b4, wax_bf, b_a, b_x, lam, g_a, g_b, w_out_bf):
    bsz, s, _ = proj.shape
    d = w_out_bf.shape[1]
    ts = 512
    tiles_per_seq = s // ts
    n_tiles = bsz * tiles_per_seq
    lane = jnp.arange(LANES)
    p_a = ((lane[:, None] // CONV_HEAD_DIM) == (lane[None, :] // CONV_HEAD_DIM)).astype(BF16) / CONV_HEAD_DIM
    p_b = jnp.full((LANES, LANES), 1.0 / LRU_HEAD_DIM, BF16)
    full = lambda shape, **kw: pl.BlockSpec(shape, lambda n: (0,) * len(shape), **kw)
    row = lambda a: a.reshape(1, -1)
    return pl.pallas_call(
        functools.partial(_mix_kernel, ts=ts, tiles_per_seq=tiles_per_seq),
        grid=(n_tiles + 1,),
        in_specs=[
            pl.BlockSpec((ts, D_IN), lambda n: (jnp.minimum(n, n_tiles - 1), 0)),
            full((3, D_CONV)), full((4, D_LRU)), full((1, D_LRU)),
            full((LRU_HEADS, LRU_HEAD_DIM, 2 * LRU_HEAD_DIM)),
            full((1, D_LRU)), full((1, D_LRU)), full((1, D_LRU)),
            full((1, D_CONV)), full((1, D_LRU)),
            full((LANES, LANES)), full((LANES, LANES)),
            full((D_MIX, d), pipeline_mode=pl.Buffered(1)),
        ],
        out_specs=pl.BlockSpec((ts, d), lambda n: (jnp.maximum(n - 1, 0), 0)),
        out_shape=jax.ShapeDtypeStruct((bsz * s, d), F32),
        scratch_shapes=[
            pltpu.VMEM((SUBLANES, D_CONV), F32),
            pltpu.VMEM((SUBLANES, D_LRU), F32),
            pltpu.VMEM((1, D_LRU), F32),
            pltpu.VMEM((2, ts, D_MIX), BF16),
        ],
        compiler_params=pltpu.CompilerParams(
            dimension_semantics=("arbitrary",), vmem_limit_bytes=VMEM_LIMIT),
        name="mix",
    )(proj.reshape(bsz * s, D_IN), w3, w4, row(b4), wax_bf, row(b_a), row(b_x), row(lam), row(g_a), row(g_b),
      p_a, p_b, w_out_bf).reshape(bsz, s, d)


def _route(logits, tri, carry):
    lane = lax.broadcasted_iota(jnp.int32, logits.shape, 1)
    neg = -jnp.inf
    is_g = lane < N_GROUPS
    gl = jnp.where(is_g, logits, neg)
    ge = jnp.exp(gl - jnp.max(gl, axis=-1, keepdims=True))
    p_g = 1.0 / jnp.sum(ge, axis=-1, keepdims=True)
    g_idx = jnp.min(jnp.where(is_g & (ge >= 1.0), lane, LANES), axis=-1, keepdims=True)
    sel = (lane >= N_GROUPS) & (lane < N_GROUPS + N_EXPERTS) & (((lane - N_GROUPS) >> 3) == g_idx)
    el = jnp.where(sel, logits, neg)
    m1 = jnp.max(el, axis=-1, keepdims=True)
    i1 = jnp.min(jnp.where(sel & (el == m1), lane, LANES), axis=-1, keepdims=True)
    sel2 = sel & (lane != i1)
    el2 = jnp.where(sel2, logits, neg)
    m2 = jnp.max(el2, axis=-1, keepdims=True)
    i2 = jnp.min(jnp.where(sel2 & (el2 == m2), lane, LANES), axis=-1, keepdims=True)
    e2 = jnp.exp(m2 - m1)
    w1 = p_g * (1.0 / (1.0 + e2))
    w2 = p_g * (e2 / (1.0 + e2))
    out = jnp.where(lane == 0, (i1 - N_GROUPS).astype(F32), 0.0)
    out = jnp.where(lane == 1, (i2 - N_GROUPS).astype(F32), out)
    out = jnp.where(lane == 2, w1, out)
    out = jnp.where(lane == 3, w2, out)
    oh1 = lane == i1
    oh2 = lane == i2
    both = jnp.where(oh1 | oh2, 1.0, 0.0)
    before = _dot(tri, both.astype(BF16)) + carry
    out = jnp.where(lane == 4, jnp.sum(jnp.where(oh1, before, 0.0), axis=-1, keepdims=True), out)
    out = jnp.where(lane == 5, jnp.sum(jnp.where(oh2, before, 0.0), axis=-1, keepdims=True), out)
    return out, carry + jnp.sum(both, axis=0, keepdims=True)


def _pack_bf16_pair(lo, hi):
    lo_bits = lax.bitcast_convert_type(lo.astype(BF16).astype(F32), jnp.uint32)
    hi_bits = lax.bitcast_convert_type(hi.astype(BF16).astype(F32), jnp.uint32)
    return (lo_bits >> 16) | hi_bits


def _unpack_bf16_pair(w):
    lo = lax.bitcast_convert_type(w << 16, F32).astype(BF16)
    hi = lax.bitcast_convert_type(w & jnp.uint32(0xFFFF0000), F32).astype(BF16)
    return lo, hi


def _out_kernel(mix_ref, x_ref, g1_ref, sc_ref, sh_ref, g_ref, wr_ref, br_ref, tri_ref,
                x1_ref, h2p_ref, rt_ref, rtt_ref, cnt_ref, carry):
    @pl.when((pl.program_id(0) == 0) & (pl.program_id(1) == 0))
    def _():
        carry[...] = jnp.zeros_like(carry)

    sub = tri_ref.shape[0]
    count = carry[...]
    for r0 in range(0, x_ref.shape[0], sub):
        rows = slice(r0, r0 + sub)
        x1 = x_ref[rows, :] + g1_ref[...] * mix_ref[rows, :]
        x1_ref[rows, :] = x1
        h2 = _modulated_norm(x1, g_ref[...], sc_ref[...], sh_ref[...])
        half = h2.shape[1] // 2
        words = _pack_bf16_pair(h2[:, :half], h2[:, half:])
        for c in range(half // LANES):
            h2p_ref[pl.ds(rows.start * SUBLANES + c, sub, stride=SUBLANES), :] = words[:, c * LANES:(c + 1) * LANES]
        logits = _dot(h2.astype(BF16), wr_ref[...]) + br_ref[...]
        rt, count = _route(logits, tri_ref[...], count)
        rt_ref[rows, :] = rt
        rtt_ref[:, rows] = rt.T[:SUBLANES, :]
    carry[...] = count
    cnt_ref[...] = count


def _out(mix, x, g1, sc2, sh2, g, wr_bf, br):
    bsz, s, d = x.shape
    tm, sub = 512, 256
    per_batch = pl.BlockSpec((None, 1, d), lambda b, t: (b, 0, 0))
    tile = lambda w: pl.BlockSpec((None, tm, w), lambda b, t: (b, t, 0))
    const = lambda shape, **kw: pl.BlockSpec(shape, lambda b, t: (0, 0), **kw)
    return pl.pallas_call(
        _out_kernel,
        grid=(bsz, s // tm),
        in_specs=[
            tile(d), tile(d), per_batch,
            per_batch, per_batch, const((1, d)),
            const((d, ROUTE_W)), const((1, ROUTE_W)), const((sub, sub)),
        ],
        out_specs=[tile(d), pl.BlockSpec((None, tm * SUBLANES, LANES), lambda b, t: (b, t, 0)),
                   tile(ROUTE_W), pl.BlockSpec((SUBLANES, tm), lambda b, t: (0, b * (s // tm) + t)),
                   const((1, ROUTE_W))],
        out_shape=[
            jax.ShapeDtypeStruct((bsz, s, d), F32),
            jax.ShapeDtypeStruct((bsz, s * SUBLANES, LANES), jnp.uint32),
            jax.ShapeDtypeStruct((bsz, s, ROUTE_W), F32),
            jax.ShapeDtypeStruct((SUBLANES, bsz * s), F32),
            jax.ShapeDtypeStruct((1, ROUTE_W), F32),
        ],
        scratch_shapes=[pltpu.VMEM((1, ROUTE_W), F32)],
        compiler_params=pltpu.CompilerParams(
            dimension_semantics=("arbitrary", "arbitrary"), vmem_limit_bytes=VMEM_LIMIT),
        name="out_route",
    )(mix, x, g1, sc2, sh2, g, wr_bf, br, jnp.tril(jnp.ones((sub, sub), BF16), -1))


def _plan(rtt, counts_f, n_tok):
    counts = counts_f[0, N_GROUPS:N_GROUPS + N_EXPERTS].astype(jnp.int32)
    padded = ((counts + MOE_BLOCK - 1) // MOE_BLOCK) * MOE_BLOCK
    padded_ends = jnp.cumsum(padded)
    padded_starts = padded_ends - padded
    n_rows = 2 * n_tok + N_EXPERTS * MOE_BLOCK
    n_blocks = n_rows // MOE_BLOCK
    expert_ids = jnp.arange(N_EXPERTS, dtype=jnp.int32)
    idx = rtt.astype(jnp.int32)
    start_of = jnp.sum(jnp.where(idx[0:2, :, None] == expert_ids, padded_starts, 0), axis=-1)
    pos = start_of + idx[4:6]
    block_start = jnp.arange(n_blocks, dtype=jnp.int32)[:, None] * MOE_BLOCK
    owns = (padded_starts <= block_start) & (block_start < padded_ends)
    n_valid = jnp.sum(jnp.where(owns, jnp.clip(counts - (block_start - padded_starts), 0, MOE_BLOCK), 0), axis=1)
    last_used = jnp.max(jnp.where(counts > 0, expert_ids, 0))
    block_expert = jnp.where(n_valid > 0, jnp.sum(jnp.where(owns, expert_ids, 0), axis=1), last_used)
    later_used = (expert_ids[None, :] > expert_ids[:, None]) & (counts[None, :] > 0)
    next_used = jnp.min(jnp.where(later_used, expert_ids[None, :], N_EXPERTS), axis=1)
    next_used = jnp.where(next_used == N_EXPERTS, -1, next_used)
    block_next = jnp.sum(jnp.where(owns, next_used, 0), axis=1)
    n_used = padded_ends[-1] // MOE_BLOCK
    block_index = jnp.minimum(jnp.arange(n_blocks, dtype=jnp.int32), n_used - 1)
    i32 = lambda a: a.astype(jnp.int32)
    return pos[0], pos[1], i32(block_expert), i32(n_valid), i32(block_index), i32(block_next), n_rows


def _dispatch_kernel(nvalid_ref, p0_ref, p1_ref, h_ref, xs_hbm, zbuf, sem, zsem, *, tm, n_blocks):
    @pl.when(pl.program_id(0) == 0)
    def _():
        zbuf[...] = jnp.zeros_like(zbuf)

        def zero_block(i):
            return pltpu.make_async_copy(zbuf, xs_hbm.at[pl.ds(i * MOE_BLOCK, MOE_BLOCK)], zsem)

        def start(i, carry):
            pl.when(nvalid_ref[i] < MOE_BLOCK)(lambda: zero_block(i).start())
            return carry

        def wait(i, carry):
            pl.when(nvalid_ref[i] < MOE_BLOCK)(lambda: zero_block(i).wait())
            return carry

        lax.fori_loop(0, n_blocks, start, 0)
        lax.fori_loop(0, n_blocks, wait, 0)

    def issue(c, carry):
        for j in range(DMA_ISSUE_UNROLL):
            r = c * DMA_ISSUE_UNROLL + j
            src = h_ref.at[r]
            pltpu.make_async_copy(src, xs_hbm.at[p0_ref[0, r]], sem).start(priority=0)
            pltpu.make_async_copy(src, xs_hbm.at[p1_ref[0, r]], sem).start(priority=1)
        return carry

    lax.fori_loop(0, tm // DMA_ISSUE_UNROLL, issue, 0)
    for _ in range(2):
        pltpu.make_async_copy(h_ref, xs_hbm.at[pl.ds(0, tm)], sem).wait()


def _dispatch(n_valid, pos0, pos1, h2p, n_rows):
    n = h2p.shape[0]
    tm = 2048
    n_blocks = n_valid.shape[0]
    idx_spec = pl.BlockSpec((None, 1, tm), lambda t, nv: (t, 0, 0), memory_space=pltpu.SMEM)
    grid_spec = pltpu.PrefetchScalarGridSpec(
        num_scalar_prefetch=1,
        grid=(n // tm,),
        in_specs=[idx_spec, idx_spec, pl.BlockSpec((tm, SUBLANES, LANES), lambda t, nv: (t, 0, 0))],
        out_specs=pl.BlockSpec(memory_space=pl.ANY),
        scratch_shapes=[
            pltpu.VMEM((MOE_BLOCK, SUBLANES, LANES), h2p.dtype),
            pltpu.SemaphoreType.DMA(()),
            pltpu.SemaphoreType.DMA(()),
        ],
    )
    return pl.pallas_call(
        functools.partial(_dispatch_kernel, tm=tm, n_blocks=n_blocks),
        grid_spec=grid_spec,
        out_shape=jax.ShapeDtypeStruct((n_rows, SUBLANES, LANES), h2p.dtype),
        compiler_params=pltpu.CompilerParams(
            dimension_semantics=("arbitrary",), vmem_limit_bytes=VMEM_LIMIT,
            disable_bounds_checks=True),
        name="dispatch",
    )(n_valid, pos0.reshape(n // tm, 1, tm), pos1.reshape(n // tm, 1, tm), h2p)


def _moe_kernel(be_ref, nvalid_ref, bi_ref, nxt_ref, xs_hbm, wg_hbm, wu_hbm, wd_hbm, y_ref,
                wg_f, wu_f, wd_f, wg_b, wu_b, wd_b, xbuf, sem, xsem, *, n_blocks):
    i = pl.program_id(0)
    n_valid = nvalid_ref[i]
    block_rows = MOE_BLOCK * SUBLANES

    def x_copy(j):
        start = pl.multiple_of(bi_ref[j] * block_rows, block_rows)
        return pltpu.make_async_copy(xs_hbm.at[pl.ds(start, block_rows)], xbuf.at[j % X_RING], xsem.at[j % X_RING])

    @pl.when(i == 0)
    def _():
        for j in range(X_RING - 1):
            x_copy(j).start()

    @pl.when(i + X_RING - 1 < n_blocks)
    def _():
        x_copy(i + X_RING - 1).start()

    x_copy(i).wait()
    x_ref = xbuf.at[i % X_RING]

    def fetch(e):
        return (pltpu.make_async_copy(wg_hbm.at[e], wg_f, sem.at[0]),
                pltpu.make_async_copy(wu_hbm.at[e], wu_f, sem.at[1]),
                pltpu.make_async_copy(wd_hbm.at[e], wd_f, sem.at[2]))

    @pl.when(n_valid > 0)
    def _():
        e = be_ref[i]

        @pl.when(i == 0)
        def _():
            for cp in fetch(e):
                cp.start(priority=1)

        @pl.when((i == 0) | (e != be_ref[jnp.maximum(i - 1, 0)]))
        def _():
            for cp in fetch(e):
                cp.wait()
            def cast_rows(src, dst, rows):
                def body(c, carry):
                    r = pl.multiple_of(c * rows, rows)
                    dst[pl.ds(r, rows), :] = src[pl.ds(r, rows), :].astype(BF16)
                    return carry

                lax.fori_loop(0, src.shape[0] // rows, body, 0)

            cast_rows(wg_f, wg_b, 128)
            cast_rows(wu_f, wu_b, 128)
            cast_rows(wd_f, wd_b, 32)
            nxt = nxt_ref[i]

            @pl.when(nxt >= 0)
            def _():
                for cp in fetch(nxt):
                    cp.start(priority=1)

        words = jnp.concatenate(
            [x_ref[pl.ds(c, MOE_BLOCK, stride=SUBLANES), :] for c in range(SUBLANES)], axis=1)
        lo, hi = _unpack_bf16_pair(words)
        half = words.shape[1]
        g = _dot(lo, wg_b[:half, :]) + _dot(hi, wg_b[half:, :])
        u = _dot(lo, wu_b[:half, :]) + _dot(hi, wu_b[half:, :])
        act = (g * jax.nn.sigmoid(g) * u).astype(BF16)
        y = _dot(act, wd_b[...])
        y_words = _pack_bf16_pair(y[:, :half], y[:, half:])
        for c in range(SUBLANES):
            y_ref[pl.ds(c, MOE_BLOCK, stride=SUBLANES), :] = y_words[:, c * LANES:(c + 1) * LANES]

    @pl.when(n_valid == 0)
    def _():
        y_ref[...] = jnp.zeros_like(y_ref)


def _moe(block_expert, n_valid, block_index, block_next, xs, w_gate, w_up, w_down):
    n_blocks = block_expert.shape[0]
    n_rows = xs.shape[0]
    d = 2 * SUBLANES * LANES
    hbm = pl.BlockSpec(memory_space=pl.ANY)
    grid_spec = pltpu.PrefetchScalarGridSpec(
        num_scalar_prefetch=4,
        grid=(n_blocks,),
        in_specs=[hbm, hbm, hbm, hbm],
        out_specs=pl.BlockSpec((MOE_BLOCK * SUBLANES, LANES), lambda i, be, nv, bi, nx: (i, 0)),
        scratch_shapes=[
            pltpu.VMEM((d, D_EXPERT), F32),
            pltpu.VMEM((d, D_EXPERT), F32),
            pltpu.VMEM((D_EXPERT, d), F32),
            pltpu.VMEM((d, D_EXPERT), BF16),
            pltpu.VMEM((d, D_EXPERT), BF16),
            pltpu.VMEM((D_EXPERT, d), BF16),
            pltpu.VMEM((X_RING, MOE_BLOCK * SUBLANES, LANES), jnp.uint32),
            pltpu.SemaphoreType.DMA((3,)),
            pltpu.SemaphoreType.DMA((X_RING,)),
        ],
    )
    return pl.pallas_call(
        functools.partial(_moe_kernel, n_blocks=n_blocks),
        grid_spec=grid_spec,
        out_shape=jax.ShapeDtypeStruct((n_rows * SUBLANES, LANES), jnp.uint32),
        compiler_params=pltpu.CompilerParams(
            dimension_semantics=("arbitrary",), vmem_limit_bytes=VMEM_LIMIT),
        name="moe",
    )(block_expert, n_valid, block_index, block_next, xs.reshape(n_rows * SUBLANES, LANES), w_gate, w_up, w_down)


def _final_kernel(p0_ref, p1_ref, q0_ref, q1_ref, x1_ref, rt_ref, g2_ref, gf_ref, yb_hbm, yb_flat_hbm, o_ref,
                  y0buf, y1buf, sem, *, tm, n_tiles):
    t = pl.program_id(0)
    slot = t & 1

    def issue(pa_ref, pb_ref, s):
        def body(c, carry):
            for j in range(DMA_ISSUE_UNROLL):
                r = c * DMA_ISSUE_UNROLL + j
                dst = pl.ds(pl.multiple_of(r * SUBLANES, SUBLANES), SUBLANES)
                pltpu.make_async_copy(yb_hbm.at[pa_ref[0, r]], y0buf.at[s].at[dst], sem.at[s]).start(priority=0)
                pltpu.make_async_copy(yb_hbm.at[pb_ref[0, r]], y1buf.at[s].at[dst], sem.at[s]).start(priority=1)
            return carry

        lax.fori_loop(0, tm // DMA_ISSUE_UNROLL, body, 0)

    @pl.when(t == 0)
    def _():
        issue(p0_ref, p1_ref, 0)

    @pl.when(t + 1 < n_tiles)
    def _():
        issue(q0_ref, q1_ref, 1 - slot)

    pltpu.make_async_copy(yb_flat_hbm.at[pl.ds(0, tm * SUBLANES)], y0buf.at[slot], sem.at[slot]).wait()
    pltpu.make_async_copy(yb_flat_hbm.at[pl.ds(0, tm * SUBLANES)], y1buf.at[slot], sem.at[slot]).wait()

    def rows_of(buf):
        tiles = buf.at[slot]
        words = jnp.concatenate([tiles[pl.ds(c, tm, stride=SUBLANES), :] for c in range(SUBLANES)], axis=1)
        lo = lax.bitcast_convert_type(words << 16, F32)
        hi = lax.bitcast_convert_type(words & jnp.uint32(0xFFFF0000), F32)
        return jnp.concatenate([lo, hi], axis=1)

    rt = rt_ref[...]
    y = rt[:, 2:3] * rows_of(y0buf) + rt[:, 3:4] * rows_of(y1buf)
    x2 = x1_ref[...] + g2_ref[...] * y
    ms = jnp.mean(x2 * x2, axis=-1, keepdims=True)
    o_ref[...] = x2 * lax.rsqrt(ms + EPS) * gf_ref[...]


def _final(pos0, pos1, x1, rt, g2, gf, yb, seq):
    n, d = x1.shape
    tm = 256
    n_tiles = n // tm
    per_seq = seq // tm
    this_tile = pl.BlockSpec((None, 1, tm), lambda t: (t, 0, 0), memory_space=pltpu.SMEM)
    next_tile = pl.BlockSpec((None, 1, tm), lambda t: (jnp.minimum(t + 1, n_tiles - 1), 0, 0),
                             memory_space=pltpu.SMEM)
    p0 = pos0.reshape(n_tiles, 1, tm)
    p1 = pos1.reshape(n_tiles, 1, tm)
    return pl.pallas_call(
        functools.partial(_final_kernel, tm=tm, n_tiles=n_tiles),
        grid=(n_tiles,),
        in_specs=[
            this_tile, this_tile, next_tile, next_tile,
            pl.BlockSpec((tm, d), lambda t: (t, 0)),
            pl.BlockSpec((tm, ROUTE_W), lambda t: (t, 0)),
            pl.BlockSpec((None, 1, d), lambda t: (t // per_seq, 0, 0)),
            pl.BlockSpec((1, d), lambda t: (0, 0)),
            pl.BlockSpec(memory_space=pl.ANY),
            pl.BlockSpec(memory_space=pl.ANY),
        ],
        out_specs=pl.BlockSpec((tm, d), lambda t: (t, 0)),
        out_shape=jax.ShapeDtypeStruct((n, d), F32),
        scratch_shapes=[
            pltpu.VMEM((2, tm * SUBLANES, LANES), jnp.uint32),
            pltpu.VMEM((2, tm * SUBLANES, LANES), jnp.uint32),
            pltpu.SemaphoreType.DMA((2,)),
        ],
        compiler_params=pltpu.CompilerParams(
            dimension_semantics=("arbitrary",), vmem_limit_bytes=VMEM_LIMIT,
            disable_bounds_checks=True),
        name="final",
    )(p0, p1, p0, p1, x1, rt, g2, gf, yb.reshape(-1, SUBLANES, LANES), yb)


def kernel(x, c, ada_w, ada_b, norm1_g, w_in, conv3_w, conv4_w, conv4_b, lru_w_a, lru_b_a, lru_w_x, lru_b_x, lru_lambda, head_norm_conv_g, head_norm_lru_g, w_out, norm2_g, route_w_group, route_b_group, route_w_expert, route_b_expert, w_e_gate, w_e_up, w_e_down, final_norm_g):
    bsz, s, d = x.shape
    n_tok = bsz * s
    l = 0

    mod = _ada(c, ada_w[l], ada_b[l]).reshape(bsz, 6, 1, d)
    sh1, sc1, g1, sh2, sc2, g2 = (mod[:, j] for j in range(6))

    proj = _in_proj(x, sc1, sh1, norm1_g[l].reshape(1, d), w_in[l].astype(BF16))

    wax = jnp.concatenate([lru_w_a[l], lru_w_x[l]], axis=-1).astype(BF16)
    mix = _mix(proj, conv3_w[l], conv4_w[l], conv4_b[l], wax, lru_b_a[l], lru_b_x[l], lru_lambda[l],
               head_norm_conv_g[l], head_norm_lru_g[l], w_out[l].astype(BF16))

    w_route = jnp.concatenate(
        [route_w_group[l], jnp.transpose(route_w_expert[l], (1, 0, 2)).reshape(d, N_EXPERTS),
         jnp.zeros((d, ROUTE_W - N_GROUPS - N_EXPERTS), F32)], axis=1).astype(BF16)
    b_route = jnp.concatenate(
        [route_b_group[l], route_b_expert[l].reshape(-1),
         jnp.zeros((ROUTE_W - N_GROUPS - N_EXPERTS,), F32)]).reshape(1, ROUTE_W)
    x1, h2p, rt, rtt, counts = _out(mix, x, g1, sc2, sh2, norm2_g[l].reshape(1, d), w_route, b_route)

    rt = rt.reshape(n_tok, ROUTE_W)
    pos0, pos1, block_expert, n_valid, block_index, block_next, n_rows = _plan(rtt, counts, n_tok)
    assert d // 2 == SUBLANES * LANES, "a token's packed words must fill exactly one (8, 128) tile"
    xs = _dispatch(n_valid, pos0, pos1, h2p.reshape(n_tok, SUBLANES, LANES), n_rows)
    yb = _moe(block_expert, n_valid, block_index, block_next, xs, w_e_gate[l], w_e_up[l], w_e_down[l])
    out = _final(pos0, pos1, x1.reshape(n_tok, d), rt, g2, final_norm_g.reshape(1, d), yb, s)
    return out.reshape(bsz, s, d)
```

```python
import functools

import jax
import jax.numpy as jnp
from jax import lax
from jax.experimental import pallas as pl
from jax.experimental.pallas import tpu as pltpu

D_MODEL = 2048
D_CONV = 1024
CONV_HEAD_DIM = 64
D_LRU = 1024
LRU_HEADS = 8
LRU_HEAD_DIM = 128
LRU_C = 8.0
D_MIX = D_CONV + D_LRU
D_IN = 3 * D_CONV + 2 * D_LRU
N_GROUPS = 8
EXPERTS_PER_GROUP = 8
N_EXPERTS = 64
D_EXPERT = 512
EPS = 1e-6

LANES = 128
SUBLANES = 8
ROUTE_W = LANES
MOE_BLOCK = 256
X_RING = 3
DMA_ISSUE_UNROLL = 8
VMEM_LIMIT = 56 * 1024 * 1024

F32 = jnp.float32
BF16 = jnp.bfloat16


def _dot(a, b):
    return jnp.dot(a, b, preferred_element_type=F32)


def _ada_kernel(c_ref, w_ref, b_ref, o_ref):
    c = c_ref[...]
    s = (c * jax.nn.sigmoid(c)).astype(BF16)
    o_ref[...] = _dot(s, w_ref[...].astype(BF16)) + b_ref[...]


def _ada(c, w, b):
    bsz, d = c.shape
    n = w.shape[1]
    tn = 1024
    return pl.pallas_call(
        _ada_kernel,
        grid=(n // tn,),
        in_specs=[
            pl.BlockSpec((bsz, d), lambda j: (0, 0)),
            pl.BlockSpec((d, tn), lambda j: (0, j)),
            pl.BlockSpec((1, tn), lambda j: (0, j)),
        ],
        out_specs=pl.BlockSpec((bsz, tn), lambda j: (0, j)),
        out_shape=jax.ShapeDtypeStruct((bsz, n), F32),
        compiler_params=pltpu.CompilerParams(
            dimension_semantics=("arbitrary",), vmem_limit_bytes=VMEM_LIMIT),
        name="ada",
    )(c, w, b.reshape(1, n))


def _modulated_norm(x, g, sc, sh):
    ms = jnp.mean(x * x, axis=-1, keepdims=True)
    return x * lax.rsqrt(ms + EPS) * (g * (1.0 + sc)) + sh


def _in_proj_kernel(x_ref, sc_ref, sh_ref, g_ref, w_ref, o_ref, *, n_chunk):
    hb = _modulated_norm(x_ref[...], g_ref[...], sc_ref[...], sh_ref[...]).astype(BF16)
    for j in range(D_IN // n_chunk):
        cols = slice(j * n_chunk, (j + 1) * n_chunk)
        o_ref[:, cols] = _dot(hb, w_ref[:, cols])


def _in_proj(x, sc, sh, g, w_bf):
    bsz, s, d = x.shape
    tm = 512
    per_batch = pl.BlockSpec((None, 1, d), lambda b, t: (b, 0, 0))
    return pl.pallas_call(
        functools.partial(_in_proj_kernel, n_chunk=1024),
        grid=(bsz, s // tm),
        in_specs=[
            pl.BlockSpec((None, tm, d), lambda b, t: (b, t, 0)),
            per_batch, per_batch,
            pl.BlockSpec((1, d), lambda b, t: (0, 0)),
            pl.BlockSpec((d, D_IN), lambda b, t: (0, 0), pipeline_mode=pl.Buffered(1)),
        ],
        out_specs=pl.BlockSpec((None, tm, D_IN), lambda b, t: (b, t, 0)),
        out_shape=jax.ShapeDtypeStruct((bsz, s, D_IN), F32),
        compiler_params=pltpu.CompilerParams(
            dimension_semantics=("arbitrary", "arbitrary"), vmem_limit_bytes=VMEM_LIMIT),
        name="in_proj",
    )(x, sc, sh, g, w_bf)


def _shift_rows(x, prev8, d, row):
    rolled = pltpu.roll(x, d, 0)
    head = jnp.where(row[:SUBLANES] >= d, rolled[:SUBLANES], pltpu.roll(prev8, d, 0))
    return jnp.concatenate([head, rolled[SUBLANES:]], axis=0)


def _head_rms(y, p_ref, g):
    sq = y * y
    hi = sq.astype(BF16)
    lo = (sq - hi.astype(F32)).astype(BF16)
    ms = _dot(hi, p_ref[...]) + _dot(lo, p_ref[...])
    return y * lax.rsqrt(ms + EPS) * g


def _linear_scan(a, v, h0, sub):
    groups = []
    for r0 in range(0, a.shape[0], SUBLANES):
        ag, vg = a[r0:r0 + SUBLANES], v[r0:r0 + SUBLANES]
        d = 1
        while d < SUBLANES:
            keep = sub >= d
            vg = vg + ag * jnp.where(keep, pltpu.roll(vg, d, 0), 0.0)
            ag = ag * jnp.where(keep, pltpu.roll(ag, d, 0), 1.0)
            d *= 2
        groups.append((ag, vg))
    out, carry = [], h0
    for ag, vg in groups:
        hg = vg + ag * carry
        carry = hg[SUBLANES - 1:, :]
        out.append(hg)
    return jnp.concatenate(out, axis=0), carry


def _gelu_tanh(x):
    c = 0.7978845608028654
    half_x = 0.5 * x
    return half_x + half_x * jnp.tanh(x * (c + (c * 0.044715) * (x * x)))


def _mix_kernel(p_ref, w3_ref, w4_ref, b4_ref, wax_ref, ba_ref, bx_ref, lam_ref, ga_ref, gb_ref,
                pa_ref, pb_ref, y_ref, ua_carry, xb_carry, h_carry, *, ts):
    @pl.when(pl.program_id(1) == 0)
    def _():
        ua_carry[...] = jnp.zeros_like(ua_carry)
        xb_carry[...] = jnp.zeros_like(xb_carry)
        h_carry[...] = jnp.zeros_like(h_carry)

    row = lax.broadcasted_iota(jnp.int32, (ts, LANES), 0)

    for k in range(D_CONV // LANES):
        cols = slice(k * LANES, (k + 1) * LANES)
        b_a = p_ref[:, k * LANES:(k + 1) * LANES]
        c_a = p_ref[:, D_CONV + k * LANES:D_CONV + (k + 1) * LANES]
        x_a = p_ref[:, 2 * D_CONV + k * LANES:2 * D_CONV + (k + 1) * LANES]
        u = c_a * x_a
        prev = ua_carry[:, cols]
        conv = (w3_ref[2:3, cols] * u
                + w3_ref[1:2, cols] * _shift_rows(u, prev, 1, row)
                + w3_ref[0:1, cols] * _shift_rows(u, prev, 2, row))
        ua_carry[:, cols] = u[ts - SUBLANES:, :]
        y_a = b_a * conv
        y_ref[:, cols] = _head_rms(y_a, pa_ref, ga_ref[:, cols]).astype(BF16)

        g_b = p_ref[:, 3 * D_CONV + k * LANES:3 * D_CONV + (k + 1) * LANES]
        x_b = p_ref[:, 3 * D_CONV + D_LRU + k * LANES:3 * D_CONV + D_LRU + (k + 1) * LANES]
        prevb = xb_carry[:, cols]
        xc = (w4_ref[3:4, cols] * x_b
              + w4_ref[2:3, cols] * _shift_rows(x_b, prevb, 1, row)
              + w4_ref[1:2, cols] * _shift_rows(x_b, prevb, 2, row)
              + w4_ref[0:1, cols] * _shift_rows(x_b, prevb, 3, row)
              + b4_ref[:, cols])
        xb_carry[:, cols] = x_b[ts - SUBLANES:, :]
        gates = _dot(xc.astype(BF16), wax_ref[k])
        r = jax.nn.sigmoid(gates[:, :LANES] + ba_ref[:, cols])
        i = jax.nn.sigmoid(gates[:, LANES:] + bx_ref[:, cols])
        nlam = -lam_ref[:, cols]
        softplus = jnp.maximum(nlam, 0.0) + jnp.log1p(jnp.exp(-jnp.abs(nlam)))
        log_a = r * ((-LRU_C) * softplus)
        a = jnp.exp(log_a)
        mult = jnp.sqrt(-jnp.tanh(log_a) * (a * a + 1.0))
        v = mult * i * xc
        hs, h_last = _linear_scan(a, v, h_carry[:, cols], row[:SUBLANES])
        h_carry[:, cols] = h_last
        y_b = hs * _gelu_tanh(g_b)
        y_ref[:, D_CONV + k * LANES:D_CONV + (k + 1) * LANES] = _head_rms(
            y_b, pb_ref, gb_ref[:, cols]).astype(BF16)


def _mix(proj, w3, w4, b4, wax_bf, b_a, b_x, lam, g_a, g_b):
    bsz, s, _ = proj.shape
    ts = 512
    lane = jnp.arange(LANES)
    p_a = ((lane[:, None] // CONV_HEAD_DIM) == (lane[None, :] // CONV_HEAD_DIM)).astype(BF16) / CONV_HEAD_DIM
    p_b = jnp.full((LANES, LANES), 1.0 / LRU_HEAD_DIM, BF16)
    full = lambda shape: pl.BlockSpec(shape, lambda b, t: (0,) * len(shape))
    row = lambda a: a.reshape(1, -1)
    return pl.pallas_call(
        functools.partial(_mix_kernel, ts=ts),
        grid=(bsz, s // ts),
        in_specs=[
            pl.BlockSpec((None, ts, D_IN), lambda b, t: (b, t, 0)),
            full((3, D_CONV)), full((4, D_LRU)), full((1, D_LRU)),
            full((LRU_HEADS, LRU_HEAD_DIM, 2 * LRU_HEAD_DIM)),
            full((1, D_LRU)), full((1, D_LRU)), full((1, D_LRU)),
            full((1, D_CONV)), full((1, D_LRU)),
            full((LANES, LANES)), full((LANES, LANES)),
        ],
        out_specs=pl.BlockSpec((None, ts, D_MIX), lambda b, t: (b, t, 0)),
        out_shape=jax.ShapeDtypeStruct((bsz, s, D_MIX), BF16),
        scratch_shapes=[
            pltpu.VMEM((SUBLANES, D_CONV), F32),
            pltpu.VMEM((SUBLANES, D_LRU), F32),
            pltpu.VMEM((1, D_LRU), F32),
        ],
        compiler_params=pltpu.CompilerParams(
            dimension_semantics=("arbitrary", "arbitrary"), vmem_limit_bytes=VMEM_LIMIT),
        name="mix",
    )(proj, w3, w4, row(b4), wax_bf, row(b_a), row(b_x), row(lam), row(g_a), row(g_b), p_a, p_b)


def _route(logits, tri, carry):
    lane = lax.broadcasted_iota(jnp.int32, logits.shape, 1)
    neg = -jnp.inf
    is_g = lane < N_GROUPS
    gl = jnp.where(is_g, logits, neg)
    ge = jnp.exp(gl - jnp.max(gl, axis=-1, keepdims=True))
    p_g = 1.0 / jnp.sum(ge, axis=-1, keepdims=True)
    g_idx = jnp.min(jnp.where(is_g & (ge >= 1.0), lane, LANES), axis=-1, keepdims=True)
    sel = (lane >= N_GROUPS) & (lane < N_GROUPS + N_EXPERTS) & (((lane - N_GROUPS) >> 3) == g_idx)
    el = jnp.where(sel, logits, neg)
    m1 = jnp.max(el, axis=-1, keepdims=True)
    i1 = jnp.min(jnp.where(sel & (el == m1), lane, LANES), axis=-1, keepdims=True)
    sel2 = sel & (lane != i1)
    el2 = jnp.where(sel2, logits, neg)
    m2 = jnp.max(el2, axis=-1, keepdims=True)
    i2 = jnp.min(jnp.where(sel2 & (el2 == m2), lane, LANES), axis=-1, keepdims=True)
    e2 = jnp.exp(m2 - m1)
    w1 = p_g * (1.0 / (1.0 + e2))
    w2 = p_g * (e2 / (1.0 + e2))
    out = jnp.where(lane == 0, (i1 - N_GROUPS).astype(F32), 0.0)
    out = jnp.where(lane == 1, (i2 - N_GROUPS).astype(F32), out)
    out = jnp.where(lane == 2, w1, out)
    out = jnp.where(lane == 3, w2, out)
    oh1 = lane == i1
    oh2 = lane == i2
    both = jnp.where(oh1 | oh2, 1.0, 0.0)
    before = _dot(tri, both.astype(BF16)) + carry
    out = jnp.where(lane == 4, jnp.sum(jnp.where(oh1, before, 0.0), axis=-1, keepdims=True), out)
    out = jnp.where(lane == 5, jnp.sum(jnp.where(oh2, before, 0.0), axis=-1, keepdims=True), out)
    return out, carry + jnp.sum(both, axis=0, keepdims=True)


def _pack_bf16_pair(lo, hi):
    lo_bits = lax.bitcast_convert_type(lo.astype(BF16).astype(F32), jnp.uint32)
    hi_bits = lax.bitcast_convert_type(hi.astype(BF16).astype(F32), jnp.uint32)
    return (lo_bits >> 16) | hi_bits


def _unpack_bf16_pair(w):
    lo = lax.bitcast_convert_type(w << 16, F32).astype(BF16)
    hi = lax.bitcast_convert_type(w & jnp.uint32(0xFFFF0000), F32).astype(BF16)
    return lo, hi


def _out_kernel(y_ref, x_ref, g1_ref, w_ref, sc_ref, sh_ref, g_ref, wr_ref, br_ref, tri_ref,
                x1_ref, h2p_ref, rt_ref, rtt_ref, cnt_ref, carry):
    @pl.when((pl.program_id(0) == 0) & (pl.program_id(1) == 0))
    def _():
        carry[...] = jnp.zeros_like(carry)

    sub = tri_ref.shape[0]
    count = carry[...]
    tiles = [slice(r0, r0 + sub) for r0 in range(0, y_ref.shape[0], sub)]
    mixes = [_dot(y_ref[rows, :], w_ref[...]) for rows in tiles]
    for rows, mix in zip(tiles, mixes):
        x1 = x_ref[rows, :] + g1_ref[...] * mix
        x1_ref[rows, :] = x1
        h2 = _modulated_norm(x1, g_ref[...], sc_ref[...], sh_ref[...])
        half = h2.shape[1] // 2
        words = _pack_bf16_pair(h2[:, :half], h2[:, half:])
        for c in range(half // LANES):
            h2p_ref[pl.ds(rows.start * SUBLANES + c, sub, stride=SUBLANES), :] = words[:, c * LANES:(c + 1) * LANES]
        logits = _dot(h2.astype(BF16), wr_ref[...]) + br_ref[...]
        rt, count = _route(logits, tri_ref[...], count)
        rt_ref[rows, :] = rt
        rtt_ref[:, rows] = rt.T[:SUBLANES, :]
    carry[...] = count
    cnt_ref[...] = count


def _out(y, x, g1, w_bf, sc2, sh2, g, wr_bf, br):
    bsz, s, d = x.shape
    tm, sub = 512, 256
    per_batch = pl.BlockSpec((None, 1, d), lambda b, t: (b, 0, 0))
    tile = lambda w: pl.BlockSpec((None, tm, w), lambda b, t: (b, t, 0))
    const = lambda shape, **kw: pl.BlockSpec(shape, lambda b, t: (0, 0), **kw)
    return pl.pallas_call(
        _out_kernel,
        grid=(bsz, s // tm),
        in_specs=[
            tile(D_MIX), tile(d), per_batch,
            const((D_MIX, d), pipeline_mode=pl.Buffered(1)),
            per_batch, per_batch, const((1, d)),
            const((d, ROUTE_W)), const((1, ROUTE_W)), const((sub, sub)),
        ],
        out_specs=[tile(d), pl.BlockSpec((None, tm * SUBLANES, LANES), lambda b, t: (b, t, 0)),
                   tile(ROUTE_W), pl.BlockSpec((SUBLANES, tm), lambda b, t: (0, b * (s // tm) + t)),
                   const((1, ROUTE_W))],
        out_shape=[
            jax.ShapeDtypeStruct((bsz, s, d), F32),
            jax.ShapeDtypeStruct((bsz, s * SUBLANES, LANES), jnp.uint32),
            jax.ShapeDtypeStruct((bsz, s, ROUTE_W), F32),
            jax.ShapeDtypeStruct((SUBLANES, bsz * s), F32),
            jax.ShapeDtypeStruct((1, ROUTE_W), F32),
        ],
        scratch_shapes=[pltpu.VMEM((1, ROUTE_W), F32)],
        compiler_params=pltpu.CompilerParams(
            dimension_semantics=("arbitrary", "arbitrary"), vmem_limit_bytes=VMEM_LIMIT),
        name="out_route",
    )(y, x, g1, w_bf, sc2, sh2, g, wr_bf, br, jnp.tril(jnp.ones((sub, sub), BF16), -1))


def _plan(rtt, counts_f, n_tok):
    counts = counts_f[0, N_GROUPS:N_GROUPS + N_EXPERTS].astype(jnp.int32)
    padded = ((counts + MOE_BLOCK - 1) // MOE_BLOCK) * MOE_BLOCK
    padded_ends = jnp.cumsum(padded)
    padded_starts = padded_ends - padded
    n_rows = 2 * n_tok + N_EXPERTS * MOE_BLOCK
    n_blocks = n_rows // MOE_BLOCK
    expert_ids = jnp.arange(N_EXPERTS, dtype=jnp.int32)
    idx = rtt.astype(jnp.int32)
    start_of = jnp.sum(jnp.where(idx[0:2, :, None] == expert_ids, padded_starts, 0), axis=-1)
    pos = start_of + idx[4:6]
    block_start = jnp.arange(n_blocks, dtype=jnp.int32)[:, None] * MOE_BLOCK
    owns = (padded_starts <= block_start) & (block_start < padded_ends)
    n_valid = jnp.sum(jnp.where(owns, jnp.clip(counts - (block_start - padded_starts), 0, MOE_BLOCK), 0), axis=1)
    last_used = jnp.max(jnp.where(counts > 0, expert_ids, 0))
    block_expert = jnp.where(n_valid > 0, jnp.sum(jnp.where(owns, expert_ids, 0), axis=1), last_used)
    later_used = (expert_ids[None, :] > expert_ids[:, None]) & (counts[None, :] > 0)
    next_used = jnp.min(jnp.where(later_used, expert_ids[None, :], N_EXPERTS), axis=1)
    next_used = jnp.where(next_used == N_EXPERTS, -1, next_used)
    block_next = jnp.sum(jnp.where(owns, next_used, 0), axis=1)
    used_ordinal = jnp.cumsum((counts > 0).astype(jnp.int32)) - 1
    block_slot = jnp.sum(jnp.where(owns, used_ordinal, 0), axis=1) & 1
    n_used = padded_ends[-1] // MOE_BLOCK
    block_index = jnp.minimum(jnp.arange(n_blocks, dtype=jnp.int32), n_used - 1)
    i32 = lambda a: a.astype(jnp.int32)
    return (pos[0], pos[1], i32(block_expert), i32(n_valid), i32(block_index), i32(block_next), i32(block_slot),
            n_rows)


def _dispatch_kernel(nvalid_ref, p0_ref, p1_ref, h_ref, xs_hbm, zbuf, sem, zsem, *, tm, n_blocks):
    @pl.when(pl.program_id(0) == 0)
    def _():
        zbuf[...] = jnp.zeros_like(zbuf)

        def zero_block(i):
            return pltpu.make_async_copy(zbuf, xs_hbm.at[pl.ds(i * MOE_BLOCK, MOE_BLOCK)], zsem)

        def start(i, carry):
            pl.when(nvalid_ref[i] < MOE_BLOCK)(lambda: zero_block(i).start())
            return carry

        def wait(i, carry):
            pl.when(nvalid_ref[i] < MOE_BLOCK)(lambda: zero_block(i).wait())
            return carry

        lax.fori_loop(0, n_blocks, start, 0)
        lax.fori_loop(0, n_blocks, wait, 0)

    def issue(c, carry):
        for j in range(DMA_ISSUE_UNROLL):
            r = c * DMA_ISSUE_UNROLL + j
            src = h_ref.at[r]
            pltpu.make_async_copy(src, xs_hbm.at[p0_ref[0, r]], sem).start(priority=0)
            pltpu.make_async_copy(src, xs_hbm.at[p1_ref[0, r]], sem).start(priority=1)
        return carry

    lax.fori_loop(0, tm // DMA_ISSUE_UNROLL, issue, 0)
    for _ in range(2):
        pltpu.make_async_copy(h_ref, xs_hbm.at[pl.ds(0, tm)], sem).wait()


def _dispatch(n_valid, pos0, pos1, h2p, n_rows):
    n = h2p.shape[0]
    tm = 2048
    n_blocks = n_valid.shape[0]
    idx_spec = pl.BlockSpec((None, 1, tm), lambda t, nv: (t, 0, 0), memory_space=pltpu.SMEM)
    grid_spec = pltpu.PrefetchScalarGridSpec(
        num_scalar_prefetch=1,
        grid=(n // tm,),
        in_specs=[idx_spec, idx_spec, pl.BlockSpec((tm, SUBLANES, LANES), lambda t, nv: (t, 0, 0))],
        out_specs=pl.BlockSpec(memory_space=pl.ANY),
        scratch_shapes=[
            pltpu.VMEM((MOE_BLOCK, SUBLANES, LANES), h2p.dtype),
            pltpu.SemaphoreType.DMA(()),
            pltpu.SemaphoreType.DMA(()),
        ],
    )
    return pl.pallas_call(
        functools.partial(_dispatch_kernel, tm=tm, n_blocks=n_blocks),
        grid_spec=grid_spec,
        out_shape=jax.ShapeDtypeStruct((n_rows, SUBLANES, LANES), h2p.dtype),
        compiler_params=pltpu.CompilerParams(
            dimension_semantics=("arbitrary",), vmem_limit_bytes=VMEM_LIMIT,
            disable_bounds_checks=True),
        name="dispatch",
    )(n_valid, pos0.reshape(n // tm, 1, tm), pos1.reshape(n // tm, 1, tm), h2p)


def _moe_kernel(be_ref, nvalid_ref, bi_ref, nxt_ref, slot_ref, xs_hbm, wg_hbm, wu_hbm, wd_hbm, y_ref,
                wg_f, wu_f, wd_f, xbuf, sem, xsem, *, n_blocks):
    i = pl.program_id(0)
    n_valid = nvalid_ref[i]
    block_rows = MOE_BLOCK * SUBLANES

    def x_copy(j):
        start = pl.multiple_of(bi_ref[j] * block_rows, block_rows)
        return pltpu.make_async_copy(xs_hbm.at[pl.ds(start, block_rows)], xbuf.at[j % X_RING], xsem.at[j % X_RING])

    @pl.when(i == 0)
    def _():
        for j in range(X_RING - 1):
            x_copy(j).start()

    @pl.when(i + X_RING - 1 < n_blocks)
    def _():
        x_copy(i + X_RING - 1).start()

    x_copy(i).wait()
    x_ref = xbuf.at[i % X_RING]

    def fetch(e, s):
        return (pltpu.make_async_copy(wg_hbm.at[e], wg_f.at[s], sem.at[s, 0]),
                pltpu.make_async_copy(wu_hbm.at[e], wu_f.at[s], sem.at[s, 1]),
                pltpu.make_async_copy(wd_hbm.at[e], wd_f.at[s], sem.at[s, 2]))

    @pl.when(n_valid > 0)
    def _():
        e = be_ref[i]
        s = slot_ref[i]

        @pl.when(i == 0)
        def _():
            for cp in fetch(e, s):
                cp.start(priority=1)

        @pl.when((i == 0) | (e != be_ref[jnp.maximum(i - 1, 0)]))
        def _():
            for cp in fetch(e, s):
                cp.wait()
            nxt = nxt_ref[i]

            @pl.when(nxt >= 0)
            def _():
                for cp in fetch(nxt, 1 - s):
                    cp.start(priority=1)

        words = jnp.concatenate(
            [x_ref[pl.ds(c, MOE_BLOCK, stride=SUBLANES), :] for c in range(SUBLANES)], axis=1)
        lo = lax.bitcast_convert_type(words << 16, F32)
        hi = lax.bitcast_convert_type(words & jnp.uint32(0xFFFF0000), F32)
        half = words.shape[1]
        g = _dot(lo, wg_f[s, :half, :]) + _dot(hi, wg_f[s, half:, :])
        u = _dot(lo, wu_f[s, :half, :]) + _dot(hi, wu_f[s, half:, :])
        act = g * jax.nn.sigmoid(g) * u
        y = _dot(act, wd_f[s])
        y_words = _pack_bf16_pair(y[:, :half], y[:, half:])
        for c in range(SUBLANES):
            y_ref[pl.ds(c, MOE_BLOCK, stride=SUBLANES), :] = y_words[:, c * LANES:(c + 1) * LANES]

    @pl.when(n_valid == 0)
    def _():
        y_ref[...] = jnp.zeros_like(y_ref)


def _moe(block_expert, n_valid, block_index, block_next, block_slot, xs, w_gate, w_up, w_down):
    n_blocks = block_expert.shape[0]
    n_rows = xs.shape[0]
    d = 2 * SUBLANES * LANES
    hbm = pl.BlockSpec(memory_space=pl.ANY)
    grid_spec = pltpu.PrefetchScalarGridSpec(
        num_scalar_prefetch=5,
        grid=(n_blocks,),
        in_specs=[hbm, hbm, hbm, hbm],
        out_specs=pl.BlockSpec((MOE_BLOCK * SUBLANES, LANES), lambda i, be, nv, bi, nx, sl: (i, 0)),
        scratch_shapes=[
            pltpu.VMEM((2, d, D_EXPERT), F32),
            pltpu.VMEM((2, d, D_EXPERT), F32),
            pltpu.VMEM((2, D_EXPERT, d), F32),
            pltpu.VMEM((X_RING, MOE_BLOCK * SUBLANES, LANES), jnp.uint32),
            pltpu.SemaphoreType.DMA((2, 3)),
            pltpu.SemaphoreType.DMA((X_RING,)),
        ],
    )
    return pl.pallas_call(
        functools.partial(_moe_kernel, n_blocks=n_blocks),
        grid_spec=grid_spec,
        out_shape=jax.ShapeDtypeStruct((n_rows * SUBLANES, LANES), jnp.uint32),
        compiler_params=pltpu.CompilerParams(
            dimension_semantics=("arbitrary",), vmem_limit_bytes=VMEM_LIMIT),
        name="moe",
    )(block_expert, n_valid, block_index, block_next, block_slot,
      xs.reshape(n_rows * SUBLANES, LANES), w_gate, w_up, w_down)


def _final_kernel(p0_ref, p1_ref, q0_ref, q1_ref, x1_ref, rt_ref, g2_ref, gf_ref, yb_hbm, yb_flat_hbm, o_ref,
                  y0buf, y1buf, sem, *, tm, n_tiles):
    t = pl.program_id(0)
    slot = t & 1

    def issue(pa_ref, pb_ref, s):
        def body(c, carry):
            for j in range(DMA_ISSUE_UNROLL):
                r = c * DMA_ISSUE_UNROLL + j
                dst = pl.ds(pl.multiple_of(r * SUBLANES, SUBLANES), SUBLANES)
                pltpu.make_async_copy(yb_hbm.at[pa_ref[0, r]], y0buf.at[s].at[dst], sem.at[s]).start(priority=0)
                pltpu.make_async_copy(yb_hbm.at[pb_ref[0, r]], y1buf.at[s].at[dst], sem.at[s]).start(priority=1)
            return carry

        lax.fori_loop(0, tm // DMA_ISSUE_UNROLL, body, 0)

    @pl.when(t == 0)
    def _():
        issue(p0_ref, p1_ref, 0)

    @pl.when(t + 1 < n_tiles)
    def _():
        issue(q0_ref, q1_ref, 1 - slot)

    pltpu.make_async_copy(yb_flat_hbm.at[pl.ds(0, tm * SUBLANES)], y0buf.at[slot], sem.at[slot]).wait()
    pltpu.make_async_copy(yb_flat_hbm.at[pl.ds(0, tm * SUBLANES)], y1buf.at[slot], sem.at[slot]).wait()

    def rows_of(buf):
        tiles = buf.at[slot]
        words = jnp.concatenate([tiles[pl.ds(c, tm, stride=SUBLANES), :] for c in range(SUBLANES)], axis=1)
        lo = lax.bitcast_convert_type(words << 16, F32)
        hi = lax.bitcast_convert_type(words & jnp.uint32(0xFFFF0000), F32)
        return jnp.concatenate([lo, hi], axis=1)

    rt = rt_ref[...]
    y = rt[:, 2:3] * rows_of(y0buf) + rt[:, 3:4] * rows_of(y1buf)
    x2 = x1_ref[...] + g2_ref[...] * y
    ms = jnp.mean(x2 * x2, axis=-1, keepdims=True)
    o_ref[...] = x2 * lax.rsqrt(ms + EPS) * gf_ref[...]


def _final(pos0, pos1, x1, rt, g2, gf, yb, seq):
    n, d = x1.shape
    tm = 256
    n_tiles = n // tm
    per_seq = seq // tm
    this_tile = pl.BlockSpec((None, 1, tm), lambda t: (t, 0, 0), memory_space=pltpu.SMEM)
    next_tile = pl.BlockSpec((None, 1, tm), lambda t: (jnp.minimum(t + 1, n_tiles - 1), 0, 0),
                             memory_space=pltpu.SMEM)
    p0 = pos0.reshape(n_tiles, 1, tm)
    p1 = pos1.reshape(n_tiles, 1, tm)
    return pl.pallas_call(
        functools.partial(_final_kernel, tm=tm, n_tiles=n_tiles),
        grid=(n_tiles,),
        in_specs=[
            this_tile, this_tile, next_tile, next_tile,
            pl.BlockSpec((tm, d), lambda t: (t, 0)),
            pl.BlockSpec((tm, ROUTE_W), lambda t: (t, 0)),
            pl.BlockSpec((None, 1, d), lambda t: (t // per_seq, 0, 0)),
            pl.BlockSpec((1, d), lambda t: (0, 0)),
            pl.BlockSpec(memory_space=pl.ANY),
            pl.BlockSpec(memory_space=pl.ANY),
        ],
        out_specs=pl.BlockSpec((tm, d), lambda t: (t, 0)),
        out_shape=jax.ShapeDtypeStruct((n, d), F32),
        scratch_shapes=[
            pltpu.VMEM((2, tm * SUBLANES, LANES), jnp.uint32),
            pltpu.VMEM((2, tm * SUBLANES, LANES), jnp.uint32),
            pltpu.SemaphoreType.DMA((2,)),
        ],
        compiler_params=pltpu.CompilerParams(
            dimension_semantics=("arbitrary",), vmem_limit_bytes=VMEM_LIMIT,
            disable_bounds_checks=True),
        name="final",
    )(p0, p1, p0, p1, x1, rt, g2, gf, yb.reshape(-1, SUBLANES, LANES), yb)


def kernel(x, c, ada_w, ada_b, norm1_g, w_in, conv3_w, conv4_w, conv4_b, lru_w_a, lru_b_a, lru_w_x, lru_b_x, lru_lambda, head_norm_conv_g, head_norm_lru_g, w_out, norm2_g, route_w_group, route_b_group, route_w_expert, route_b_expert, w_e_gate, w_e_up, w_e_down, final_norm_g):
    bsz, s, d = x.shape
    n_tok = bsz * s
    l = 0

    mod = _ada(c, ada_w[l], ada_b[l]).reshape(bsz, 6, 1, d)
    sh1, sc1, g1, sh2, sc2, g2 = (mod[:, j] for j in range(6))

    proj = _in_proj(x, sc1, sh1, norm1_g[l].reshape(1, d), w_in[l].astype(BF16))

    wax = jnp.concatenate([lru_w_a[l], lru_w_x[l]], axis=-1).astype(BF16)
    y = _mix(proj, conv3_w[l], conv4_w[l], conv4_b[l], wax, lru_b_a[l], lru_b_x[l], lru_lambda[l],
             head_norm_conv_g[l], head_norm_lru_g[l])

    w_route = jnp.concatenate(
        [route_w_group[l], jnp.transpose(route_w_expert[l], (1, 0, 2)).reshape(d, N_EXPERTS),
         jnp.zeros((d, ROUTE_W - N_GROUPS - N_EXPERTS), F32)], axis=1).astype(BF16)
    b_route = jnp.concatenate(
        [route_b_group[l], route_b_expert[l].reshape(-1),
         jnp.zeros((ROUTE_W - N_GROUPS - N_EXPERTS,), F32)]).reshape(1, ROUTE_W)
    x1, h2p, rt, rtt, counts = _out(y, x, g1, w_out[l].astype(BF16), sc2, sh2, norm2_g[l].reshape(1, d),
                                    w_route, b_route)

    rt = rt.reshape(n_tok, ROUTE_W)
    pos0, pos1, block_expert, n_valid, block_index, block_next, block_slot, n_rows = _plan(rtt, counts, n_tok)
    assert d // 2 == SUBLANES * LANES, "a token's packed words must fill exactly one (8, 128) tile"
    xs = _dispatch(n_valid, pos0, pos1, h2p.reshape(n_tok, SUBLANES, LANES), n_rows)
    yb = _moe(block_expert, n_valid, block_index, block_next, block_slot, xs,
              w_e_gate[l], w_e_up[l], w_e_down[l])
    out = _final(pos0, pos1, x1.reshape(n_tok, d), rt, g2, final_norm_g.reshape(1, d), yb, s)
    return out.reshape(bsz, s, d)
```

```python
import functools

import jax
import jax.numpy as jnp
from jax import lax
from jax.experimental import pallas as pl
from jax.experimental.pallas import tpu as pltpu

D_CONV = 1024
CONV_HEAD_DIM = 64
D_LRU = 1024
LRU_HEADS = 8
LRU_HEAD_DIM = 128
LRU_C = 8.0
D_MIX = D_CONV + D_LRU
D_IN = 3 * D_CONV + 2 * D_LRU
N_GROUPS = 8
EXPERTS_PER_GROUP = 8
N_EXPERTS = 64
D_EXPERT = 512
EPS = 1e-6

LANES = 128
SUBLANES = 8
ROUTE_W = LANES
MOE_BLOCK = 256
X_RING = 3
DMA_ISSUE_UNROLL = 8
VMEM_LIMIT = 56 * 1024 * 1024

F32 = jnp.float32
BF16 = jnp.bfloat16


def _dot(a, b):
    return jnp.dot(a, b, preferred_element_type=F32)


def _ada_kernel(c_ref, w_ref, b_ref, o_ref):
    c = c_ref[...]
    s = (c * jax.nn.sigmoid(c)).astype(BF16)
    o_ref[...] = _dot(s, w_ref[...].astype(BF16)) + b_ref[...]


def _ada(c, w, b):
    bsz, d = c.shape
    n = w.shape[1]
    tn = 1024
    return pl.pallas_call(
        _ada_kernel,
        grid=(n // tn,),
        in_specs=[
            pl.BlockSpec((bsz, d), lambda j: (0, 0)),
            pl.BlockSpec((d, tn), lambda j: (0, j)),
            pl.BlockSpec((1, tn), lambda j: (0, j)),
        ],
        out_specs=pl.BlockSpec((bsz, tn), lambda j: (0, j)),
        out_shape=jax.ShapeDtypeStruct((bsz, n), F32),
        compiler_params=pltpu.CompilerParams(
            dimension_semantics=("arbitrary",), vmem_limit_bytes=VMEM_LIMIT),
        name="ada",
    )(c, w, b.reshape(1, n))


def _modulated_norm(x, g, sc, sh):
    ms = jnp.mean(x * x, axis=-1, keepdims=True)
    return x * lax.rsqrt(ms + EPS) * (g * (1.0 + sc)) + sh


def _in_proj_kernel(x_ref, sc_ref, sh_ref, g_ref, w_ref, o_ref, *, n_chunk):
    hb = _modulated_norm(x_ref[...], g_ref[...], sc_ref[...], sh_ref[...]).astype(BF16)
    for j in range(D_IN // n_chunk):
        cols = slice(j * n_chunk, (j + 1) * n_chunk)
        o_ref[:, cols] = _dot(hb, w_ref[:, cols])


def _in_proj(x, sc, sh, g, w_bf):
    bsz, s, d = x.shape
    tm = 512
    per_batch = pl.BlockSpec((None, 1, d), lambda b, t: (b, 0, 0))
    return pl.pallas_call(
        functools.partial(_in_proj_kernel, n_chunk=1024),
        grid=(bsz, s // tm),
        in_specs=[
            pl.BlockSpec((None, tm, d), lambda b, t: (b, t, 0)),
            per_batch, per_batch,
            pl.BlockSpec((1, d), lambda b, t: (0, 0)),
            pl.BlockSpec((d, D_IN), lambda b, t: (0, 0), pipeline_mode=pl.Buffered(1)),
        ],
        out_specs=pl.BlockSpec((None, tm, D_IN), lambda b, t: (b, t, 0)),
        out_shape=jax.ShapeDtypeStruct((bsz, s, D_IN), F32),
        compiler_params=pltpu.CompilerParams(
            dimension_semantics=("arbitrary", "arbitrary"), vmem_limit_bytes=VMEM_LIMIT),
        name="in_proj",
    )(x, sc, sh, g, w_bf)


def _shift_rows(x, prev8, d, row):
    rolled = pltpu.roll(x, d, 0)
    head = jnp.where(row[:SUBLANES] >= d, rolled[:SUBLANES], pltpu.roll(prev8, d, 0))
    return jnp.concatenate([head, rolled[SUBLANES:]], axis=0)


def _head_rms(y, p_ref, g):
    sq = y * y
    hi = sq.astype(BF16)
    lo = (sq - hi.astype(F32)).astype(BF16)
    ms = _dot(hi, p_ref[...]) + _dot(lo, p_ref[...])
    return y * lax.rsqrt(ms + EPS) * g


def _linear_scan(a, v, h0, sub):
    groups = []
    for r0 in range(0, a.shape[0], SUBLANES):
        ag, vg = a[r0:r0 + SUBLANES], v[r0:r0 + SUBLANES]
        d = 1
        while d < SUBLANES:
            keep = sub >= d
            vg = vg + ag * jnp.where(keep, pltpu.roll(vg, d, 0), 0.0)
            ag = ag * jnp.where(keep, pltpu.roll(ag, d, 0), 1.0)
            d *= 2
        groups.append((ag, vg))
    out, carry = [], h0
    for ag, vg in groups:
        hg = vg + ag * carry
        carry = hg[SUBLANES - 1:, :]
        out.append(hg)
    return jnp.concatenate(out, axis=0), carry


def _gelu_tanh(x):
    c = 0.7978845608028654
    half_x = 0.5 * x
    return half_x + half_x * jnp.tanh(x * (c + (c * 0.044715) * (x * x)))


def _mix_kernel(p_ref, w3_ref, w4_ref, b4_ref, wax_ref, ba_ref, bx_ref, lam_ref, ga_ref, gb_ref,
                pa_ref, pb_ref, y_ref, ua_carry, xb_carry, h_carry, *, ts):
    @pl.when(pl.program_id(1) == 0)
    def _():
        ua_carry[...] = jnp.zeros_like(ua_carry)
        xb_carry[...] = jnp.zeros_like(xb_carry)
        h_carry[...] = jnp.zeros_like(h_carry)

    row = lax.broadcasted_iota(jnp.int32, (ts, LANES), 0)

    for k in range(D_CONV // LANES):
        cols = slice(k * LANES, (k + 1) * LANES)
        b_a = p_ref[:, k * LANES:(k + 1) * LANES]
        c_a = p_ref[:, D_CONV + k * LANES:D_CONV + (k + 1) * LANES]
        x_a = p_ref[:, 2 * D_CONV + k * LANES:2 * D_CONV + (k + 1) * LANES]
        u = c_a * x_a
        prev = ua_carry[:, cols]
        conv = (w3_ref[2:3, cols] * u
                + w3_ref[1:2, cols] * _shift_rows(u, prev, 1, row)
                + w3_ref[0:1, cols] * _shift_rows(u, prev, 2, row))
        ua_carry[:, cols] = u[ts - SUBLANES:, :]
        y_a = b_a * conv
        y_ref[:, cols] = _head_rms(y_a, pa_ref, ga_ref[:, cols]).astype(BF16)

        g_b = p_ref[:, 3 * D_CONV + k * LANES:3 * D_CONV + (k + 1) * LANES]
        x_b = p_ref[:, 3 * D_CONV + D_LRU + k * LANES:3 * D_CONV + D_LRU + (k + 1) * LANES]
        prevb = xb_carry[:, cols]
        xc = (w4_ref[3:4, cols] * x_b
              + w4_ref[2:3, cols] * _shift_rows(x_b, prevb, 1, row)
              + w4_ref[1:2, cols] * _shift_rows(x_b, prevb, 2, row)
              + w4_ref[0:1, cols] * _shift_rows(x_b, prevb, 3, row)
              + b4_ref[:, cols])
        xb_carry[:, cols] = x_b[ts - SUBLANES:, :]
        gates = _dot(xc.astype(BF16), wax_ref[k])
        r = jax.nn.sigmoid(gates[:, :LANES] + ba_ref[:, cols])
        i = jax.nn.sigmoid(gates[:, LANES:] + bx_ref[:, cols])
        nlam = -lam_ref[:, cols]
        softplus = jnp.maximum(nlam, 0.0) + jnp.log1p(jnp.exp(-jnp.abs(nlam)))
        log_a = r * ((-LRU_C) * softplus)
        a = jnp.exp(log_a)
        mult = jnp.sqrt(-jnp.tanh(log_a) * (a * a + 1.0))
        v = mult * i * xc
        hs, h_last = _linear_scan(a, v, h_carry[:, cols], row[:SUBLANES])
        h_carry[:, cols] = h_last
        y_b = hs * _gelu_tanh(g_b)
        y_ref[:, D_CONV + k * LANES:D_CONV + (k + 1) * LANES] = _head_rms(
            y_b, pb_ref, gb_ref[:, cols]).astype(BF16)


def _mix(proj, w3, w4, b4, wax_bf, b_a, b_x, lam, g_a, g_b):
    bsz, s, _ = proj.shape
    ts = 512
    lane = jnp.arange(LANES)
    p_a = ((lane[:, None] // CONV_HEAD_DIM) == (lane[None, :] // CONV_HEAD_DIM)).astype(BF16) / CONV_HEAD_DIM
    p_b = jnp.full((LANES, LANES), 1.0 / LRU_HEAD_DIM, BF16)
    full = lambda shape: pl.BlockSpec(shape, lambda b, t: (0,) * len(shape))
    row = lambda a: a.reshape(1, -1)
    return pl.pallas_call(
        functools.partial(_mix_kernel, ts=ts),
        grid=(bsz, s // ts),
        in_specs=[
            pl.BlockSpec((None, ts, D_IN), lambda b, t: (b, t, 0)),
            full((3, D_CONV)), full((4, D_LRU)), full((1, D_LRU)),
            full((LRU_HEADS, LRU_HEAD_DIM, 2 * LRU_HEAD_DIM)),
            full((1, D_LRU)), full((1, D_LRU)), full((1, D_LRU)),
            full((1, D_CONV)), full((1, D_LRU)),
            full((LANES, LANES)), full((LANES, LANES)),
        ],
        out_specs=pl.BlockSpec((None, ts, D_MIX), lambda b, t: (b, t, 0)),
        out_shape=jax.ShapeDtypeStruct((bsz, s, D_MIX), BF16),
        scratch_shapes=[
            pltpu.VMEM((SUBLANES, D_CONV), F32),
            pltpu.VMEM((SUBLANES, D_LRU), F32),
            pltpu.VMEM((1, D_LRU), F32),
        ],
        compiler_params=pltpu.CompilerParams(
            dimension_semantics=("arbitrary", "arbitrary"), vmem_limit_bytes=VMEM_LIMIT),
        name="mix",
    )(proj, w3, w4, row(b4), wax_bf, row(b_a), row(b_x), row(lam), row(g_a), row(g_b), p_a, p_b)


def _route(logits, tri, carry):
    lane = lax.broadcasted_iota(jnp.int32, logits.shape, 1)
    neg = -jnp.inf
    is_g = lane < N_GROUPS
    gl = jnp.where(is_g, logits, neg)
    ge = jnp.exp(gl - jnp.max(gl, axis=-1, keepdims=True))
    p_g = 1.0 / jnp.sum(ge, axis=-1, keepdims=True)
    g_idx = jnp.min(jnp.where(is_g & (ge >= 1.0), lane, LANES), axis=-1, keepdims=True)
    sel = (lane >= N_GROUPS) & (lane < N_GROUPS + N_EXPERTS) & (((lane - N_GROUPS) // EXPERTS_PER_GROUP) == g_idx)
    el = jnp.where(sel, logits, neg)
    m1 = jnp.max(el, axis=-1, keepdims=True)
    i1 = jnp.min(jnp.where(sel & (el == m1), lane, LANES), axis=-1, keepdims=True)
    sel2 = sel & (lane != i1)
    el2 = jnp.where(sel2, logits, neg)
    m2 = jnp.max(el2, axis=-1, keepdims=True)
    i2 = jnp.min(jnp.where(sel2 & (el2 == m2), lane, LANES), axis=-1, keepdims=True)
    e2 = jnp.exp(m2 - m1)
    w1 = p_g * (1.0 / (1.0 + e2))
    w2 = p_g * (e2 / (1.0 + e2))
    out = jnp.where(lane == 0, (i1 - N_GROUPS).astype(F32), 0.0)
    out = jnp.where(lane == 1, (i2 - N_GROUPS).astype(F32), out)
    out = jnp.where(lane == 2, w1, out)
    out = jnp.where(lane == 3, w2, out)
    oh1 = lane == i1
    oh2 = lane == i2
    both = jnp.where(oh1 | oh2, 1.0, 0.0)
    before = _dot(tri, both.astype(BF16)) + carry
    out = jnp.where(lane == 4, jnp.sum(jnp.where(oh1, before, 0.0), axis=-1, keepdims=True), out)
    out = jnp.where(lane == 5, jnp.sum(jnp.where(oh2, before, 0.0), axis=-1, keepdims=True), out)
    return out, carry + jnp.sum(both, axis=0, keepdims=True)


def _pack_bf16_pair(lo, hi):
    lo_bits = lax.bitcast_convert_type(lo.astype(BF16).astype(F32), jnp.uint32)
    hi_bits = lax.bitcast_convert_type(hi.astype(BF16).astype(F32), jnp.uint32)
    return (lo_bits >> 16) | hi_bits


def _unpack_bf16_pair(w):
    lo = lax.bitcast_convert_type(w << 16, F32)
    hi = lax.bitcast_convert_type(w & jnp.uint32(0xFFFF0000), F32)
    return lo, hi


def _out_kernel(y_ref, x_ref, g1_ref, w_ref, sc_ref, sh_ref, g_ref, wr_ref, br_ref, tri_ref,
                x1_ref, h2p_ref, rt_ref, rtt_ref, cnt_ref, carry):
    @pl.when((pl.program_id(0) == 0) & (pl.program_id(1) == 0))
    def _():
        carry[...] = jnp.zeros_like(carry)

    sub = tri_ref.shape[0]
    count = carry[...]
    tiles = [slice(r0, r0 + sub) for r0 in range(0, y_ref.shape[0], sub)]
    mixes = [_dot(y_ref[rows, :], w_ref[...]) for rows in tiles]
    for rows, mix in zip(tiles, mixes):
        x1 = x_ref[rows, :] + g1_ref[...] * mix
        x1_ref[rows, :] = x1
        h2 = _modulated_norm(x1, g_ref[...], sc_ref[...], sh_ref[...])
        half = h2.shape[1] // 2
        words = _pack_bf16_pair(h2[:, :half], h2[:, half:])
        for c in range(half // LANES):
            h2p_ref[pl.ds(rows.start * SUBLANES + c, sub, stride=SUBLANES), :] = words[:, c * LANES:(c + 1) * LANES]
        logits = _dot(h2.astype(BF16), wr_ref[...]) + br_ref[...]
        rt, count = _route(logits, tri_ref[...], count)
        rt_ref[rows, :] = rt
        rtt_ref[:, rows] = rt.T[:SUBLANES, :]
    carry[...] = count
    cnt_ref[...] = count


def _out(y, x, g1, w_bf, sc2, sh2, g, wr_bf, br):
    bsz, s, d = x.shape
    tm, sub = 512, 256
    per_batch = pl.BlockSpec((None, 1, d), lambda b, t: (b, 0, 0))
    tile = lambda w: pl.BlockSpec((None, tm, w), lambda b, t: (b, t, 0))
    const = lambda shape, **kw: pl.BlockSpec(shape, lambda b, t: (0, 0), **kw)
    return pl.pallas_call(
        _out_kernel,
        grid=(bsz, s // tm),
        in_specs=[
            tile(D_MIX), tile(d), per_batch,
            const((D_MIX, d), pipeline_mode=pl.Buffered(1)),
            per_batch, per_batch, const((1, d)),
            const((d, ROUTE_W)), const((1, ROUTE_W)), const((sub, sub)),
        ],
        out_specs=[tile(d), pl.BlockSpec((None, tm * SUBLANES, LANES), lambda b, t: (b, t, 0)),
                   tile(ROUTE_W), pl.BlockSpec((SUBLANES, tm), lambda b, t: (0, b * (s // tm) + t)),
                   const((1, ROUTE_W))],
        out_shape=[
            jax.ShapeDtypeStruct((bsz, s, d), F32),
            jax.ShapeDtypeStruct((bsz, s * SUBLANES, LANES), jnp.uint32),
            jax.ShapeDtypeStruct((bsz, s, ROUTE_W), F32),
            jax.ShapeDtypeStruct((SUBLANES, bsz * s), F32),
            jax.ShapeDtypeStruct((1, ROUTE_W), F32),
        ],
        scratch_shapes=[pltpu.VMEM((1, ROUTE_W), F32)],
        compiler_params=pltpu.CompilerParams(
            dimension_semantics=("arbitrary", "arbitrary"), vmem_limit_bytes=VMEM_LIMIT),
        name="out_route",
    )(y, x, g1, w_bf, sc2, sh2, g, wr_bf, br, jnp.tril(jnp.ones((sub, sub), BF16), -1))


def _plan(rtt, counts_f, n_tok):
    counts = counts_f[0, N_GROUPS:N_GROUPS + N_EXPERTS].astype(jnp.int32)
    padded = ((counts + MOE_BLOCK - 1) // MOE_BLOCK) * MOE_BLOCK
    padded_ends = jnp.cumsum(padded)
    padded_starts = padded_ends - padded
    n_rows = 2 * n_tok + N_EXPERTS * MOE_BLOCK
    n_blocks = n_rows // MOE_BLOCK
    expert_ids = jnp.arange(N_EXPERTS, dtype=jnp.int32)
    idx = rtt.astype(jnp.int32)
    start_of = jnp.sum(jnp.where(idx[0:2, :, None] == expert_ids, padded_starts, 0), axis=-1)
    pos = start_of + idx[4:6]
    block_start = jnp.arange(n_blocks, dtype=jnp.int32)[:, None] * MOE_BLOCK
    owns = (padded_starts <= block_start) & (block_start < padded_ends)
    n_valid = jnp.sum(jnp.where(owns, jnp.clip(counts - (block_start - padded_starts), 0, MOE_BLOCK), 0), axis=1)
    last_used = jnp.max(jnp.where(counts > 0, expert_ids, 0))
    block_expert = jnp.where(n_valid > 0, jnp.sum(jnp.where(owns, expert_ids, 0), axis=1), last_used)
    later_used = (expert_ids[None, :] > expert_ids[:, None]) & (counts[None, :] > 0)
    next_used = jnp.min(jnp.where(later_used, expert_ids[None, :], N_EXPERTS), axis=1)
    next_used = jnp.where(next_used == N_EXPERTS, -1, next_used)
    block_next = jnp.sum(jnp.where(owns, next_used, 0), axis=1)
    used_ordinal = jnp.cumsum((counts > 0).astype(jnp.int32)) - 1
    block_slot = jnp.sum(jnp.where(owns, used_ordinal, 0), axis=1) & 1
    n_used = padded_ends[-1] // MOE_BLOCK
    block_index = jnp.minimum(jnp.arange(n_blocks, dtype=jnp.int32), n_used - 1)
    i32 = lambda a: a.astype(jnp.int32)
    return (pos[0], pos[1], i32(block_expert), i32(n_valid), i32(block_index), i32(block_next), i32(block_slot),
            n_rows)


def _dispatch_kernel(nvalid_ref, p0_ref, p1_ref, h_ref, xs_hbm, zbuf, sem, zsem, *, tm, n_blocks):
    @pl.when(pl.program_id(0) == 0)
    def _():
        zbuf[...] = jnp.zeros_like(zbuf)

        def zero_block(i):
            return pltpu.make_async_copy(zbuf, xs_hbm.at[pl.ds(i * MOE_BLOCK, MOE_BLOCK)], zsem)

        def start(i, carry):
            pl.when(nvalid_ref[i] < MOE_BLOCK)(lambda: zero_block(i).start())
            return carry

        def wait(i, carry):
            pl.when(nvalid_ref[i] < MOE_BLOCK)(lambda: zero_block(i).wait())
            return carry

        lax.fori_loop(0, n_blocks, start, 0)
        lax.fori_loop(0, n_blocks, wait, 0)

    def issue(c, carry):
        for j in range(DMA_ISSUE_UNROLL):
            r = c * DMA_ISSUE_UNROLL + j
            src = h_ref.at[r]
            pltpu.make_async_copy(src, xs_hbm.at[p0_ref[0, r]], sem).start(priority=0)
            pltpu.make_async_copy(src, xs_hbm.at[p1_ref[0, r]], sem).start(priority=1)
        return carry

    lax.fori_loop(0, tm // DMA_ISSUE_UNROLL, issue, 0)
    for _ in range(2):
        pltpu.make_async_copy(h_ref, xs_hbm.at[pl.ds(0, tm)], sem).wait()


def _dispatch(n_valid, pos0, pos1, h2p, n_rows):
    n = h2p.shape[0]
    tm = 2048
    n_blocks = n_valid.shape[0]
    idx_spec = pl.BlockSpec((None, 1, tm), lambda t, nv: (t, 0, 0), memory_space=pltpu.SMEM)
    grid_spec = pltpu.PrefetchScalarGridSpec(
        num_scalar_prefetch=1,
        grid=(n // tm,),
        in_specs=[idx_spec, idx_spec, pl.BlockSpec((tm, SUBLANES, LANES), lambda t, nv: (t, 0, 0))],
        out_specs=pl.BlockSpec(memory_space=pl.ANY),
        scratch_shapes=[
            pltpu.VMEM((MOE_BLOCK, SUBLANES, LANES), h2p.dtype),
            pltpu.SemaphoreType.DMA(()),
            pltpu.SemaphoreType.DMA(()),
        ],
    )
    return pl.pallas_call(
        functools.partial(_dispatch_kernel, tm=tm, n_blocks=n_blocks),
        grid_spec=grid_spec,
        out_shape=jax.ShapeDtypeStruct((n_rows, SUBLANES, LANES), h2p.dtype),
        compiler_params=pltpu.CompilerParams(
            dimension_semantics=("arbitrary",), vmem_limit_bytes=VMEM_LIMIT,
            disable_bounds_checks=True),
        name="dispatch",
    )(n_valid, pos0.reshape(n // tm, 1, tm), pos1.reshape(n // tm, 1, tm), h2p)


def _moe_kernel(be_ref, nvalid_ref, bi_ref, nxt_ref, slot_ref, xs_hbm, wg_hbm, wu_hbm, wd_hbm, y_ref,
                wg_f, wu_f, wd_f, xbuf, sem, xsem, *, n_blocks):
    i = pl.program_id(0)
    n_valid = nvalid_ref[i]
    block_rows = MOE_BLOCK * SUBLANES

    def x_copy(j):
        start = pl.multiple_of(bi_ref[j] * block_rows, block_rows)
        return pltpu.make_async_copy(xs_hbm.at[pl.ds(start, block_rows)], xbuf.at[j % X_RING], xsem.at[j % X_RING])

    @pl.when(i == 0)
    def _():
        for j in range(X_RING - 1):
            x_copy(j).start()

    @pl.when(i + X_RING - 1 < n_blocks)
    def _():
        x_copy(i + X_RING - 1).start()

    x_copy(i).wait()
    x_ref = xbuf.at[i % X_RING]

    def fetch(e, s):
        return (pltpu.make_async_copy(wg_hbm.at[e], wg_f.at[s], sem.at[s, 0]),
                pltpu.make_async_copy(wu_hbm.at[e], wu_f.at[s], sem.at[s, 1]),
                pltpu.make_async_copy(wd_hbm.at[e], wd_f.at[s], sem.at[s, 2]))

    @pl.when(n_valid > 0)
    def _():
        e = be_ref[i]
        s = slot_ref[i]

        @pl.when(i == 0)
        def _():
            for cp in fetch(e, s):
                cp.start(priority=1)

        @pl.when((i == 0) | (e != be_ref[jnp.maximum(i - 1, 0)]))
        def _():
            for cp in fetch(e, s):
                cp.wait()
            nxt = nxt_ref[i]

            @pl.when(nxt >= 0)
            def _():
                for cp in fetch(nxt, 1 - s):
                    cp.start(priority=1)

        words = jnp.concatenate(
            [x_ref[pl.ds(c, MOE_BLOCK, stride=SUBLANES), :] for c in range(SUBLANES)], axis=1)
        lo, hi = _unpack_bf16_pair(words)
        half = words.shape[1]
        g = _dot(lo, wg_f[s, :half, :]) + _dot(hi, wg_f[s, half:, :])
        u = _dot(lo, wu_f[s, :half, :]) + _dot(hi, wu_f[s, half:, :])
        act = g * jax.nn.sigmoid(g) * u
        y = _dot(act, wd_f[s])
        y_words = _pack_bf16_pair(y[:, :half], y[:, half:])
        for c in range(SUBLANES):
            y_ref[pl.ds(c, MOE_BLOCK, stride=SUBLANES), :] = y_words[:, c * LANES:(c + 1) * LANES]

    @pl.when(n_valid == 0)
    def _():
        y_ref[...] = jnp.zeros_like(y_ref)


def _moe(block_expert, n_valid, block_index, block_next, block_slot, xs, w_gate, w_up, w_down):
    n_blocks = block_expert.shape[0]
    n_rows = xs.shape[0]
    d = 2 * SUBLANES * LANES
    hbm = pl.BlockSpec(memory_space=pl.ANY)
    grid_spec = pltpu.PrefetchScalarGridSpec(
        num_scalar_prefetch=5,
        grid=(n_blocks,),
        in_specs=[hbm, hbm, hbm, hbm],
        out_specs=pl.BlockSpec((MOE_BLOCK * SUBLANES, LANES), lambda i, be, nv, bi, nx, sl: (i, 0)),
        scratch_shapes=[
            pltpu.VMEM((2, d, D_EXPERT), F32),
            pltpu.VMEM((2, d, D_EXPERT), F32),
            pltpu.VMEM((2, D_EXPERT, d), F32),
            pltpu.VMEM((X_RING, MOE_BLOCK * SUBLANES, LANES), jnp.uint32),
            pltpu.SemaphoreType.DMA((2, 3)),
            pltpu.SemaphoreType.DMA((X_RING,)),
        ],
    )
    return pl.pallas_call(
        functools.partial(_moe_kernel, n_blocks=n_blocks),
        grid_spec=grid_spec,
        out_shape=jax.ShapeDtypeStruct((n_rows * SUBLANES, LANES), jnp.uint32),
        compiler_params=pltpu.CompilerParams(
            dimension_semantics=("arbitrary",), vmem_limit_bytes=VMEM_LIMIT),
        name="moe",
    )(block_expert, n_valid, block_index, block_next, block_slot,
      xs.reshape(n_rows * SUBLANES, LANES), w_gate, w_up, w_down)


def _final_kernel(p0_ref, p1_ref, q0_ref, q1_ref, x1_ref, rt_ref, g2_ref, gf_ref, yb_hbm, yb_flat_hbm, o_ref,
                  y0buf, y1buf, sem, *, tm, n_tiles):
    t = pl.program_id(0)
    slot = t & 1

    def issue(pa_ref, pb_ref, s):
        def body(c, carry):
            for j in range(DMA_ISSUE_UNROLL):
                r = c * DMA_ISSUE_UNROLL + j
                dst = pl.ds(pl.multiple_of(r * SUBLANES, SUBLANES), SUBLANES)
                pltpu.make_async_copy(yb_hbm.at[pa_ref[0, r]], y0buf.at[s].at[dst], sem.at[s]).start(priority=0)
                pltpu.make_async_copy(yb_hbm.at[pb_ref[0, r]], y1buf.at[s].at[dst], sem.at[s]).start(priority=1)
            return carry

        lax.fori_loop(0, tm // DMA_ISSUE_UNROLL, body, 0)

    @pl.when(t == 0)
    def _():
        issue(p0_ref, p1_ref, 0)

    @pl.when(t + 1 < n_tiles)
    def _():
        issue(q0_ref, q1_ref, 1 - slot)

    pltpu.make_async_copy(yb_flat_hbm.at[pl.ds(0, tm * SUBLANES)], y0buf.at[slot], sem.at[slot]).wait()
    pltpu.make_async_copy(yb_flat_hbm.at[pl.ds(0, tm * SUBLANES)], y1buf.at[slot], sem.at[slot]).wait()

    def rows_of(buf):
        tiles = buf.at[slot]
        words = jnp.concatenate([tiles[pl.ds(c, tm, stride=SUBLANES), :] for c in range(SUBLANES)], axis=1)
        return jnp.concatenate(_unpack_bf16_pair(words), axis=1)

    rt = rt_ref[...]
    y = rt[:, 2:3] * rows_of(y0buf) + rt[:, 3:4] * rows_of(y1buf)
    x2 = x1_ref[...] + g2_ref[...] * y
    ms = jnp.mean(x2 * x2, axis=-1, keepdims=True)
    o_ref[...] = x2 * lax.rsqrt(ms + EPS) * gf_ref[...]


def _final(pos0, pos1, x1, rt, g2, gf, yb, seq):
    n, d = x1.shape
    tm = 256
    n_tiles = n // tm
    per_seq = seq // tm
    this_tile = pl.BlockSpec((None, 1, tm), lambda t: (t, 0, 0), memory_space=pltpu.SMEM)
    next_tile = pl.BlockSpec((None, 1, tm), lambda t: (jnp.minimum(t + 1, n_tiles - 1), 0, 0),
                             memory_space=pltpu.SMEM)
    p0 = pos0.reshape(n_tiles, 1, tm)
    p1 = pos1.reshape(n_tiles, 1, tm)
    return pl.pallas_call(
        functools.partial(_final_kernel, tm=tm, n_tiles=n_tiles),
        grid=(n_tiles,),
        in_specs=[
            this_tile, this_tile, next_tile, next_tile,
            pl.BlockSpec((tm, d), lambda t: (t, 0)),
            pl.BlockSpec((tm, ROUTE_W), lambda t: (t, 0)),
            pl.BlockSpec((None, 1, d), lambda t: (t // per_seq, 0, 0)),
            pl.BlockSpec((1, d), lambda t: (0, 0)),
            pl.BlockSpec(memory_space=pl.ANY),
            pl.BlockSpec(memory_space=pl.ANY),
        ],
        out_specs=pl.BlockSpec((tm, d), lambda t: (t, 0)),
        out_shape=jax.ShapeDtypeStruct((n, d), F32),
        scratch_shapes=[
            pltpu.VMEM((2, tm * SUBLANES, LANES), jnp.uint32),
            pltpu.VMEM((2, tm * SUBLANES, LANES), jnp.uint32),
            pltpu.SemaphoreType.DMA((2,)),
        ],
        compiler_params=pltpu.CompilerParams(
            dimension_semantics=("arbitrary",), vmem_limit_bytes=VMEM_LIMIT,
            disable_bounds_checks=True),
        name="final",
    )(p0, p1, p0, p1, x1, rt, g2, gf, yb.reshape(-1, SUBLANES, LANES), yb)


def kernel(x, c, ada_w, ada_b, norm1_g, w_in, conv3_w, conv4_w, conv4_b, lru_w_a, lru_b_a, lru_w_x, lru_b_x, lru_lambda, head_norm_conv_g, head_norm_lru_g, w_out, norm2_g, route_w_group, route_b_group, route_w_expert, route_b_expert, w_e_gate, w_e_up, w_e_down, final_norm_g):
    bsz, s, d = x.shape
    n_tok = bsz * s
    l = 0

    mod = _ada(c, ada_w[l], ada_b[l]).reshape(bsz, 6, 1, d)
    sh1, sc1, g1, sh2, sc2, g2 = (mod[:, j] for j in range(6))

    proj = _in_proj(x, sc1, sh1, norm1_g[l].reshape(1, d), w_in[l].astype(BF16))

    wax = jnp.concatenate([lru_w_a[l], lru_w_x[l]], axis=-1).astype(BF16)
    y = _mix(proj, conv3_w[l], conv4_w[l], conv4_b[l], wax, lru_b_a[l], lru_b_x[l], lru_lambda[l],
             head_norm_conv_g[l], head_norm_lru_g[l])

    w_route = jnp.concatenate(
        [route_w_group[l], jnp.transpose(route_w_expert[l], (1, 0, 2)).reshape(d, N_EXPERTS),
         jnp.zeros((d, ROUTE_W - N_GROUPS - N_EXPERTS), F32)], axis=1).astype(BF16)
    b_route = jnp.concatenate(
        [route_b_group[l], route_b_expert[l].reshape(-1),
         jnp.zeros((ROUTE_W - N_GROUPS - N_EXPERTS,), F32)]).reshape(1, ROUTE_W)
    x1, h2p, rt, rtt, counts = _out(y, x, g1, w_out[l].astype(BF16), sc2, sh2, norm2_g[l].reshape(1, d),
                                    w_route, b_route)

    rt = rt.reshape(n_tok, ROUTE_W)
    pos0, pos1, block_expert, n_valid, block_index, block_next, block_slot, n_rows = _plan(rtt, counts, n_tok)
    assert d // 2 == SUBLANES * LANES, "a token's packed words must fill exactly one (8, 128) tile"
    xs = _dispatch(n_valid, pos0, pos1, h2p.reshape(n_tok, SUBLANES, LANES), n_rows)
    yb = _moe(block_expert, n_valid, block_index, block_next, block_slot, xs,
              w_e_gate[l], w_e_up[l], w_e_down[l])
    out = _final(pos0, pos1, x1.reshape(n_tok, d), rt, g2, final_norm_g.reshape(1, d), yb, s)
    return out.reshape(bsz, s, d)
```

```python
import functools

import jax
import jax.numpy as jnp
from jax import lax
from jax.experimental import pallas as pl
from jax.experimental.pallas import tpu as pltpu

D_CONV = 1024
CONV_HEAD_DIM = 64
D_LRU = 1024
LRU_HEADS = 8
LRU_HEAD_DIM = 128
LRU_C = 8.0
D_MIX = D_CONV + D_LRU
D_IN = 3 * D_CONV + 2 * D_LRU
N_GROUPS = 8
EXPERTS_PER_GROUP = 8
N_EXPERTS = 64
D_EXPERT = 512
EPS = 1e-6

LANES = 128
SUBLANES = 8
ROUTE_W = LANES
MOE_BLOCK = 256
X_RING = 3
DMA_ISSUE_UNROLL = 8
VMEM_LIMIT = 56 * 1024 * 1024

F32 = jnp.float32
BF16 = jnp.bfloat16


def _dot(a, b):
    return jnp.dot(a, b, preferred_element_type=F32)


def _ada_block(c_ref, w_ref, b_ref):
    c = c_ref[...]
    s = (c * jax.nn.sigmoid(c)).astype(BF16)
    return _dot(s, w_ref[...].astype(BF16)) + b_ref[...]


def _ada_kernel(c_ref, w_ref, b_ref, o_ref):
    o_ref[...] = _ada_block(c_ref, w_ref, b_ref)


def _ada(c, w, b, n_cols):
    bsz, d = c.shape
    tn = 1024
    return pl.pallas_call(
        _ada_kernel,
        grid=(n_cols // tn,),
        in_specs=[
            pl.BlockSpec((bsz, d), lambda j: (0, 0)),
            pl.BlockSpec((d, tn), lambda j: (0, j)),
            pl.BlockSpec((1, tn), lambda j: (0, j)),
        ],
        out_specs=pl.BlockSpec((bsz, tn), lambda j: (0, j)),
        out_shape=jax.ShapeDtypeStruct((bsz, n_cols), F32),
        compiler_params=pltpu.CompilerParams(
            dimension_semantics=("arbitrary",), vmem_limit_bytes=VMEM_LIMIT),
        name="ada",
    )(c, w, b)


def _modulated_norm(x, g, sc, sh):
    ms = jnp.mean(x * x, axis=-1, keepdims=True)
    return x * lax.rsqrt(ms + EPS) * (g * (1.0 + sc)) + sh


def _in_proj_kernel(x_ref, sc_ref, sh_ref, g_ref, w_ref, c_ref, aw_ref, ab_ref, o_ref, mod_ref, *, n_chunk):
    mod_ref[...] = _ada_block(c_ref, aw_ref, ab_ref)
    hb = _modulated_norm(x_ref[...], g_ref[...], sc_ref[...], sh_ref[...]).astype(BF16)
    for j in range(D_IN // n_chunk):
        cols = slice(j * n_chunk, (j + 1) * n_chunk)
        o_ref[:, cols] = _dot(hb, w_ref[:, cols])


def _in_proj(x, sc, sh, g, w_bf, c, ada_w, ada_b, ada_done):
    bsz, s, d = x.shape
    tm = 512
    steps_per_seq = s // tm
    n_steps = bsz * steps_per_seq
    ada_rest = ada_w.shape[1] - ada_done
    tn = ada_rest // n_steps
    assert tn * n_steps == ada_rest and tn % LANES == 0 and ada_done % tn == 0
    per_batch = pl.BlockSpec((None, 1, d), lambda b, t: (b, 0, 0))
    ada_block = lambda b, t: (0, ada_done // tn + b * steps_per_seq + t)
    return pl.pallas_call(
        functools.partial(_in_proj_kernel, n_chunk=1024),
        grid=(bsz, steps_per_seq),
        in_specs=[
            pl.BlockSpec((None, tm, d), lambda b, t: (b, t, 0)),
            per_batch, per_batch,
            pl.BlockSpec((1, d), lambda b, t: (0, 0)),
            pl.BlockSpec((d, D_IN), lambda b, t: (0, 0), pipeline_mode=pl.Buffered(1)),
            pl.BlockSpec((bsz, d), lambda b, t: (0, 0)),
            pl.BlockSpec((d, tn), ada_block),
            pl.BlockSpec((1, tn), ada_block),
        ],
        out_specs=[pl.BlockSpec((None, tm, D_IN), lambda b, t: (b, t, 0)),
                   pl.BlockSpec((bsz, tn), lambda b, t: (0, b * steps_per_seq + t))],
        out_shape=[jax.ShapeDtypeStruct((bsz, s, D_IN), F32),
                   jax.ShapeDtypeStruct((bsz, ada_rest), F32)],
        compiler_params=pltpu.CompilerParams(
            dimension_semantics=("arbitrary", "arbitrary"), vmem_limit_bytes=VMEM_LIMIT),
        name="in_proj",
    )(x, sc, sh, g, w_bf, c, ada_w, ada_b)


def _shift_rows(x, prev8, d, row):
    rolled = pltpu.roll(x, d, 0)
    head = jnp.where(row[:SUBLANES] >= d, rolled[:SUBLANES], pltpu.roll(prev8, d, 0))
    return jnp.concatenate([head, rolled[SUBLANES:]], axis=0)


def _head_rms(y, p_ref, g):
    sq = y * y
    hi = sq.astype(BF16)
    lo = (sq - hi.astype(F32)).astype(BF16)
    ms = _dot(hi, p_ref[...]) + _dot(lo, p_ref[...])
    return y * lax.rsqrt(ms + EPS) * g


def _linear_scan(a, v, h0, sub):
    groups = []
    for r0 in range(0, a.shape[0], SUBLANES):
        ag, vg = a[r0:r0 + SUBLANES], v[r0:r0 + SUBLANES]
        d = 1
        while d < SUBLANES:
            keep = sub >= d
            vg = vg + ag * jnp.where(keep, pltpu.roll(vg, d, 0), 0.0)
            ag = ag * jnp.where(keep, pltpu.roll(ag, d, 0), 1.0)
            d *= 2
        groups.append((ag, vg))
    out, carry = [], h0
    for ag, vg in groups:
        hg = vg + ag * carry
        carry = hg[SUBLANES - 1:, :]
        out.append(hg)
    return jnp.concatenate(out, axis=0), carry


def _gelu_tanh(x):
    c = 0.7978845608028654
    half_x = 0.5 * x
    return half_x + half_x * jnp.tanh(x * (c + (c * 0.044715) * (x * x)))


def _mix_kernel(p_ref, w3_ref, w4_ref, b4_ref, wax_ref, ba_ref, bx_ref, lam_ref, ga_ref, gb_ref,
                pa_ref, pb_ref, y_ref, ua_carry, xb_carry, h_carry, *, ts):
    @pl.when(pl.program_id(1) == 0)
    def _():
        ua_carry[...] = jnp.zeros_like(ua_carry)
        xb_carry[...] = jnp.zeros_like(xb_carry)
        h_carry[...] = jnp.zeros_like(h_carry)

    row = lax.broadcasted_iota(jnp.int32, (ts, LANES), 0)

    for k in range(D_CONV // LANES):
        cols = slice(k * LANES, (k + 1) * LANES)
        b_a = p_ref[:, k * LANES:(k + 1) * LANES]
        c_a = p_ref[:, D_CONV + k * LANES:D_CONV + (k + 1) * LANES]
        x_a = p_ref[:, 2 * D_CONV + k * LANES:2 * D_CONV + (k + 1) * LANES]
        u = c_a * x_a
        prev = ua_carry[:, cols]
        conv = (w3_ref[2:3, cols] * u
                + w3_ref[1:2, cols] * _shift_rows(u, prev, 1, row)
                + w3_ref[0:1, cols] * _shift_rows(u, prev, 2, row))
        ua_carry[:, cols] = u[ts - SUBLANES:, :]
        y_a = b_a * conv
        y_ref[:, cols] = _head_rms(y_a, pa_ref, ga_ref[:, cols]).astype(BF16)

        g_b = p_ref[:, 3 * D_CONV + k * LANES:3 * D_CONV + (k + 1) * LANES]
        x_b = p_ref[:, 3 * D_CONV + D_LRU + k * LANES:3 * D_CONV + D_LRU + (k + 1) * LANES]
        prevb = xb_carry[:, cols]
        xc = (w4_ref[3:4, cols] * x_b
              + w4_ref[2:3, cols] * _shift_rows(x_b, prevb, 1, row)
              + w4_ref[1:2, cols] * _shift_rows(x_b, prevb, 2, row)
              + w4_ref[0:1, cols] * _shift_rows(x_b, prevb, 3, row)
              + b4_ref[:, cols])
        xb_carry[:, cols] = x_b[ts - SUBLANES:, :]
        gates = _dot(xc.astype(BF16), wax_ref[k])
        r = jax.nn.sigmoid(gates[:, :LANES] + ba_ref[:, cols])
        i = jax.nn.sigmoid(gates[:, LANES:] + bx_ref[:, cols])
        nlam = -lam_ref[:, cols]
        softplus = jnp.maximum(nlam, 0.0) + jnp.log1p(jnp.exp(-jnp.abs(nlam)))
        log_a = r * ((-LRU_C) * softplus)
        a = jnp.exp(log_a)
        mult = jnp.sqrt(-jnp.tanh(log_a) * (a * a + 1.0))
        v = mult * i * xc
        hs, h_last = _linear_scan(a, v, h_carry[:, cols], row[:SUBLANES])
        h_carry[:, cols] = h_last
        y_b = hs * _gelu_tanh(g_b)
        y_ref[:, D_CONV + k * LANES:D_CONV + (k + 1) * LANES] = _head_rms(
            y_b, pb_ref, gb_ref[:, cols]).astype(BF16)


def _mix(proj, w3, w4, b4, wax_bf, b_a, b_x, lam, g_a, g_b):
    bsz, s, _ = proj.shape
    ts = 512
    lane = jnp.arange(LANES)
    p_a = ((lane[:, None] // CONV_HEAD_DIM) == (lane[None, :] // CONV_HEAD_DIM)).astype(BF16) / CONV_HEAD_DIM
    p_b = jnp.full((LANES, LANES), 1.0 / LRU_HEAD_DIM, BF16)
    full = lambda shape: pl.BlockSpec(shape, lambda b, t: (0,) * len(shape))
    row = lambda a: a.reshape(1, -1)
    return pl.pallas_call(
        functools.partial(_mix_kernel, ts=ts),
        grid=(bsz, s // ts),
        in_specs=[
            pl.BlockSpec((None, ts, D_IN), lambda b, t: (b, t, 0)),
            full((3, D_CONV)), full((4, D_LRU)), full((1, D_LRU)),
            full((LRU_HEADS, LRU_HEAD_DIM, 2 * LRU_HEAD_DIM)),
            full((1, D_LRU)), full((1, D_LRU)), full((1, D_LRU)),
            full((1, D_CONV)), full((1, D_LRU)),
            full((LANES, LANES)), full((LANES, LANES)),
        ],
        out_specs=pl.BlockSpec((None, ts, D_MIX), lambda b, t: (b, t, 0)),
        out_shape=jax.ShapeDtypeStruct((bsz, s, D_MIX), BF16),
        scratch_shapes=[
            pltpu.VMEM((SUBLANES, D_CONV), F32),
            pltpu.VMEM((SUBLANES, D_LRU), F32),
            pltpu.VMEM((1, D_LRU), F32),
        ],
        compiler_params=pltpu.CompilerParams(
            dimension_semantics=("arbitrary", "arbitrary"), vmem_limit_bytes=VMEM_LIMIT),
        name="mix",
    )(proj, w3, w4, row(b4), wax_bf, row(b_a), row(b_x), row(lam), row(g_a), row(g_b), p_a, p_b)


def _route(logits, tri, carry):
    lane = lax.broadcasted_iota(jnp.int32, logits.shape, 1)
    neg = -jnp.inf
    is_g = lane < N_GROUPS
    gl = jnp.where(is_g, logits, neg)
    ge = jnp.exp(gl - jnp.max(gl, axis=-1, keepdims=True))
    p_g = 1.0 / jnp.sum(ge, axis=-1, keepdims=True)
    g_idx = jnp.min(jnp.where(is_g & (ge >= 1.0), lane, LANES), axis=-1, keepdims=True)
    sel = (lane >= N_GROUPS) & (lane < N_GROUPS + N_EXPERTS) & (((lane - N_GROUPS) // EXPERTS_PER_GROUP) == g_idx)
    el = jnp.where(sel, logits, neg)
    m1 = jnp.max(el, axis=-1, keepdims=True)
    i1 = jnp.min(jnp.where(sel & (el == m1), lane, LANES), axis=-1, keepdims=True)
    sel2 = sel & (lane != i1)
    el2 = jnp.where(sel2, logits, neg)
    m2 = jnp.max(el2, axis=-1, keepdims=True)
    i2 = jnp.min(jnp.where(sel2 & (el2 == m2), lane, LANES), axis=-1, keepdims=True)
    e2 = jnp.exp(m2 - m1)
    w1 = p_g * (1.0 / (1.0 + e2))
    w2 = p_g * (e2 / (1.0 + e2))
    out = jnp.where(lane == 0, (i1 - N_GROUPS).astype(F32), 0.0)
    out = jnp.where(lane == 1, (i2 - N_GROUPS).astype(F32), out)
    out = jnp.where(lane == 2, w1, out)
    out = jnp.where(lane == 3, w2, out)
    oh1 = lane == i1
    oh2 = lane == i2
    both = jnp.where(oh1 | oh2, 1.0, 0.0)
    before = _dot(tri, both.astype(BF16)) + carry
    out = jnp.where(lane == 4, jnp.sum(jnp.where(oh1, before, 0.0), axis=-1, keepdims=True), out)
    out = jnp.where(lane == 5, jnp.sum(jnp.where(oh2, before, 0.0), axis=-1, keepdims=True), out)
    return out, carry + jnp.sum(both, axis=0, keepdims=True)


def _pack_bf16_pair(lo, hi):
    lo_bits = lax.bitcast_convert_type(lo.astype(BF16).astype(F32), jnp.uint32)
    hi_bits = lax.bitcast_convert_type(hi.astype(BF16).astype(F32), jnp.uint32)
    return (lo_bits >> 16) | hi_bits


def _unpack_bf16_pair(w):
    lo = lax.bitcast_convert_type(w << 16, F32)
    hi = lax.bitcast_convert_type(w & jnp.uint32(0xFFFF0000), F32)
    return lo, hi


def _out_kernel(y_ref, x_ref, g1_ref, w_ref, sc_ref, sh_ref, g_ref, wr_ref, br_ref, tri_ref,
                x1_ref, h2p_ref, rt_ref, rtt_ref, cnt_ref, carry):
    @pl.when((pl.program_id(0) == 0) & (pl.program_id(1) == 0))
    def _():
        carry[...] = jnp.zeros_like(carry)

    sub = tri_ref.shape[0]
    count = carry[...]
    tiles = [slice(r0, r0 + sub) for r0 in range(0, y_ref.shape[0], sub)]
    mixes = [_dot(y_ref[rows, :], w_ref[...]) for rows in tiles]
    for rows, mix in zip(tiles, mixes):
        x1 = x_ref[rows, :] + g1_ref[...] * mix
        x1_ref[rows, :] = x1
        h2 = _modulated_norm(x1, g_ref[...], sc_ref[...], sh_ref[...])
        half = h2.shape[1] // 2
        words = _pack_bf16_pair(h2[:, :half], h2[:, half:])
        for c in range(half // LANES):
            h2p_ref[pl.ds(rows.start * SUBLANES + c, sub, stride=SUBLANES), :] = words[:, c * LANES:(c + 1) * LANES]
        logits = _dot(h2.astype(BF16), wr_ref[...]) + br_ref[...]
        rt, count = _route(logits, tri_ref[...], count)
        rt_ref[rows, :] = rt
        rtt_ref[:, rows] = rt.T[:SUBLANES, :]
    carry[...] = count
    cnt_ref[...] = count


def _out(y, x, g1, w_bf, sc2, sh2, g, wr_bf, br):
    bsz, s, d = x.shape
    tm, sub = 512, 256
    per_batch = pl.BlockSpec((None, 1, d), lambda b, t: (b, 0, 0))
    tile = lambda w: pl.BlockSpec((None, tm, w), lambda b, t: (b, t, 0))
    const = lambda shape, **kw: pl.BlockSpec(shape, lambda b, t: (0, 0), **kw)
    return pl.pallas_call(
        _out_kernel,
        grid=(bsz, s // tm),
        in_specs=[
            tile(D_MIX), tile(d), per_batch,
            const((D_MIX, d), pipeline_mode=pl.Buffered(1)),
            per_batch, per_batch, const((1, d)),
            const((d, ROUTE_W)), const((1, ROUTE_W)), const((sub, sub)),
        ],
        out_specs=[tile(d), pl.BlockSpec((None, tm * SUBLANES, LANES), lambda b, t: (b, t, 0)),
                   tile(ROUTE_W), pl.BlockSpec((SUBLANES, tm), lambda b, t: (0, b * (s // tm) + t)),
                   const((1, ROUTE_W))],
        out_shape=[
            jax.ShapeDtypeStruct((bsz, s, d), F32),
            jax.ShapeDtypeStruct((bsz, s * SUBLANES, LANES), jnp.uint32),
            jax.ShapeDtypeStruct((bsz, s, ROUTE_W), F32),
            jax.ShapeDtypeStruct((SUBLANES, bsz * s), F32),
            jax.ShapeDtypeStruct((1, ROUTE_W), F32),
        ],
        scratch_shapes=[pltpu.VMEM((1, ROUTE_W), F32)],
        compiler_params=pltpu.CompilerParams(
            dimension_semantics=("arbitrary", "arbitrary"), vmem_limit_bytes=VMEM_LIMIT),
        name="out_route",
    )(y, x, g1, w_bf, sc2, sh2, g, wr_bf, br, jnp.tril(jnp.ones((sub, sub), BF16), -1))


def _plan(rtt, counts_f, n_tok):
    counts = counts_f[0, N_GROUPS:N_GROUPS + N_EXPERTS].astype(jnp.int32)
    padded = ((counts + MOE_BLOCK - 1) // MOE_BLOCK) * MOE_BLOCK
    padded_ends = jnp.cumsum(padded)
    padded_starts = padded_ends - padded
    n_rows = 2 * n_tok + N_EXPERTS * MOE_BLOCK
    n_blocks = n_rows // MOE_BLOCK
    expert_ids = jnp.arange(N_EXPERTS, dtype=jnp.int32)
    idx = rtt.astype(jnp.int32)
    start_of = jnp.sum(jnp.where(idx[0:2, :, None] == expert_ids, padded_starts, 0), axis=-1)
    pos = start_of + idx[4:6]
    block_start = jnp.arange(n_blocks, dtype=jnp.int32)[:, None] * MOE_BLOCK
    owns = (padded_starts <= block_start) & (block_start < padded_ends)
    n_valid = jnp.sum(jnp.where(owns, jnp.clip(counts - (block_start - padded_starts), 0, MOE_BLOCK), 0), axis=1)
    last_used = jnp.max(jnp.where(counts > 0, expert_ids, 0))
    block_expert = jnp.where(n_valid > 0, jnp.sum(jnp.where(owns, expert_ids, 0), axis=1), last_used)
    later_used = (expert_ids[None, :] > expert_ids[:, None]) & (counts[None, :] > 0)
    next_used = jnp.min(jnp.where(later_used, expert_ids[None, :], N_EXPERTS), axis=1)
    next_used = jnp.where(next_used == N_EXPERTS, -1, next_used)
    block_next = jnp.sum(jnp.where(owns, next_used, 0), axis=1)
    used_ordinal = jnp.cumsum((counts > 0).astype(jnp.int32)) - 1
    block_slot = jnp.sum(jnp.where(owns, used_ordinal, 0), axis=1) & 1
    n_used = padded_ends[-1] // MOE_BLOCK
    block_index = jnp.minimum(jnp.arange(n_blocks, dtype=jnp.int32), n_used - 1)
    i32 = lambda a: a.astype(jnp.int32)
    return (pos[0], pos[1], i32(block_expert), i32(n_valid), i32(block_index), i32(block_next), i32(block_slot),
            n_rows)


def _dispatch_kernel(nvalid_ref, p0_ref, p1_ref, h_ref, xs_hbm, zbuf, sem, zsem, *, tm, n_blocks):
    @pl.when(pl.program_id(0) == 0)
    def _():
        zbuf[...] = jnp.zeros_like(zbuf)

        def zero_block(i):
            return pltpu.make_async_copy(zbuf, xs_hbm.at[pl.ds(i * MOE_BLOCK, MOE_BLOCK)], zsem)

        def start(i, carry):
            pl.when(nvalid_ref[i] < MOE_BLOCK)(lambda: zero_block(i).start())
            return carry

        def wait(i, carry):
            pl.when(nvalid_ref[i] < MOE_BLOCK)(lambda: zero_block(i).wait())
            return carry

        lax.fori_loop(0, n_blocks, start, 0)
        lax.fori_loop(0, n_blocks, wait, 0)

    def issue(c, carry):
        for j in range(DMA_ISSUE_UNROLL):
            r = c * DMA_ISSUE_UNROLL + j
            src = h_ref.at[r]
            pltpu.make_async_copy(src, xs_hbm.at[p0_ref[0, r]], sem).start(priority=0)
            pltpu.make_async_copy(src, xs_hbm.at[p1_ref[0, r]], sem).start(priority=1)
        return carry

    lax.fori_loop(0, tm // DMA_ISSUE_UNROLL, issue, 0)
    for _ in range(2):
        pltpu.make_async_copy(h_ref, xs_hbm.at[pl.ds(0, tm)], sem).wait()


def _dispatch(n_valid, pos0, pos1, h2p, n_rows):
    n = h2p.shape[0]
    tm = 2048
    n_blocks = n_valid.shape[0]
    idx_spec = pl.BlockSpec((None, 1, tm), lambda t, nv: (t, 0, 0), memory_space=pltpu.SMEM)
    grid_spec = pltpu.PrefetchScalarGridSpec(
        num_scalar_prefetch=1,
        grid=(n // tm,),
        in_specs=[idx_spec, idx_spec, pl.BlockSpec((tm, SUBLANES, LANES), lambda t, nv: (t, 0, 0))],
        out_specs=pl.BlockSpec(memory_space=pl.ANY),
        scratch_shapes=[
            pltpu.VMEM((MOE_BLOCK, SUBLANES, LANES), h2p.dtype),
            pltpu.SemaphoreType.DMA(()),
            pltpu.SemaphoreType.DMA(()),
        ],
    )
    return pl.pallas_call(
        functools.partial(_dispatch_kernel, tm=tm, n_blocks=n_blocks),
        grid_spec=grid_spec,
        out_shape=jax.ShapeDtypeStruct((n_rows, SUBLANES, LANES), h2p.dtype),
        compiler_params=pltpu.CompilerParams(
            dimension_semantics=("arbitrary",), vmem_limit_bytes=VMEM_LIMIT,
            disable_bounds_checks=True),
        name="dispatch",
    )(n_valid, pos0.reshape(n // tm, 1, tm), pos1.reshape(n // tm, 1, tm), h2p)


def _moe_kernel(be_ref, nvalid_ref, bi_ref, nxt_ref, slot_ref, xs_hbm, wg_hbm, wu_hbm, wd_hbm, y_ref,
                wg_f, wu_f, wd_f, xbuf, sem, xsem, *, n_blocks):
    i = pl.program_id(0)
    n_valid = nvalid_ref[i]
    block_rows = MOE_BLOCK * SUBLANES

    def x_copy(j):
        start = pl.multiple_of(bi_ref[j] * block_rows, block_rows)
        return pltpu.make_async_copy(xs_hbm.at[pl.ds(start, block_rows)], xbuf.at[j % X_RING], xsem.at[j % X_RING])

    @pl.when(i == 0)
    def _():
        for j in range(X_RING - 1):
            x_copy(j).start()

    @pl.when(i + X_RING - 1 < n_blocks)
    def _():
        x_copy(i + X_RING - 1).start()

    x_copy(i).wait()
    x_ref = xbuf.at[i % X_RING]

    def fetch(e, s):
        return (pltpu.make_async_copy(wg_hbm.at[e], wg_f.at[s], sem.at[s, 0]),
                pltpu.make_async_copy(wu_hbm.at[e], wu_f.at[s], sem.at[s, 1]),
                pltpu.make_async_copy(wd_hbm.at[e], wd_f.at[s], sem.at[s, 2]))

    @pl.when(n_valid > 0)
    def _():
        e = be_ref[i]
        s = slot_ref[i]

        @pl.when(i == 0)
        def _():
            for cp in fetch(e, s):
                cp.start(priority=1)

        @pl.when((i == 0) | (e != be_ref[jnp.maximum(i - 1, 0)]))
        def _():
            for cp in fetch(e, s):
                cp.wait()
            nxt = nxt_ref[i]

            @pl.when(nxt >= 0)
            def _():
                for cp in fetch(nxt, 1 - s):
                    cp.start(priority=1)

        words = jnp.concatenate(
            [x_ref[pl.ds(c, MOE_BLOCK, stride=SUBLANES), :] for c in range(SUBLANES)], axis=1)
        lo, hi = _unpack_bf16_pair(words)
        half = words.shape[1]
        g = _dot(lo, wg_f[s, :half, :]) + _dot(hi, wg_f[s, half:, :])
        u = _dot(lo, wu_f[s, :half, :]) + _dot(hi, wu_f[s, half:, :])
        act = g * jax.nn.sigmoid(g) * u
        y = _dot(act, wd_f[s])
        y_words = _pack_bf16_pair(y[:, :half], y[:, half:])
        for c in range(SUBLANES):
            y_ref[pl.ds(c, MOE_BLOCK, stride=SUBLANES), :] = y_words[:, c * LANES:(c + 1) * LANES]

    @pl.when(n_valid == 0)
    def _():
        y_ref[...] = jnp.zeros_like(y_ref)


def _moe(block_expert, n_valid, block_index, block_next, block_slot, xs, w_gate, w_up, w_down):
    n_blocks = block_expert.shape[0]
    n_rows = xs.shape[0]
    d = 2 * SUBLANES * LANES
    hbm = pl.BlockSpec(memory_space=pl.ANY)
    grid_spec = pltpu.PrefetchScalarGridSpec(
        num_scalar_prefetch=5,
        grid=(n_blocks,),
        in_specs=[hbm, hbm, hbm, hbm],
        out_specs=pl.BlockSpec((MOE_BLOCK * SUBLANES, LANES), lambda i, be, nv, bi, nx, sl: (i, 0)),
        scratch_shapes=[
            pltpu.VMEM((2, d, D_EXPERT), F32),
            pltpu.VMEM((2, d, D_EXPERT), F32),
            pltpu.VMEM((2, D_EXPERT, d), F32),
            pltpu.VMEM((X_RING, MOE_BLOCK * SUBLANES, LANES), jnp.uint32),
            pltpu.SemaphoreType.DMA((2, 3)),
            pltpu.SemaphoreType.DMA((X_RING,)),
        ],
    )
    return pl.pallas_call(
        functools.partial(_moe_kernel, n_blocks=n_blocks),
        grid_spec=grid_spec,
        out_shape=jax.ShapeDtypeStruct((n_rows * SUBLANES, LANES), jnp.uint32),
        compiler_params=pltpu.CompilerParams(
            dimension_semantics=("arbitrary",), vmem_limit_bytes=VMEM_LIMIT),
        name="moe",
    )(block_expert, n_valid, block_index, block_next, block_slot,
      xs.reshape(n_rows * SUBLANES, LANES), w_gate, w_up, w_down)


def _final_kernel(p0_ref, p1_ref, q0_ref, q1_ref, x1_ref, rt_ref, g2_ref, gf_ref, yb_hbm, yb_flat_hbm, o_ref,
                  y0buf, y1buf, sem, *, tm, n_tiles):
    t = pl.program_id(0)
    slot = t & 1

    def issue(pa_ref, pb_ref, s):
        def body(c, carry):
            for j in range(DMA_ISSUE_UNROLL):
                r = c * DMA_ISSUE_UNROLL + j
                dst = pl.ds(pl.multiple_of(r * SUBLANES, SUBLANES), SUBLANES)
                pltpu.make_async_copy(yb_hbm.at[pa_ref[0, r]], y0buf.at[s].at[dst], sem.at[s]).start(priority=0)
                pltpu.make_async_copy(yb_hbm.at[pb_ref[0, r]], y1buf.at[s].at[dst], sem.at[s]).start(priority=1)
            return carry

        lax.fori_loop(0, tm // DMA_ISSUE_UNROLL, body, 0)

    @pl.when(t == 0)
    def _():
        issue(p0_ref, p1_ref, 0)

    @pl.when(t + 1 < n_tiles)
    def _():
        issue(q0_ref, q1_ref, 1 - slot)

    pltpu.make_async_copy(yb_flat_hbm.at[pl.ds(0, tm * SUBLANES)], y0buf.at[slot], sem.at[slot]).wait()
    pltpu.make_async_copy(yb_flat_hbm.at[pl.ds(0, tm * SUBLANES)], y1buf.at[slot], sem.at[slot]).wait()

    def rows_of(buf):
        tiles = buf.at[slot]
        words = jnp.concatenate([tiles[pl.ds(c, tm, stride=SUBLANES), :] for c in range(SUBLANES)], axis=1)
        return jnp.concatenate(_unpack_bf16_pair(words), axis=1)

    rt = rt_ref[...]
    y = rt[:, 2:3] * rows_of(y0buf) + rt[:, 3:4] * rows_of(y1buf)
    x2 = x1_ref[...] + g2_ref[...] * y
    ms = jnp.mean(x2 * x2, axis=-1, keepdims=True)
    o_ref[...] = x2 * lax.rsqrt(ms + EPS) * gf_ref[...]


def _final(pos0, pos1, x1, rt, g2, gf, yb, seq):
    n, d = x1.shape
    tm = 256
    n_tiles = n // tm
    per_seq = seq // tm
    this_tile = pl.BlockSpec((None, 1, tm), lambda t: (t, 0, 0), memory_space=pltpu.SMEM)
    next_tile = pl.BlockSpec((None, 1, tm), lambda t: (jnp.minimum(t + 1, n_tiles - 1), 0, 0),
                             memory_space=pltpu.SMEM)
    p0 = pos0.reshape(n_tiles, 1, tm)
    p1 = pos1.reshape(n_tiles, 1, tm)
    return pl.pallas_call(
        functools.partial(_final_kernel, tm=tm, n_tiles=n_tiles),
        grid=(n_tiles,),
        in_specs=[
            this_tile, this_tile, next_tile, next_tile,
            pl.BlockSpec((tm, d), lambda t: (t, 0)),
            pl.BlockSpec((tm, ROUTE_W), lambda t: (t, 0)),
            pl.BlockSpec((None, 1, d), lambda t: (t // per_seq, 0, 0)),
            pl.BlockSpec((1, d), lambda t: (0, 0)),
            pl.BlockSpec(memory_space=pl.ANY),
            pl.BlockSpec(memory_space=pl.ANY),
        ],
        out_specs=pl.BlockSpec((tm, d), lambda t: (t, 0)),
        out_shape=jax.ShapeDtypeStruct((n, d), F32),
        scratch_shapes=[
            pltpu.VMEM((2, tm * SUBLANES, LANES), jnp.uint32),
            pltpu.VMEM((2, tm * SUBLANES, LANES), jnp.uint32),
            pltpu.SemaphoreType.DMA((2,)),
        ],
        compiler_params=pltpu.CompilerParams(
            dimension_semantics=("arbitrary",), vmem_limit_bytes=VMEM_LIMIT,
            disable_bounds_checks=True),
        name="final",
    )(p0, p1, p0, p1, x1, rt, g2, gf, yb.reshape(-1, SUBLANES, LANES), yb)


def kernel(x, c, ada_w, ada_b, norm1_g, w_in, conv3_w, conv4_w, conv4_b, lru_w_a, lru_b_a, lru_w_x, lru_b_x, lru_lambda, head_norm_conv_g, head_norm_lru_g, w_out, norm2_g, route_w_group, route_b_group, route_w_expert, route_b_expert, w_e_gate, w_e_up, w_e_down, final_norm_g):
    bsz, s, d = x.shape
    n_tok = bsz * s
    l = 0

    ada_bias = ada_b[l].reshape(1, 6 * d)
    sh1, sc1 = (v.reshape(bsz, 1, d) for v in jnp.split(_ada(c, ada_w[l], ada_bias, 2 * d), 2, axis=1))
    proj, mod_rest = _in_proj(x, sc1, sh1, norm1_g[l].reshape(1, d), w_in[l].astype(BF16),
                              c, ada_w[l], ada_bias, 2 * d)
    g1, sh2, sc2, g2 = (v.reshape(bsz, 1, d) for v in jnp.split(mod_rest, 4, axis=1))

    wax = jnp.concatenate([lru_w_a[l], lru_w_x[l]], axis=-1).astype(BF16)
    y = _mix(proj, conv3_w[l], conv4_w[l], conv4_b[l], wax, lru_b_a[l], lru_b_x[l], lru_lambda[l],
             head_norm_conv_g[l], head_norm_lru_g[l])

    w_route = jnp.concatenate(
        [route_w_group[l], jnp.transpose(route_w_expert[l], (1, 0, 2)).reshape(d, N_EXPERTS),
         jnp.zeros((d, ROUTE_W - N_GROUPS - N_EXPERTS), F32)], axis=1).astype(BF16)
    b_route = jnp.concatenate(
        [route_b_group[l], route_b_expert[l].reshape(-1),
         jnp.zeros((ROUTE_W - N_GROUPS - N_EXPERTS,), F32)]).reshape(1, ROUTE_W)
    x1, h2p, rt, rtt, counts = _out(y, x, g1, w_out[l].astype(BF16), sc2, sh2, norm2_g[l].reshape(1, d),
                                    w_route, b_route)

    rt = rt.reshape(n_tok, ROUTE_W)
    pos0, pos1, block_expert, n_valid, block_index, block_next, block_slot, n_rows = _plan(rtt, counts, n_tok)
    assert d // 2 == SUBLANES * LANES, "a token's packed words must fill exactly one (8, 128) tile"
    xs = _dispatch(n_valid, pos0, pos1, h2p.reshape(n_tok, SUBLANES, LANES), n_rows)
    yb = _moe(block_expert, n_valid, block_index, block_next, block_slot, xs,
              w_e_gate[l], w_e_up[l], w_e_down[l])
    out = _final(pos0, pos1, x1.reshape(n_tok, d), rt, g2, final_norm_g.reshape(1, d), yb, s)
    return out.reshape(bsz, s, d)
```

```python
import functools

import jax
import jax.numpy as jnp
from jax import lax
from jax.experimental import pallas as pl
from jax.experimental.pallas import tpu as pltpu

D_CONV = 1024
CONV_HEAD_DIM = 64
D_LRU = 1024
LRU_HEADS = 8
LRU_HEAD_DIM = 128
LRU_C = 8.0
D_MIX = D_CONV + D_LRU
D_IN = 3 * D_CONV + 2 * D_LRU
N_GROUPS = 8
EXPERTS_PER_GROUP = 8
N_EXPERTS = 64
D_EXPERT = 512
EPS = 1e-6

LANES = 128
SUBLANES = 8
ROUTE_W = LANES
MOE_BLOCK = 256
X_RING = 3
DMA_ISSUE_UNROLL = 8
VMEM_LIMIT = 56 * 1024 * 1024

F32 = jnp.float32
BF16 = jnp.bfloat16


def _dot(a, b):
    return jnp.dot(a, b, preferred_element_type=F32)


def _ada_kernel(c_ref, w_ref, b_ref, o_ref):
    c = c_ref[...]
    s = (c * jax.nn.sigmoid(c)).astype(BF16)
    o_ref[...] = _dot(s, w_ref[...].astype(BF16)) + b_ref[...]


def _ada(c, w, b):
    bsz, d = c.shape
    n = w.shape[1]
    tn = 1024
    return pl.pallas_call(
        _ada_kernel,
        grid=(n // tn,),
        in_specs=[
            pl.BlockSpec((bsz, d), lambda j: (0, 0)),
            pl.BlockSpec((d, tn), lambda j: (0, j)),
            pl.BlockSpec((1, tn), lambda j: (0, j)),
        ],
        out_specs=pl.BlockSpec((bsz, tn), lambda j: (0, j)),
        out_shape=jax.ShapeDtypeStruct((bsz, n), F32),
        compiler_params=pltpu.CompilerParams(
            dimension_semantics=("arbitrary",), vmem_limit_bytes=VMEM_LIMIT),
        name="ada",
    )(c, w, b.reshape(1, n))


def _modulated_norm(x, g, sc, sh):
    ms = jnp.mean(x * x, axis=-1, keepdims=True)
    return x * lax.rsqrt(ms + EPS) * (g * (1.0 + sc)) + sh


def _in_proj_kernel(x_ref, sc_ref, sh_ref, g_ref, w_ref, o_ref, *, n_chunk):
    hb = _modulated_norm(x_ref[...], g_ref[...], sc_ref[...], sh_ref[...]).astype(BF16)
    for j in range(D_IN // n_chunk):
        cols = slice(j * n_chunk, (j + 1) * n_chunk)
        o_ref[:, cols] = _dot(hb, w_ref[:, cols])


def _in_proj(x, sc, sh, g, w_bf):
    bsz, s, d = x.shape
    tm = 512
    per_batch = pl.BlockSpec((None, 1, d), lambda b, t: (b, 0, 0))
    return pl.pallas_call(
        functools.partial(_in_proj_kernel, n_chunk=1024),
        grid=(bsz, s // tm),
        in_specs=[
            pl.BlockSpec((None, tm, d), lambda b, t: (b, t, 0)),
            per_batch, per_batch,
            pl.BlockSpec((1, d), lambda b, t: (0, 0)),
            pl.BlockSpec((d, D_IN), lambda b, t: (0, 0), pipeline_mode=pl.Buffered(1)),
        ],
        out_specs=pl.BlockSpec((None, tm, D_IN), lambda b, t: (b, t, 0)),
        out_shape=jax.ShapeDtypeStruct((bsz, s, D_IN), F32),
        compiler_params=pltpu.CompilerParams(
            dimension_semantics=("arbitrary", "arbitrary"), vmem_limit_bytes=VMEM_LIMIT),
        name="in_proj",
    )(x, sc, sh, g, w_bf)


def _shift_rows(x, prev8, d, row):
    rolled = pltpu.roll(x, d, 0)
    head = jnp.where(row[:SUBLANES] >= d, rolled[:SUBLANES], pltpu.roll(prev8, d, 0))
    return jnp.concatenate([head, rolled[SUBLANES:]], axis=0)


def _head_rms(y, g, head_dim):
    sq = y * y
    if head_dim == LANES:
        ms = jnp.mean(sq, axis=-1, keepdims=True)
    else:
        assert 2 * head_dim == LANES
        left = lax.broadcasted_iota(jnp.int32, sq.shape, 1) < head_dim
        s_left = jnp.sum(jnp.where(left, sq, 0.0), axis=-1, keepdims=True)
        s_right = jnp.sum(jnp.where(left, 0.0, sq), axis=-1, keepdims=True)
        ms = jnp.where(left, s_left, s_right) * (1.0 / head_dim)
    return y * lax.rsqrt(ms + EPS) * g


def _linear_scan(a, v, h0, sub):
    groups = []
    for r0 in range(0, a.shape[0], SUBLANES):
        ag, vg = a[r0:r0 + SUBLANES], v[r0:r0 + SUBLANES]
        d = 1
        while d < SUBLANES:
            keep = sub >= d
            vg = vg + ag * jnp.where(keep, pltpu.roll(vg, d, 0), 0.0)
            ag = ag * jnp.where(keep, pltpu.roll(ag, d, 0), 1.0)
            d *= 2
        groups.append((ag, vg))
    out, carry = [], h0
    for ag, vg in groups:
        hg = vg + ag * carry
        carry = hg[SUBLANES - 1:, :]
        out.append(hg)
    return jnp.concatenate(out, axis=0), carry


def _gelu_tanh(x):
    c = 0.7978845608028654
    half_x = 0.5 * x
    return half_x + half_x * jnp.tanh(x * (c + (c * 0.044715) * (x * x)))


def _mix_kernel(p_ref, w3_ref, w4_ref, b4_ref, wax_ref, ba_ref, bx_ref, lam_ref, ga_ref, gb_ref,
                y_ref, ua_carry, xb_carry, h_carry, *, ts):
    @pl.when(pl.program_id(1) == 0)
    def _():
        ua_carry[...] = jnp.zeros_like(ua_carry)
        xb_carry[...] = jnp.zeros_like(xb_carry)
        h_carry[...] = jnp.zeros_like(h_carry)

    row = lax.broadcasted_iota(jnp.int32, (ts, LANES), 0)

    for k in range(D_CONV // LANES):
        cols = slice(k * LANES, (k + 1) * LANES)
        b_a = p_ref[:, k * LANES:(k + 1) * LANES]
        c_a = p_ref[:, D_CONV + k * LANES:D_CONV + (k + 1) * LANES]
        x_a = p_ref[:, 2 * D_CONV + k * LANES:2 * D_CONV + (k + 1) * LANES]
        u = c_a * x_a
        prev = ua_carry[:, cols]
        conv = (w3_ref[2:3, cols] * u
                + w3_ref[1:2, cols] * _shift_rows(u, prev, 1, row)
                + w3_ref[0:1, cols] * _shift_rows(u, prev, 2, row))
        ua_carry[:, cols] = u[ts - SUBLANES:, :]
        y_a = b_a * conv
        y_ref[:, cols] = _head_rms(y_a, ga_ref[:, cols], CONV_HEAD_DIM).astype(BF16)

        g_b = p_ref[:, 3 * D_CONV + k * LANES:3 * D_CONV + (k + 1) * LANES]
        x_b = p_ref[:, 3 * D_CONV + D_LRU + k * LANES:3 * D_CONV + D_LRU + (k + 1) * LANES]
        prevb = xb_carry[:, cols]
        xc = (w4_ref[3:4, cols] * x_b
              + w4_ref[2:3, cols] * _shift_rows(x_b, prevb, 1, row)
              + w4_ref[1:2, cols] * _shift_rows(x_b, prevb, 2, row)
              + w4_ref[0:1, cols] * _shift_rows(x_b, prevb, 3, row)
              + b4_ref[:, cols])
        xb_carry[:, cols] = x_b[ts - SUBLANES:, :]
        gates = _dot(xc.astype(BF16), wax_ref[k])
        r = jax.nn.sigmoid(gates[:, :LANES] + ba_ref[:, cols])
        i = jax.nn.sigmoid(gates[:, LANES:] + bx_ref[:, cols])
        nlam = -lam_ref[:, cols]
        softplus = jnp.maximum(nlam, 0.0) + jnp.log1p(jnp.exp(-jnp.abs(nlam)))
        log_a = r * ((-LRU_C) * softplus)
        a = jnp.exp(log_a)
        mult = jnp.sqrt(-jnp.tanh(log_a) * (a * a + 1.0))
        v = mult * i * xc
        hs, h_last = _linear_scan(a, v, h_carry[:, cols], row[:SUBLANES])
        h_carry[:, cols] = h_last
        y_b = hs * _gelu_tanh(g_b)
        y_ref[:, D_CONV + k * LANES:D_CONV + (k + 1) * LANES] = _head_rms(
            y_b, gb_ref[:, cols], LRU_HEAD_DIM).astype(BF16)


def _mix(proj, w3, w4, b4, wax_bf, b_a, b_x, lam, g_a, g_b):
    bsz, s, _ = proj.shape
    ts = 512
    full = lambda shape: pl.BlockSpec(shape, lambda b, t: (0,) * len(shape))
    row = lambda a: a.reshape(1, -1)
    return pl.pallas_call(
        functools.partial(_mix_kernel, ts=ts),
        grid=(bsz, s // ts),
        in_specs=[
            pl.BlockSpec((None, ts, D_IN), lambda b, t: (b, t, 0)),
            full((3, D_CONV)), full((4, D_LRU)), full((1, D_LRU)),
            full((LRU_HEADS, LRU_HEAD_DIM, 2 * LRU_HEAD_DIM)),
            full((1, D_LRU)), full((1, D_LRU)), full((1, D_LRU)),
            full((1, D_CONV)), full((1, D_LRU)),
        ],
        out_specs=pl.BlockSpec((None, ts, D_MIX), lambda b, t: (b, t, 0)),
        out_shape=jax.ShapeDtypeStruct((bsz, s, D_MIX), BF16),
        scratch_shapes=[
            pltpu.VMEM((SUBLANES, D_CONV), F32),
            pltpu.VMEM((SUBLANES, D_LRU), F32),
            pltpu.VMEM((1, D_LRU), F32),
        ],
        compiler_params=pltpu.CompilerParams(
            dimension_semantics=("arbitrary", "arbitrary"), vmem_limit_bytes=VMEM_LIMIT),
        name="mix",
    )(proj, w3, w4, row(b4), wax_bf, row(b_a), row(b_x), row(lam), row(g_a), row(g_b))


def _route(logits, tri, carry):
    lane = lax.broadcasted_iota(jnp.int32, logits.shape, 1)
    neg = -jnp.inf
    is_g = lane < N_GROUPS
    gl = jnp.where(is_g, logits, neg)
    ge = jnp.exp(gl - jnp.max(gl, axis=-1, keepdims=True))
    p_g = 1.0 / jnp.sum(ge, axis=-1, keepdims=True)
    g_idx = jnp.min(jnp.where(is_g & (ge >= 1.0), lane, LANES), axis=-1, keepdims=True)
    sel = (lane >= N_GROUPS) & (lane < N_GROUPS + N_EXPERTS) & (((lane - N_GROUPS) // EXPERTS_PER_GROUP) == g_idx)
    el = jnp.where(sel, logits, neg)
    m1 = jnp.max(el, axis=-1, keepdims=True)
    i1 = jnp.min(jnp.where(sel & (el == m1), lane, LANES), axis=-1, keepdims=True)
    sel2 = sel & (lane != i1)
    el2 = jnp.where(sel2, logits, neg)
    m2 = jnp.max(el2, axis=-1, keepdims=True)
    i2 = jnp.min(jnp.where(sel2 & (el2 == m2), lane, LANES), axis=-1, keepdims=True)
    e2 = jnp.exp(m2 - m1)
    w1 = p_g * (1.0 / (1.0 + e2))
    w2 = p_g * (e2 / (1.0 + e2))
    out = jnp.where(lane == 0, (i1 - N_GROUPS).astype(F32), 0.0)
    out = jnp.where(lane == 1, (i2 - N_GROUPS).astype(F32), out)
    out = jnp.where(lane == 2, w1, out)
    out = jnp.where(lane == 3, w2, out)
    oh1 = lane == i1
    oh2 = lane == i2
    both = jnp.where(oh1 | oh2, 1.0, 0.0)
    before = _dot(tri, both.astype(BF16)) + carry
    out = jnp.where(lane == 4, jnp.sum(jnp.where(oh1, before, 0.0), axis=-1, keepdims=True), out)
    out = jnp.where(lane == 5, jnp.sum(jnp.where(oh2, before, 0.0), axis=-1, keepdims=True), out)
    return out, carry + jnp.sum(both, axis=0, keepdims=True)


def _pack_bf16_pair(lo, hi):
    lo_bits = lax.bitcast_convert_type(lo.astype(BF16).astype(F32), jnp.uint32)
    hi_bits = lax.bitcast_convert_type(hi.astype(BF16).astype(F32), jnp.uint32)
    return (lo_bits >> 16) | hi_bits


def _unpack_bf16_pair(w):
    lo = lax.bitcast_convert_type(w << 16, F32)
    hi = lax.bitcast_convert_type(w & jnp.uint32(0xFFFF0000), F32)
    return lo, hi


def _out_kernel(y_ref, x_ref, g1_ref, w_ref, sc_ref, sh_ref, g_ref, wr_ref, br_ref, tri_ref,
                x1_ref, h2p_ref, rt_ref, rtt_ref, cnt_ref, carry):
    @pl.when((pl.program_id(0) == 0) & (pl.program_id(1) == 0))
    def _():
        carry[...] = jnp.zeros_like(carry)

    sub = tri_ref.shape[0]
    count = carry[...]
    tiles = [slice(r0, r0 + sub) for r0 in range(0, y_ref.shape[0], sub)]
    mixes = [_dot(y_ref[rows, :], w_ref[...]) for rows in tiles]
    for rows, mix in zip(tiles, mixes):
        x1 = x_ref[rows, :] + g1_ref[...] * mix
        x1_ref[rows, :] = x1
        h2 = _modulated_norm(x1, g_ref[...], sc_ref[...], sh_ref[...])
        half = h2.shape[1] // 2
        words = _pack_bf16_pair(h2[:, :half], h2[:, half:])
        for c in range(half // LANES):
            h2p_ref[pl.ds(rows.start * SUBLANES + c, sub, stride=SUBLANES), :] = words[:, c * LANES:(c + 1) * LANES]
        logits = _dot(h2.astype(BF16), wr_ref[...]) + br_ref[...]
        rt, count = _route(logits, tri_ref[...], count)
        rt_ref[rows, :] = rt
        rtt_ref[:, rows] = rt.T[:SUBLANES, :]
    carry[...] = count
    cnt_ref[...] = count


def _out(y, x, g1, w_bf, sc2, sh2, g, wr_bf, br):
    bsz, s, d = x.shape
    tm, sub = 512, 256
    per_batch = pl.BlockSpec((None, 1, d), lambda b, t: (b, 0, 0))
    tile = lambda w: pl.BlockSpec((None, tm, w), lambda b, t: (b, t, 0))
    const = lambda shape, **kw: pl.BlockSpec(shape, lambda b, t: (0, 0), **kw)
    return pl.pallas_call(
        _out_kernel,
        grid=(bsz, s // tm),
        in_specs=[
            tile(D_MIX), tile(d), per_batch,
            const((D_MIX, d), pipeline_mode=pl.Buffered(1)),
            per_batch, per_batch, const((1, d)),
            const((d, ROUTE_W)), const((1, ROUTE_W)), const((sub, sub)),
        ],
        out_specs=[tile(d), pl.BlockSpec((None, tm * SUBLANES, LANES), lambda b, t: (b, t, 0)),
                   tile(ROUTE_W), pl.BlockSpec((SUBLANES, tm), lambda b, t: (0, b * (s // tm) + t)),
                   const((1, ROUTE_W))],
        out_shape=[
            jax.ShapeDtypeStruct((bsz, s, d), F32),
            jax.ShapeDtypeStruct((bsz, s * SUBLANES, LANES), jnp.uint32),
            jax.ShapeDtypeStruct((bsz, s, ROUTE_W), F32),
            jax.ShapeDtypeStruct((SUBLANES, bsz * s), F32),
            jax.ShapeDtypeStruct((1, ROUTE_W), F32),
        ],
        scratch_shapes=[pltpu.VMEM((1, ROUTE_W), F32)],
        compiler_params=pltpu.CompilerParams(
            dimension_semantics=("arbitrary", "arbitrary"), vmem_limit_bytes=VMEM_LIMIT),
        name="out_route",
    )(y, x, g1, w_bf, sc2, sh2, g, wr_bf, br, jnp.tril(jnp.ones((sub, sub), BF16), -1))


def _plan(rtt, counts_f, n_tok):
    counts = counts_f[0, N_GROUPS:N_GROUPS + N_EXPERTS].astype(jnp.int32)
    padded = ((counts + MOE_BLOCK - 1) // MOE_BLOCK) * MOE_BLOCK
    padded_ends = jnp.cumsum(padded)
    padded_starts = padded_ends - padded
    n_rows = 2 * n_tok + N_EXPERTS * MOE_BLOCK
    n_blocks = n_rows // MOE_BLOCK
    expert_ids = jnp.arange(N_EXPERTS, dtype=jnp.int32)
    idx = rtt.astype(jnp.int32)
    start_of = jnp.sum(jnp.where(idx[0:2, :, None] == expert_ids, padded_starts, 0), axis=-1)
    pos = start_of + idx[4:6]
    block_start = jnp.arange(n_blocks, dtype=jnp.int32)[:, None] * MOE_BLOCK
    owns = (padded_starts <= block_start) & (block_start < padded_ends)
    n_valid = jnp.sum(jnp.where(owns, jnp.clip(counts - (block_start - padded_starts), 0, MOE_BLOCK), 0), axis=1)
    last_used = jnp.max(jnp.where(counts > 0, expert_ids, 0))
    block_expert = jnp.where(n_valid > 0, jnp.sum(jnp.where(owns, expert_ids, 0), axis=1), last_used)
    later_used = (expert_ids[None, :] > expert_ids[:, None]) & (counts[None, :] > 0)
    next_used = jnp.min(jnp.where(later_used, expert_ids[None, :], N_EXPERTS), axis=1)
    next_used = jnp.where(next_used == N_EXPERTS, -1, next_used)
    block_next = jnp.sum(jnp.where(owns, next_used, 0), axis=1)
    used_ordinal = jnp.cumsum((counts > 0).astype(jnp.int32)) - 1
    block_slot = jnp.sum(jnp.where(owns, used_ordinal, 0), axis=1) & 1
    n_used = padded_ends[-1] // MOE_BLOCK
    block_index = jnp.minimum(jnp.arange(n_blocks, dtype=jnp.int32), n_used - 1)
    i32 = lambda a: a.astype(jnp.int32)
    return (pos[0], pos[1], i32(block_expert), i32(n_valid), i32(block_index), i32(block_next), i32(block_slot),
            n_rows)


def _dispatch_kernel(nvalid_ref, p0_ref, p1_ref, h_ref, xs_hbm, zbuf, sem, zsem, *, tm, n_blocks):
    @pl.when(pl.program_id(0) == 0)
    def _():
        zbuf[...] = jnp.zeros_like(zbuf)

        def zero_block(i):
            return pltpu.make_async_copy(zbuf, xs_hbm.at[pl.ds(i * MOE_BLOCK, MOE_BLOCK)], zsem)

        def start(i, carry):
            pl.when(nvalid_ref[i] < MOE_BLOCK)(lambda: zero_block(i).start())
            return carry

        def wait(i, carry):
            pl.when(nvalid_ref[i] < MOE_BLOCK)(lambda: zero_block(i).wait())
            return carry

        lax.fori_loop(0, n_blocks, start, 0)
        lax.fori_loop(0, n_blocks, wait, 0)

    def issue(c, carry):
        for j in range(DMA_ISSUE_UNROLL):
            r = c * DMA_ISSUE_UNROLL + j
            src = h_ref.at[r]
            pltpu.make_async_copy(src, xs_hbm.at[p0_ref[0, r]], sem).start(priority=0)
            pltpu.make_async_copy(src, xs_hbm.at[p1_ref[0, r]], sem).start(priority=1)
        return carry

    lax.fori_loop(0, tm // DMA_ISSUE_UNROLL, issue, 0)
    for _ in range(2):
        pltpu.make_async_copy(h_ref, xs_hbm.at[pl.ds(0, tm)], sem).wait()


def _dispatch(n_valid, pos0, pos1, h2p, n_rows):
    n = h2p.shape[0]
    tm = 2048
    n_blocks = n_valid.shape[0]
    idx_spec = pl.BlockSpec((None, 1, tm), lambda t, nv: (t, 0, 0), memory_space=pltpu.SMEM)
    grid_spec = pltpu.PrefetchScalarGridSpec(
        num_scalar_prefetch=1,
        grid=(n // tm,),
        in_specs=[idx_spec, idx_spec, pl.BlockSpec((tm, SUBLANES, LANES), lambda t, nv: (t, 0, 0))],
        out_specs=pl.BlockSpec(memory_space=pl.ANY),
        scratch_shapes=[
            pltpu.VMEM((MOE_BLOCK, SUBLANES, LANES), h2p.dtype),
            pltpu.SemaphoreType.DMA(()),
            pltpu.SemaphoreType.DMA(()),
        ],
    )
    return pl.pallas_call(
        functools.partial(_dispatch_kernel, tm=tm, n_blocks=n_blocks),
        grid_spec=grid_spec,
        out_shape=jax.ShapeDtypeStruct((n_rows, SUBLANES, LANES), h2p.dtype),
        compiler_params=pltpu.CompilerParams(
            dimension_semantics=("arbitrary",), vmem_limit_bytes=VMEM_LIMIT,
            disable_bounds_checks=True),
        name="dispatch",
    )(n_valid, pos0.reshape(n // tm, 1, tm), pos1.reshape(n // tm, 1, tm), h2p)


def _moe_kernel(be_ref, nvalid_ref, bi_ref, nxt_ref, slot_ref, xs_hbm, wg_hbm, wu_hbm, wd_hbm, y_ref,
                wg_f, wu_f, wd_f, xbuf, sem, xsem, *, n_blocks):
    i = pl.program_id(0)
    n_valid = nvalid_ref[i]
    block_rows = MOE_BLOCK * SUBLANES

    def x_copy(j):
        start = pl.multiple_of(bi_ref[j] * block_rows, block_rows)
        return pltpu.make_async_copy(xs_hbm.at[pl.ds(start, block_rows)], xbuf.at[j % X_RING], xsem.at[j % X_RING])

    @pl.when(i == 0)
    def _():
        for j in range(X_RING - 1):
            x_copy(j).start()

    @pl.when(i + X_RING - 1 < n_blocks)
    def _():
        x_copy(i + X_RING - 1).start()

    x_copy(i).wait()
    x_ref = xbuf.at[i % X_RING]

    def fetch(e, s):
        return (pltpu.make_async_copy(wg_hbm.at[e], wg_f.at[s], sem.at[s, 0]),
                pltpu.make_async_copy(wu_hbm.at[e], wu_f.at[s], sem.at[s, 1]),
                pltpu.make_async_copy(wd_hbm.at[e], wd_f.at[s], sem.at[s, 2]))

    @pl.when(n_valid > 0)
    def _():
        e = be_ref[i]
        s = slot_ref[i]

        @pl.when(i == 0)
        def _():
            for cp in fetch(e, s):
                cp.start(priority=1)

        @pl.when((i == 0) | (e != be_ref[jnp.maximum(i - 1, 0)]))
        def _():
            for cp in fetch(e, s):
                cp.wait()
            nxt = nxt_ref[i]

            @pl.when(nxt >= 0)
            def _():
                for cp in fetch(nxt, 1 - s):
                    cp.start(priority=1)

        words = jnp.concatenate(
            [x_ref[pl.ds(c, MOE_BLOCK, stride=SUBLANES), :] for c in range(SUBLANES)], axis=1)
        lo, hi = _unpack_bf16_pair(words)
        half = words.shape[1]
        g = _dot(lo, wg_f[s, :half, :]) + _dot(hi, wg_f[s, half:, :])
        u = _dot(lo, wu_f[s, :half, :]) + _dot(hi, wu_f[s, half:, :])
        act = g * jax.nn.sigmoid(g) * u
        y = _dot(act, wd_f[s])
        y_words = _pack_bf16_pair(y[:, :half], y[:, half:])
        for c in range(SUBLANES):
            y_ref[pl.ds(c, MOE_BLOCK, stride=SUBLANES), :] = y_words[:, c * LANES:(c + 1) * LANES]

    @pl.when(n_valid == 0)
    def _():
        y_ref[...] = jnp.zeros_like(y_ref)


def _moe(block_expert, n_valid, block_index, block_next, block_slot, xs, w_gate, w_up, w_down):
    n_blocks = block_expert.shape[0]
    n_rows = xs.shape[0]
    d = 2 * SUBLANES * LANES
    hbm = pl.BlockSpec(memory_space=pl.ANY)
    grid_spec = pltpu.PrefetchScalarGridSpec(
        num_scalar_prefetch=5,
        grid=(n_blocks,),
        in_specs=[hbm, hbm, hbm, hbm],
        out_specs=pl.BlockSpec((MOE_BLOCK * SUBLANES, LANES), lambda i, be, nv, bi, nx, sl: (i, 0)),
        scratch_shapes=[
            pltpu.VMEM((2, d, D_EXPERT), F32),
            pltpu.VMEM((2, d, D_EXPERT), F32),
            pltpu.VMEM((2, D_EXPERT, d), F32),
            pltpu.VMEM((X_RING, MOE_BLOCK * SUBLANES, LANES), jnp.uint32),
            pltpu.SemaphoreType.DMA((2, 3)),
            pltpu.SemaphoreType.DMA((X_RING,)),
        ],
    )
    return pl.pallas_call(
        functools.partial(_moe_kernel, n_blocks=n_blocks),
        grid_spec=grid_spec,
        out_shape=jax.ShapeDtypeStruct((n_rows * SUBLANES, LANES), jnp.uint32),
        compiler_params=pltpu.CompilerParams(
            dimension_semantics=("arbitrary",), vmem_limit_bytes=VMEM_LIMIT),
        name="moe",
    )(block_expert, n_valid, block_index, block_next, block_slot,
      xs.reshape(n_rows * SUBLANES, LANES), w_gate, w_up, w_down)


def _final_kernel(p0_ref, p1_ref, q0_ref, q1_ref, x1_ref, rt_ref, g2_ref, gf_ref, yb_hbm, yb_flat_hbm, o_ref,
                  y0buf, y1buf, sem, *, tm, n_tiles):
    t = pl.program_id(0)
    slot = t & 1

    def issue(pa_ref, pb_ref, s):
        def body(c, carry):
            for j in range(DMA_ISSUE_UNROLL):
                r = c * DMA_ISSUE_UNROLL + j
                dst = pl.ds(pl.multiple_of(r * SUBLANES, SUBLANES), SUBLANES)
                pltpu.make_async_copy(yb_hbm.at[pa_ref[0, r]], y0buf.at[s].at[dst], sem.at[s]).start(priority=0)
                pltpu.make_async_copy(yb_hbm.at[pb_ref[0, r]], y1buf.at[s].at[dst], sem.at[s]).start(priority=1)
            return carry

        lax.fori_loop(0, tm // DMA_ISSUE_UNROLL, body, 0)

    @pl.when(t == 0)
    def _():
        issue(p0_ref, p1_ref, 0)

    @pl.when(t + 1 < n_tiles)
    def _():
        issue(q0_ref, q1_ref, 1 - slot)

    pltpu.make_async_copy(yb_flat_hbm.at[pl.ds(0, tm * SUBLANES)], y0buf.at[slot], sem.at[slot]).wait()
    pltpu.make_async_copy(yb_flat_hbm.at[pl.ds(0, tm * SUBLANES)], y1buf.at[slot], sem.at[slot]).wait()

    def rows_of(buf):
        tiles = buf.at[slot]
        words = jnp.concatenate([tiles[pl.ds(c, tm, stride=SUBLANES), :] for c in range(SUBLANES)], axis=1)
        return jnp.concatenate(_unpack_bf16_pair(words), axis=1)

    rt = rt_ref[...]
    y = rt[:, 2:3] * rows_of(y0buf) + rt[:, 3:4] * rows_of(y1buf)
    x2 = x1_ref[...] + g2_ref[...] * y
    ms = jnp.mean(x2 * x2, axis=-1, keepdims=True)
    o_ref[...] = x2 * lax.rsqrt(ms + EPS) * gf_ref[...]


def _final(pos0, pos1, x1, rt, g2, gf, yb, seq):
    n, d = x1.shape
    tm = 256
    n_tiles = n // tm
    per_seq = seq // tm
    this_tile = pl.BlockSpec((None, 1, tm), lambda t: (t, 0, 0), memory_space=pltpu.SMEM)
    next_tile = pl.BlockSpec((None, 1, tm), lambda t: (jnp.minimum(t + 1, n_tiles - 1), 0, 0),
                             memory_space=pltpu.SMEM)
    p0 = pos0.reshape(n_tiles, 1, tm)
    p1 = pos1.reshape(n_tiles, 1, tm)
    return pl.pallas_call(
        functools.partial(_final_kernel, tm=tm, n_tiles=n_tiles),
        grid=(n_tiles,),
        in_specs=[
            this_tile, this_tile, next_tile, next_tile,
            pl.BlockSpec((tm, d), lambda t: (t, 0)),
            pl.BlockSpec((tm, ROUTE_W), lambda t: (t, 0)),
            pl.BlockSpec((None, 1, d), lambda t: (t // per_seq, 0, 0)),
            pl.BlockSpec((1, d), lambda t: (0, 0)),
            pl.BlockSpec(memory_space=pl.ANY),
            pl.BlockSpec(memory_space=pl.ANY),
        ],
        out_specs=pl.BlockSpec((tm, d), lambda t: (t, 0)),
        out_shape=jax.ShapeDtypeStruct((n, d), F32),
        scratch_shapes=[
            pltpu.VMEM((2, tm * SUBLANES, LANES), jnp.uint32),
            pltpu.VMEM((2, tm * SUBLANES, LANES), jnp.uint32),
            pltpu.SemaphoreType.DMA((2,)),
        ],
        compiler_params=pltpu.CompilerParams(
            dimension_semantics=("arbitrary",), vmem_limit_bytes=VMEM_LIMIT,
            disable_bounds_checks=True),
        name="final",
    )(p0, p1, p0, p1, x1, rt, g2, gf, yb.reshape(-1, SUBLANES, LANES), yb)


def kernel(x, c, ada_w, ada_b, norm1_g, w_in, conv3_w, conv4_w, conv4_b, lru_w_a, lru_b_a, lru_w_x, lru_b_x, lru_lambda, head_norm_conv_g, head_norm_lru_g, w_out, norm2_g, route_w_group, route_b_group, route_w_expert, route_b_expert, w_e_gate, w_e_up, w_e_down, final_norm_g):
    bsz, s, d = x.shape
    n_tok = bsz * s
    l = 0

    mod = _ada(c, ada_w[l], ada_b[l]).reshape(bsz, 6, 1, d)
    sh1, sc1, g1, sh2, sc2, g2 = (mod[:, j] for j in range(6))

    proj = _in_proj(x, sc1, sh1, norm1_g[l].reshape(1, d), w_in[l].astype(BF16))

    wax = jnp.concatenate([lru_w_a[l], lru_w_x[l]], axis=-1).astype(BF16)
    y = _mix(proj, conv3_w[l], conv4_w[l], conv4_b[l], wax, lru_b_a[l], lru_b_x[l], lru_lambda[l],
             head_norm_conv_g[l], head_norm_lru_g[l])

    w_route = jnp.concatenate(
        [route_w_group[l], jnp.transpose(route_w_expert[l], (1, 0, 2)).reshape(d, N_EXPERTS),
         jnp.zeros((d, ROUTE_W - N_GROUPS - N_EXPERTS), F32)], axis=1).astype(BF16)
    b_route = jnp.concatenate(
        [route_b_group[l], route_b_expert[l].reshape(-1),
         jnp.zeros((ROUTE_W - N_GROUPS - N_EXPERTS,), F32)]).reshape(1, ROUTE_W)
    x1, h2p, rt, rtt, counts = _out(y, x, g1, w_out[l].astype(BF16), sc2, sh2, norm2_g[l].reshape(1, d),
                                    w_route, b_route)

    rt = rt.reshape(n_tok, ROUTE_W)
    pos0, pos1, block_expert, n_valid, block_index, block_next, block_slot, n_rows = _plan(rtt, counts, n_tok)
    assert d // 2 == SUBLANES * LANES, "a token's packed words must fill exactly one (8, 128) tile"
    xs = _dispatch(n_valid, pos0, pos1, h2p.reshape(n_tok, SUBLANES, LANES), n_rows)
    yb = _moe(block_expert, n_valid, block_index, block_next, block_slot, xs,
              w_e_gate[l], w_e_up[l], w_e_down[l])
    out = _final(pos0, pos1, x1.reshape(n_tok, d), rt, g2, final_norm_g.reshape(1, d), yb, s)
    return out.reshape(bsz, s, d)
```

```python
import functools

import jax
import jax.numpy as jnp
from jax import lax
from jax.experimental import pallas as pl
from jax.experimental.pallas import tpu as pltpu

D_CONV = 1024
CONV_HEAD_DIM = 64
D_LRU = 1024
LRU_HEADS = 8
LRU_HEAD_DIM = 128
LRU_C = 8.0
D_MIX = D_CONV + D_LRU
D_IN = 3 * D_CONV + 2 * D_LRU
N_GROUPS = 8
EXPERTS_PER_GROUP = 8
N_EXPERTS = 64
D_EXPERT = 512
EPS = 1e-6

LANES = 128
SUBLANES = 8
ROUTE_W = LANES
MOE_BLOCK = 256
X_RING = 3
DMA_ISSUE_UNROLL = 8
VMEM_LIMIT = 56 * 1024 * 1024

F32 = jnp.float32
BF16 = jnp.bfloat16


def _dot(a, b):
    return jnp.dot(a, b, preferred_element_type=F32)


def _ada_kernel(c_ref, w_ref, b_ref, o_ref):
    c = c_ref[...]
    s = (c * jax.nn.sigmoid(c)).astype(BF16)
    o_ref[...] = _dot(s, w_ref[...].astype(BF16)) + b_ref[...]


def _ada(c, w, b):
    bsz, d = c.shape
    n = w.shape[1]
    tn = 1024
    return pl.pallas_call(
        _ada_kernel,
        grid=(n // tn,),
        in_specs=[
            pl.BlockSpec((bsz, d), lambda j: (0, 0)),
            pl.BlockSpec((d, tn), lambda j: (0, j)),
            pl.BlockSpec((1, tn), lambda j: (0, j)),
        ],
        out_specs=pl.BlockSpec((bsz, tn), lambda j: (0, j)),
        out_shape=jax.ShapeDtypeStruct((bsz, n), F32),
        compiler_params=pltpu.CompilerParams(
            dimension_semantics=("arbitrary",), vmem_limit_bytes=VMEM_LIMIT),
        name="ada",
    )(c, w, b.reshape(1, n))


def _modulated_norm(x, g, sc, sh):
    ms = jnp.mean(x * x, axis=-1, keepdims=True)
    return x * lax.rsqrt(ms + EPS) * (g * (1.0 + sc)) + sh


def _in_proj_kernel(x_ref, sc_ref, sh_ref, g_ref, w_ref, o_ref, *, n_chunk):
    hb = _modulated_norm(x_ref[...], g_ref[...], sc_ref[...], sh_ref[...]).astype(BF16)
    for j in range(D_IN // n_chunk):
        cols = slice(j * n_chunk, (j + 1) * n_chunk)
        o_ref[:, cols] = _dot(hb, w_ref[:, cols])


def _in_proj(x, sc, sh, g, w_bf):
    bsz, s, d = x.shape
    tm = 512
    per_batch = pl.BlockSpec((None, 1, d), lambda b, t: (b, 0, 0))
    return pl.pallas_call(
        functools.partial(_in_proj_kernel, n_chunk=1024),
        grid=(bsz, s // tm),
        in_specs=[
            pl.BlockSpec((None, tm, d), lambda b, t: (b, t, 0)),
            per_batch, per_batch,
            pl.BlockSpec((1, d), lambda b, t: (0, 0)),
            pl.BlockSpec((d, D_IN), lambda b, t: (0, 0), pipeline_mode=pl.Buffered(1)),
        ],
        out_specs=pl.BlockSpec((None, tm, D_IN), lambda b, t: (b, t, 0)),
        out_shape=jax.ShapeDtypeStruct((bsz, s, D_IN), F32),
        compiler_params=pltpu.CompilerParams(
            dimension_semantics=("arbitrary", "arbitrary"), vmem_limit_bytes=VMEM_LIMIT),
        name="in_proj",
    )(x, sc, sh, g, w_bf)


def _shift_rows(x, prev8, d, row):
    rolled = pltpu.roll(x, d, 0)
    head = jnp.where(row[:SUBLANES] >= d, rolled[:SUBLANES], pltpu.roll(prev8, d, 0))
    return jnp.concatenate([head, rolled[SUBLANES:]], axis=0)


def _head_rms(y, g, head_dim):
    sq = y * y
    if head_dim == LANES:
        ms = jnp.mean(sq, axis=-1, keepdims=True)
    else:
        assert 2 * head_dim == LANES
        left = lax.broadcasted_iota(jnp.int32, sq.shape, 1) < head_dim
        s_left = jnp.sum(jnp.where(left, sq, 0.0), axis=-1, keepdims=True)
        s_right = jnp.sum(jnp.where(left, 0.0, sq), axis=-1, keepdims=True)
        ms = jnp.where(left, s_left, s_right) * (1.0 / head_dim)
    return y * lax.rsqrt(ms + EPS) * g


def _linear_scan(a, v, h0, sub):
    groups = []
    for r0 in range(0, a.shape[0], SUBLANES):
        ag, vg = a[r0:r0 + SUBLANES], v[r0:r0 + SUBLANES]
        d = 1
        while d < SUBLANES:
            keep = sub >= d
            vg = vg + ag * jnp.where(keep, pltpu.roll(vg, d, 0), 0.0)
            ag = ag * jnp.where(keep, pltpu.roll(ag, d, 0), 1.0)
            d *= 2
        groups.append((ag, vg))
    out, carry = [], h0
    for ag, vg in groups:
        hg = vg + ag * carry
        carry = hg[SUBLANES - 1:, :]
        out.append(hg)
    return jnp.concatenate(out, axis=0), carry


def _gelu_tanh(x):
    c = 0.7978845608028654
    half_x = 0.5 * x
    return half_x + half_x * jnp.tanh(x * (c + (c * 0.044715) * (x * x)))


def _mix_kernel(p_ref, w3_ref, w4_ref, b4_ref, wax_ref, ba_ref, bx_ref, lam_ref, ga_ref, gb_ref,
                y_ref, ua_carry, xb_carry, h_carry, *, ts):
    @pl.when(pl.program_id(1) == 0)
    def _():
        ua_carry[...] = jnp.zeros_like(ua_carry)
        xb_carry[...] = jnp.zeros_like(xb_carry)
        h_carry[...] = jnp.zeros_like(h_carry)

    row = lax.broadcasted_iota(jnp.int32, (ts, LANES), 0)

    for k in range(D_CONV // LANES):
        cols = slice(k * LANES, (k + 1) * LANES)
        b_a = p_ref[:, k * LANES:(k + 1) * LANES]
        c_a = p_ref[:, D_CONV + k * LANES:D_CONV + (k + 1) * LANES]
        x_a = p_ref[:, 2 * D_CONV + k * LANES:2 * D_CONV + (k + 1) * LANES]
        u = c_a * x_a
        prev = ua_carry[:, cols]
        conv = (w3_ref[2:3, cols] * u
                + w3_ref[1:2, cols] * _shift_rows(u, prev, 1, row)
                + w3_ref[0:1, cols] * _shift_rows(u, prev, 2, row))
        ua_carry[:, cols] = u[ts - SUBLANES:, :]
        y_a = b_a * conv
        y_ref[:, cols] = _head_rms(y_a, ga_ref[:, cols], CONV_HEAD_DIM).astype(BF16)

        g_b = p_ref[:, 3 * D_CONV + k * LANES:3 * D_CONV + (k + 1) * LANES]
        x_b = p_ref[:, 3 * D_CONV + D_LRU + k * LANES:3 * D_CONV + D_LRU + (k + 1) * LANES]
        prevb = xb_carry[:, cols]
        xc = (w4_ref[3:4, cols] * x_b
              + w4_ref[2:3, cols] * _shift_rows(x_b, prevb, 1, row)
              + w4_ref[1:2, cols] * _shift_rows(x_b, prevb, 2, row)
              + w4_ref[0:1, cols] * _shift_rows(x_b, prevb, 3, row)
              + b4_ref[:, cols])
        xb_carry[:, cols] = x_b[ts - SUBLANES:, :]
        gates = _dot(xc.astype(BF16), wax_ref[k])
        r = jax.nn.sigmoid(gates[:, :LANES] + ba_ref[:, cols])
        i = jax.nn.sigmoid(gates[:, LANES:] + bx_ref[:, cols])
        nlam = -lam_ref[:, cols]
        softplus = jnp.maximum(nlam, 0.0) + jnp.log1p(jnp.exp(-jnp.abs(nlam)))
        log_a = r * ((-LRU_C) * softplus)
        a = jnp.exp(log_a)
        mult = jnp.sqrt(-jnp.tanh(log_a) * (a * a + 1.0))
        v = mult * i * xc
        hs, h_last = _linear_scan(a, v, h_carry[:, cols], row[:SUBLANES])
        h_carry[:, cols] = h_last
        y_b = hs * _gelu_tanh(g_b)
        y_ref[:, D_CONV + k * LANES:D_CONV + (k + 1) * LANES] = _head_rms(
            y_b, gb_ref[:, cols], LRU_HEAD_DIM).astype(BF16)


def _mix(proj, w3, w4, b4, wax_bf, b_a, b_x, lam, g_a, g_b):
    bsz, s, _ = proj.shape
    ts = 512
    full = lambda shape: pl.BlockSpec(shape, lambda b, t: (0,) * len(shape))
    row = lambda a: a.reshape(1, -1)
    return pl.pallas_call(
        functools.partial(_mix_kernel, ts=ts),
        grid=(bsz, s // ts),
        in_specs=[
            pl.BlockSpec((None, ts, D_IN), lambda b, t: (b, t, 0)),
            full((3, D_CONV)), full((4, D_LRU)), full((1, D_LRU)),
            full((LRU_HEADS, LRU_HEAD_DIM, 2 * LRU_HEAD_DIM)),
            full((1, D_LRU)), full((1, D_LRU)), full((1, D_LRU)),
            full((1, D_CONV)), full((1, D_LRU)),
        ],
        out_specs=pl.BlockSpec((None, ts, D_MIX), lambda b, t: (b, t, 0)),
        out_shape=jax.ShapeDtypeStruct((bsz, s, D_MIX), BF16),
        scratch_shapes=[
            pltpu.VMEM((SUBLANES, D_CONV), F32),
            pltpu.VMEM((SUBLANES, D_LRU), F32),
            pltpu.VMEM((1, D_LRU), F32),
        ],
        compiler_params=pltpu.CompilerParams(
            dimension_semantics=("arbitrary", "arbitrary"), vmem_limit_bytes=VMEM_LIMIT),
        name="mix",
    )(proj, w3, w4, row(b4), wax_bf, row(b_a), row(b_x), row(lam), row(g_a), row(g_b))


def _route(logits, tri, carry):
    lane = lax.broadcasted_iota(jnp.int32, logits.shape, 1)
    neg = -jnp.inf
    is_g = lane < N_GROUPS
    gl = jnp.where(is_g, logits, neg)
    ge = jnp.exp(gl - jnp.max(gl, axis=-1, keepdims=True))
    p_g = 1.0 / jnp.sum(ge, axis=-1, keepdims=True)
    g_idx = jnp.min(jnp.where(is_g & (ge >= 1.0), lane, LANES), axis=-1, keepdims=True)
    sel = (lane >= N_GROUPS) & (lane < N_GROUPS + N_EXPERTS) & (((lane - N_GROUPS) // EXPERTS_PER_GROUP) == g_idx)
    el = jnp.where(sel, logits, neg)
    m1 = jnp.max(el, axis=-1, keepdims=True)
    i1 = jnp.min(jnp.where(sel & (el == m1), lane, LANES), axis=-1, keepdims=True)
    sel2 = sel & (lane != i1)
    el2 = jnp.where(sel2, logits, neg)
    m2 = jnp.max(el2, axis=-1, keepdims=True)
    i2 = jnp.min(jnp.where(sel2 & (el2 == m2), lane, LANES), axis=-1, keepdims=True)
    e2 = jnp.exp(m2 - m1)
    w1 = p_g * (1.0 / (1.0 + e2))
    w2 = p_g * (e2 / (1.0 + e2))
    out = jnp.where(lane == 0, (i1 - N_GROUPS).astype(F32), 0.0)
    out = jnp.where(lane == 1, (i2 - N_GROUPS).astype(F32), out)
    out = jnp.where(lane == 2, w1, out)
    out = jnp.where(lane == 3, w2, out)
    oh1 = lane == i1
    oh2 = lane == i2
    both = jnp.where(oh1 | oh2, 1.0, 0.0)
    before = _dot(tri, both.astype(BF16)) + carry
    out = jnp.where(lane == 4, jnp.sum(jnp.where(oh1, before, 0.0), axis=-1, keepdims=True), out)
    out = jnp.where(lane == 5, jnp.sum(jnp.where(oh2, before, 0.0), axis=-1, keepdims=True), out)
    return out, carry + jnp.sum(both, axis=0, keepdims=True)


def _pack_bf16_pair(lo, hi):
    lo_bits = lax.bitcast_convert_type(lo.astype(BF16).astype(F32), jnp.uint32)
    hi_bits = lax.bitcast_convert_type(hi.astype(BF16).astype(F32), jnp.uint32)
    return (lo_bits >> 16) | hi_bits


def _unpack_bf16_pair(w):
    lo = lax.bitcast_convert_type(w << 16, F32)
    hi = lax.bitcast_convert_type(w & jnp.uint32(0xFFFF0000), F32)
    return lo, hi


def _out_kernel(y_ref, x_ref, g1_ref, w_ref, sc_ref, sh_ref, g_ref, wr_ref, br_ref, tri_ref,
                x1_ref, h2p_ref, rt_ref, rtt_ref, cnt_ref, carry):
    @pl.when((pl.program_id(0) == 0) & (pl.program_id(1) == 0))
    def _():
        carry[...] = jnp.zeros_like(carry)

    sub = tri_ref.shape[0]
    count = carry[...]
    tiles = [slice(r0, r0 + sub) for r0 in range(0, y_ref.shape[0], sub)]
    mixes = [_dot(y_ref[rows, :], w_ref[...]) for rows in tiles]
    for rows, mix in zip(tiles, mixes):
        x1 = x_ref[rows, :] + g1_ref[...] * mix
        x1_ref[rows, :] = x1
        h2 = _modulated_norm(x1, g_ref[...], sc_ref[...], sh_ref[...])
        half = h2.shape[1] // 2
        words = _pack_bf16_pair(h2[:, :half], h2[:, half:])
        for c in range(half // LANES):
            h2p_ref[pl.ds(rows.start * SUBLANES + c, sub, stride=SUBLANES), :] = words[:, c * LANES:(c + 1) * LANES]
        logits = _dot(h2.astype(BF16), wr_ref[...]) + br_ref[...]
        rt, count = _route(logits, tri_ref[...], count)
        rt_ref[rows, :] = rt
        rtt_ref[:, rows] = rt.T[:SUBLANES, :]
    carry[...] = count
    cnt_ref[...] = count


def _out(y, x, g1, w_bf, sc2, sh2, g, wr_bf, br):
    bsz, s, d = x.shape
    tm, sub = 512, 256
    per_batch = pl.BlockSpec((None, 1, d), lambda b, t: (b, 0, 0))
    tile = lambda w: pl.BlockSpec((None, tm, w), lambda b, t: (b, t, 0))
    const = lambda shape, **kw: pl.BlockSpec(shape, lambda b, t: (0, 0), **kw)
    return pl.pallas_call(
        _out_kernel,
        grid=(bsz, s // tm),
        in_specs=[
            tile(D_MIX), tile(d), per_batch,
            const((D_MIX, d), pipeline_mode=pl.Buffered(1)),
            per_batch, per_batch, const((1, d)),
            const((d, ROUTE_W)), const((1, ROUTE_W)), const((sub, sub)),
        ],
        out_specs=[tile(d), pl.BlockSpec((None, tm * SUBLANES, LANES), lambda b, t: (b, t, 0)),
                   tile(ROUTE_W), pl.BlockSpec((SUBLANES, tm), lambda b, t: (0, b * (s // tm) + t)),
                   const((1, ROUTE_W))],
        out_shape=[
            jax.ShapeDtypeStruct((bsz, s, d), F32),
            jax.ShapeDtypeStruct((bsz, s * SUBLANES, LANES), jnp.uint32),
            jax.ShapeDtypeStruct((bsz, s, ROUTE_W), F32),
            jax.ShapeDtypeStruct((SUBLANES, bsz * s), F32),
            jax.ShapeDtypeStruct((1, ROUTE_W), F32),
        ],
        scratch_shapes=[pltpu.VMEM((1, ROUTE_W), F32)],
        compiler_params=pltpu.CompilerParams(
            dimension_semantics=("arbitrary", "arbitrary"), vmem_limit_bytes=VMEM_LIMIT),
        name="out_route",
    )(y, x, g1, w_bf, sc2, sh2, g, wr_bf, br, jnp.tril(jnp.ones((sub, sub), BF16), -1))


def _plan(rtt, counts_f, n_tok):
    counts = counts_f[0, N_GROUPS:N_GROUPS + N_EXPERTS].astype(jnp.int32)
    padded = ((counts + MOE_BLOCK - 1) // MOE_BLOCK) * MOE_BLOCK
    padded_ends = jnp.cumsum(padded)
    padded_starts = padded_ends - padded
    n_rows = 2 * n_tok + N_EXPERTS * MOE_BLOCK
    n_blocks = n_rows // MOE_BLOCK
    expert_ids = jnp.arange(N_EXPERTS, dtype=jnp.int32)
    idx = rtt.astype(jnp.int32)
    start_of = jnp.sum(jnp.where(idx[0:2, :, None] == expert_ids, padded_starts, 0), axis=-1)
    pos = start_of + idx[4:6]
    block_start = jnp.arange(n_blocks, dtype=jnp.int32)[:, None] * MOE_BLOCK
    owns = (padded_starts <= block_start) & (block_start < padded_ends)
    n_valid = jnp.sum(jnp.where(owns, jnp.clip(counts - (block_start - padded_starts), 0, MOE_BLOCK), 0), axis=1)
    last_used = jnp.max(jnp.where(counts > 0, expert_ids, 0))
    block_expert = jnp.where(n_valid > 0, jnp.sum(jnp.where(owns, expert_ids, 0), axis=1), last_used)
    later_used = (expert_ids[None, :] > expert_ids[:, None]) & (counts[None, :] > 0)
    next_used = jnp.min(jnp.where(later_used, expert_ids[None, :], N_EXPERTS), axis=1)
    next_used = jnp.where(next_used == N_EXPERTS, -1, next_used)
    block_next = jnp.sum(jnp.where(owns, next_used, 0), axis=1)
    used_ordinal = jnp.cumsum((counts > 0).astype(jnp.int32)) - 1
    block_slot = jnp.sum(jnp.where(owns, used_ordinal, 0), axis=1) & 1
    n_used = padded_ends[-1] // MOE_BLOCK
    block_index = jnp.minimum(jnp.arange(n_blocks, dtype=jnp.int32), n_used - 1)
    i32 = lambda a: a.astype(jnp.int32)
    return (pos[0], pos[1], i32(block_expert), i32(n_valid), i32(block_index), i32(block_next), i32(block_slot),
            n_rows)


def _dispatch_kernel(nvalid_ref, p0_ref, p1_ref, h_ref, xs_hbm, zbuf, sem, zsem, *, tm, n_blocks):
    @pl.when(pl.program_id(0) == 0)
    def _():
        zbuf[...] = jnp.zeros_like(zbuf)

        def zero_block(i):
            return pltpu.make_async_copy(zbuf, xs_hbm.at[pl.ds(i * MOE_BLOCK, MOE_BLOCK)], zsem)

        def start(i, carry):
            pl.when(nvalid_ref[i] < MOE_BLOCK)(lambda: zero_block(i).start())
            return carry

        def wait(i, carry):
            pl.when(nvalid_ref[i] < MOE_BLOCK)(lambda: zero_block(i).wait())
            return carry

        lax.fori_loop(0, n_blocks, start, 0)
        lax.fori_loop(0, n_blocks, wait, 0)

    def issue(c, carry):
        for j in range(DMA_ISSUE_UNROLL):
            r = c * DMA_ISSUE_UNROLL + j
            src = h_ref.at[r]
            pltpu.make_async_copy(src, xs_hbm.at[p0_ref[0, r]], sem).start(priority=0)
            pltpu.make_async_copy(src, xs_hbm.at[p1_ref[0, r]], sem).start(priority=1)
        return carry

    lax.fori_loop(0, tm // DMA_ISSUE_UNROLL, issue, 0)
    for _ in range(2):
        pltpu.make_async_copy(h_ref, xs_hbm.at[pl.ds(0, tm)], sem).wait()


def _dispatch(n_valid, pos0, pos1, h2p, n_rows):
    n = h2p.shape[0]
    tm = 2048
    n_blocks = n_valid.shape[0]
    idx_spec = pl.BlockSpec((None, 1, tm), lambda t, nv: (t, 0, 0), memory_space=pltpu.SMEM)
    grid_spec = pltpu.PrefetchScalarGridSpec(
        num_scalar_prefetch=1,
        grid=(n // tm,),
        in_specs=[idx_spec, idx_spec, pl.BlockSpec((tm, SUBLANES, LANES), lambda t, nv: (t, 0, 0))],
        out_specs=pl.BlockSpec(memory_space=pl.ANY),
        scratch_shapes=[
            pltpu.VMEM((MOE_BLOCK, SUBLANES, LANES), h2p.dtype),
            pltpu.SemaphoreType.DMA(()),
            pltpu.SemaphoreType.DMA(()),
        ],
    )
    return pl.pallas_call(
        functools.partial(_dispatch_kernel, tm=tm, n_blocks=n_blocks),
        grid_spec=grid_spec,
        out_shape=jax.ShapeDtypeStruct((n_rows, SUBLANES, LANES), h2p.dtype),
        compiler_params=pltpu.CompilerParams(
            dimension_semantics=("arbitrary",), vmem_limit_bytes=VMEM_LIMIT,
            disable_bounds_checks=True),
        name="dispatch",
    )(n_valid, pos0.reshape(n // tm, 1, tm), pos1.reshape(n // tm, 1, tm), h2p)


def _moe_kernel(be_ref, nvalid_ref, bi_ref, nxt_ref, slot_ref, xs_hbm, wg_hbm, wu_hbm, wd_hbm, yb_hbm,
                wg_f, wu_f, wd_f, xbuf, ybuf, sem, xsem, ysem, *, n_blocks):
    i = pl.program_id(0)
    n_valid = nvalid_ref[i]
    block_rows = MOE_BLOCK * SUBLANES

    def y_copy(j):
        start = pl.multiple_of(j * block_rows, block_rows)
        return pltpu.make_async_copy(ybuf.at[j % 2], yb_hbm.at[pl.ds(start, block_rows)], ysem.at[j % 2])

    @pl.when(i >= 2)
    def _():
        y_copy(i - 2).wait()

    y_ref = ybuf.at[i % 2]

    def x_copy(j):
        start = pl.multiple_of(bi_ref[j] * block_rows, block_rows)
        return pltpu.make_async_copy(xs_hbm.at[pl.ds(start, block_rows)], xbuf.at[j % X_RING], xsem.at[j % X_RING])

    @pl.when(i == 0)
    def _():
        for j in range(X_RING - 1):
            x_copy(j).start()

    @pl.when(i + X_RING - 1 < n_blocks)
    def _():
        x_copy(i + X_RING - 1).start()

    x_copy(i).wait()
    x_ref = xbuf.at[i % X_RING]

    def fetch(e, s):
        return (pltpu.make_async_copy(wg_hbm.at[e], wg_f.at[s], sem.at[s, 0]),
                pltpu.make_async_copy(wu_hbm.at[e], wu_f.at[s], sem.at[s, 1]),
                pltpu.make_async_copy(wd_hbm.at[e], wd_f.at[s], sem.at[s, 2]))

    @pl.when(n_valid > 0)
    def _():
        e = be_ref[i]
        s = slot_ref[i]

        @pl.when(i == 0)
        def _():
            for cp in fetch(e, s):
                cp.start(priority=1)

        @pl.when((i == 0) | (e != be_ref[jnp.maximum(i - 1, 0)]))
        def _():
            for cp in fetch(e, s):
                cp.wait()
            nxt = nxt_ref[i]

            @pl.when(nxt >= 0)
            def _():
                for cp in fetch(nxt, 1 - s):
                    cp.start(priority=1)

        words = jnp.concatenate(
            [x_ref[pl.ds(c, MOE_BLOCK, stride=SUBLANES), :] for c in range(SUBLANES)], axis=1)
        lo, hi = _unpack_bf16_pair(words)
        half = words.shape[1]
        g = _dot(lo, wg_f[s, :half, :]) + _dot(hi, wg_f[s, half:, :])
        u = _dot(lo, wu_f[s, :half, :]) + _dot(hi, wu_f[s, half:, :])
        act = g * jax.nn.sigmoid(g) * u
        y = _dot(act, wd_f[s])
        y_words = _pack_bf16_pair(y[:, :half], y[:, half:])
        for c in range(SUBLANES):
            y_ref[pl.ds(c, MOE_BLOCK, stride=SUBLANES), :] = y_words[:, c * LANES:(c + 1) * LANES]

    @pl.when(n_valid == 0)
    def _():
        y_ref[...] = jnp.zeros_like(y_ref)

    y_copy(i).start()

    @pl.when(i == n_blocks - 1)
    def _():
        y_copy(i - 1).wait()
        y_copy(i).wait()


def _moe(block_expert, n_valid, block_index, block_next, block_slot, xs, w_gate, w_up, w_down):
    n_blocks = block_expert.shape[0]
    n_rows = xs.shape[0]
    d = 2 * SUBLANES * LANES
    hbm = pl.BlockSpec(memory_space=pl.ANY)
    grid_spec = pltpu.PrefetchScalarGridSpec(
        num_scalar_prefetch=5,
        grid=(n_blocks,),
        in_specs=[hbm, hbm, hbm, hbm],
        out_specs=hbm,
        scratch_shapes=[
            pltpu.VMEM((2, d, D_EXPERT), F32),
            pltpu.VMEM((2, d, D_EXPERT), F32),
            pltpu.VMEM((2, D_EXPERT, d), F32),
            pltpu.VMEM((X_RING, MOE_BLOCK * SUBLANES, LANES), jnp.uint32),
            pltpu.VMEM((2, MOE_BLOCK * SUBLANES, LANES), jnp.uint32),
            pltpu.SemaphoreType.DMA((2, 3)),
            pltpu.SemaphoreType.DMA((X_RING,)),
            pltpu.SemaphoreType.DMA((2,)),
        ],
    )
    return pl.pallas_call(
        functools.partial(_moe_kernel, n_blocks=n_blocks),
        grid_spec=grid_spec,
        out_shape=jax.ShapeDtypeStruct((n_rows * SUBLANES, LANES), jnp.uint32),
        compiler_params=pltpu.CompilerParams(
            dimension_semantics=("arbitrary",), vmem_limit_bytes=VMEM_LIMIT),
        name="moe",
    )(block_expert, n_valid, block_index, block_next, block_slot,
      xs.reshape(n_rows * SUBLANES, LANES), w_gate, w_up, w_down)


def _final_kernel(p0_ref, p1_ref, q0_ref, q1_ref, x1_ref, rt_ref, g2_ref, gf_ref, yb_hbm, yb_flat_hbm, o_ref,
                  y0buf, y1buf, sem, *, tm, n_tiles):
    t = pl.program_id(0)
    slot = t & 1

    def issue(pa_ref, pb_ref, s):
        def body(c, carry):
            for j in range(DMA_ISSUE_UNROLL):
                r = c * DMA_ISSUE_UNROLL + j
                dst = pl.ds(pl.multiple_of(r * SUBLANES, SUBLANES), SUBLANES)
                pltpu.make_async_copy(yb_hbm.at[pa_ref[0, r]], y0buf.at[s].at[dst], sem.at[s]).start(priority=0)
                pltpu.make_async_copy(yb_hbm.at[pb_ref[0, r]], y1buf.at[s].at[dst], sem.at[s]).start(priority=1)
            return carry

        lax.fori_loop(0, tm // DMA_ISSUE_UNROLL, body, 0)

    @pl.when(t == 0)
    def _():
        issue(p0_ref, p1_ref, 0)

    @pl.when(t + 1 < n_tiles)
    def _():
        issue(q0_ref, q1_ref, 1 - slot)

    pltpu.make_async_copy(yb_flat_hbm.at[pl.ds(0, tm * SUBLANES)], y0buf.at[slot], sem.at[slot]).wait()
    pltpu.make_async_copy(yb_flat_hbm.at[pl.ds(0, tm * SUBLANES)], y1buf.at[slot], sem.at[slot]).wait()

    def rows_of(buf):
        tiles = buf.at[slot]
        words = jnp.concatenate([tiles[pl.ds(c, tm, stride=SUBLANES), :] for c in range(SUBLANES)], axis=1)
        return jnp.concatenate(_unpack_bf16_pair(words), axis=1)

    rt = rt_ref[...]
    y = rt[:, 2:3] * rows_of(y0buf) + rt[:, 3:4] * rows_of(y1buf)
    x2 = x1_ref[...] + g2_ref[...] * y
    ms = jnp.mean(x2 * x2, axis=-1, keepdims=True)
    o_ref[...] = x2 * lax.rsqrt(ms + EPS) * gf_ref[...]


def _final(pos0, pos1, x1, rt, g2, gf, yb, seq):
    n, d = x1.shape
    tm = 256
    n_tiles = n // tm
    per_seq = seq // tm
    this_tile = pl.BlockSpec((None, 1, tm), lambda t: (t, 0, 0), memory_space=pltpu.SMEM)
    next_tile = pl.BlockSpec((None, 1, tm), lambda t: (jnp.minimum(t + 1, n_tiles - 1), 0, 0),
                             memory_space=pltpu.SMEM)
    p0 = pos0.reshape(n_tiles, 1, tm)
    p1 = pos1.reshape(n_tiles, 1, tm)
    return pl.pallas_call(
        functools.partial(_final_kernel, tm=tm, n_tiles=n_tiles),
        grid=(n_tiles,),
        in_specs=[
            this_tile, this_tile, next_tile, next_tile,
            pl.BlockSpec((tm, d), lambda t: (t, 0)),
            pl.BlockSpec((tm, ROUTE_W), lambda t: (t, 0)),
            pl.BlockSpec((None, 1, d), lambda t: (t // per_seq, 0, 0)),
            pl.BlockSpec((1, d), lambda t: (0, 0)),
            pl.BlockSpec(memory_space=pl.ANY),
            pl.BlockSpec(memory_space=pl.ANY),
        ],
        out_specs=pl.BlockSpec((tm, d), lambda t: (t, 0)),
        out_shape=jax.ShapeDtypeStruct((n, d), F32),
        scratch_shapes=[
            pltpu.VMEM((2, tm * SUBLANES, LANES), jnp.uint32),
            pltpu.VMEM((2, tm * SUBLANES, LANES), jnp.uint32),
            pltpu.SemaphoreType.DMA((2,)),
        ],
        compiler_params=pltpu.CompilerParams(
            dimension_semantics=("arbitrary",), vmem_limit_bytes=VMEM_LIMIT,
            disable_bounds_checks=True),
        name="final",
    )(p0, p1, p0, p1, x1, rt, g2, gf, yb.reshape(-1, SUBLANES, LANES), yb)


def kernel(x, c, ada_w, ada_b, norm1_g, w_in, conv3_w, conv4_w, conv4_b, lru_w_a, lru_b_a, lru_w_x, lru_b_x, lru_lambda, head_norm_conv_g, head_norm_lru_g, w_out, norm2_g, route_w_group, route_b_group, route_w_expert, route_b_expert, w_e_gate, w_e_up, w_e_down, final_norm_g):
    bsz, s, d = x.shape
    n_tok = bsz * s
    l = 0

    mod = _ada(c, ada_w[l], ada_b[l]).reshape(bsz, 6, 1, d)
    sh1, sc1, g1, sh2, sc2, g2 = (mod[:, j] for j in range(6))

    proj = _in_proj(x, sc1, sh1, norm1_g[l].reshape(1, d), w_in[l].astype(BF16))

    wax = jnp.concatenate([lru_w_a[l], lru_w_x[l]], axis=-1).astype(BF16)
    y = _mix(proj, conv3_w[l], conv4_w[l], conv4_b[l], wax, lru_b_a[l], lru_b_x[l], lru_lambda[l],
             head_norm_conv_g[l], head_norm_lru_g[l])

    w_route = jnp.concatenate(
        [route_w_group[l], jnp.transpose(route_w_expert[l], (1, 0, 2)).reshape(d, N_EXPERTS),
         jnp.zeros((d, ROUTE_W - N_GROUPS - N_EXPERTS), F32)], axis=1).astype(BF16)
    b_route = jnp.concatenate(
        [route_b_group[l], route_b_expert[l].reshape(-1),
         jnp.zeros((ROUTE_W - N_GROUPS - N_EXPERTS,), F32)]).reshape(1, ROUTE_W)
    x1, h2p, rt, rtt, counts = _out(y, x, g1, w_out[l].astype(BF16), sc2, sh2, norm2_g[l].reshape(1, d),
                                    w_route, b_route)

    rt = rt.reshape(n_tok, ROUTE_W)
    pos0, pos1, block_expert, n_valid, block_index, block_next, block_slot, n_rows = _plan(rtt, counts, n_tok)
    assert d // 2 == SUBLANES * LANES, "a token's packed words must fill exactly one (8, 128) tile"
    xs = _dispatch(n_valid, pos0, pos1, h2p.reshape(n_tok, SUBLANES, LANES), n_rows)
    yb = _moe(block_expert, n_valid, block_index, block_next, block_slot, xs,
              w_e_gate[l], w_e_up[l], w_e_down[l])
    out = _final(pos0, pos1, x1.reshape(n_tok, d), rt, g2, final_norm_g.reshape(1, d), yb, s)
    return out.reshape(bsz, s, d)
```
